```python
import jax, jax.numpy as jnp
from jax import lax
import numpy as np

D_MODEL = 1024
BATCH = 8
SEQ = 4096
DEPTH = 1

EPS = 1e-6
CHUNK = 64
GLA_HEADS = 4
GLA_DK = 128
GLA_DV = 256
GLA_QK = GLA_HEADS * GLA_DK
GLA_V = GLA_HEADS * GLA_DV
GLA_GATE_RANK = 16
GLA_GATE_TEMP = 16.0
RET_HEADS = 4
RET_DK = 256
RET_DV = 512
RET_QK = RET_HEADS * RET_DK
RET_V = RET_HEADS * RET_DV
ROPE_BASE = 10000.0
IN_SPLITS = (GLA_QK, GLA_QK, GLA_V, GLA_V, GLA_GATE_RANK, RET_QK, RET_QK, RET_V, RET_V, D_MODEL, D_MODEL)
D_IN = sum(IN_SPLITS)
N_GROUPS = 4
EXPERTS_PER_GROUP = 8
N_EXPERTS = N_GROUPS * EXPERTS_PER_GROUP
TOP_K_IN_GROUP = 2
EXPERT_HIDDEN = 512
EXPERT_BLOCK = 128

kernel_name = "hybrid_gla_retnet_hmoe_block"


def _rmsnorm(x, w):
    x32 = x.astype(jnp.float32)
    y = x32 * lax.rsqrt(jnp.mean(x32 * x32, axis=-1, keepdims=True) + EPS)
    return (y * w.astype(jnp.float32)).astype(x.dtype)


def _group_norm(o, w):
    mu = jnp.mean(o, axis=-1, keepdims=True)
    var = jnp.mean(jnp.square(o - mu), axis=-1, keepdims=True)
    y = ((o - mu) * lax.rsqrt(var + EPS)).reshape(o.shape[0], o.shape[1], -1)
    return y * w.astype(jnp.float32)


def _to_chunks(t):
    b, h, s, d = t.shape
    return t.reshape(b, h, s // CHUNK, CHUNK, d).transpose(2, 0, 1, 3, 4)


def _from_chunks(t):
    n, b, h, c, d = t.shape
    return t.transpose(1, 0, 3, 2, 4).reshape(b, n * c, h, d)


def _gla_chunked(q, k, v, log_a):
    b, h, _, dk = q.shape
    dv = v.shape[-1]
    causal = jnp.tril(jnp.ones((CHUNK, CHUNK), dtype=bool))[:, :, None]

    def step(state, inp):
        qi, ki, vi, ai = inp
        cum = jnp.cumsum(ai, axis=-2)
        diff = cum[..., :, None, :] - cum[..., None, :, :]
        decay = jnp.exp(jnp.where(causal, diff, -jnp.inf))
        scores = jnp.einsum("bhid,bhijd,bhjd->bhij", qi, decay, ki)
        o = (jnp.einsum("bhij,bhjv->bhiv", scores, vi)
             + jnp.einsum("bhid,bhdv->bhiv", qi * jnp.exp(cum), state))
        last = cum[..., -1:, :]
        state = (jnp.exp(last[..., 0, :])[..., None] * state
                 + jnp.einsum("bhjd,bhjv->bhdv", ki * jnp.exp(last - cum), vi))
        return state, o

    init = jnp.zeros((b, h, dk, dv), jnp.float32)
    xs = tuple(_to_chunks(t.astype(jnp.float32)) for t in (q, k, v, log_a))
    _, out = lax.scan(step, init, xs)
    return _from_chunks(out)


def _retention_chunked(q, k, v):
    b, h, _, dk = q.shape
    dv = v.shape[-1]
    log_gamma = jnp.log(1.0 - 2.0 ** (-5.0 - jnp.arange(h, dtype=jnp.float32)))
    idx = jnp.arange(CHUNK, dtype=jnp.float32)
    rel = idx[:, None] - idx[None, :]
    intra = jnp.where(rel >= 0, jnp.exp(log_gamma[:, None, None] * jnp.maximum(rel, 0.0)), 0.0)
    q_dec = jnp.exp(log_gamma[:, None] * (idx + 1.0))[:, :, None]
    k_dec = jnp.exp(log_gamma[:, None] * (CHUNK - 1.0 - idx))[:, :, None]
    chunk_dec = jnp.exp(log_gamma * CHUNK)[:, None, None]

    def step(state, inp):
        qi, ki, vi = inp
        scores = jnp.einsum("bhid,bhjd->bhij", qi, ki) * intra
        o = (jnp.einsum("bhij,bhjv->bhiv", scores, vi)
             + jnp.einsum("bhid,bhdv->bhiv", qi * q_dec, state))
        state = chunk_dec * state + jnp.einsum("bhjd,bhjv->bhdv", ki * k_dec, vi)
        return state, o

    init = jnp.zeros((b, h, dk, dv), jnp.float32)
    xs = tuple(_to_chunks(t.astype(jnp.float32)) for t in (q, k, v))
    _, out = lax.scan(step, init, xs)
    return _from_chunks(out)


def _rotary(t, positions):
    dk = t.shape[-1]
    theta = 1.0 / (ROPE_BASE ** jnp.linspace(0.0, 1.0, dk // 2, dtype=jnp.float32))
    theta = jnp.repeat(theta, 2)
    ang = positions.astype(jnp.float32)[:, :, None, None] * theta
    t32 = t.astype(jnp.float32)
    rot = jnp.stack([-t32[..., 1::2], t32[..., 0::2]], axis=-1).reshape(t.shape)
    return t32 * jnp.cos(ang) + rot * jnp.sin(ang)


def _hybrid_mixer(u, positions, w_in, gk_up, gk_bias, gla_norm_w, w_br_gla, ret_norm_w, w_br_ret, w_out):
    B, S, _ = u.shape
    proj = u @ w_in
    splits = np.cumsum(IN_SPLITS)[:-1].tolist()
    gq, gk, gv, gg, gdown, rq, rk, rv, rg, mga, mgb = jnp.split(proj, splits, axis=-1)

    def heads(t, h):
        return t.reshape(B, S, h, -1).transpose(0, 2, 1, 3)

    log_a = jax.nn.log_sigmoid((gdown @ gk_up + gk_bias).astype(jnp.float32)) / GLA_GATE_TEMP
    o_gla = _gla_chunked(heads(gq * (GLA_DK ** -0.5), GLA_HEADS), heads(gk, GLA_HEADS),
                         heads(gv, GLA_HEADS), heads(log_a, GLA_HEADS))
    o_gla = _rmsnorm(o_gla, gla_norm_w).reshape(B, S, GLA_V) * jax.nn.silu(gg.astype(jnp.float32))

    q_r = _rotary(rq.reshape(B, S, RET_HEADS, RET_DK), positions).transpose(0, 2, 1, 3)
    k_r = (_rotary(rk.reshape(B, S, RET_HEADS, RET_DK), positions) * (RET_DK ** -0.5)).transpose(0, 2, 1, 3)
    o_ret = _retention_chunked(q_r, k_r, heads(rv, RET_HEADS))
    o_ret = _group_norm(o_ret, ret_norm_w) * jax.nn.silu(rg.astype(jnp.float32))

    y_a = o_gla.astype(u.dtype) @ w_br_gla
    y_b = o_ret.astype(u.dtype) @ w_br_ret
    merged = (jax.nn.sigmoid(mga.astype(jnp.float32)) * y_a.astype(jnp.float32)
              + jax.nn.sigmoid(mgb.astype(jnp.float32)) * y_b.astype(jnp.float32))
    return merged.astype(u.dtype) @ w_out


def _hierarchical_moe(u, rg_w, rg_b, re_w, re_b, wg, wu, wd):
    B, S, D = u.shape
    t = u.reshape(-1, D)
    n = t.shape[0]
    g_logits = (t @ rg_w + rg_b).astype(jnp.float32)
    g_prob = jax.nn.softmax(g_logits, axis=-1)
    g_idx = jnp.argmax(g_logits, axis=-1)
    g_w = jnp.take_along_axis(g_prob, g_idx[:, None], axis=-1)
    e_all = (jnp.einsum("nd,gde->nge", t, re_w) + re_b).astype(jnp.float32)
    e_logits = jnp.take_along_axis(e_all, g_idx[:, None, None], axis=1)[:, 0]
    top_v, top_i = lax.top_k(e_logits, TOP_K_IN_GROUP)
    weights = g_w * jax.nn.softmax(top_v, axis=-1)
    expert_ids = g_idx[:, None] * EXPERTS_PER_GROUP + top_i

    n_assign = n * TOP_K_IN_GROUP
    flat_e = expert_ids.reshape(-1)
    flat_w = weights.reshape(-1)
    flat_tok = jnp.repeat(jnp.arange(n, dtype=jnp.int32), TOP_K_IN_GROUP)
    order = jnp.argsort(flat_e)
    sorted_e = flat_e[order]
    counts = jnp.bincount(flat_e, length=N_EXPERTS)
    padded = ((counts + EXPERT_BLOCK - 1) // EXPERT_BLOCK) * EXPERT_BLOCK
    pad_end = jnp.cumsum(padded)
    pad_start = pad_end - padded
    start = jnp.cumsum(counts) - counts
    dest = pad_start[sorted_e] + (jnp.arange(n_assign) - start[sorted_e])
    n_slots = ((n_assign + N_EXPERTS * (EXPERT_BLOCK - 1) + EXPERT_BLOCK - 1) // EXPERT_BLOCK) * EXPERT_BLOCK
    n_blocks = n_slots // EXPERT_BLOCK
    slot_tok = jnp.zeros((n_slots,), jnp.int32).at[dest].set(flat_tok[order])
    slot_w = jnp.zeros((n_slots,), jnp.float32).at[dest].set(flat_w[order])
    block_start = jnp.arange(n_blocks) * EXPERT_BLOCK
    block_expert = jnp.minimum(jnp.searchsorted(pad_end, block_start, side="right"), N_EXPERTS - 1)

    def run_block(args):
        e, tok, w = args
        xb = t[tok]
        hid = jax.nn.silu(xb @ wg[e]) * (xb @ wu[e])
        return ((hid @ wd[e]).astype(jnp.float32) * w[:, None])

    outs = lax.map(run_block, (block_expert, slot_tok.reshape(n_blocks, EXPERT_BLOCK),
                               slot_w.reshape(n_blocks, EXPERT_BLOCK)))
    y = jnp.zeros((n, D), jnp.float32).at[slot_tok].add(outs.reshape(n_slots, D))
    return y.reshape(B, S, D).astype(u.dtype)


def setup_inputs(seed: int = 0) -> dict:
    key = jax.random.key(seed)
    ks = jax.random.split(key, 24)
    f32 = jnp.float32

    def nrm(k, shape, scale):
        return jax.random.normal(k, shape, f32) * scale

    x = jax.random.normal(ks[0], (BATCH, SEQ, D_MODEL), f32)
    offsets = jax.random.randint(ks[1], (BATCH, 1), 0, 512)
    positions = (jnp.arange(SEQ, dtype=jnp.int32)[None, :] + offsets).astype(jnp.int32)
    return {
        "x": x,
        "positions": positions,
        "norm_mix_w": 1.0 + nrm(ks[2], (DEPTH, D_MODEL), 0.02),
        "w_in": nrm(ks[3], (DEPTH, D_MODEL, D_IN), D_MODEL ** -0.5),
        "gla_gk_up": nrm(ks[4], (DEPTH, GLA_GATE_RANK, GLA_QK), GLA_GATE_RANK ** -0.5),
        "gla_gk_bias": nrm(ks[5], (DEPTH, GLA_QK), 0.1),
        "gla_norm_w": 1.0 + nrm(ks[6], (DEPTH, GLA_DV), 0.02),
        "w_branch_gla": nrm(ks[7], (DEPTH, GLA_V, D_MODEL), GLA_V ** -0.5),
        "ret_norm_w": 1.0 + nrm(ks[8], (DEPTH, RET_V), 0.02),
        "w_branch_ret": nrm(ks[9], (DEPTH, RET_V, D_MODEL), RET_V ** -0.5),
        "w_out": nrm(ks[10], (DEPTH, D_MODEL, D_MODEL), D_MODEL ** -0.5),
        "norm_ffn_w": 1.0 + nrm(ks[11], (DEPTH, D_MODEL), 0.02),
        "router_group_w": nrm(ks[12], (DEPTH, D_MODEL, N_GROUPS), D_MODEL ** -0.5),
        "router_group_b": nrm(ks[13], (DEPTH, N_GROUPS), 0.01),
        "router_expert_w": nrm(ks[14], (DEPTH, N_GROUPS, D_MODEL, EXPERTS_PER_GROUP), D_MODEL ** -0.5),
        "router_expert_b": nrm(ks[15], (DEPTH, N_GROUPS, EXPERTS_PER_GROUP), 0.01),
        "expert_w_gate": nrm(ks[16], (DEPTH, N_EXPERTS, D_MODEL, EXPERT_HIDDEN), D_MODEL ** -0.5),
        "expert_w_up": nrm(ks[17], (DEPTH, N_EXPERTS, D_MODEL, EXPERT_HIDDEN), D_MODEL ** -0.5),
        "expert_w_down": nrm(ks[18], (DEPTH, N_EXPERTS, EXPERT_HIDDEN, D_MODEL), EXPERT_HIDDEN ** -0.5),
        "norm_final_w": 1.0 + nrm(ks[19], (D_MODEL,), 0.02),
    }


def reference(x, positions, norm_mix_w, w_in, gla_gk_up, gla_gk_bias, gla_norm_w, w_branch_gla,
              ret_norm_w, w_branch_ret, w_out, norm_ffn_w, router_group_w, router_group_b,
              router_expert_w, router_expert_b, expert_w_gate, expert_w_up, expert_w_down, norm_final_w):
    h = x
    for l in range(DEPTH):
        mix = _hybrid_mixer(_rmsnorm(h, norm_mix_w[l]), positions, w_in[l], gla_gk_up[l], gla_gk_bias[l],
                            gla_norm_w[l], w_branch_gla[l], ret_norm_w[l], w_branch_ret[l], w_out[l])
        h = h + mix.astype(h.dtype)
        ffn = _hierarchical_moe(_rmsnorm(h, norm_ffn_w[l]), router_group_w[l], router_group_b[l],
                                router_expert_w[l], router_expert_b[l], expert_w_gate[l],
                                expert_w_up[l], expert_w_down[l])
        h = h + ffn.astype(h.dtype)
    return _rmsnorm(h, norm_final_w)
```

```python
import functools
import math

import jax
import jax.numpy as jnp
import numpy as np
from jax import lax
from jax.experimental import pallas as pl
from jax.experimental.pallas import tpu as pltpu

F32 = jnp.float32
BF16 = jnp.bfloat16

EPS = 1e-6
GLA_HEADS = 4
GLA_DK = 128
GLA_DV = 256
GLA_GATE_RANK = 16
GLA_GATE_TEMP = 16.0
RET_HEADS = 4
RET_DK = 256
RET_DV = 512
ROPE_BASE = 10000.0
N_GROUPS = 4
EXPERTS_PER_GROUP = 8
N_EXPERTS = N_GROUPS * EXPERTS_PER_GROUP
EXPERT_HIDDEN = 512

LANES = 128
GLA_CHUNK = 64
GLA_DIAG = 8
RET_CHUNK = 256
EXPERT_ROWS = 256
VMEM_LIMIT = 48 * 1024 * 1024


def _dot(a, b):
    return jnp.dot(a, b, preferred_element_type=F32)


def _dot_nt(a, b):
    return lax.dot_general(a, b, (((1,), (1,)), ((), ())), preferred_element_type=F32)


def _split2(a):
    hi = a.astype(BF16)
    lo = (a - hi.astype(F32)).astype(BF16)
    return hi, lo


def _params(sem, vmem=VMEM_LIMIT):
    return pltpu.CompilerParams(dimension_semantics=sem, vmem_limit_bytes=vmem)


def _inproj_kernel(x_ref, nw_ref, w_ref, wgd_ref, proj_ref, gd_ref, u_scr):
    @pl.when(pl.program_id(1) == 0)
    def _():
        x = x_ref[...]
        ms = jnp.mean(x * x, axis=-1, keepdims=True)
        u = (x * lax.rsqrt(ms + EPS) * nw_ref[...]).astype(BF16)
        u_scr[...] = u
        gd_ref[...] = _dot(u, wgd_ref[...])

    proj_ref[...] = _dot(u_scr[...], w_ref[...]).astype(BF16)


def _inproj(x2, nw, w_main, w_gd, *, tm, tn):
    n, d = x2.shape
    p = w_main.shape[1]
    return pl.pallas_call(
        _inproj_kernel,
        grid=(n // tm, p // tn),
        in_specs=[
            pl.BlockSpec((tm, d), lambda i, j: (i, 0)),
            pl.BlockSpec((1, d), lambda i, j: (0, 0)),
            pl.BlockSpec((d, tn), lambda i, j: (0, j)),
            pl.BlockSpec((d, LANES), lambda i, j: (0, 0)),
        ],
        out_specs=[
            pl.BlockSpec((tm, tn), lambda i, j: (i, j)),
            pl.BlockSpec((tm, LANES), lambda i, j: (i, 0)),
        ],
        out_shape=[
            jax.ShapeDtypeStruct((n, p), BF16),
            jax.ShapeDtypeStruct((n, LANES), F32),
        ],
        scratch_shapes=[pltpu.VMEM((tm, d), BF16)],
        compiler_params=_params(("arbitrary", "arbitrary")),
        name="inproj",
    )(x2, nw, w_main, w_gd)


def _gla_tables(c):
    levels = []
    s = c // 2
    while s >= GLA_DIAG:
        levels.append(s)
        s //= 2
    i = np.arange(c)[:, None]
    t = np.arange(c)[None, :]
    mats = [t <= i, t > i]
    masks = []
    for s in levels:
        bs = (i // s) * s
        mats.append((t > bs) & (t <= i))
        mats.append((t > i) & (t <= np.minimum(bs + s, c - 1)))
        masks.append(((i // (2 * s)) == (t // (2 * s))) & (((i // s) % 2) == 1) & (((t // s) % 2) == 0))
    for d in range(1, GLA_DIAG):
        mats.append((t > i - d) & (t <= i))
    for d in range(GLA_DIAG):
        masks.append((t == i - d) & ((i % GLA_DIAG) >= d))
    g = np.concatenate(mats, 0).astype(np.float32)
    m = np.stack(masks, 0).astype(np.float32)
    return g, m, tuple(levels)


def _gla_kernel(q_ref, k_ref, v_ref, gg_ref, gd_ref, up_ref, bias_ref, nw_ref, g_ref, m_ref,
                o_ref, st_ref, *, c, nlev):
    h_n, dk, dv = GLA_HEADS, GLA_DK, GLA_DV
    rows = q_ref.shape[0]

    @pl.when(pl.program_id(1) == 0)
    def _():
        st_ref[...] = jnp.zeros_like(st_ref)

    g_tab = g_ref[...]
    up_hi = up_ref[0]
    up_lo = up_ref[1]
    q_scale = dk ** -0.5
    band0 = 2 + 2 * nlev

    def chunk(ci, carry):
        r0 = pl.multiple_of(ci * c, c)
        rs = pl.ds(r0, c)
        gd_hi, gd_lo = _split2(gd_ref[rs, :])
        xg = _dot(gd_hi, up_hi) + _dot(gd_hi, up_lo) + _dot(gd_lo, up_hi) + bias_ref[...]
        la = (jnp.minimum(xg, 0.0) - jnp.log1p(jnp.exp(-jnp.abs(xg)))) * (1.0 / GLA_GATE_TEMP)
        la_hi, la_lo = _split2(la)
        ex = jnp.exp(_dot(g_tab, la_hi) + _dot(g_tab, la_lo))

        for h in range(h_n):
            ks = slice(h * dk, (h + 1) * dk)
            vs = slice(h * dv, (h + 1) * dv)

            def tab(m, ks=ks):
                return ex[m * c:(m + 1) * c, ks]

            qh = q_ref[rs, ks].astype(F32) * q_scale
            kh = k_ref[rs, ks].astype(F32)
            vh = v_ref[rs, vs]
            st = st_ref[h]
            eb = tab(0)
            o = _dot_nt((qh * eb).astype(BF16), st.astype(BF16))
            a = m_ref[nlev] * jnp.sum(qh * kh, axis=-1, keepdims=True)
            for li in range(nlev):
                qs = (qh * tab(2 + 2 * li)).astype(BF16)
                kk = (kh * tab(3 + 2 * li)).astype(BF16)
                a = a + m_ref[li] * _dot_nt(qs, kk)
            for d in range(1, GLA_DIAG):
                kr = pltpu.roll(kh, d, 0)
                dd = jnp.sum(qh * kr * tab(band0 + d - 1), axis=-1, keepdims=True)
                a = a + m_ref[nlev + d] * dd
            o = o + _dot(a.astype(BF16), vh)
            e_last = eb[c - 1:c, :]
            k_st = (kh * tab(1)).astype(BF16)
            v_t = vh.astype(F32).T.astype(BF16)
            st_ref[h] = st * e_last + _dot(v_t, k_st)

            ms = jnp.mean(o * o, axis=-1, keepdims=True)
            y = o * lax.rsqrt(ms + EPS) * nw_ref[...]
            gg = gg_ref[rs, vs].astype(F32)
            o_ref[rs, vs] = (y * (gg * jax.nn.sigmoid(gg))).astype(BF16)
        return carry

    lax.fori_loop(0, rows // c, chunk, 0)


def _gla(proj, gd, up2, bias, nw, *, batch, seq, rows, cols):
    n = proj.shape[0]
    c = GLA_CHUNK
    g_np, m_np, levels = _gla_tables(c)
    g_tab = jnp.asarray(g_np, BF16)
    m_tab = jnp.asarray(m_np, F32)
    qk_w = GLA_HEADS * GLA_DK
    v_w = GLA_HEADS * GLA_DV
    spb = seq // rows
    row = lambda b, s: b * spb + s
    kern = functools.partial(_gla_kernel, c=c, nlev=len(levels))
    return pl.pallas_call(
        kern,
        grid=(batch, spb),
        in_specs=[
            pl.BlockSpec((rows, qk_w), lambda b, s: (row(b, s), cols[0])),
            pl.BlockSpec((rows, qk_w), lambda b, s: (row(b, s), cols[1])),
            pl.BlockSpec((rows, v_w), lambda b, s: (row(b, s), cols[2])),
            pl.BlockSpec((rows, v_w), lambda b, s: (row(b, s), cols[3])),
            pl.BlockSpec((rows, LANES), lambda b, s: (row(b, s), 0)),
            pl.BlockSpec(up2.shape, lambda b, s: (0, 0, 0)),
            pl.BlockSpec(bias.shape, lambda b, s: (0, 0)),
            pl.BlockSpec(nw.shape, lambda b, s: (0, 0)),
            pl.BlockSpec(g_tab.shape, lambda b, s: (0, 0)),
            pl.BlockSpec(m_tab.shape, lambda b, s: (0, 0, 0)),
        ],
        out_specs=pl.BlockSpec((rows, v_w), lambda b, s: (row(b, s), 0)),
        out_shape=jax.ShapeDtypeStruct((n, v_w), BF16),
        scratch_shapes=[pltpu.VMEM((GLA_HEADS, GLA_DV, GLA_DK), F32)],
        compiler_params=_params(("arbitrary", "arbitrary")),
        name="gla",
    )(proj, proj, proj, proj, gd, up2, bias, nw, g_tab, m_tab)


def _ret_log_gamma(h):
    return math.log(1.0 - 2.0 ** (-5.0 - h))


def _ret_kernel(pos_ref, th_ref, q_ref, k_ref, v_ref, rg_ref, nw_ref, o_ref, s_ref, dm_ref):
    h_n, dk, dv = RET_HEADS, RET_DK, RET_DV
    c = q_ref.shape[0]
    half = dk // 2

    @pl.when(pl.program_id(1) == 0)
    def _():
        s_ref[...] = jnp.zeros_like(s_ref)

    @pl.when((pl.program_id(0) == 0) & (pl.program_id(1) == 0))
    def _():
        rel = (lax.broadcasted_iota(jnp.int32, (c, c), 0) - lax.broadcasted_iota(jnp.int32, (c, c), 1)).astype(F32)
        for h in range(h_n):
            dm_ref[h] = jnp.where(rel >= 0, jnp.exp(_ret_log_gamma(h) * jnp.maximum(rel, 0.0)), 0.0)

    ang = pos_ref[...].astype(F32) * th_ref[...]
    cs = jnp.cos(ang)
    sn = jnp.sin(ang)
    idx = lax.broadcasted_iota(jnp.int32, (c, 1), 0).astype(F32)
    k_scale = dk ** -0.5

    def rot(ref, h):
        a = ref[:, h * dk:h * dk + half].astype(F32)
        b = ref[:, h * dk + half:(h + 1) * dk].astype(F32)
        return jnp.concatenate([a * cs - b * sn, b * cs + a * sn], axis=-1)

    for h in range(h_n):
        lg = _ret_log_gamma(h)
        vs = slice(h * dv, (h + 1) * dv)
        qr = rot(q_ref, h)
        kr = rot(k_ref, h) * k_scale
        vh = v_ref[:, vs]
        scores = _dot_nt(qr.astype(BF16), kr.astype(BF16)) * dm_ref[h]
        q_dec = jnp.exp(lg * (idx + 1.0))
        k_dec = jnp.exp(lg * (c - 1.0 - idx))
        s = s_ref[h]
        o = _dot(scores.astype(BF16), vh) + _dot((qr * q_dec).astype(BF16), s.astype(BF16))
        k_t = (kr * k_dec).T.astype(BF16)
        s_ref[h] = math.exp(lg * c) * s + _dot(k_t, vh)

        mu = jnp.mean(o, axis=-1, keepdims=True)
        oc = o - mu
        var = jnp.mean(oc * oc, axis=-1, keepdims=True)
        y = oc * lax.rsqrt(var + EPS) * nw_ref[:, vs]
        rg = rg_ref[:, vs].astype(F32)
        o_ref[:, vs] = (y * (rg * jax.nn.sigmoid(rg))).astype(BF16)


def _retention(proj, pos, theta, nw, *, batch, seq, cols):
    n = proj.shape[0]
    c = RET_CHUNK
    qk_w = RET_HEADS * RET_DK
    v_w = RET_HEADS * RET_DV
    spb = seq // c
    row = lambda b, s: b * spb + s
    return pl.pallas_call(
        _ret_kernel,
        grid=(batch, spb),
        in_specs=[
            pl.BlockSpec((c, 1), lambda b, s: (row(b, s), 0)),
            pl.BlockSpec(theta.shape, lambda b, s: (0, 0)),
            pl.BlockSpec((c, qk_w), lambda b, s: (row(b, s), cols[0])),
            pl.BlockSpec((c, qk_w), lambda b, s: (row(b, s), cols[1])),
            pl.BlockSpec((c, v_w), lambda b, s: (row(b, s), cols[2])),
            pl.BlockSpec((c, v_w), lambda b, s: (row(b, s), cols[3])),
            pl.BlockSpec(nw.shape, lambda b, s: (0, 0)),
        ],
        out_specs=pl.BlockSpec((c, v_w), lambda b, s: (row(b, s), 0)),
        out_shape=jax.ShapeDtypeStruct((n, v_w), BF16),
        scratch_shapes=[
            pltpu.VMEM((RET_HEADS, RET_DK, RET_DV), F32),
            pltpu.VMEM((RET_HEADS, c, c), F32),
        ],
        compiler_params=_params(("arbitrary", "arbitrary")),
        name="retention",
    )(pos, theta, proj, proj, proj, proj, nw)


def _mix_kernel(og_ref, or_ref, ma_ref, mb_ref, x_ref, wa_ref, wb_ref, wo_ref, nw_ref, wr_ref, br_ref,
                tri_ref, h_ref, t_ref, r_ref, cnt_ref, base_ref):
    tm = x_ref.shape[0]

    @pl.when(pl.program_id(0) == 0)
    def _():
        base_ref[...] = jnp.zeros_like(base_ref)

    ya = _dot(og_ref[...], wa_ref[...])
    yb = _dot(or_ref[...], wb_ref[...])
    merged = (jax.nn.sigmoid(ma_ref[...].astype(F32)) * ya + jax.nn.sigmoid(mb_ref[...].astype(F32)) * yb)
    h = x_ref[...] + _dot(merged.astype(BF16), wo_ref[...])
    h_ref[...] = h
    ms = jnp.mean(h * h, axis=-1, keepdims=True)
    t = h * lax.rsqrt(ms + EPS) * nw_ref[...]
    t_ref[...] = t

    t_hi, t_lo = _split2(t)
    lg = _dot(t_hi, wr_ref[0]) + _dot(t_hi, wr_ref[1]) + _dot(t_lo, wr_ref[0]) + br_ref[...]
    lane = lax.broadcasted_iota(jnp.int32, (tm, LANES), 1)
    neg = jnp.float32(-1e30)
    big = jnp.int32(1 << 20)

    def first_max(v):
        m = jnp.max(v, axis=-1, keepdims=True)
        return m, jnp.min(jnp.where(v == m, lane, big), axis=-1, keepdims=True)

    g_valid = (lane >= N_EXPERTS) & (lane < N_EXPERTS + N_GROUPS)
    g_m, g_lane = first_max(jnp.where(g_valid, lg, neg))
    g_w = 1.0 / jnp.sum(jnp.where(g_valid, jnp.exp(lg - g_m), 0.0), axis=-1, keepdims=True)
    g_idx = g_lane - N_EXPERTS
    e_valid = (lane < N_EXPERTS) & ((lane // EXPERTS_PER_GROUP) == g_idx)
    el = jnp.where(e_valid, lg, neg)
    v1, i1 = first_max(el)
    v2, i2 = first_max(jnp.where(lane == i1, neg, el))
    e21 = jnp.exp(v2 - v1)
    w1 = g_w / (1.0 + e21)
    w2 = g_w * e21 / (1.0 + e21)

    o1 = lane == i1
    o2 = lane == i2
    osum = jnp.where(o1 | o2, 1.0, 0.0)
    before = _dot(tri_ref[...], osum.astype(BF16)) + base_ref[0:1, :]
    r1 = jnp.sum(jnp.where(o1, before, 0.0), axis=-1, keepdims=True)
    r2 = jnp.sum(jnp.where(o2, before, 0.0), axis=-1, keepdims=True)
    base_ref[...] = base_ref[...] + jnp.sum(osum, axis=0, keepdims=True)
    cnt_ref[...] = base_ref[...]

    cols = (i1.astype(F32), i2.astype(F32), w1, w2, r1, r2)
    wide = jnp.zeros((tm, LANES), F32)
    for ci, col in enumerate(cols):
        wide = jnp.where(lane == ci, col, wide)
    r_ref[...] = wide[:, :r_ref.shape[1]]


def _mix(o_gla, o_ret, proj, x2, wa, wb, wo, nw, wr2, br, *, tm, cols):
    n, d = x2.shape
    tri = jnp.asarray(np.tril(np.ones((tm, tm), np.float32), -1), BF16)
    const = lambda shape: pl.BlockSpec(shape, lambda i: (0,) * len(shape))
    return pl.pallas_call(
        _mix_kernel,
        grid=(n // tm,),
        in_specs=[
            pl.BlockSpec((tm, o_gla.shape[1]), lambda i: (i, 0)),
            pl.BlockSpec((tm, o_ret.shape[1]), lambda i: (i, 0)),
            pl.BlockSpec((tm, d), lambda i: (i, cols[0])),
            pl.BlockSpec((tm, d), lambda i: (i, cols[1])),
            pl.BlockSpec((tm, d), lambda i: (i, 0)),
            const(wa.shape), const(wb.shape), const(wo.shape), const(nw.shape),
            const(wr2.shape), const(br.shape), const(tri.shape),
        ],
        out_specs=[
            pl.BlockSpec((tm, d), lambda i: (i, 0)),
            pl.BlockSpec((tm, d), lambda i: (i, 0)),
            pl.BlockSpec((tm, 8), lambda i: (i, 0)),
            pl.BlockSpec((8, LANES), lambda i: (0, 0)),
        ],
        out_shape=[
            jax.ShapeDtypeStruct((n, d), F32),
            jax.ShapeDtypeStruct((n, d), F32),
            jax.ShapeDtypeStruct((n, 8), F32),
            jax.ShapeDtypeStruct((8, LANES), F32),
        ],
        scratch_shapes=[pltpu.VMEM((8, LANES), F32)],
        compiler_params=_params(("arbitrary",)),
        name="mix_router",
    )(o_gla, o_ret, proj, proj, x2, wa, wb, wo, nw, wr2, br, tri)


def _dispatch_kernel(tail_ref, has_ref, dest_ref, t_hbm, xs_hbm, zero_ref, sem, zsem, *, window):
    step = pl.program_id(0)
    chunk = dest_ref.shape[0]
    blk = zero_ref.shape[0]

    def tail_copy(e):
        return pltpu.make_async_copy(zero_ref, xs_hbm.at[pl.ds(pl.multiple_of(tail_ref[e], blk), blk), :], zsem)

    @pl.when(step == 0)
    def _():
        zero_ref[...] = jnp.zeros_like(zero_ref)
        for e in range(tail_ref.shape[0]):
            @pl.when(has_ref[e] > 0)
            def _(e=e):
                tail_copy(e).start()
        for e in range(tail_ref.shape[0]):
            @pl.when(has_ref[e] > 0)
            def _(e=e):
                tail_copy(e).wait()

    def row_copy(tok, dst):
        return pltpu.make_async_copy(t_hbm.at[pl.ds(tok, 1), :], xs_hbm.at[pl.ds(dst, 1), :], sem)

    tok0 = step * (chunk // 2)

    def body(a, carry):
        @pl.when(a >= window)
        def _():
            row_copy(0, 0).wait()
        row_copy(tok0 + a // 2, dest_ref[a]).start()
        return carry

    lax.fori_loop(0, chunk, body, 0)

    def drain(a, carry):
        row_copy(0, 0).wait()
        return carry

    lax.fori_loop(0, min(window, chunk), drain, 0)


def _dispatch(t, dest_flat, tail, has, *, n_slots, chunk, window=16):
    n, d = t.shape
    kern = functools.partial(_dispatch_kernel, window=window)
    return pl.pallas_call(
        kern,
        grid_spec=pltpu.PrefetchScalarGridSpec(
            num_scalar_prefetch=2,
            grid=(dest_flat.shape[0] // chunk,),
            in_specs=[
                pl.BlockSpec((chunk,), lambda i, *_: (i,), memory_space=pltpu.SMEM),
                pl.BlockSpec(memory_space=pl.ANY),
            ],
            out_specs=pl.BlockSpec(memory_space=pl.ANY),
            scratch_shapes=[
                pltpu.VMEM((EXPERT_ROWS, d), F32),
                pltpu.SemaphoreType.DMA,
                pltpu.SemaphoreType.DMA,
            ],
        ),
        out_shape=jax.ShapeDtypeStruct((n_slots, d), F32),
        compiler_params=pltpu.CompilerParams(dimension_semantics=("arbitrary",), has_side_effects=True),
        name="dispatch",
    )(tail, has, dest_flat, t)


def _expert_kernel(be_ref, nu_ref, xs_ref, wg_ref, wu_ref, wd_ref, ys_ref):
    b = pl.program_id(0)

    @pl.when(b < nu_ref[0])
    def _():
        x = xs_ref[...].astype(BF16)
        g = _dot(x, wg_ref[0])
        u = _dot(x, wu_ref[0])
        hid = (g * jax.nn.sigmoid(g) * u).astype(BF16)
        ys_ref[...] = _dot(hid, wd_ref[0])

    @pl.when(b >= nu_ref[0])
    def _():
        ys_ref[...] = jnp.zeros_like(ys_ref)


def _experts(xs, block_expert, n_used, wg, wu, wd):
    n_slots, d = xs.shape
    blk = EXPERT_ROWS
    hid = wg.shape[2]
    return pl.pallas_call(
        _expert_kernel,
        grid_spec=pltpu.PrefetchScalarGridSpec(
            num_scalar_prefetch=2,
            grid=(n_slots // blk,),
            in_specs=[
                pl.BlockSpec((blk, d), lambda b, be, nu: (jnp.minimum(b, nu[0] - 1), 0)),
                pl.BlockSpec((1, d, hid), lambda b, be, nu: (be[b], 0, 0)),
                pl.BlockSpec((1, d, hid), lambda b, be, nu: (be[b], 0, 0)),
                pl.BlockSpec((1, hid, d), lambda b, be, nu: (be[b], 0, 0)),
            ],
            out_specs=pl.BlockSpec((blk, d), lambda b, be, nu: (b, 0)),
        ),
        out_shape=jax.ShapeDtypeStruct((n_slots, d), F32),
        compiler_params=_params(("arbitrary",)),
        name="experts",
    )(block_expert, n_used, xs, wg, wu, wd)


def _combine_kernel(dcur_ref, dnext_ref, h_ref, r_ref, nw_ref, ys_hbm, o_ref, buf, sem):
    i = pl.program_id(0)
    n_steps = pl.num_programs(0)
    tm = h_ref.shape[0]

    def row_copy(dst_row, src_row, slot):
        return pltpu.make_async_copy(ys_hbm.at[pl.ds(src_row, 1), :], buf.at[slot, pl.ds(dst_row, 1), :], sem.at[slot])

    def issue(dref, slot):
        def body(a, carry):
            row_copy((a % 2) * tm + a // 2, dref[a], slot).start()
            return carry
        lax.fori_loop(0, 2 * tm, body, 0)

    @pl.when(i == 0)
    def _():
        issue(dcur_ref, 0)

    @pl.when(i + 1 < n_steps)
    def _():
        issue(dnext_ref, (i + 1) % 2)

    slot = i % 2

    def wait_one(a, carry):
        row_copy(0, 0, slot).wait()
        return carry

    lax.fori_loop(0, 2 * tm, wait_one, 0)

    r = r_ref[...]
    y = buf[slot, 0:tm, :] * r[:, 2:3] + buf[slot, tm:2 * tm, :] * r[:, 3:4]
    h = h_ref[...] + y
    ms = jnp.mean(h * h, axis=-1, keepdims=True)
    o_ref[...] = h * lax.rsqrt(ms + EPS) * nw_ref[...]


def _combine(dest_flat, h1, r, nw, ys, *, tm):
    n, d = h1.shape
    n_steps = n // tm
    return pl.pallas_call(
        _combine_kernel,
        grid=(n_steps,),
        in_specs=[
            pl.BlockSpec((2 * tm,), lambda i: (i,), memory_space=pltpu.SMEM),
            pl.BlockSpec((2 * tm,), lambda i: (jnp.minimum(i + 1, n_steps - 1),), memory_space=pltpu.SMEM),
            pl.BlockSpec((tm, d), lambda i: (i, 0)),
            pl.BlockSpec((tm, r.shape[1]), lambda i: (i, 0)),
            pl.BlockSpec((1, d), lambda i: (0, 0)),
            pl.BlockSpec(memory_space=pl.ANY),
        ],
        out_specs=pl.BlockSpec((tm, d), lambda i: (i, 0)),
        out_shape=jax.ShapeDtypeStruct((n, d), F32),
        scratch_shapes=[pltpu.VMEM((2, 2 * tm, d), F32), pltpu.SemaphoreType.DMA((2,))],
        compiler_params=_params(("arbitrary",)),
        name="combine",
    )(dest_flat, dest_flat, h1, r, nw, ys)


def _layer(h, positions, norm_mix_w, w_in, gk_up, gk_bias, gla_norm_w, w_br_gla, ret_norm_w, w_br_ret, w_out,
           norm_ffn_w, rg_w, rg_b, re_w, re_b, wg, wu, wd, norm_final_w, *, tm_in, tn_in, gla_rows, tm_mix,
           tm_comb, disp_chunk):
    batch, seq, d = h.shape
    n = batch * seq
    x2 = h.reshape(n, d)
    gqk, gv = GLA_HEADS * GLA_DK, GLA_HEADS * GLA_DV
    rqk, rv = RET_HEADS * RET_DK, RET_HEADS * RET_DV

    sizes = (gqk, gqk, gv, gv, GLA_GATE_RANK, rqk, rqk, rv, rv, d, d)
    offs = np.concatenate([[0], np.cumsum(sizes)])
    seg = lambda i: w_in[:, offs[i]:offs[i + 1]]
    perm = np.concatenate([np.concatenate([np.arange(0, RET_DK, 2), np.arange(1, RET_DK, 2)]) + hh * RET_DK
                           for hh in range(RET_HEADS)])
    w_main = jnp.concatenate([seg(7), seg(8), seg(2), seg(3), seg(5)[:, perm], seg(6)[:, perm], seg(9), seg(10),
                              seg(0), seg(1)], axis=1).astype(BF16)
    assert rv % gv == 0 and gv == rqk == d and rv == 2 * d and gqk * 2 == d
    ret_cols = (4 + 2, 4 + 3, 0, 1)
    gla_cols = (2 * (4 + 6), 2 * (4 + 6) + 1, 4, 5)
    mix_cols = (4 + 4, 4 + 5)
    w_gd = jnp.pad(seg(4), ((0, 0), (0, LANES - GLA_GATE_RANK))).astype(BF16)

    proj, gd = _inproj(x2, norm_mix_w.reshape(1, d), w_main, w_gd, tm=tm_in, tn=tn_in)

    up = jnp.pad(gk_up, ((0, LANES - GLA_GATE_RANK), (0, 0)))
    up_hi = up.astype(BF16)
    up2 = jnp.stack([up_hi, (up - up_hi.astype(F32)).astype(BF16)])
    o_gla = _gla(proj, gd, up2, gk_bias.reshape(1, gqk), gla_norm_w.reshape(1, GLA_DV),
                 batch=batch, seq=seq, rows=gla_rows, cols=gla_cols)

    theta = (1.0 / (ROPE_BASE ** jnp.linspace(0.0, 1.0, RET_DK // 2, dtype=F32))).reshape(1, RET_DK // 2)
    o_ret = _retention(proj, positions.reshape(n, 1), theta, ret_norm_w.reshape(1, rv),
                       batch=batch, seq=seq, cols=ret_cols)

    wr = jnp.concatenate([re_w.transpose(1, 0, 2).reshape(d, N_EXPERTS), rg_w], axis=1)
    wr = jnp.pad(wr, ((0, 0), (0, LANES - N_EXPERTS - N_GROUPS)))
    wr_hi = wr.astype(BF16)
    wr2 = jnp.stack([wr_hi, (wr - wr_hi.astype(F32)).astype(BF16)])
    br = jnp.pad(jnp.concatenate([re_b.reshape(-1), rg_b]), (0, LANES - N_EXPERTS - N_GROUPS)).reshape(1, LANES)
    h1, t, r, cnt = _mix(o_gla, o_ret, proj, x2, w_br_gla.astype(BF16), w_br_ret.astype(BF16), w_out.astype(BF16),
                         norm_ffn_w.reshape(1, d), wr2, br, tm=tm_mix, cols=mix_cols)

    blk = EXPERT_ROWS
    n_assign = 2 * n
    n_slots = -(-(n_assign + N_EXPERTS * (blk - 1)) // blk) * blk
    counts = cnt[0, :N_EXPERTS].astype(jnp.int32)
    padded = ((counts + blk - 1) // blk) * blk
    pad_end = jnp.cumsum(padded)
    pad_start = pad_end - padded
    eid = r[:, 0:2].astype(jnp.int32)
    dest = (pad_start[eid] + r[:, 4:6].astype(jnp.int32)).reshape(-1)
    block_start = jnp.arange(n_slots // blk, dtype=jnp.int32) * blk
    block_expert = jnp.minimum(jnp.sum((pad_end[None, :] <= block_start[:, None]).astype(jnp.int32), axis=1),
                               N_EXPERTS - 1)
    n_used = (pad_end[-1:] // blk).astype(jnp.int32)
    spare = pad_end[-1] + jnp.arange(N_EXPERTS, dtype=jnp.int32) * blk
    tail = jnp.concatenate([jnp.maximum(pad_end - blk, 0), jnp.minimum(spare, n_slots - blk)]).astype(jnp.int32)
    has = jnp.concatenate([counts, (spare < n_slots).astype(jnp.int32)])

    xs = _dispatch(t, dest, tail, has, n_slots=n_slots, chunk=disp_chunk)
    ys = _experts(xs, block_expert, n_used, wg.astype(BF16), wu.astype(BF16), wd.astype(BF16))
    out = _combine(dest, h1, r, norm_final_w.reshape(1, d), ys, tm=tm_comb)
    return out.reshape(batch, seq, d)


def kernel(x, positions, norm_mix_w, w_in, gla_gk_up, gla_gk_bias, gla_norm_w, w_branch_gla, ret_norm_w,
           w_branch_ret, w_out, norm_ffn_w, router_group_w, router_group_b, router_expert_w, router_expert_b,
           expert_w_gate, expert_w_up, expert_w_down, norm_final_w):
    assert norm_mix_w.shape[0] == 1, "single-layer block"
    return _layer(x, positions, norm_mix_w[0], w_in[0], gla_gk_up[0], gla_gk_bias[0], gla_norm_w[0], w_branch_gla[0],
                  ret_norm_w[0], w_branch_ret[0], w_out[0], norm_ffn_w[0], router_group_w[0], router_group_b[0],
                  router_expert_w[0], router_expert_b[0], expert_w_gate[0], expert_w_up[0], expert_w_down[0],
                  norm_final_w, tm_in=1024, tn_in=1024, gla_rows=256, tm_mix=512, tm_comb=256, disp_chunk=2048)
```

```python
import functools
import math

import jax
import jax.numpy as jnp
import numpy as np
from jax import lax
from jax.experimental import pallas as pl
from jax.experimental.pallas import tpu as pltpu

F32 = jnp.float32
BF16 = jnp.bfloat16

EPS = 1e-6
GLA_HEADS = 4
GLA_DK = 128
GLA_DV = 256
GLA_GATE_RANK = 16
GLA_GATE_TEMP = 16.0
RET_HEADS = 4
RET_DK = 256
RET_DV = 512
ROPE_BASE = 10000.0
N_GROUPS = 4
EXPERTS_PER_GROUP = 8
N_EXPERTS = N_GROUPS * EXPERTS_PER_GROUP
EXPERT_HIDDEN = 512

LANES = 128
GLA_CHUNK = 64
GLA_DIAG = 8
RET_CHUNK = 256
EXPERT_ROWS = 256
VMEM_LIMIT = 48 * 1024 * 1024


def _dot(a, b):
    return jnp.dot(a, b, preferred_element_type=F32)


def _dot_nt(a, b):
    return lax.dot_general(a, b, (((1,), (1,)), ((), ())), preferred_element_type=F32)


def _split2(a):
    hi = a.astype(BF16)
    lo = (a - hi.astype(F32)).astype(BF16)
    return hi, lo


def _params(sem, vmem=VMEM_LIMIT):
    return pltpu.CompilerParams(dimension_semantics=sem, vmem_limit_bytes=vmem)


def _inproj_kernel(x_ref, nw_ref, w_ref, wgd_ref, proj_ref, gd_ref, u_scr):
    @pl.when(pl.program_id(1) == 0)
    def _():
        x = x_ref[...]
        ms = jnp.mean(x * x, axis=-1, keepdims=True)
        u = (x * lax.rsqrt(ms + EPS) * nw_ref[...]).astype(BF16)
        u_scr[...] = u
        gd_ref[...] = _dot(u, wgd_ref[...])

    proj_ref[...] = _dot(u_scr[...], w_ref[...]).astype(BF16)


def _inproj(x2, nw, w_main, w_gd, *, tm, tn):
    n, d = x2.shape
    p = w_main.shape[1]
    return pl.pallas_call(
        _inproj_kernel,
        grid=(n // tm, p // tn),
        in_specs=[
            pl.BlockSpec((tm, d), lambda i, j: (i, 0)),
            pl.BlockSpec((1, d), lambda i, j: (0, 0)),
            pl.BlockSpec((d, tn), lambda i, j: (0, j)),
            pl.BlockSpec((d, LANES), lambda i, j: (0, 0)),
        ],
        out_specs=[
            pl.BlockSpec((tm, tn), lambda i, j: (i, j)),
            pl.BlockSpec((tm, LANES), lambda i, j: (i, 0)),
        ],
        out_shape=[
            jax.ShapeDtypeStruct((n, p), BF16),
            jax.ShapeDtypeStruct((n, LANES), F32),
        ],
        scratch_shapes=[pltpu.VMEM((tm, d), BF16)],
        compiler_params=_params(("arbitrary", "arbitrary")),
        name="inproj",
    )(x2, nw, w_main, w_gd)


def _gla_tables(c):
    levels = []
    s = c // 2
    while s >= GLA_DIAG:
        levels.append(s)
        s //= 2
    i = np.arange(c)[:, None]
    t = np.arange(c)[None, :]
    mats = [t <= i, t > i]
    masks = []
    for s in levels:
        bs = (i // s) * s
        mats.append((t > bs) & (t <= i))
        mats.append((t > i) & (t <= np.minimum(bs + s, c - 1)))
        masks.append(((i // (2 * s)) == (t // (2 * s))) & (((i // s) % 2) == 1) & (((t // s) % 2) == 0))
    for d in range(1, GLA_DIAG):
        mats.append((t > i - d) & (t <= i))
    for d in range(GLA_DIAG):
        masks.append((t == i - d) & ((i % GLA_DIAG) >= d))
    g = np.concatenate(mats, 0).astype(np.float32)
    m = np.stack(masks, 0).astype(np.float32)
    return g, m, tuple(levels)


def _gla_kernel(q_ref, k_ref, v_ref, gg_ref, gd_ref, up_ref, bias_ref, nw_ref, g_ref, m_ref,
                o_ref, st_ref, *, c, nlev):
    h_n, dk, dv = GLA_HEADS, GLA_DK, GLA_DV
    rows = q_ref.shape[0]

    @pl.when(pl.program_id(1) == 0)
    def _():
        st_ref[...] = jnp.zeros_like(st_ref)

    g_tab = g_ref[...]
    up_hi = up_ref[0]
    up_lo = up_ref[1]
    q_scale = dk ** -0.5
    band0 = 2 + 2 * nlev

    def chunk(ci, carry):
        r0 = pl.multiple_of(ci * c, c)
        rs = pl.ds(r0, c)
        gd_hi, gd_lo = _split2(gd_ref[rs, :])
        xg = _dot(gd_hi, up_hi) + _dot(gd_hi, up_lo) + _dot(gd_lo, up_hi) + bias_ref[...]
        la = (jnp.minimum(xg, 0.0) - jnp.log1p(jnp.exp(-jnp.abs(xg)))) * (1.0 / GLA_GATE_TEMP)
        la_hi, la_lo = _split2(la)
        ex = jnp.exp(_dot(g_tab, la_hi) + _dot(g_tab, la_lo))

        for h in range(h_n):
            ks = slice(h * dk, (h + 1) * dk)
            vs = slice(h * dv, (h + 1) * dv)

            def tab(m, ks=ks):
                return ex[m * c:(m + 1) * c, ks]

            qh = q_ref[rs, ks].astype(F32) * q_scale
            kh = k_ref[rs, ks].astype(F32)
            vh = v_ref[rs, vs]
            st = st_ref[h]
            eb = tab(0)
            o = _dot_nt((qh * eb).astype(BF16), st.astype(BF16))
            a = m_ref[nlev] * jnp.sum(qh * kh, axis=-1, keepdims=True)
            for li in range(nlev):
                qs = (qh * tab(2 + 2 * li)).astype(BF16)
                kk = (kh * tab(3 + 2 * li)).astype(BF16)
                a = a + m_ref[li] * _dot_nt(qs, kk)
            for d in range(1, GLA_DIAG):
                kr = pltpu.roll(kh, d, 0)
                dd = jnp.sum(qh * kr * tab(band0 + d - 1), axis=-1, keepdims=True)
                a = a + m_ref[nlev + d] * dd
            o = o + _dot(a.astype(BF16), vh)
            e_last = eb[c - 1:c, :]
            k_st = (kh * tab(1)).astype(BF16)
            v_t = vh.astype(F32).T.astype(BF16)
            st_ref[h] = st * e_last + _dot(v_t, k_st)

            ms = jnp.mean(o * o, axis=-1, keepdims=True)
            y = o * lax.rsqrt(ms + EPS) * nw_ref[...]
            gg = gg_ref[rs, vs].astype(F32)
            o_ref[rs, vs] = (y * (gg * jax.nn.sigmoid(gg))).astype(BF16)
        return carry

    lax.fori_loop(0, rows // c, chunk, 0, unroll=2)


def _gla(proj, gd, up2, bias, nw, *, batch, seq, rows, cols):
    n = proj.shape[0]
    c = GLA_CHUNK
    g_np, m_np, levels = _gla_tables(c)
    g_tab = jnp.asarray(g_np, BF16)
    m_tab = jnp.asarray(m_np, F32)
    qk_w = GLA_HEADS * GLA_DK
    v_w = GLA_HEADS * GLA_DV
    spb = seq // rows
    row = lambda b, s: b * spb + s
    kern = functools.partial(_gla_kernel, c=c, nlev=len(levels))
    return pl.pallas_call(
        kern,
        grid=(batch, spb),
        in_specs=[
            pl.BlockSpec((rows, qk_w), lambda b, s: (row(b, s), cols[0])),
            pl.BlockSpec((rows, qk_w), lambda b, s: (row(b, s), cols[1])),
            pl.BlockSpec((rows, v_w), lambda b, s: (row(b, s), cols[2])),
            pl.BlockSpec((rows, v_w), lambda b, s: (row(b, s), cols[3])),
            pl.BlockSpec((rows, LANES), lambda b, s: (row(b, s), 0)),
            pl.BlockSpec(up2.shape, lambda b, s: (0, 0, 0)),
            pl.BlockSpec(bias.shape, lambda b, s: (0, 0)),
            pl.BlockSpec(nw.shape, lambda b, s: (0, 0)),
            pl.BlockSpec(g_tab.shape, lambda b, s: (0, 0)),
            pl.BlockSpec(m_tab.shape, lambda b, s: (0, 0, 0)),
        ],
        out_specs=pl.BlockSpec((rows, v_w), lambda b, s: (row(b, s), 0)),
        out_shape=jax.ShapeDtypeStruct((n, v_w), BF16),
        scratch_shapes=[pltpu.VMEM((GLA_HEADS, GLA_DV, GLA_DK), F32)],
        compiler_params=_params(("arbitrary", "arbitrary")),
        name="gla",
    )(proj, proj, proj, proj, gd, up2, bias, nw, g_tab, m_tab)


def _ret_log_gamma(h):
    return math.log(1.0 - 2.0 ** (-5.0 - h))


def _ret_kernel(pos_ref, th_ref, q_ref, k_ref, v_ref, rg_ref, nw_ref, o_ref, s_ref, dm_ref):
    h_n, dk, dv = RET_HEADS, RET_DK, RET_DV
    c = q_ref.shape[0]
    half = dk // 2

    @pl.when(pl.program_id(1) == 0)
    def _():
        s_ref[...] = jnp.zeros_like(s_ref)

    @pl.when((pl.program_id(0) == 0) & (pl.program_id(1) == 0))
    def _():
        rel = (lax.broadcasted_iota(jnp.int32, (c, c), 0) - lax.broadcasted_iota(jnp.int32, (c, c), 1)).astype(F32)
        for h in range(h_n):
            dm_ref[h] = jnp.where(rel >= 0, jnp.exp(_ret_log_gamma(h) * jnp.maximum(rel, 0.0)), 0.0)

    ang = pos_ref[...].astype(F32) * th_ref[...]
    cs = jnp.cos(ang)
    sn = jnp.sin(ang)
    idx = lax.broadcasted_iota(jnp.int32, (c, 1), 0).astype(F32)
    k_scale = dk ** -0.5

    def rot(ref, h):
        a = ref[:, h * dk:h * dk + half].astype(F32)
        b = ref[:, h * dk + half:(h + 1) * dk].astype(F32)
        return jnp.concatenate([a * cs - b * sn, b * cs + a * sn], axis=-1)

    for h in range(h_n):
        lg = _ret_log_gamma(h)
        vs = slice(h * dv, (h + 1) * dv)
        qr = rot(q_ref, h)
        kr = rot(k_ref, h) * k_scale
        vh = v_ref[:, vs]
        scores = _dot_nt(qr.astype(BF16), kr.astype(BF16)) * dm_ref[h]
        q_dec = jnp.exp(lg * (idx + 1.0))
        k_dec = jnp.exp(lg * (c - 1.0 - idx))
        s = s_ref[h]
        o = _dot(scores.astype(BF16), vh) + _dot((qr * q_dec).astype(BF16), s.astype(BF16))
        k_t = (kr * k_dec).T.astype(BF16)
        s_ref[h] = math.exp(lg * c) * s + _dot(k_t, vh)

        mu = jnp.mean(o, axis=-1, keepdims=True)
        oc = o - mu
        var = jnp.mean(oc * oc, axis=-1, keepdims=True)
        y = oc * lax.rsqrt(var + EPS) * nw_ref[:, vs]
        rg = rg_ref[:, vs].astype(F32)
        o_ref[:, vs] = (y * (rg * jax.nn.sigmoid(rg))).astype(BF16)


def _retention(proj, pos, theta, nw, *, batch, seq, cols):
    n = proj.shape[0]
    c = RET_CHUNK
    qk_w = RET_HEADS * RET_DK
    v_w = RET_HEADS * RET_DV
    spb = seq // c
    row = lambda b, s: b * spb + s
    return pl.pallas_call(
        _ret_kernel,
        grid=(batch, spb),
        in_specs=[
            pl.BlockSpec((c, 1), lambda b, s: (row(b, s), 0)),
            pl.BlockSpec(theta.shape, lambda b, s: (0, 0)),
            pl.BlockSpec((c, qk_w), lambda b, s: (row(b, s), cols[0])),
            pl.BlockSpec((c, qk_w), lambda b, s: (row(b, s), cols[1])),
            pl.BlockSpec((c, v_w), lambda b, s: (row(b, s), cols[2])),
            pl.BlockSpec((c, v_w), lambda b, s: (row(b, s), cols[3])),
            pl.BlockSpec(nw.shape, lambda b, s: (0, 0)),
        ],
        out_specs=pl.BlockSpec((c, v_w), lambda b, s: (row(b, s), 0)),
        out_shape=jax.ShapeDtypeStruct((n, v_w), BF16),
        scratch_shapes=[
            pltpu.VMEM((RET_HEADS, RET_DK, RET_DV), F32),
            pltpu.VMEM((RET_HEADS, c, c), F32),
        ],
        compiler_params=_params(("arbitrary", "arbitrary")),
        name="retention",
    )(pos, theta, proj, proj, proj, proj, nw)


def _mix_kernel(og_ref, or_ref, ma_ref, mb_ref, x_ref, wa_ref, wb_ref, wo_ref, nw_ref, wr_ref, br_ref,
                tri_ref, h_ref, t_ref, r_ref, cnt_ref, base_ref):
    tm = x_ref.shape[0]

    @pl.when(pl.program_id(0) == 0)
    def _():
        base_ref[...] = jnp.zeros_like(base_ref)

    ya = _dot(og_ref[...], wa_ref[...])
    yb = _dot(or_ref[...], wb_ref[...])
    merged = (jax.nn.sigmoid(ma_ref[...].astype(F32)) * ya + jax.nn.sigmoid(mb_ref[...].astype(F32)) * yb)
    h = x_ref[...] + _dot(merged.astype(BF16), wo_ref[...])
    h_ref[...] = h
    ms = jnp.mean(h * h, axis=-1, keepdims=True)
    t = h * lax.rsqrt(ms + EPS) * nw_ref[...]
    t_ref[...] = t

    t_hi, t_lo = _split2(t)
    lg = _dot(t_hi, wr_ref[0]) + _dot(t_hi, wr_ref[1]) + _dot(t_lo, wr_ref[0]) + br_ref[...]
    lane = lax.broadcasted_iota(jnp.int32, (tm, LANES), 1)
    neg = jnp.float32(-1e30)
    big = jnp.int32(1 << 20)

    def first_max(v):
        m = jnp.max(v, axis=-1, keepdims=True)
        return m, jnp.min(jnp.where(v == m, lane, big), axis=-1, keepdims=True)

    g_valid = (lane >= N_EXPERTS) & (lane < N_EXPERTS + N_GROUPS)
    g_m, g_lane = first_max(jnp.where(g_valid, lg, neg))
    g_w = 1.0 / jnp.sum(jnp.where(g_valid, jnp.exp(lg - g_m), 0.0), axis=-1, keepdims=True)
    g_idx = g_lane - N_EXPERTS
    e_valid = (lane < N_EXPERTS) & ((lane // EXPERTS_PER_GROUP) == g_idx)
    el = jnp.where(e_valid, lg, neg)
    v1, i1 = first_max(el)
    v2, i2 = first_max(jnp.where(lane == i1, neg, el))
    e21 = jnp.exp(v2 - v1)
    w1 = g_w / (1.0 + e21)
    w2 = g_w * e21 / (1.0 + e21)

    o1 = lane == i1
    o2 = lane == i2
    osum = jnp.where(o1 | o2, 1.0, 0.0)
    before = _dot(tri_ref[...], osum.astype(BF16)) + base_ref[0:1, :]
    r1 = jnp.sum(jnp.where(o1, before, 0.0), axis=-1, keepdims=True)
    r2 = jnp.sum(jnp.where(o2, before, 0.0), axis=-1, keepdims=True)
    base_ref[...] = base_ref[...] + jnp.sum(osum, axis=0, keepdims=True)
    cnt_ref[...] = base_ref[...]

    cols = (i1.astype(F32), i2.astype(F32), w1, w2, r1, r2)
    wide = jnp.zeros((tm, LANES), F32)
    for ci, col in enumerate(cols):
        wide = jnp.where(lane == ci, col, wide)
    r_ref[...] = wide[:, :r_ref.shape[1]]


def _mix(o_gla, o_ret, proj, x2, wa, wb, wo, nw, wr2, br, *, tm, cols):
    n, d = x2.shape
    tri = jnp.asarray(np.tril(np.ones((tm, tm), np.float32), -1), BF16)
    const = lambda shape: pl.BlockSpec(shape, lambda i: (0,) * len(shape))
    return pl.pallas_call(
        _mix_kernel,
        grid=(n // tm,),
        in_specs=[
            pl.BlockSpec((tm, o_gla.shape[1]), lambda i: (i, 0)),
            pl.BlockSpec((tm, o_ret.shape[1]), lambda i: (i, 0)),
            pl.BlockSpec((tm, d), lambda i: (i, cols[0])),
            pl.BlockSpec((tm, d), lambda i: (i, cols[1])),
            pl.BlockSpec((tm, d), lambda i: (i, 0)),
            const(wa.shape), const(wb.shape), const(wo.shape), const(nw.shape),
            const(wr2.shape), const(br.shape), const(tri.shape),
        ],
        out_specs=[
            pl.BlockSpec((tm, d), lambda i: (i, 0)),
            pl.BlockSpec((tm, d), lambda i: (i, 0)),
            pl.BlockSpec((tm, 8), lambda i: (i, 0)),
            pl.BlockSpec((8, LANES), lambda i: (0, 0)),
        ],
        out_shape=[
            jax.ShapeDtypeStruct((n, d), F32),
            jax.ShapeDtypeStruct((n, d), F32),
            jax.ShapeDtypeStruct((n, 8), F32),
            jax.ShapeDtypeStruct((8, LANES), F32),
        ],
        scratch_shapes=[pltpu.VMEM((8, LANES), F32)],
        compiler_params=_params(("arbitrary",)),
        name="mix_router",
    )(o_gla, o_ret, proj, proj, x2, wa, wb, wo, nw, wr2, br, tri)


def _dispatch_kernel(tail_ref, has_ref, d0_ref, d1_ref, t_hbm, xs_hbm, zero_ref, sem, zsem):
    step = pl.program_id(0)
    n_steps = pl.num_programs(0)
    tok_n = d0_ref.shape[0]
    blk = zero_ref.shape[0]

    def tail_copy(e):
        return pltpu.make_async_copy(zero_ref, xs_hbm.at[pl.ds(pl.multiple_of(tail_ref[e], blk), blk), :], zsem)

    @pl.when(step == 0)
    def _():
        zero_ref[...] = jnp.zeros_like(zero_ref)
        for e in range(tail_ref.shape[0]):
            @pl.when(has_ref[e] > 0)
            def _(e=e):
                tail_copy(e).start()
        for e in range(tail_ref.shape[0]):
            @pl.when(has_ref[e] > 0)
            def _(e=e):
                tail_copy(e).wait()

    def row_copy(tok, dst):
        return pltpu.make_async_copy(t_hbm.at[pl.ds(tok, 1), :], xs_hbm.at[pl.ds(dst, 1), :], sem)

    def wait_step_rows():
        pltpu.make_async_copy(t_hbm.at[pl.ds(0, 2 * tok_n), :], xs_hbm.at[pl.ds(0, 2 * tok_n), :], sem).wait()

    tok0 = step * tok_n

    def body(a, carry):
        row_copy(tok0 + a, d0_ref[a]).start()
        row_copy(tok0 + a, d1_ref[a]).start()
        return carry

    lax.fori_loop(0, tok_n, body, 0, unroll=8)

    @pl.when(step > 0)
    def _():
        wait_step_rows()

    @pl.when(step == n_steps - 1)
    def _():
        wait_step_rows()


def _dispatch(t, dest0, dest1, tail, has, *, n_slots, chunk):
    n, d = t.shape
    return pl.pallas_call(
        _dispatch_kernel,
        grid_spec=pltpu.PrefetchScalarGridSpec(
            num_scalar_prefetch=2,
            grid=(n // chunk,),
            in_specs=[
                pl.BlockSpec((chunk,), lambda i, *_: (i,), memory_space=pltpu.SMEM),
                pl.BlockSpec((chunk,), lambda i, *_: (i,), memory_space=pltpu.SMEM),
                pl.BlockSpec(memory_space=pl.ANY),
            ],
            out_specs=pl.BlockSpec(memory_space=pl.ANY),
            scratch_shapes=[
                pltpu.VMEM((EXPERT_ROWS, d), F32),
                pltpu.SemaphoreType.DMA,
                pltpu.SemaphoreType.DMA,
            ],
        ),
        out_shape=jax.ShapeDtypeStruct((n_slots, d), F32),
        compiler_params=pltpu.CompilerParams(dimension_semantics=("arbitrary",), has_side_effects=True,
                                             disable_bounds_checks=True),
        name="dispatch",
    )(tail, has, dest0, dest1, t)


def _expert_kernel(be_ref, nu_ref, xs_ref, wg_ref, wu_ref, wd_ref, ys_ref):
    b = pl.program_id(0)

    @pl.when(b < nu_ref[0])
    def _():
        x = xs_ref[...].astype(BF16)
        g = _dot(x, wg_ref[0])
        u = _dot(x, wu_ref[0])
        hid = (g * jax.nn.sigmoid(g) * u).astype(BF16)
        ys_ref[...] = _dot(hid, wd_ref[0])

    @pl.when(b >= nu_ref[0])
    def _():
        ys_ref[...] = jnp.zeros_like(ys_ref)


def _experts(xs, block_expert, n_used, wg, wu, wd):
    n_slots, d = xs.shape
    blk = EXPERT_ROWS
    hid = wg.shape[2]
    return pl.pallas_call(
        _expert_kernel,
        grid_spec=pltpu.PrefetchScalarGridSpec(
            num_scalar_prefetch=2,
            grid=(n_slots // blk,),
            in_specs=[
                pl.BlockSpec((blk, d), lambda b, be, nu: (jnp.minimum(b, nu[0] - 1), 0)),
                pl.BlockSpec((1, d, hid), lambda b, be, nu: (be[b], 0, 0)),
                pl.BlockSpec((1, d, hid), lambda b, be, nu: (be[b], 0, 0)),
                pl.BlockSpec((1, hid, d), lambda b, be, nu: (be[b], 0, 0)),
            ],
            out_specs=pl.BlockSpec((blk, d), lambda b, be, nu: (b, 0)),
        ),
        out_shape=jax.ShapeDtypeStruct((n_slots, d), F32),
        compiler_params=_params(("arbitrary",)),
        name="experts",
    )(block_expert, n_used, xs, wg, wu, wd)


def _combine_kernel(c0_ref, c1_ref, n0_ref, n1_ref, h_ref, r_ref, nw_ref, ys_hbm, o_ref, buf, sem):
    i = pl.program_id(0)
    n_steps = pl.num_programs(0)
    tm = h_ref.shape[0]

    def row_copy(dst_row, src_row, slot):
        return pltpu.make_async_copy(ys_hbm.at[pl.ds(src_row, 1), :], buf.at[slot, pl.ds(dst_row, 1), :], sem.at[slot])

    def issue(d0, d1, slot):
        def body(a, carry):
            row_copy(a, d0[a], slot).start()
            row_copy(tm + a, d1[a], slot).start()
            return carry
        lax.fori_loop(0, tm, body, 0, unroll=8)

    @pl.when(i == 0)
    def _():
        issue(c0_ref, c1_ref, 0)

    @pl.when(i + 1 < n_steps)
    def _():
        issue(n0_ref, n1_ref, (i + 1) % 2)

    slot = i % 2
    pltpu.make_async_copy(ys_hbm.at[pl.ds(0, 2 * tm), :], buf.at[slot], sem.at[slot]).wait()

    r = r_ref[...]
    y = buf[slot, 0:tm, :] * r[:, 2:3] + buf[slot, tm:2 * tm, :] * r[:, 3:4]
    h = h_ref[...] + y
    ms = jnp.mean(h * h, axis=-1, keepdims=True)
    o_ref[...] = h * lax.rsqrt(ms + EPS) * nw_ref[...]


def _combine(dest0, dest1, h1, r, nw, ys, *, tm):
    n, d = h1.shape
    n_steps = n // tm
    cur = pl.BlockSpec((tm,), lambda i: (i,), memory_space=pltpu.SMEM)
    nxt = pl.BlockSpec((tm,), lambda i: (jnp.minimum(i + 1, n_steps - 1),), memory_space=pltpu.SMEM)
    return pl.pallas_call(
        _combine_kernel,
        grid=(n_steps,),
        in_specs=[
            cur, cur, nxt, nxt,
            pl.BlockSpec((tm, d), lambda i: (i, 0)),
            pl.BlockSpec((tm, r.shape[1]), lambda i: (i, 0)),
            pl.BlockSpec((1, d), lambda i: (0, 0)),
            pl.BlockSpec(memory_space=pl.ANY),
        ],
        out_specs=pl.BlockSpec((tm, d), lambda i: (i, 0)),
        out_shape=jax.ShapeDtypeStruct((n, d), F32),
        scratch_shapes=[pltpu.VMEM((2, 2 * tm, d), F32), pltpu.SemaphoreType.DMA((2,))],
        compiler_params=pltpu.CompilerParams(dimension_semantics=("arbitrary",), vmem_limit_bytes=VMEM_LIMIT,
                                             disable_bounds_checks=True),
        name="combine",
    )(dest0, dest1, dest0, dest1, h1, r, nw, ys)


def _layer(h, positions, norm_mix_w, w_in, gk_up, gk_bias, gla_norm_w, w_br_gla, ret_norm_w, w_br_ret, w_out,
           norm_ffn_w, rg_w, rg_b, re_w, re_b, wg, wu, wd, norm_final_w, *, tm_in, tn_in, gla_rows, tm_mix,
           tm_comb, disp_chunk):
    batch, seq, d = h.shape
    n = batch * seq
    x2 = h.reshape(n, d)
    gqk, gv = GLA_HEADS * GLA_DK, GLA_HEADS * GLA_DV
    rqk, rv = RET_HEADS * RET_DK, RET_HEADS * RET_DV

    sizes = (gqk, gqk, gv, gv, GLA_GATE_RANK, rqk, rqk, rv, rv, d, d)
    offs = np.concatenate([[0], np.cumsum(sizes)])
    seg = lambda i: w_in[:, offs[i]:offs[i + 1]]
    perm = np.concatenate([np.concatenate([np.arange(0, RET_DK, 2), np.arange(1, RET_DK, 2)]) + hh * RET_DK
                           for hh in range(RET_HEADS)])
    w_main = jnp.concatenate([seg(7), seg(8), seg(2), seg(3), seg(5)[:, perm], seg(6)[:, perm], seg(9), seg(10),
                              seg(0), seg(1)], axis=1).astype(BF16)
    assert rv % gv == 0 and gv == rqk == d and rv == 2 * d and gqk * 2 == d
    ret_cols = (4 + 2, 4 + 3, 0, 1)
    gla_cols = (2 * (4 + 6), 2 * (4 + 6) + 1, 4, 5)
    mix_cols = (4 + 4, 4 + 5)
    w_gd = jnp.pad(seg(4), ((0, 0), (0, LANES - GLA_GATE_RANK))).astype(BF16)

    proj, gd = _inproj(x2, norm_mix_w.reshape(1, d), w_main, w_gd, tm=tm_in, tn=tn_in)

    up = jnp.pad(gk_up, ((0, LANES - GLA_GATE_RANK), (0, 0)))
    up_hi = up.astype(BF16)
    up2 = jnp.stack([up_hi, (up - up_hi.astype(F32)).astype(BF16)])
    o_gla = _gla(proj, gd, up2, gk_bias.reshape(1, gqk), gla_norm_w.reshape(1, GLA_DV),
                 batch=batch, seq=seq, rows=gla_rows, cols=gla_cols)

    theta = (1.0 / (ROPE_BASE ** jnp.linspace(0.0, 1.0, RET_DK // 2, dtype=F32))).reshape(1, RET_DK // 2)
    o_ret = _retention(proj, positions.reshape(n, 1), theta, ret_norm_w.reshape(1, rv),
                       batch=batch, seq=seq, cols=ret_cols)

    wr = jnp.concatenate([re_w.transpose(1, 0, 2).reshape(d, N_EXPERTS), rg_w], axis=1)
    wr = jnp.pad(wr, ((0, 0), (0, LANES - N_EXPERTS - N_GROUPS)))
    wr_hi = wr.astype(BF16)
    wr2 = jnp.stack([wr_hi, (wr - wr_hi.astype(F32)).astype(BF16)])
    br = jnp.pad(jnp.concatenate([re_b.reshape(-1), rg_b]), (0, LANES - N_EXPERTS - N_GROUPS)).reshape(1, LANES)
    h1, t, r, cnt = _mix(o_gla, o_ret, proj, x2, w_br_gla.astype(BF16), w_br_ret.astype(BF16), w_out.astype(BF16),
                         norm_ffn_w.reshape(1, d), wr2, br, tm=tm_mix, cols=mix_cols)

    blk = EXPERT_ROWS
    n_assign = 2 * n
    n_slots = -(-(n_assign + N_EXPERTS * (blk - 1)) // blk) * blk
    counts = cnt[0, :N_EXPERTS].astype(jnp.int32)
    padded = ((counts + blk - 1) // blk) * blk
    pad_end = jnp.cumsum(padded)
    pad_start = pad_end - padded
    eid = r[:, 0:2].astype(jnp.int32)
    dest = pad_start[eid] + r[:, 4:6].astype(jnp.int32)
    dest0, dest1 = dest[:, 0], dest[:, 1]
    block_start = jnp.arange(n_slots // blk, dtype=jnp.int32) * blk
    block_expert = jnp.minimum(jnp.sum((pad_end[None, :] <= block_start[:, None]).astype(jnp.int32), axis=1),
                               N_EXPERTS - 1)
    n_used = (pad_end[-1:] // blk).astype(jnp.int32)
    spare = pad_end[-1] + jnp.arange(N_EXPERTS, dtype=jnp.int32) * blk
    tail = jnp.concatenate([jnp.maximum(pad_end - blk, 0), jnp.minimum(spare, n_slots - blk)]).astype(jnp.int32)
    has = jnp.concatenate([counts, (spare < n_slots).astype(jnp.int32)])

    xs = _dispatch(t, dest0, dest1, tail, has, n_slots=n_slots, chunk=disp_chunk)
    ys = _experts(xs, block_expert, n_used, wg.astype(BF16), wu.astype(BF16), wd.astype(BF16))
    out = _combine(dest0, dest1, h1, r, norm_final_w.reshape(1, d), ys, tm=tm_comb)
    return out.reshape(batch, seq, d)


def kernel(x, positions, norm_mix_w, w_in, gla_gk_up, gla_gk_bias, gla_norm_w, w_branch_gla, ret_norm_w,
           w_branch_ret, w_out, norm_ffn_w, router_group_w, router_group_b, router_expert_w, router_expert_b,
           expert_w_gate, expert_w_up, expert_w_down, norm_final_w):
    assert norm_mix_w.shape[0] == 1, "single-layer block"
    return _layer(x, positions, norm_mix_w[0], w_in[0], gla_gk_up[0], gla_gk_bias[0], gla_norm_w[0], w_branch_gla[0],
                  ret_norm_w[0], w_branch_ret[0], w_out[0], norm_ffn_w[0], router_group_w[0], router_group_b[0],
                  router_expert_w[0], router_expert_b[0], expert_w_gate[0], expert_w_up[0], expert_w_down[0],
                  norm_final_w, tm_in=1024, tn_in=1024, gla_rows=256, tm_mix=512, tm_comb=256, disp_chunk=1024)
```

```python
import functools
import math

import jax
import jax.numpy as jnp
import numpy as np
from jax import lax
from jax.experimental import pallas as pl
from jax.experimental.pallas import tpu as pltpu

F32 = jnp.float32
BF16 = jnp.bfloat16

EPS = 1e-6
GLA_HEADS = 4
GLA_DK = 128
GLA_DV = 256
GLA_GATE_RANK = 16
GLA_GATE_TEMP = 16.0
RET_HEADS = 4
RET_DK = 256
RET_DV = 512
ROPE_BASE = 10000.0
N_GROUPS = 4
EXPERTS_PER_GROUP = 8
N_EXPERTS = N_GROUPS * EXPERTS_PER_GROUP
EXPERT_HIDDEN = 512

LANES = 128
GLA_CHUNK = 64
GLA_DIAG = 8
RET_CHUNK = 256
EXPERT_ROWS = 256
VMEM_LIMIT = 48 * 1024 * 1024


def _dot(a, b):
    return jnp.dot(a, b, preferred_element_type=F32)


def _dot_nt(a, b):
    return lax.dot_general(a, b, (((1,), (1,)), ((), ())), preferred_element_type=F32)


def _split2(a):
    hi = a.astype(BF16)
    lo = (a - hi.astype(F32)).astype(BF16)
    return hi, lo


def _params(sem, vmem=VMEM_LIMIT):
    return pltpu.CompilerParams(dimension_semantics=sem, vmem_limit_bytes=vmem)


def _inproj_kernel(x_ref, nw_ref, w_ref, wgd_ref, proj_ref, gd_ref, u_scr):
    @pl.when(pl.program_id(1) == 0)
    def _():
        x = x_ref[...]
        ms = jnp.mean(x * x, axis=-1, keepdims=True)
        u = (x * lax.rsqrt(ms + EPS) * nw_ref[...]).astype(BF16)
        u_scr[...] = u
        gd_ref[...] = _dot(u, wgd_ref[...])

    proj_ref[...] = _dot(u_scr[...], w_ref[...]).astype(BF16)


def _inproj(x2, nw, w_main, w_gd, *, tm, tn):
    n, d = x2.shape
    p = w_main.shape[1]
    return pl.pallas_call(
        _inproj_kernel,
        grid=(n // tm, p // tn),
        in_specs=[
            pl.BlockSpec((tm, d), lambda i, j: (i, 0)),
            pl.BlockSpec((1, d), lambda i, j: (0, 0)),
            pl.BlockSpec((d, tn), lambda i, j: (0, j)),
            pl.BlockSpec((d, LANES), lambda i, j: (0, 0)),
        ],
        out_specs=[
            pl.BlockSpec((tm, tn), lambda i, j: (i, j)),
            pl.BlockSpec((tm, LANES), lambda i, j: (i, 0)),
        ],
        out_shape=[
            jax.ShapeDtypeStruct((n, p), BF16),
            jax.ShapeDtypeStruct((n, LANES), F32),
        ],
        scratch_shapes=[pltpu.VMEM((tm, d), BF16)],
        compiler_params=_params(("arbitrary", "arbitrary")),
        name="inproj",
    )(x2, nw, w_main, w_gd)


def _gla_tables(c):
    levels = []
    s = c // 2
    while s >= GLA_DIAG:
        levels.append(s)
        s //= 2
    i = np.arange(c)[:, None]
    t = np.arange(c)[None, :]
    mats = [t <= i, t > i]
    masks = []
    for s in levels:
        bs = (i // s) * s
        mats.append((t > bs) & (t <= i))
        mats.append((t > i) & (t <= np.minimum(bs + s, c - 1)))
        masks.append(((i // (2 * s)) == (t // (2 * s))) & (((i // s) % 2) == 1) & (((t // s) % 2) == 0))
    for d in range(1, GLA_DIAG):
        mats.append((t > i - d) & (t <= i))
    for d in range(GLA_DIAG):
        masks.append((t == i - d) & ((i % GLA_DIAG) >= d))
    g = np.concatenate(mats, 0).astype(np.float32)
    m = np.stack(masks, 0).astype(np.float32)
    return g, m, tuple(levels)


def _gla_kernel(q_ref, k_ref, v_ref, gg_ref, gd_ref, up_ref, bias_ref, nw_ref, g_ref, m_ref,
                o_ref, st_ref, *, c, nlev):
    h_n, dk, dv = GLA_HEADS, GLA_DK, GLA_DV
    rows = q_ref.shape[0]

    @pl.when(pl.program_id(1) == 0)
    def _():
        st_ref[...] = jnp.zeros_like(st_ref)

    g_tab = g_ref[...]
    up_hi = up_ref[0]
    up_lo = up_ref[1]
    q_scale = dk ** -0.5
    band0 = 2 + 2 * nlev

    def chunk(ci, carry):
        r0 = pl.multiple_of(ci * c, c)
        rs = pl.ds(r0, c)
        gd_hi, gd_lo = _split2(gd_ref[rs, :])
        xg = _dot(gd_hi, up_hi) + _dot(gd_hi, up_lo) + _dot(gd_lo, up_hi) + bias_ref[...]
        la = (jnp.minimum(xg, 0.0) - jnp.log1p(jnp.exp(-jnp.abs(xg)))) * (1.0 / GLA_GATE_TEMP)
        la_hi, la_lo = _split2(la)
        ex = jnp.exp(_dot(g_tab, la_hi) + _dot(g_tab, la_lo))

        for h in range(h_n):
            ks = slice(h * dk, (h + 1) * dk)
            vs = slice(h * dv, (h + 1) * dv)

            def tab(m, ks=ks):
                return ex[m * c:(m + 1) * c, ks]

            qh = q_ref[rs, ks].astype(F32) * q_scale
            kh = k_ref[rs, ks].astype(F32)
            vh = v_ref[rs, vs]
            st = st_ref[h]
            eb = tab(0)
            o = _dot_nt((qh * eb).astype(BF16), st.astype(BF16))
            a = m_ref[nlev] * jnp.sum(qh * kh, axis=-1, keepdims=True)
            for li in range(nlev):
                qs = (qh * tab(2 + 2 * li)).astype(BF16)
                kk = (kh * tab(3 + 2 * li)).astype(BF16)
                a = a + m_ref[li] * _dot_nt(qs, kk)
            for d in range(1, GLA_DIAG):
                kr = pltpu.roll(kh, d, 0)
                dd = jnp.sum(qh * kr * tab(band0 + d - 1), axis=-1, keepdims=True)
                a = a + m_ref[nlev + d] * dd
            o = o + _dot(a.astype(BF16), vh)
            e_last = eb[c - 1:c, :]
            k_st = (kh * tab(1)).astype(BF16)
            v_t = vh.astype(F32).T.astype(BF16)
            st_ref[h] = st * e_last + _dot(v_t, k_st)

            ms = jnp.mean(o * o, axis=-1, keepdims=True)
            y = o * lax.rsqrt(ms + EPS) * nw_ref[...]
            gg = gg_ref[rs, vs].astype(F32)
            o_ref[rs, vs] = (y * (gg * jax.nn.sigmoid(gg))).astype(BF16)
        return carry

    lax.fori_loop(0, rows // c, chunk, 0, unroll=2)


def _gla(proj, gd, up2, bias, nw, *, batch, seq, rows, cols):
    n = proj.shape[0]
    c = GLA_CHUNK
    g_np, m_np, levels = _gla_tables(c)
    g_tab = jnp.asarray(g_np, BF16)
    m_tab = jnp.asarray(m_np, F32)
    qk_w = GLA_HEADS * GLA_DK
    v_w = GLA_HEADS * GLA_DV
    spb = seq // rows
    row = lambda b, s: b * spb + s
    kern = functools.partial(_gla_kernel, c=c, nlev=len(levels))
    return pl.pallas_call(
        kern,
        grid=(batch, spb),
        in_specs=[
            pl.BlockSpec((rows, qk_w), lambda b, s: (row(b, s), cols[0])),
            pl.BlockSpec((rows, qk_w), lambda b, s: (row(b, s), cols[1])),
            pl.BlockSpec((rows, v_w), lambda b, s: (row(b, s), cols[2])),
            pl.BlockSpec((rows, v_w), lambda b, s: (row(b, s), cols[3])),
            pl.BlockSpec((rows, LANES), lambda b, s: (row(b, s), 0)),
            pl.BlockSpec(up2.shape, lambda b, s: (0, 0, 0)),
            pl.BlockSpec(bias.shape, lambda b, s: (0, 0)),
            pl.BlockSpec(nw.shape, lambda b, s: (0, 0)),
            pl.BlockSpec(g_tab.shape, lambda b, s: (0, 0)),
            pl.BlockSpec(m_tab.shape, lambda b, s: (0, 0, 0)),
        ],
        out_specs=pl.BlockSpec((rows, v_w), lambda b, s: (row(b, s), 0)),
        out_shape=jax.ShapeDtypeStruct((n, v_w), BF16),
        scratch_shapes=[pltpu.VMEM((GLA_HEADS, GLA_DV, GLA_DK), F32)],
        compiler_params=_params(("arbitrary", "arbitrary")),
        name="gla",
    )(proj, proj, proj, proj, gd, up2, bias, nw, g_tab, m_tab)


def _ret_log_gamma(h):
    return math.log(1.0 - 2.0 ** (-5.0 - h))


def _ret_kernel(pos_ref, th_ref, q_ref, k_ref, v_ref, rg_ref, nw_ref, o_ref, s_ref, dm_ref):
    h_n, dk, dv = RET_HEADS, RET_DK, RET_DV
    c = q_ref.shape[0]
    half = dk // 2

    @pl.when(pl.program_id(1) == 0)
    def _():
        s_ref[...] = jnp.zeros_like(s_ref)

    @pl.when((pl.program_id(0) == 0) & (pl.program_id(1) == 0))
    def _():
        rel = (lax.broadcasted_iota(jnp.int32, (c, c), 0) - lax.broadcasted_iota(jnp.int32, (c, c), 1)).astype(F32)
        for h in range(h_n):
            dm_ref[h] = jnp.where(rel >= 0, jnp.exp(_ret_log_gamma(h) * jnp.maximum(rel, 0.0)), 0.0)

    ang = pos_ref[...].astype(F32) * th_ref[...]
    cs = jnp.cos(ang)
    sn = jnp.sin(ang)
    idx = lax.broadcasted_iota(jnp.int32, (c, 1), 0).astype(F32)
    k_scale = dk ** -0.5

    def rot(ref, h):
        a = ref[:, h * dk:h * dk + half].astype(F32)
        b = ref[:, h * dk + half:(h + 1) * dk].astype(F32)
        return jnp.concatenate([a * cs - b * sn, b * cs + a * sn], axis=-1)

    for h in range(h_n):
        lg = _ret_log_gamma(h)
        vs = slice(h * dv, (h + 1) * dv)
        qr = rot(q_ref, h)
        kr = rot(k_ref, h) * k_scale
        vh = v_ref[:, vs]
        scores = _dot_nt(qr.astype(BF16), kr.astype(BF16)) * dm_ref[h]
        q_dec = jnp.exp(lg * (idx + 1.0))
        k_dec = jnp.exp(lg * (c - 1.0 - idx))
        s = s_ref[h]
        o = _dot(scores.astype(BF16), vh) + _dot((qr * q_dec).astype(BF16), s.astype(BF16))
        k_t = (kr * k_dec).T.astype(BF16)
        s_ref[h] = math.exp(lg * c) * s + _dot(k_t, vh)

        mu = jnp.mean(o, axis=-1, keepdims=True)
        oc = o - mu
        var = jnp.mean(oc * oc, axis=-1, keepdims=True)
        y = oc * lax.rsqrt(var + EPS) * nw_ref[:, vs]
        rg = rg_ref[:, vs].astype(F32)
        o_ref[:, vs] = (y * (rg * jax.nn.sigmoid(rg))).astype(BF16)


def _retention(proj, pos, theta, nw, *, batch, seq, cols):
    n = proj.shape[0]
    c = RET_CHUNK
    qk_w = RET_HEADS * RET_DK
    v_w = RET_HEADS * RET_DV
    spb = seq // c
    row = lambda b, s: b * spb + s
    return pl.pallas_call(
        _ret_kernel,
        grid=(batch, spb),
        in_specs=[
            pl.BlockSpec((c, 1), lambda b, s: (row(b, s), 0)),
            pl.BlockSpec(theta.shape, lambda b, s: (0, 0)),
            pl.BlockSpec((c, qk_w), lambda b, s: (row(b, s), cols[0])),
            pl.BlockSpec((c, qk_w), lambda b, s: (row(b, s), cols[1])),
            pl.BlockSpec((c, v_w), lambda b, s: (row(b, s), cols[2])),
            pl.BlockSpec((c, v_w), lambda b, s: (row(b, s), cols[3])),
            pl.BlockSpec(nw.shape, lambda b, s: (0, 0)),
        ],
        out_specs=pl.BlockSpec((c, v_w), lambda b, s: (row(b, s), 0)),
        out_shape=jax.ShapeDtypeStruct((n, v_w), BF16),
        scratch_shapes=[
            pltpu.VMEM((RET_HEADS, RET_DK, RET_DV), F32),
            pltpu.VMEM((RET_HEADS, c, c), F32),
        ],
        compiler_params=_params(("arbitrary", "arbitrary")),
        name="retention",
    )(pos, theta, proj, proj, proj, proj, nw)


def _mix_kernel(og_ref, or_ref, ma_ref, mb_ref, x_ref, wa_ref, wb_ref, wo_ref, nw_ref, wr_ref, br_ref,
                tri_ref, h_ref, t_ref, r_ref, cnt_ref, base_ref):
    tm = x_ref.shape[0]

    @pl.when(pl.program_id(0) == 0)
    def _():
        base_ref[...] = jnp.zeros_like(base_ref)

    ya = _dot(og_ref[...], wa_ref[...])
    yb = _dot(or_ref[...], wb_ref[...])
    merged = (jax.nn.sigmoid(ma_ref[...].astype(F32)) * ya + jax.nn.sigmoid(mb_ref[...].astype(F32)) * yb)
    h = x_ref[...] + _dot(merged.astype(BF16), wo_ref[...])
    h_ref[...] = h
    ms = jnp.mean(h * h, axis=-1, keepdims=True)
    t = h * lax.rsqrt(ms + EPS) * nw_ref[...]
    t_ref[...] = t

    t_hi, t_lo = _split2(t)
    lg = _dot(t_hi, wr_ref[0]) + _dot(t_hi, wr_ref[1]) + _dot(t_lo, wr_ref[0]) + br_ref[...]
    lane = lax.broadcasted_iota(jnp.int32, (tm, LANES), 1)
    neg = jnp.float32(-1e30)
    big = jnp.int32(1 << 20)

    def first_max(v):
        m = jnp.max(v, axis=-1, keepdims=True)
        return m, jnp.min(jnp.where(v == m, lane, big), axis=-1, keepdims=True)

    g_valid = (lane >= N_EXPERTS) & (lane < N_EXPERTS + N_GROUPS)
    g_m, g_lane = first_max(jnp.where(g_valid, lg, neg))
    g_w = 1.0 / jnp.sum(jnp.where(g_valid, jnp.exp(lg - g_m), 0.0), axis=-1, keepdims=True)
    g_idx = g_lane - N_EXPERTS
    e_valid = (lane < N_EXPERTS) & ((lane // EXPERTS_PER_GROUP) == g_idx)
    el = jnp.where(e_valid, lg, neg)
    v1, i1 = first_max(el)
    v2, i2 = first_max(jnp.where(lane == i1, neg, el))
    e21 = jnp.exp(v2 - v1)
    w1 = g_w / (1.0 + e21)
    w2 = g_w * e21 / (1.0 + e21)

    o1 = lane == i1
    o2 = lane == i2
    osum = jnp.where(o1 | o2, 1.0, 0.0)
    before = _dot(tri_ref[...], osum.astype(BF16)) + base_ref[0:1, :]
    r1 = jnp.sum(jnp.where(o1, before, 0.0), axis=-1, keepdims=True)
    r2 = jnp.sum(jnp.where(o2, before, 0.0), axis=-1, keepdims=True)
    base_ref[...] = base_ref[...] + jnp.sum(osum, axis=0, keepdims=True)
    cnt_ref[...] = base_ref[...]

    cols = (i1.astype(F32), i2.astype(F32), w1, w2, r1, r2)
    wide = jnp.zeros((tm, LANES), F32)
    for ci, col in enumerate(cols):
        wide = jnp.where(lane == ci, col, wide)
    r_ref[...] = wide[:, :r_ref.shape[1]]


def _mix(o_gla, o_ret, proj, x2, wa, wb, wo, nw, wr2, br, *, tm, cols):
    n, d = x2.shape
    tri = jnp.asarray(np.tril(np.ones((tm, tm), np.float32), -1), BF16)
    const = lambda shape: pl.BlockSpec(shape, lambda i: (0,) * len(shape))
    return pl.pallas_call(
        _mix_kernel,
        grid=(n // tm,),
        in_specs=[
            pl.BlockSpec((tm, o_gla.shape[1]), lambda i: (i, 0)),
            pl.BlockSpec((tm, o_ret.shape[1]), lambda i: (i, 0)),
            pl.BlockSpec((tm, d), lambda i: (i, cols[0])),
            pl.BlockSpec((tm, d), lambda i: (i, cols[1])),
            pl.BlockSpec((tm, d), lambda i: (i, 0)),
            const(wa.shape), const(wb.shape), const(wo.shape), const(nw.shape),
            const(wr2.shape), const(br.shape), const(tri.shape),
        ],
        out_specs=[
            pl.BlockSpec((tm, d), lambda i: (i, 0)),
            pl.BlockSpec((tm, d), lambda i: (i, 0)),
            pl.BlockSpec((tm, 8), lambda i: (i, 0)),
            pl.BlockSpec((8, LANES), lambda i: (0, 0)),
        ],
        out_shape=[
            jax.ShapeDtypeStruct((n, d), F32),
            jax.ShapeDtypeStruct((n, d), F32),
            jax.ShapeDtypeStruct((n, 8), F32),
            jax.ShapeDtypeStruct((8, LANES), F32),
        ],
        scratch_shapes=[pltpu.VMEM((8, LANES), F32)],
        compiler_params=_params(("arbitrary",)),
        name="mix_router",
    )(o_gla, o_ret, proj, proj, x2, wa, wb, wo, nw, wr2, br, tri)


def _dispatch_kernel(tail_ref, has_ref, d0_ref, d1_ref, t_ref, xs_hbm, zero_ref, sem, zsem):
    step = pl.program_id(0)
    tok_n = d0_ref.shape[0]
    blk = zero_ref.shape[0]

    def tail_copy(e):
        return pltpu.make_async_copy(zero_ref, xs_hbm.at[pl.ds(pl.multiple_of(tail_ref[e], blk), blk), :], zsem)

    @pl.when(step == 0)
    def _():
        zero_ref[...] = jnp.zeros_like(zero_ref)
        for e in range(tail_ref.shape[0]):
            @pl.when(has_ref[e] > 0)
            def _(e=e):
                tail_copy(e).start()
        for e in range(tail_ref.shape[0]):
            @pl.when(has_ref[e] > 0)
            def _(e=e):
                tail_copy(e).wait()

    def row_copy(row, dst):
        return pltpu.make_async_copy(t_ref.at[pl.ds(row, 1), :], xs_hbm.at[pl.ds(dst, 1), :], sem)

    def body(a, carry):
        row_copy(a, d0_ref[a]).start()
        row_copy(a, d1_ref[a]).start()
        return carry

    lax.fori_loop(0, tok_n, body, 0, unroll=8)

    for _ in range(2):
        pltpu.make_async_copy(t_ref, xs_hbm.at[pl.ds(0, tok_n), :], sem).wait()


def _dispatch(t, dest0, dest1, tail, has, *, n_slots, chunk):
    n, d = t.shape
    return pl.pallas_call(
        _dispatch_kernel,
        grid_spec=pltpu.PrefetchScalarGridSpec(
            num_scalar_prefetch=2,
            grid=(n // chunk,),
            in_specs=[
                pl.BlockSpec((chunk,), lambda i, *_: (i,), memory_space=pltpu.SMEM),
                pl.BlockSpec((chunk,), lambda i, *_: (i,), memory_space=pltpu.SMEM),
                pl.BlockSpec((chunk, d), lambda i, *_: (i, 0)),
            ],
            out_specs=pl.BlockSpec(memory_space=pl.ANY),
            scratch_shapes=[
                pltpu.VMEM((EXPERT_ROWS, d), F32),
                pltpu.SemaphoreType.DMA,
                pltpu.SemaphoreType.DMA,
            ],
        ),
        out_shape=jax.ShapeDtypeStruct((n_slots, d), F32),
        compiler_params=pltpu.CompilerParams(dimension_semantics=("arbitrary",), has_side_effects=True,
                                             vmem_limit_bytes=VMEM_LIMIT, disable_bounds_checks=True),
        name="dispatch",
    )(tail, has, dest0, dest1, t)


def _expert_kernel(be_ref, nu_ref, xs_ref, wg_ref, wu_ref, wd_ref, ys_ref, wg_s, wu_s, wd_s):
    b = pl.program_id(0)

    @pl.when((b == 0) | (be_ref[b] != be_ref[jnp.maximum(b - 1, 0)]))
    def _():
        wg_s[...] = wg_ref[0].astype(BF16)
        wu_s[...] = wu_ref[0].astype(BF16)
        wd_s[...] = wd_ref[0].astype(BF16)

    @pl.when(b < nu_ref[0])
    def _():
        x = xs_ref[...].astype(BF16)
        g = _dot(x, wg_s[...])
        u = _dot(x, wu_s[...])
        hid = (g * jax.nn.sigmoid(g) * u).astype(BF16)
        ys_ref[...] = _dot(hid, wd_s[...])

    @pl.when(b >= nu_ref[0])
    def _():
        ys_ref[...] = jnp.zeros_like(ys_ref)


def _experts(xs, block_expert, n_used, wg, wu, wd):
    n_slots, d = xs.shape
    blk = EXPERT_ROWS
    hid = wg.shape[2]
    return pl.pallas_call(
        _expert_kernel,
        grid_spec=pltpu.PrefetchScalarGridSpec(
            num_scalar_prefetch=2,
            grid=(n_slots // blk,),
            in_specs=[
                pl.BlockSpec((blk, d), lambda b, be, nu: (jnp.minimum(b, nu[0] - 1), 0)),
                pl.BlockSpec((1, d, hid), lambda b, be, nu: (be[b], 0, 0)),
                pl.BlockSpec((1, d, hid), lambda b, be, nu: (be[b], 0, 0)),
                pl.BlockSpec((1, hid, d), lambda b, be, nu: (be[b], 0, 0)),
            ],
            out_specs=pl.BlockSpec((blk, d), lambda b, be, nu: (b, 0)),
            scratch_shapes=[pltpu.VMEM((d, hid), BF16), pltpu.VMEM((d, hid), BF16), pltpu.VMEM((hid, d), BF16)],
        ),
        out_shape=jax.ShapeDtypeStruct((n_slots, d), F32),
        compiler_params=_params(("arbitrary",)),
        name="experts",
    )(block_expert, n_used, xs, wg, wu, wd)


def _combine_kernel(c0_ref, c1_ref, n0_ref, n1_ref, h_ref, r_ref, nw_ref, ys_hbm, o_ref, buf, sem):
    i = pl.program_id(0)
    n_steps = pl.num_programs(0)
    tm = h_ref.shape[0]

    def row_copy(dst_row, src_row, slot):
        return pltpu.make_async_copy(ys_hbm.at[pl.ds(src_row, 1), :], buf.at[slot, pl.ds(dst_row, 1), :], sem.at[slot])

    def issue(d0, d1, slot):
        def body(a, carry):
            row_copy(a, d0[a], slot).start()
            row_copy(tm + a, d1[a], slot).start()
            return carry
        lax.fori_loop(0, tm, body, 0, unroll=8)

    @pl.when(i == 0)
    def _():
        issue(c0_ref, c1_ref, 0)

    @pl.when(i + 1 < n_steps)
    def _():
        issue(n0_ref, n1_ref, (i + 1) % 2)

    slot = i % 2
    pltpu.make_async_copy(ys_hbm.at[pl.ds(0, 2 * tm), :], buf.at[slot], sem.at[slot]).wait()

    r = r_ref[...]
    y = buf[slot, 0:tm, :] * r[:, 2:3] + buf[slot, tm:2 * tm, :] * r[:, 3:4]
    h = h_ref[...] + y
    ms = jnp.mean(h * h, axis=-1, keepdims=True)
    o_ref[...] = h * lax.rsqrt(ms + EPS) * nw_ref[...]


def _combine(dest0, dest1, h1, r, nw, ys, *, tm):
    n, d = h1.shape
    n_steps = n // tm
    cur = pl.BlockSpec((tm,), lambda i: (i,), memory_space=pltpu.SMEM)
    nxt = pl.BlockSpec((tm,), lambda i: (jnp.minimum(i + 1, n_steps - 1),), memory_space=pltpu.SMEM)
    return pl.pallas_call(
        _combine_kernel,
        grid=(n_steps,),
        in_specs=[
            cur, cur, nxt, nxt,
            pl.BlockSpec((tm, d), lambda i: (i, 0)),
            pl.BlockSpec((tm, r.shape[1]), lambda i: (i, 0)),
            pl.BlockSpec((1, d), lambda i: (0, 0)),
            pl.BlockSpec(memory_space=pl.ANY),
        ],
        out_specs=pl.BlockSpec((tm, d), lambda i: (i, 0)),
        out_shape=jax.ShapeDtypeStruct((n, d), F32),
        scratch_shapes=[pltpu.VMEM((2, 2 * tm, d), F32), pltpu.SemaphoreType.DMA((2,))],
        compiler_params=pltpu.CompilerParams(dimension_semantics=("arbitrary",), vmem_limit_bytes=VMEM_LIMIT,
                                             disable_bounds_checks=True),
        name="combine",
    )(dest0, dest1, dest0, dest1, h1, r, nw, ys)


def _layer(h, positions, norm_mix_w, w_in, gk_up, gk_bias, gla_norm_w, w_br_gla, ret_norm_w, w_br_ret, w_out,
           norm_ffn_w, rg_w, rg_b, re_w, re_b, wg, wu, wd, norm_final_w, *, tm_in, tn_in, gla_rows, tm_mix,
           tm_comb, disp_chunk):
    batch, seq, d = h.shape
    n = batch * seq
    x2 = h.reshape(n, d)
    gqk, gv = GLA_HEADS * GLA_DK, GLA_HEADS * GLA_DV
    rqk, rv = RET_HEADS * RET_DK, RET_HEADS * RET_DV

    sizes = (gqk, gqk, gv, gv, GLA_GATE_RANK, rqk, rqk, rv, rv, d, d)
    offs = np.concatenate([[0], np.cumsum(sizes)])
    seg = lambda i: w_in[:, offs[i]:offs[i + 1]]
    perm = np.concatenate([np.concatenate([np.arange(0, RET_DK, 2), np.arange(1, RET_DK, 2)]) + hh * RET_DK
                           for hh in range(RET_HEADS)])
    w_main = jnp.concatenate([seg(7), seg(8), seg(2), seg(3), seg(5)[:, perm], seg(6)[:, perm], seg(9), seg(10),
                              seg(0), seg(1)], axis=1).astype(BF16)
    assert rv % gv == 0 and gv == rqk == d and rv == 2 * d and gqk * 2 == d
    ret_cols = (4 + 2, 4 + 3, 0, 1)
    gla_cols = (2 * (4 + 6), 2 * (4 + 6) + 1, 4, 5)
    mix_cols = (4 + 4, 4 + 5)
    w_gd = jnp.pad(seg(4), ((0, 0), (0, LANES - GLA_GATE_RANK))).astype(BF16)

    proj, gd = _inproj(x2, norm_mix_w.reshape(1, d), w_main, w_gd, tm=tm_in, tn=tn_in)

    up = jnp.pad(gk_up, ((0, LANES - GLA_GATE_RANK), (0, 0)))
    up_hi = up.astype(BF16)
    up2 = jnp.stack([up_hi, (up - up_hi.astype(F32)).astype(BF16)])
    o_gla = _gla(proj, gd, up2, gk_bias.reshape(1, gqk), gla_norm_w.reshape(1, GLA_DV),
                 batch=batch, seq=seq, rows=gla_rows, cols=gla_cols)

    theta = (1.0 / (ROPE_BASE ** jnp.linspace(0.0, 1.0, RET_DK // 2, dtype=F32))).reshape(1, RET_DK // 2)
    o_ret = _retention(proj, positions.reshape(n, 1), theta, ret_norm_w.reshape(1, rv),
                       batch=batch, seq=seq, cols=ret_cols)

    wr = jnp.concatenate([re_w.transpose(1, 0, 2).reshape(d, N_EXPERTS), rg_w], axis=1)
    wr = jnp.pad(wr, ((0, 0), (0, LANES - N_EXPERTS - N_GROUPS)))
    wr_hi = wr.astype(BF16)
    wr2 = jnp.stack([wr_hi, (wr - wr_hi.astype(F32)).astype(BF16)])
    br = jnp.pad(jnp.concatenate([re_b.reshape(-1), rg_b]), (0, LANES - N_EXPERTS - N_GROUPS)).reshape(1, LANES)
    h1, t, r, cnt = _mix(o_gla, o_ret, proj, x2, w_br_gla.astype(BF16), w_br_ret.astype(BF16), w_out.astype(BF16),
                         norm_ffn_w.reshape(1, d), wr2, br, tm=tm_mix, cols=mix_cols)

    blk = EXPERT_ROWS
    n_assign = 2 * n
    n_slots = -(-(n_assign + N_EXPERTS * (blk - 1)) // blk) * blk
    counts = cnt[0, :N_EXPERTS].astype(jnp.int32)
    padded = ((counts + blk - 1) // blk) * blk
    pad_end = jnp.cumsum(padded)
    pad_start = pad_end - padded
    eid = r[:, 0:2].astype(jnp.int32)
    dest = pad_start[eid] + r[:, 4:6].astype(jnp.int32)
    dest0, dest1 = dest[:, 0], dest[:, 1]
    block_start = jnp.arange(n_slots // blk, dtype=jnp.int32) * blk
    block_expert = jnp.minimum(jnp.sum((pad_end[None, :] <= block_start[:, None]).astype(jnp.int32), axis=1),
                               N_EXPERTS - 1)
    n_used = (pad_end[-1:] // blk).astype(jnp.int32)
    spare = pad_end[-1] + jnp.arange(N_EXPERTS, dtype=jnp.int32) * blk
    tail = jnp.concatenate([jnp.maximum(pad_end - blk, 0), jnp.minimum(spare, n_slots - blk)]).astype(jnp.int32)
    has = jnp.concatenate([counts, (spare < n_slots).astype(jnp.int32)])

    xs = _dispatch(t, dest0, dest1, tail, has, n_slots=n_slots, chunk=disp_chunk)
    ys = _experts(xs, block_expert, n_used, wg, wu, wd)
    out = _combine(dest0, dest1, h1, r, norm_final_w.reshape(1, d), ys, tm=tm_comb)
    return out.reshape(batch, seq, d)


def kernel(x, positions, norm_mix_w, w_in, gla_gk_up, gla_gk_bias, gla_norm_w, w_branch_gla, ret_norm_w,
           w_branch_ret, w_out, norm_ffn_w, router_group_w, router_group_b, router_expert_w, router_expert_b,
           expert_w_gate, expert_w_up, expert_w_down, norm_final_w):
    assert norm_mix_w.shape[0] == 1, "single-layer block"
    return _layer(x, positions, norm_mix_w[0], w_in[0], gla_gk_up[0], gla_gk_bias[0], gla_norm_w[0], w_branch_gla[0],
                  ret_norm_w[0], w_branch_ret[0], w_out[0], norm_ffn_w[0], router_group_w[0], router_group_b[0],
                  router_expert_w[0], router_expert_b[0], expert_w_gate[0], expert_w_up[0], expert_w_down[0],
                  norm_final_w, tm_in=1024, tn_in=1024, gla_rows=256, tm_mix=512, tm_comb=256, disp_chunk=1024)
```

```python
import functools
import math

import jax
import jax.numpy as jnp
import numpy as np
from jax import lax
from jax.experimental import pallas as pl
from jax.experimental.pallas import tpu as pltpu

F32 = jnp.float32
BF16 = jnp.bfloat16

EPS = 1e-6
GLA_HEADS = 4
GLA_DK = 128
GLA_DV = 256
GLA_GATE_RANK = 16
GLA_GATE_TEMP = 16.0
RET_HEADS = 4
RET_DK = 256
RET_DV = 512
ROPE_BASE = 10000.0
N_GROUPS = 4
EXPERTS_PER_GROUP = 8
N_EXPERTS = N_GROUPS * EXPERTS_PER_GROUP
EXPERT_HIDDEN = 512

LANES = 128
SUBLANES = 8
GLA_CHUNK = 64
GLA_DIAG = 8
RET_CHUNK = 256
EXPERT_ROWS = 256
VMEM_LIMIT = 48 * 1024 * 1024


def _dot(a, b):
    return jnp.dot(a, b, preferred_element_type=F32)


def _dot_nt(a, b):
    return lax.dot_general(a, b, (((1,), (1,)), ((), ())), preferred_element_type=F32)


def _split2(a):
    hi = a.astype(BF16)
    lo = (a - hi.astype(F32)).astype(BF16)
    return hi, lo


def _params(sem, vmem=VMEM_LIMIT):
    return pltpu.CompilerParams(dimension_semantics=sem, vmem_limit_bytes=vmem)


def _inproj_kernel(x_ref, nw_ref, w_ref, wgd_ref, proj_ref, gd_ref, u_scr):
    @pl.when(pl.program_id(1) == 0)
    def _():
        x = x_ref[...]
        ms = jnp.mean(x * x, axis=-1, keepdims=True)
        u = (x * lax.rsqrt(ms + EPS) * nw_ref[...]).astype(BF16)
        u_scr[...] = u
        gd_ref[...] = _dot(u, wgd_ref[...])

    proj_ref[...] = _dot(u_scr[...], w_ref[...]).astype(BF16)


def _inproj(x2, nw, w_main, w_gd, *, tm, tn):
    n, d = x2.shape
    p = w_main.shape[1]
    return pl.pallas_call(
        _inproj_kernel,
        grid=(n // tm, p // tn),
        in_specs=[
            pl.BlockSpec((tm, d), lambda i, j: (i, 0)),
            pl.BlockSpec((1, d), lambda i, j: (0, 0)),
            pl.BlockSpec((d, tn), lambda i, j: (0, j)),
            pl.BlockSpec((d, LANES), lambda i, j: (0, 0)),
        ],
        out_specs=[
            pl.BlockSpec((tm, tn), lambda i, j: (i, j)),
            pl.BlockSpec((tm, LANES), lambda i, j: (i, 0)),
        ],
        out_shape=[
            jax.ShapeDtypeStruct((n, p), BF16),
            jax.ShapeDtypeStruct((n, LANES), F32),
        ],
        scratch_shapes=[pltpu.VMEM((tm, d), BF16)],
        compiler_params=_params(("arbitrary", "arbitrary")),
        name="inproj",
    )(x2, nw, w_main, w_gd)


def _gla_tables(c):
    levels = []
    s = c // 2
    while s >= GLA_DIAG:
        levels.append(s)
        s //= 2
    i = np.arange(c)[:, None]
    t = np.arange(c)[None, :]
    mats = [t <= i, t > i]
    masks = []
    for s in levels:
        bs = (i // s) * s
        mats.append((t > bs) & (t <= i))
        mats.append((t > i) & (t <= np.minimum(bs + s, c - 1)))
        masks.append(((i // (2 * s)) == (t // (2 * s))) & (((i // s) % 2) == 1) & (((t // s) % 2) == 0))
    for d in range(1, GLA_DIAG):
        mats.append((t > i - d) & (t <= i))
    for d in range(GLA_DIAG):
        masks.append((t == i - d) & ((i % GLA_DIAG) >= d))
    g = np.concatenate(mats, 0).astype(np.float32)
    m = np.stack(masks, 0).astype(np.float32)
    return g, m, tuple(levels)


def _gla_kernel(q_ref, k_ref, v_ref, gg_ref, gd_ref, up_ref, bias_ref, nw_ref, g_ref, m_ref,
                o_ref, st_ref, *, c, nlev):
    h_n, dk, dv = GLA_HEADS, GLA_DK, GLA_DV
    rows = q_ref.shape[0]

    @pl.when(pl.program_id(1) == 0)
    def _():
        st_ref[...] = jnp.zeros_like(st_ref)

    g_tab = g_ref[...]
    up_hi = up_ref[0]
    up_lo = up_ref[1]
    q_scale = dk ** -0.5
    band0 = 2 + 2 * nlev

    def chunk(ci, carry):
        r0 = pl.multiple_of(ci * c, c)
        rs = pl.ds(r0, c)
        gd_hi, gd_lo = _split2(gd_ref[rs, :])
        xg = _dot(gd_hi, up_hi) + _dot(gd_hi, up_lo) + _dot(gd_lo, up_hi) + bias_ref[...]
        la = (jnp.minimum(xg, 0.0) - jnp.log1p(jnp.exp(-jnp.abs(xg)))) * (1.0 / GLA_GATE_TEMP)
        la_hi, la_lo = _split2(la)
        ex = jnp.exp(_dot(g_tab, la_hi) + _dot(g_tab, la_lo))

        for h in range(h_n):
            ks = slice(h * dk, (h + 1) * dk)
            vs = slice(h * dv, (h + 1) * dv)

            def tab(m, ks=ks):
                return ex[m * c:(m + 1) * c, ks]

            qh = q_ref[rs, ks].astype(F32) * q_scale
            kh = k_ref[rs, ks].astype(F32)
            vh = v_ref[rs, vs]
            st = st_ref[h]
            eb = tab(0)
            o = _dot_nt((qh * eb).astype(BF16), st.astype(BF16))
            a = m_ref[nlev] * jnp.sum(qh * kh, axis=-1, keepdims=True)
            for li in range(nlev):
                qs = (qh * tab(2 + 2 * li)).astype(BF16)
                kk = (kh * tab(3 + 2 * li)).astype(BF16)
                a = a + m_ref[li] * _dot_nt(qs, kk)
            for d in range(1, GLA_DIAG):
                kr = pltpu.roll(kh, d, 0)
                dd = jnp.sum(qh * kr * tab(band0 + d - 1), axis=-1, keepdims=True)
                a = a + m_ref[nlev + d] * dd
            o = o + _dot(a.astype(BF16), vh)
            e_last = eb[c - 1:c, :]
            k_st = (kh * tab(1)).astype(BF16)
            v_t = vh.astype(F32).T.astype(BF16)
            st_ref[h] = st * e_last + _dot(v_t, k_st)

            ms = jnp.mean(o * o, axis=-1, keepdims=True)
            y = o * lax.rsqrt(ms + EPS) * nw_ref[...]
            gg = gg_ref[rs, vs].astype(F32)
            o_ref[rs, vs] = (y * (gg * jax.nn.sigmoid(gg))).astype(BF16)
        return carry

    lax.fori_loop(0, rows // c, chunk, 0, unroll=2)


def _gla(proj, gd, up2, bias, nw, *, batch, seq, rows, cols):
    n = proj.shape[0]
    c = GLA_CHUNK
    g_np, m_np, levels = _gla_tables(c)
    g_tab = jnp.asarray(g_np, BF16)
    m_tab = jnp.asarray(m_np, F32)
    qk_w = GLA_HEADS * GLA_DK
    v_w = GLA_HEADS * GLA_DV
    spb = seq // rows
    row = lambda b, s: b * spb + s
    kern = functools.partial(_gla_kernel, c=c, nlev=len(levels))
    return pl.pallas_call(
        kern,
        grid=(batch, spb),
        in_specs=[
            pl.BlockSpec((rows, qk_w), lambda b, s: (row(b, s), cols[0])),
            pl.BlockSpec((rows, qk_w), lambda b, s: (row(b, s), cols[1])),
            pl.BlockSpec((rows, v_w), lambda b, s: (row(b, s), cols[2])),
            pl.BlockSpec((rows, v_w), lambda b, s: (row(b, s), cols[3])),
            pl.BlockSpec((rows, LANES), lambda b, s: (row(b, s), 0)),
            pl.BlockSpec(up2.shape, lambda b, s: (0, 0, 0)),
            pl.BlockSpec(bias.shape, lambda b, s: (0, 0)),
            pl.BlockSpec(nw.shape, lambda b, s: (0, 0)),
            pl.BlockSpec(g_tab.shape, lambda b, s: (0, 0)),
            pl.BlockSpec(m_tab.shape, lambda b, s: (0, 0, 0)),
        ],
        out_specs=pl.BlockSpec((rows, v_w), lambda b, s: (row(b, s), 0)),
        out_shape=jax.ShapeDtypeStruct((n, v_w), BF16),
        scratch_shapes=[pltpu.VMEM((GLA_HEADS, GLA_DV, GLA_DK), F32)],
        compiler_params=_params(("arbitrary", "arbitrary")),
        name="gla",
    )(proj, proj, proj, proj, gd, up2, bias, nw, g_tab, m_tab)


def _ret_log_gamma(h):
    return math.log(1.0 - 2.0 ** (-5.0 - h))


def _ret_kernel(pos_ref, th_ref, q_ref, k_ref, v_ref, rg_ref, nw_ref, o_ref, s_ref, dm_ref):
    h_n, dk, dv = RET_HEADS, RET_DK, RET_DV
    c = q_ref.shape[0]
    half = dk // 2

    @pl.when(pl.program_id(1) == 0)
    def _():
        s_ref[...] = jnp.zeros_like(s_ref)

    @pl.when((pl.program_id(0) == 0) & (pl.program_id(1) == 0))
    def _():
        rel = (lax.broadcasted_iota(jnp.int32, (c, c), 0) - lax.broadcasted_iota(jnp.int32, (c, c), 1)).astype(F32)
        for h in range(h_n):
            dm_ref[h] = jnp.where(rel >= 0, jnp.exp(_ret_log_gamma(h) * jnp.maximum(rel, 0.0)), 0.0)

    ang = pos_ref[...].astype(F32) * th_ref[...]
    cs = jnp.cos(ang)
    sn = jnp.sin(ang)
    idx = lax.broadcasted_iota(jnp.int32, (c, 1), 0).astype(F32)
    k_scale = dk ** -0.5

    def rot(ref, h):
        a = ref[:, h * dk:h * dk + half].astype(F32)
        b = ref[:, h * dk + half:(h + 1) * dk].astype(F32)
        return jnp.concatenate([a * cs - b * sn, b * cs + a * sn], axis=-1)

    for h in range(h_n):
        lg = _ret_log_gamma(h)
        vs = slice(h * dv, (h + 1) * dv)
        qr = rot(q_ref, h)
        kr = rot(k_ref, h) * k_scale
        vh = v_ref[:, vs]
        scores = _dot_nt(qr.astype(BF16), kr.astype(BF16)) * dm_ref[h]
        q_dec = jnp.exp(lg * (idx + 1.0))
        k_dec = jnp.exp(lg * (c - 1.0 - idx))
        s = s_ref[h]
        o = _dot(scores.astype(BF16), vh) + _dot((qr * q_dec).astype(BF16), s.astype(BF16))
        k_t = (kr * k_dec).T.astype(BF16)
        s_ref[h] = math.exp(lg * c) * s + _dot(k_t, vh)

        mu = jnp.mean(o, axis=-1, keepdims=True)
        oc = o - mu
        var = jnp.mean(oc * oc, axis=-1, keepdims=True)
        y = oc * lax.rsqrt(var + EPS) * nw_ref[:, vs]
        rg = rg_ref[:, vs].astype(F32)
        o_ref[:, vs] = (y * (rg * jax.nn.sigmoid(rg))).astype(BF16)


def _retention(proj, pos, theta, nw, *, batch, seq, cols):
    n = proj.shape[0]
    c = RET_CHUNK
    qk_w = RET_HEADS * RET_DK
    v_w = RET_HEADS * RET_DV
    spb = seq // c
    row = lambda b, s: b * spb + s
    return pl.pallas_call(
        _ret_kernel,
        grid=(batch, spb),
        in_specs=[
            pl.BlockSpec((c, 1), lambda b, s: (row(b, s), 0)),
            pl.BlockSpec(theta.shape, lambda b, s: (0, 0)),
            pl.BlockSpec((c, qk_w), lambda b, s: (row(b, s), cols[0])),
            pl.BlockSpec((c, qk_w), lambda b, s: (row(b, s), cols[1])),
            pl.BlockSpec((c, v_w), lambda b, s: (row(b, s), cols[2])),
            pl.BlockSpec((c, v_w), lambda b, s: (row(b, s), cols[3])),
            pl.BlockSpec(nw.shape, lambda b, s: (0, 0)),
        ],
        out_specs=pl.BlockSpec((c, v_w), lambda b, s: (row(b, s), 0)),
        out_shape=jax.ShapeDtypeStruct((n, v_w), BF16),
        scratch_shapes=[
            pltpu.VMEM((RET_HEADS, RET_DK, RET_DV), F32),
            pltpu.VMEM((RET_HEADS, c, c), F32),
        ],
        compiler_params=_params(("arbitrary", "arbitrary")),
        name="retention",
    )(pos, theta, proj, proj, proj, proj, nw)


def _mix_kernel(og_ref, or_ref, ma_ref, mb_ref, x_ref, wa_ref, wb_ref, wo_ref, nw_ref, wr_ref, br_ref,
                tri_ref, h_ref, t_ref, r_ref, rt_ref, cnt_ref, base_ref):
    sub = tri_ref.shape[0]

    @pl.when(pl.program_id(0) == 0)
    def _():
        base_ref[...] = jnp.zeros_like(base_ref)

    lane = lax.broadcasted_iota(jnp.int32, (sub, LANES), 1)
    neg = jnp.float32(-1e30)
    big = jnp.int32(1 << 20)

    def first_max(v):
        m = jnp.max(v, axis=-1, keepdims=True)
        return m, jnp.min(jnp.where(v == m, lane, big), axis=-1, keepdims=True)

    base = base_ref[0:1, :]
    for s0 in range(0, x_ref.shape[0], sub):
        rs = slice(s0, s0 + sub)
        ya = _dot(og_ref[rs, :], wa_ref[...])
        yb = _dot(or_ref[rs, :], wb_ref[...])
        merged = (jax.nn.sigmoid(ma_ref[rs, :].astype(F32)) * ya + jax.nn.sigmoid(mb_ref[rs, :].astype(F32)) * yb)
        h = x_ref[rs, :] + _dot(merged.astype(BF16), wo_ref[...])
        h_ref[rs, :] = h
        ms = jnp.mean(h * h, axis=-1, keepdims=True)
        t = h * lax.rsqrt(ms + EPS) * nw_ref[...]
        t_ref[rs, :] = t

        t_hi, t_lo = _split2(t)
        lg = _dot(t_hi, wr_ref[0]) + _dot(t_hi, wr_ref[1]) + _dot(t_lo, wr_ref[0]) + br_ref[...]
        g_valid = (lane >= N_EXPERTS) & (lane < N_EXPERTS + N_GROUPS)
        g_m, g_lane = first_max(jnp.where(g_valid, lg, neg))
        g_w = 1.0 / jnp.sum(jnp.where(g_valid, jnp.exp(lg - g_m), 0.0), axis=-1, keepdims=True)
        g_idx = g_lane - N_EXPERTS
        e_valid = (lane < N_EXPERTS) & ((lane // EXPERTS_PER_GROUP) == g_idx)
        el = jnp.where(e_valid, lg, neg)
        v1, i1 = first_max(el)
        v2, i2 = first_max(jnp.where(lane == i1, neg, el))
        e21 = jnp.exp(v2 - v1)
        w1 = g_w / (1.0 + e21)
        w2 = g_w * e21 / (1.0 + e21)

        o1 = lane == i1
        o2 = lane == i2
        osum = jnp.where(o1 | o2, 1.0, 0.0)
        before = _dot(tri_ref[...], osum.astype(BF16)) + base
        r1 = jnp.sum(jnp.where(o1, before, 0.0), axis=-1, keepdims=True)
        r2 = jnp.sum(jnp.where(o2, before, 0.0), axis=-1, keepdims=True)
        base = base + jnp.sum(osum, axis=0, keepdims=True)

        cols = (i1.astype(F32), i2.astype(F32), w1, w2, r1, r2)
        wide = jnp.zeros((sub, LANES), F32)
        for ci, col in enumerate(cols):
            wide = jnp.where(lane == ci, col, wide)
        r_ref[rs, :] = wide[:, :r_ref.shape[1]]
        rt_ref[:, rs] = wide.T[:rt_ref.shape[0], :]

    base_ref[...] = jnp.broadcast_to(base, base_ref.shape)
    cnt_ref[...] = jnp.broadcast_to(base, cnt_ref.shape)


def _mix(o_gla, o_ret, proj, x2, wa, wb, wo, nw, wr2, br, *, tm, sub, cols):
    n, d = x2.shape
    tri = jnp.asarray(np.tril(np.ones((sub, sub), np.float32), -1), BF16)
    const = lambda shape: pl.BlockSpec(shape, lambda i: (0,) * len(shape))
    return pl.pallas_call(
        _mix_kernel,
        grid=(n // tm,),
        in_specs=[
            pl.BlockSpec((tm, o_gla.shape[1]), lambda i: (i, 0)),
            pl.BlockSpec((tm, o_ret.shape[1]), lambda i: (i, 0)),
            pl.BlockSpec((tm, d), lambda i: (i, cols[0])),
            pl.BlockSpec((tm, d), lambda i: (i, cols[1])),
            pl.BlockSpec((tm, d), lambda i: (i, 0)),
            const(wa.shape), const(wb.shape), const(wo.shape), const(nw.shape),
            const(wr2.shape), const(br.shape), const(tri.shape),
        ],
        out_specs=[
            pl.BlockSpec((tm, d), lambda i: (i, 0)),
            pl.BlockSpec((tm, d), lambda i: (i, 0)),
            pl.BlockSpec((tm, 8), lambda i: (i, 0)),
            pl.BlockSpec((8, tm), lambda i: (0, i)),
            pl.BlockSpec((8, LANES), lambda i: (0, 0)),
        ],
        out_shape=[
            jax.ShapeDtypeStruct((n, d), F32),
            jax.ShapeDtypeStruct((n, d), F32),
            jax.ShapeDtypeStruct((n, 8), F32),
            jax.ShapeDtypeStruct((8, n), F32),
            jax.ShapeDtypeStruct((8, LANES), F32),
        ],
        scratch_shapes=[pltpu.VMEM((8, LANES), F32)],
        compiler_params=_params(("arbitrary",)),
        name="mix_router",
    )(o_gla, o_ret, proj, proj, x2, wa, wb, wo, nw, wr2, br, tri)


def _row_of(ref3, row):
    return ref3.at[lax.shift_right_logical(row, 3), pl.ds(lax.bitwise_and(row, SUBLANES - 1), 1), :]


def _dispatch_kernel(tail_ref, has_ref, d0_ref, d1_ref, t_ref, xs_hbm, zero_ref, sem, zsem):
    step = pl.program_id(0)
    groups = t_ref.shape[0]
    blk_groups = zero_ref.shape[0]

    def tail_copy(e):
        return pltpu.make_async_copy(zero_ref, xs_hbm.at[pl.ds(tail_ref[e], blk_groups)], zsem)

    @pl.when(step == 0)
    def _():
        zero_ref[...] = jnp.zeros_like(zero_ref)
        for e in range(tail_ref.shape[0]):
            @pl.when(has_ref[e] > 0)
            def _(e=e):
                tail_copy(e).start()
        for e in range(tail_ref.shape[0]):
            @pl.when(has_ref[e] > 0)
            def _(e=e):
                tail_copy(e).wait()

    def row_copy(g, j, dst):
        return pltpu.make_async_copy(t_ref.at[g, pl.ds(j, 1), :], _row_of(xs_hbm, dst), sem)

    def body(g, carry):
        for j in range(SUBLANES):
            a = g * SUBLANES + j
            row_copy(g, j, d0_ref[a]).start()
            row_copy(g, j, d1_ref[a]).start()
        return carry

    lax.fori_loop(0, groups, body, 0)

    for _ in range(2):
        pltpu.make_async_copy(t_ref, xs_hbm.at[pl.ds(0, groups)], sem).wait()


def _dispatch(t, dest0, dest1, tail, has, *, n_slots, chunk):
    n, d = t.shape
    t3 = t.reshape(n // SUBLANES, SUBLANES, d)
    cg = chunk // SUBLANES
    return pl.pallas_call(
        _dispatch_kernel,
        grid_spec=pltpu.PrefetchScalarGridSpec(
            num_scalar_prefetch=2,
            grid=(n // chunk,),
            in_specs=[
                pl.BlockSpec((chunk,), lambda i, *_: (i,), memory_space=pltpu.SMEM),
                pl.BlockSpec((chunk,), lambda i, *_: (i,), memory_space=pltpu.SMEM),
                pl.BlockSpec((cg, SUBLANES, d), lambda i, *_: (i, 0, 0)),
            ],
            out_specs=pl.BlockSpec(memory_space=pl.ANY),
            scratch_shapes=[
                pltpu.VMEM((EXPERT_ROWS // SUBLANES, SUBLANES, d), F32),
                pltpu.SemaphoreType.DMA,
                pltpu.SemaphoreType.DMA,
            ],
        ),
        out_shape=jax.ShapeDtypeStruct((n_slots // SUBLANES, SUBLANES, d), F32),
        compiler_params=pltpu.CompilerParams(dimension_semantics=("arbitrary",), has_side_effects=True,
                                             vmem_limit_bytes=VMEM_LIMIT, disable_bounds_checks=True),
        name="dispatch",
    )(tail, has, dest0, dest1, t3)


def _expert_kernel(be_ref, nu_ref, xs_ref, wg_ref, wu_ref, wd_ref, ys_ref, wg_s, wu_s, wd_s):
    b = pl.program_id(0)

    @pl.when((b == 0) | (be_ref[b] != be_ref[jnp.maximum(b - 1, 0)]))
    def _():
        wg_s[...] = wg_ref[0].astype(BF16)
        wu_s[...] = wu_ref[0].astype(BF16)
        wd_s[...] = wd_ref[0].astype(BF16)

    @pl.when(b < nu_ref[0])
    def _():
        x = xs_ref[...].astype(BF16)
        g = _dot(x, wg_s[...])
        u = _dot(x, wu_s[...])
        hid = (g * jax.nn.sigmoid(g) * u).astype(BF16)
        ys_ref[...] = _dot(hid, wd_s[...])

    @pl.when(b >= nu_ref[0])
    def _():
        ys_ref[...] = jnp.zeros_like(ys_ref)


def _experts(xs, block_expert, n_used, wg, wu, wd):
    n_slots, d = xs.shape
    blk = EXPERT_ROWS
    hid = wg.shape[2]
    return pl.pallas_call(
        _expert_kernel,
        grid_spec=pltpu.PrefetchScalarGridSpec(
            num_scalar_prefetch=2,
            grid=(n_slots // blk,),
            in_specs=[
                pl.BlockSpec((blk, d), lambda b, be, nu: (jnp.minimum(b, nu[0] - 1), 0)),
                pl.BlockSpec((1, d, hid), lambda b, be, nu: (be[b], 0, 0)),
                pl.BlockSpec((1, d, hid), lambda b, be, nu: (be[b], 0, 0)),
                pl.BlockSpec((1, hid, d), lambda b, be, nu: (be[b], 0, 0)),
            ],
            out_specs=pl.BlockSpec((blk, d), lambda b, be, nu: (b, 0)),
            scratch_shapes=[pltpu.VMEM((d, hid), BF16), pltpu.VMEM((d, hid), BF16), pltpu.VMEM((hid, d), BF16)],
        ),
        out_shape=jax.ShapeDtypeStruct((n_slots, d), F32),
        compiler_params=_params(("arbitrary",)),
        name="experts",
    )(block_expert, n_used, xs, wg, wu, wd)


def _combine_kernel(c0_ref, c1_ref, n0_ref, n1_ref, h_ref, r_ref, nw_ref, ys_hbm, o_ref, buf, sem):
    i = pl.program_id(0)
    n_steps = pl.num_programs(0)
    tm = h_ref.shape[0]

    groups = tm // SUBLANES

    def row_copy(dst_group, j, src_row, slot):
        return pltpu.make_async_copy(_row_of(ys_hbm, src_row), buf.at[slot, dst_group, pl.ds(j, 1), :], sem.at[slot])

    def issue(d0, d1, slot):
        def body(g, carry):
            for j in range(SUBLANES):
                a = g * SUBLANES + j
                row_copy(g, j, d0[a], slot).start()
                row_copy(groups + g, j, d1[a], slot).start()
            return carry
        lax.fori_loop(0, groups, body, 0)

    @pl.when(i == 0)
    def _():
        issue(c0_ref, c1_ref, 0)

    @pl.when(i + 1 < n_steps)
    def _():
        issue(n0_ref, n1_ref, (i + 1) % 2)

    slot = i % 2
    pltpu.make_async_copy(ys_hbm.at[pl.ds(0, 2 * groups)], buf.at[slot], sem.at[slot]).wait()

    r = r_ref[...]
    d = h_ref.shape[1]
    y = (buf[slot, 0:groups].reshape(tm, d) * r[:, 2:3] + buf[slot, groups:2 * groups].reshape(tm, d) * r[:, 3:4])
    h = h_ref[...] + y
    ms = jnp.mean(h * h, axis=-1, keepdims=True)
    o_ref[...] = h * lax.rsqrt(ms + EPS) * nw_ref[...]


def _combine(dest0, dest1, h1, r, nw, ys, *, tm):
    n, d = h1.shape
    n_steps = n // tm
    cur = pl.BlockSpec((tm,), lambda i: (i,), memory_space=pltpu.SMEM)
    nxt = pl.BlockSpec((tm,), lambda i: (jnp.minimum(i + 1, n_steps - 1),), memory_space=pltpu.SMEM)
    return pl.pallas_call(
        _combine_kernel,
        grid=(n_steps,),
        in_specs=[
            cur, cur, nxt, nxt,
            pl.BlockSpec((tm, d), lambda i: (i, 0)),
            pl.BlockSpec((tm, r.shape[1]), lambda i: (i, 0)),
            pl.BlockSpec((1, d), lambda i: (0, 0)),
            pl.BlockSpec(memory_space=pl.ANY),
        ],
        out_specs=pl.BlockSpec((tm, d), lambda i: (i, 0)),
        out_shape=jax.ShapeDtypeStruct((n, d), F32),
        scratch_shapes=[pltpu.VMEM((2, 2 * tm // SUBLANES, SUBLANES, d), F32), pltpu.SemaphoreType.DMA((2,))],
        compiler_params=pltpu.CompilerParams(dimension_semantics=("arbitrary",), vmem_limit_bytes=VMEM_LIMIT,
                                             disable_bounds_checks=True),
        name="combine",
    )(dest0, dest1, dest0, dest1, h1, r, nw, ys)


def _layer(h, positions, norm_mix_w, w_in, gk_up, gk_bias, gla_norm_w, w_br_gla, ret_norm_w, w_br_ret, w_out,
           norm_ffn_w, rg_w, rg_b, re_w, re_b, wg, wu, wd, norm_final_w, *, tm_in, tn_in, gla_rows, tm_mix,
           sub_mix, tm_comb, disp_chunk):
    batch, seq, d = h.shape
    n = batch * seq
    x2 = h.reshape(n, d)
    gqk, gv = GLA_HEADS * GLA_DK, GLA_HEADS * GLA_DV
    rqk, rv = RET_HEADS * RET_DK, RET_HEADS * RET_DV

    sizes = (gqk, gqk, gv, gv, GLA_GATE_RANK, rqk, rqk, rv, rv, d, d)
    offs = np.concatenate([[0], np.cumsum(sizes)])
    seg = lambda i: w_in[:, offs[i]:offs[i + 1]]
    perm = np.concatenate([np.concatenate([np.arange(0, RET_DK, 2), np.arange(1, RET_DK, 2)]) + hh * RET_DK
                           for hh in range(RET_HEADS)])
    w_main = jnp.concatenate([seg(7), seg(8), seg(2), seg(3), seg(5)[:, perm], seg(6)[:, perm], seg(9), seg(10),
                              seg(0), seg(1)], axis=1).astype(BF16)
    assert rv % gv == 0 and gv == rqk == d and rv == 2 * d and gqk * 2 == d
    ret_cols = (4 + 2, 4 + 3, 0, 1)
    gla_cols = (2 * (4 + 6), 2 * (4 + 6) + 1, 4, 5)
    mix_cols = (4 + 4, 4 + 5)
    w_gd = jnp.pad(seg(4), ((0, 0), (0, LANES - GLA_GATE_RANK))).astype(BF16)

    proj, gd = _inproj(x2, norm_mix_w.reshape(1, d), w_main, w_gd, tm=tm_in, tn=tn_in)

    up = jnp.pad(gk_up, ((0, LANES - GLA_GATE_RANK), (0, 0)))
    up_hi = up.astype(BF16)
    up2 = jnp.stack([up_hi, (up - up_hi.astype(F32)).astype(BF16)])
    o_gla = _gla(proj, gd, up2, gk_bias.reshape(1, gqk), gla_norm_w.reshape(1, GLA_DV),
                 batch=batch, seq=seq, rows=gla_rows, cols=gla_cols)

    theta = (1.0 / (ROPE_BASE ** jnp.linspace(0.0, 1.0, RET_DK // 2, dtype=F32))).reshape(1, RET_DK // 2)
    o_ret = _retention(proj, positions.reshape(n, 1), theta, ret_norm_w.reshape(1, rv),
                       batch=batch, seq=seq, cols=ret_cols)

    wr = jnp.concatenate([re_w.transpose(1, 0, 2).reshape(d, N_EXPERTS), rg_w], axis=1)
    wr = jnp.pad(wr, ((0, 0), (0, LANES - N_EXPERTS - N_GROUPS)))
    wr_hi = wr.astype(BF16)
    wr2 = jnp.stack([wr_hi, (wr - wr_hi.astype(F32)).astype(BF16)])
    br = jnp.pad(jnp.concatenate([re_b.reshape(-1), rg_b]), (0, LANES - N_EXPERTS - N_GROUPS)).reshape(1, LANES)
    h1, t, r, rt, cnt = _mix(o_gla, o_ret, proj, x2, w_br_gla.astype(BF16), w_br_ret.astype(BF16),
                             w_out.astype(BF16), norm_ffn_w.reshape(1, d), wr2, br, tm=tm_mix, sub=sub_mix,
                             cols=mix_cols)

    blk = EXPERT_ROWS
    n_assign = 2 * n
    n_slots = -(-(n_assign + N_EXPERTS * (blk - 1)) // blk) * blk
    counts = cnt[0, :N_EXPERTS].astype(jnp.int32)
    padded = ((counts + blk - 1) // blk) * blk
    pad_end = jnp.cumsum(padded)
    pad_start = pad_end - padded
    dest0 = pad_start[rt[0].astype(jnp.int32)] + rt[4].astype(jnp.int32)
    dest1 = pad_start[rt[1].astype(jnp.int32)] + rt[5].astype(jnp.int32)
    block_start = jnp.arange(n_slots // blk, dtype=jnp.int32) * blk
    block_expert = jnp.minimum(jnp.sum((pad_end[None, :] <= block_start[:, None]).astype(jnp.int32), axis=1),
                               N_EXPERTS - 1)
    n_used = (pad_end[-1:] // blk).astype(jnp.int32)
    spare = pad_end[-1] + jnp.arange(N_EXPERTS, dtype=jnp.int32) * blk
    tail = jnp.concatenate([jnp.maximum(pad_end - blk, 0), jnp.minimum(spare, n_slots - blk)]).astype(jnp.int32)
    has = jnp.concatenate([counts, (spare < n_slots).astype(jnp.int32)])

    xs3 = _dispatch(t, dest0, dest1, tail // SUBLANES, has, n_slots=n_slots, chunk=disp_chunk)
    ys = _experts(xs3.reshape(n_slots, d), block_expert, n_used, wg, wu, wd)
    out = _combine(dest0, dest1, h1, r, norm_final_w.reshape(1, d), ys.reshape(n_slots // SUBLANES, SUBLANES, d),
                   tm=tm_comb)
    return out.reshape(batch, seq, d)


def kernel(x, positions, norm_mix_w, w_in, gla_gk_up, gla_gk_bias, gla_norm_w, w_branch_gla, ret_norm_w,
           w_branch_ret, w_out, norm_ffn_w, router_group_w, router_group_b, router_expert_w, router_expert_b,
           expert_w_gate, expert_w_up, expert_w_down, norm_final_w):
    assert norm_mix_w.shape[0] == 1, "single-layer block"
    return _layer(x, positions, norm_mix_w[0], w_in[0], gla_gk_up[0], gla_gk_bias[0], gla_norm_w[0], w_branch_gla[0],
                  ret_norm_w[0], w_branch_ret[0], w_out[0], norm_ffn_w[0], router_group_w[0], router_group_b[0],
                  router_expert_w[0], router_expert_b[0], expert_w_gate[0], expert_w_up[0], expert_w_down[0],
                  norm_final_w, tm_in=2048, tn_in=1024, gla_rows=256, tm_mix=512, sub_mix=512, tm_comb=256,
                  disp_chunk=1024)
```

```python
import functools
import math

import jax
import jax.numpy as jnp
import numpy as np
from jax import lax
from jax.experimental import pallas as pl
from jax.experimental.pallas import tpu as pltpu

F32 = jnp.float32
BF16 = jnp.bfloat16

EPS = 1e-6
GLA_HEADS = 4
GLA_DK = 128
GLA_DV = 256
GLA_GATE_RANK = 16
GLA_GATE_TEMP = 16.0
RET_HEADS = 4
RET_DK = 256
RET_DV = 512
ROPE_BASE = 10000.0
N_GROUPS = 4
EXPERTS_PER_GROUP = 8
N_EXPERTS = N_GROUPS * EXPERTS_PER_GROUP
EXPERT_HIDDEN = 512

LANES = 128
GLA_CHUNK = 64
GLA_DIAG = 8
RET_CHUNK = 256
EXPERT_ROWS = 256
VMEM_LIMIT = 48 * 1024 * 1024


def _dot(a, b):
    return jnp.dot(a, b, preferred_element_type=F32)


def _dot_nt(a, b):
    return lax.dot_general(a, b, (((1,), (1,)), ((), ())), preferred_element_type=F32)


def _split2(a):
    hi = a.astype(BF16)
    lo = (a - hi.astype(F32)).astype(BF16)
    return hi, lo


def _params(sem, vmem=VMEM_LIMIT):
    return pltpu.CompilerParams(dimension_semantics=sem, vmem_limit_bytes=vmem)


def _inproj_kernel(x_ref, nw_ref, w_ref, wgd_ref, proj_ref, gd_ref, u_scr):
    @pl.when(pl.program_id(1) == 0)
    def _():
        x = x_ref[...]
        ms = jnp.mean(x * x, axis=-1, keepdims=True)
        u = (x * lax.rsqrt(ms + EPS) * nw_ref[...]).astype(BF16)
        u_scr[...] = u
        gd_ref[...] = _dot(u, wgd_ref[...])

    proj_ref[...] = _dot(u_scr[...], w_ref[...]).astype(BF16)


def _inproj(x2, nw, w_main, w_gd, *, tm, tn):
    n, d = x2.shape
    p = w_main.shape[1]
    return pl.pallas_call(
        _inproj_kernel,
        grid=(n // tm, p // tn),
        in_specs=[
            pl.BlockSpec((tm, d), lambda i, j: (i, 0)),
            pl.BlockSpec((1, d), lambda i, j: (0, 0)),
            pl.BlockSpec((d, tn), lambda i, j: (0, j)),
            pl.BlockSpec((d, LANES), lambda i, j: (0, 0)),
        ],
        out_specs=[
            pl.BlockSpec((tm, tn), lambda i, j: (i, j)),
            pl.BlockSpec((tm, LANES), lambda i, j: (i, 0)),
        ],
        out_shape=[
            jax.ShapeDtypeStruct((n, p), BF16),
            jax.ShapeDtypeStruct((n, LANES), F32),
        ],
        scratch_shapes=[pltpu.VMEM((tm, d), BF16)],
        compiler_params=_params(("arbitrary", "arbitrary")),
        name="inproj",
    )(x2, nw, w_main, w_gd)


def _gla_tables(c):
    levels = []
    s = c // 2
    while s >= GLA_DIAG:
        levels.append(s)
        s //= 2
    i = np.arange(c)[:, None]
    t = np.arange(c)[None, :]
    mats = [t <= i, t > i]
    masks = []
    for s in levels:
        bs = (i // s) * s
        mats.append((t > bs) & (t <= i))
        mats.append((t > i) & (t <= np.minimum(bs + s, c - 1)))
        masks.append(((i // (2 * s)) == (t // (2 * s))) & (((i // s) % 2) == 1) & (((t // s) % 2) == 0))
    for d in range(1, GLA_DIAG):
        mats.append((t > i - d) & (t <= i))
    for d in range(GLA_DIAG):
        masks.append((t == i - d) & ((i % GLA_DIAG) >= d))
    g = np.concatenate(mats, 0).astype(np.float32)
    m = np.stack(masks, 0).astype(np.float32)
    return g, m, tuple(levels)


def _gla_kernel(q_ref, k_ref, v_ref, gg_ref, gd_ref, up_ref, bias_ref, nw_ref, g_ref, m_ref,
                o_ref, st_ref, *, c, nlev):
    h_n, dk, dv = GLA_HEADS, GLA_DK, GLA_DV
    rows = q_ref.shape[0]

    @pl.when(pl.program_id(1) == 0)
    def _():
        st_ref[...] = jnp.zeros_like(st_ref)

    g_tab = g_ref[...]
    up_hi = up_ref[0]
    up_lo = up_ref[1]
    q_scale = dk ** -0.5
    band0 = 2 + 2 * nlev

    def chunk(ci, carry):
        r0 = pl.multiple_of(ci * c, c)
        rs = pl.ds(r0, c)
        gd_hi, gd_lo = _split2(gd_ref[rs, :])
        xg = _dot(gd_hi, up_hi) + _dot(gd_hi, up_lo) + _dot(gd_lo, up_hi) + bias_ref[...]
        la = (jnp.minimum(xg, 0.0) - jnp.log1p(jnp.exp(-jnp.abs(xg)))) * (1.0 / GLA_GATE_TEMP)
        la_hi, la_lo = _split2(la)
        ex = jnp.exp(_dot(g_tab, la_hi) + _dot(g_tab, la_lo))

        for h in range(h_n):
            ks = slice(h * dk, (h + 1) * dk)
            vs = slice(h * dv, (h + 1) * dv)

            def tab(m, ks=ks):
                return ex[m * c:(m + 1) * c, ks]

            qh = q_ref[rs, ks].astype(F32) * q_scale
            kh = k_ref[rs, ks].astype(F32)
            vh = v_ref[rs, vs]
            st = st_ref[h]
            eb = tab(0)
            o = _dot_nt((qh * eb).astype(BF16), st.astype(BF16))
            a = m_ref[nlev] * jnp.sum(qh * kh, axis=-1, keepdims=True)
            for li in range(nlev):
                qs = (qh * tab(2 + 2 * li)).astype(BF16)
                kk = (kh * tab(3 + 2 * li)).astype(BF16)
                a = a + m_ref[li] * _dot_nt(qs, kk)
            for d in range(1, GLA_DIAG):
                kr = pltpu.roll(kh, d, 0)
                dd = jnp.sum(qh * kr * tab(band0 + d - 1), axis=-1, keepdims=True)
                a = a + m_ref[nlev + d] * dd
            o = o + _dot(a.astype(BF16), vh)
            e_last = eb[c - 1:c, :]
            k_st = (kh * tab(1)).astype(BF16)
            v_t = vh.astype(F32).T.astype(BF16)
            st_ref[h] = st * e_last + _dot(v_t, k_st)

            ms = jnp.mean(o * o, axis=-1, keepdims=True)
            y = o * lax.rsqrt(ms + EPS) * nw_ref[...]
            gg = gg_ref[rs, vs].astype(F32)
            o_ref[rs, vs] = (y * (gg * jax.nn.sigmoid(gg))).astype(BF16)
        return carry

    lax.fori_loop(0, rows // c, chunk, 0, unroll=2)


def _gla(proj, gd, up2, bias, nw, *, batch, seq, rows, cols):
    n = proj.shape[0]
    c = GLA_CHUNK
    g_np, m_np, levels = _gla_tables(c)
    g_tab = jnp.asarray(g_np, BF16)
    m_tab = jnp.asarray(m_np, F32)
    qk_w = GLA_HEADS * GLA_DK
    v_w = GLA_HEADS * GLA_DV
    spb = seq // rows
    row = lambda b, s: b * spb + s
    kern = functools.partial(_gla_kernel, c=c, nlev=len(levels))
    return pl.pallas_call(
        kern,
        grid=(batch, spb),
        in_specs=[
            pl.BlockSpec((rows, qk_w), lambda b, s: (row(b, s), cols[0])),
            pl.BlockSpec((rows, qk_w), lambda b, s: (row(b, s), cols[1])),
            pl.BlockSpec((rows, v_w), lambda b, s: (row(b, s), cols[2])),
            pl.BlockSpec((rows, v_w), lambda b, s: (row(b, s), cols[3])),
            pl.BlockSpec((rows, LANES), lambda b, s: (row(b, s), 0)),
            pl.BlockSpec(up2.shape, lambda b, s: (0, 0, 0)),
            pl.BlockSpec(bias.shape, lambda b, s: (0, 0)),
            pl.BlockSpec(nw.shape, lambda b, s: (0, 0)),
            pl.BlockSpec(g_tab.shape, lambda b, s: (0, 0)),
            pl.BlockSpec(m_tab.shape, lambda b, s: (0, 0, 0)),
        ],
        out_specs=pl.BlockSpec((rows, v_w), lambda b, s: (row(b, s), 0)),
        out_shape=jax.ShapeDtypeStruct((n, v_w), BF16),
        scratch_shapes=[pltpu.VMEM((GLA_HEADS, GLA_DV, GLA_DK), F32)],
        compiler_params=_params(("arbitrary", "arbitrary")),
        name="gla",
    )(proj, proj, proj, proj, gd, up2, bias, nw, g_tab, m_tab)


def _ret_log_gamma(h):
    return math.log(1.0 - 2.0 ** (-5.0 - h))


def _ret_kernel(pos_ref, th_ref, q_ref, k_ref, v_ref, rg_ref, nw_ref, o_ref, s_ref, dm_ref):
    h_n, dk, dv = RET_HEADS, RET_DK, RET_DV
    c = q_ref.shape[0]
    half = dk // 2

    @pl.when(pl.program_id(1) == 0)
    def _():
        s_ref[...] = jnp.zeros_like(s_ref)

    @pl.when((pl.program_id(0) == 0) & (pl.program_id(1) == 0))
    def _():
        rel = (lax.broadcasted_iota(jnp.int32, (c, c), 0) - lax.broadcasted_iota(jnp.int32, (c, c), 1)).astype(F32)
        for h in range(h_n):
            dm_ref[h] = jnp.where(rel >= 0, jnp.exp(_ret_log_gamma(h) * jnp.maximum(rel, 0.0)), 0.0)

    ang = pos_ref[...].astype(F32) * th_ref[...]
    cs = jnp.cos(ang)
    sn = jnp.sin(ang)
    idx = lax.broadcasted_iota(jnp.int32, (c, 1), 0).astype(F32)
    k_scale = dk ** -0.5

    def rot(ref, h):
        a = ref[:, h * dk:h * dk + half].astype(F32)
        b = ref[:, h * dk + half:(h + 1) * dk].astype(F32)
        return jnp.concatenate([a * cs - b * sn, b * cs + a * sn], axis=-1)

    for h in range(h_n):
        lg = _ret_log_gamma(h)
        vs = slice(h * dv, (h + 1) * dv)
        qr = rot(q_ref, h)
        kr = rot(k_ref, h) * k_scale
        vh = v_ref[:, vs]
        scores = _dot_nt(qr.astype(BF16), kr.astype(BF16)) * dm_ref[h]
        q_dec = jnp.exp(lg * (idx + 1.0))
        k_dec = jnp.exp(lg * (c - 1.0 - idx))
        s = s_ref[h]
        o = _dot(scores.astype(BF16), vh) + _dot((qr * q_dec).astype(BF16), s.astype(BF16))
        k_t = (kr * k_dec).T.astype(BF16)
        s_ref[h] = math.exp(lg * c) * s + _dot(k_t, vh)

        mu = jnp.mean(o, axis=-1, keepdims=True)
        oc = o - mu
        var = jnp.mean(oc * oc, axis=-1, keepdims=True)
        y = oc * lax.rsqrt(var + EPS) * nw_ref[:, vs]
        rg = rg_ref[:, vs].astype(F32)
        o_ref[:, vs] = (y * (rg * jax.nn.sigmoid(rg))).astype(BF16)


def _retention(proj, pos, theta, nw, *, batch, seq, cols):
    n = proj.shape[0]
    c = RET_CHUNK
    qk_w = RET_HEADS * RET_DK
    v_w = RET_HEADS * RET_DV
    spb = seq // c
    row = lambda b, s: b * spb + s
    return pl.pallas_call(
        _ret_kernel,
        grid=(batch, spb),
        in_specs=[
            pl.BlockSpec((c, 1), lambda b, s: (row(b, s), 0)),
            pl.BlockSpec(theta.shape, lambda b, s: (0, 0)),
            pl.BlockSpec((c, qk_w), lambda b, s: (row(b, s), cols[0])),
            pl.BlockSpec((c, qk_w), lambda b, s: (row(b, s), cols[1])),
            pl.BlockSpec((c, v_w), lambda b, s: (row(b, s), cols[2])),
            pl.BlockSpec((c, v_w), lambda b, s: (row(b, s), cols[3])),
            pl.BlockSpec(nw.shape, lambda b, s: (0, 0)),
        ],
        out_specs=pl.BlockSpec((c, v_w), lambda b, s: (row(b, s), 0)),
        out_shape=jax.ShapeDtypeStruct((n, v_w), BF16),
        scratch_shapes=[
            pltpu.VMEM((RET_HEADS, RET_DK, RET_DV), F32),
            pltpu.VMEM((RET_HEADS, c, c), F32),
        ],
        compiler_params=_params(("arbitrary", "arbitrary")),
        name="retention",
    )(pos, theta, proj, proj, proj, proj, nw)


def _mix_kernel(og_ref, or_ref, ma_ref, mb_ref, x_ref, wa_ref, wb_ref, wo_ref, nw_ref, wr_ref, br_ref,
                tri_ref, h_ref, t_ref, r_ref, rt_ref, cnt_ref, base_ref):
    sub = tri_ref.shape[0]

    @pl.when(pl.program_id(0) == 0)
    def _():
        base_ref[...] = jnp.zeros_like(base_ref)

    lane = lax.broadcasted_iota(jnp.int32, (sub, LANES), 1)
    neg = jnp.float32(-1e30)
    big = jnp.int32(1 << 20)

    def first_max(v):
        m = jnp.max(v, axis=-1, keepdims=True)
        return m, jnp.min(jnp.where(v == m, lane, big), axis=-1, keepdims=True)

    base = base_ref[0:1, :]
    for s0 in range(0, x_ref.shape[0], sub):
        rs = slice(s0, s0 + sub)
        ya = _dot(og_ref[rs, :], wa_ref[...])
        yb = _dot(or_ref[rs, :], wb_ref[...])
        merged = (jax.nn.sigmoid(ma_ref[rs, :].astype(F32)) * ya + jax.nn.sigmoid(mb_ref[rs, :].astype(F32)) * yb)
        h = x_ref[rs, :] + _dot(merged.astype(BF16), wo_ref[...])
        h_ref[rs, :] = h
        ms = jnp.mean(h * h, axis=-1, keepdims=True)
        t = h * lax.rsqrt(ms + EPS) * nw_ref[...]
        t_ref[rs] = _to_row_tiles(t)

        t_hi, t_lo = _split2(t)
        lg = _dot(t_hi, wr_ref[0]) + _dot(t_hi, wr_ref[1]) + _dot(t_lo, wr_ref[0]) + br_ref[...]
        g_valid = (lane >= N_EXPERTS) & (lane < N_EXPERTS + N_GROUPS)
        g_m, g_lane = first_max(jnp.where(g_valid, lg, neg))
        g_w = 1.0 / jnp.sum(jnp.where(g_valid, jnp.exp(lg - g_m), 0.0), axis=-1, keepdims=True)
        g_idx = g_lane - N_EXPERTS
        e_valid = (lane < N_EXPERTS) & ((lane // EXPERTS_PER_GROUP) == g_idx)
        el = jnp.where(e_valid, lg, neg)
        v1, i1 = first_max(el)
        v2, i2 = first_max(jnp.where(lane == i1, neg, el))
        e21 = jnp.exp(v2 - v1)
        w1 = g_w / (1.0 + e21)
        w2 = g_w * e21 / (1.0 + e21)

        o1 = lane == i1
        o2 = lane == i2
        osum = jnp.where(o1 | o2, 1.0, 0.0)
        before = _dot(tri_ref[...], osum.astype(BF16)) + base
        r1 = jnp.sum(jnp.where(o1, before, 0.0), axis=-1, keepdims=True)
        r2 = jnp.sum(jnp.where(o2, before, 0.0), axis=-1, keepdims=True)
        base = base + jnp.sum(osum, axis=0, keepdims=True)

        cols = (i1.astype(F32), i2.astype(F32), w1, w2, r1, r2)
        wide = jnp.zeros((sub, LANES), F32)
        for ci, col in enumerate(cols):
            wide = jnp.where(lane == ci, col, wide)
        r_ref[rs, :] = wide[:, :r_ref.shape[1]]
        rt_ref[:, rs] = wide.T[:rt_ref.shape[0], :]

    base_ref[...] = jnp.broadcast_to(base, base_ref.shape)
    cnt_ref[...] = jnp.broadcast_to(base, cnt_ref.shape)


def _mix(o_gla, o_ret, proj, x2, wa, wb, wo, nw, wr2, br, *, tm, sub, cols):
    n, d = x2.shape
    tri = jnp.asarray(np.tril(np.ones((sub, sub), np.float32), -1), BF16)
    const = lambda shape: pl.BlockSpec(shape, lambda i: (0,) * len(shape))
    return pl.pallas_call(
        _mix_kernel,
        grid=(n // tm,),
        in_specs=[
            pl.BlockSpec((tm, o_gla.shape[1]), lambda i: (i, 0)),
            pl.BlockSpec((tm, o_ret.shape[1]), lambda i: (i, 0)),
            pl.BlockSpec((tm, d), lambda i: (i, cols[0])),
            pl.BlockSpec((tm, d), lambda i: (i, cols[1])),
            pl.BlockSpec((tm, d), lambda i: (i, 0)),
            const(wa.shape), const(wb.shape), const(wo.shape), const(nw.shape),
            const(wr2.shape), const(br.shape), const(tri.shape),
        ],
        out_specs=[
            pl.BlockSpec((tm, d), lambda i: (i, 0)),
            pl.BlockSpec((tm, d // LANES, LANES), lambda i: (i, 0, 0)),
            pl.BlockSpec((tm, 8), lambda i: (i, 0)),
            pl.BlockSpec((8, tm), lambda i: (0, i)),
            pl.BlockSpec((8, LANES), lambda i: (0, 0)),
        ],
        out_shape=[
            jax.ShapeDtypeStruct((n, d), F32),
            jax.ShapeDtypeStruct((n, d // LANES, LANES), BF16),
            jax.ShapeDtypeStruct((n, 8), F32),
            jax.ShapeDtypeStruct((8, n), F32),
            jax.ShapeDtypeStruct((8, LANES), F32),
        ],
        scratch_shapes=[pltpu.VMEM((8, LANES), F32)],
        compiler_params=_params(("arbitrary",)),
        name="mix_router",
    )(o_gla, o_ret, proj, proj, x2, wa, wb, wo, nw, wr2, br, tri)


def _to_row_tiles(x):
    return x.reshape(x.shape[0], x.shape[1] // LANES, LANES).astype(BF16)


def _from_row_tiles(x3):
    return x3.astype(F32).reshape(x3.shape[0], x3.shape[1] * x3.shape[2])


def _dispatch_kernel(tail_ref, has_ref, d0_ref, d1_ref, t_ref, xs_hbm, zero_ref, sem, zsem):
    step = pl.program_id(0)
    tok_n = t_ref.shape[0]
    blk = zero_ref.shape[0]

    def tail_copy(e):
        return pltpu.make_async_copy(zero_ref, xs_hbm.at[pl.ds(tail_ref[e], blk)], zsem)

    @pl.when(step == 0)
    def _():
        zero_ref[...] = jnp.zeros_like(zero_ref)
        for e in range(tail_ref.shape[0]):
            @pl.when(has_ref[e] > 0)
            def _(e=e):
                tail_copy(e).start()
        for e in range(tail_ref.shape[0]):
            @pl.when(has_ref[e] > 0)
            def _(e=e):
                tail_copy(e).wait()

    def row_copy(a, dst):
        return pltpu.make_async_copy(t_ref.at[a], xs_hbm.at[dst], sem)

    def body(a, carry):
        row_copy(a, d0_ref[a]).start()
        row_copy(a, d1_ref[a]).start()
        return carry

    lax.fori_loop(0, tok_n, body, 0, unroll=8)

    for _ in range(2):
        pltpu.make_async_copy(t_ref, xs_hbm.at[pl.ds(0, tok_n)], sem).wait()


def _dispatch(t3, dest0, dest1, tail, has, *, n_slots, chunk):
    n, s, l = t3.shape
    return pl.pallas_call(
        _dispatch_kernel,
        grid_spec=pltpu.PrefetchScalarGridSpec(
            num_scalar_prefetch=2,
            grid=(n // chunk,),
            in_specs=[
                pl.BlockSpec((chunk,), lambda i, *_: (i,), memory_space=pltpu.SMEM),
                pl.BlockSpec((chunk,), lambda i, *_: (i,), memory_space=pltpu.SMEM),
                pl.BlockSpec((chunk, s, l), lambda i, *_: (i, 0, 0)),
            ],
            out_specs=pl.BlockSpec(memory_space=pl.ANY),
            scratch_shapes=[
                pltpu.VMEM((EXPERT_ROWS, s, l), BF16),
                pltpu.SemaphoreType.DMA,
                pltpu.SemaphoreType.DMA,
            ],
        ),
        out_shape=jax.ShapeDtypeStruct((n_slots, s, l), BF16),
        compiler_params=pltpu.CompilerParams(dimension_semantics=("arbitrary",), has_side_effects=True,
                                             vmem_limit_bytes=VMEM_LIMIT, disable_bounds_checks=True),
        name="dispatch",
    )(tail, has, dest0, dest1, t3)


def _expert_kernel(be_ref, nu_ref, xs_ref, wg_ref, wu_ref, wd_ref, ys_ref, wg_s, wu_s, wd_s):
    b = pl.program_id(0)

    @pl.when((b == 0) | (be_ref[b] != be_ref[jnp.maximum(b - 1, 0)]))
    def _():
        wg_s[...] = wg_ref[0].astype(BF16)
        wu_s[...] = wu_ref[0].astype(BF16)
        wd_s[...] = wd_ref[0].astype(BF16)

    @pl.when(b < nu_ref[0])
    def _():
        x = _from_row_tiles(xs_ref[...]).astype(BF16)
        g = _dot(x, wg_s[...])
        u = _dot(x, wu_s[...])
        hid = (g * jax.nn.sigmoid(g) * u).astype(BF16)
        ys_ref[...] = _to_row_tiles(_dot(hid, wd_s[...]))

    @pl.when(b >= nu_ref[0])
    def _():
        ys_ref[...] = jnp.zeros_like(ys_ref)


def _experts(xs3, block_expert, n_used, wg, wu, wd):
    n_slots, s, l = xs3.shape
    d = s * l
    blk = EXPERT_ROWS
    hid = wg.shape[2]
    return pl.pallas_call(
        _expert_kernel,
        grid_spec=pltpu.PrefetchScalarGridSpec(
            num_scalar_prefetch=2,
            grid=(n_slots // blk,),
            in_specs=[
                pl.BlockSpec((blk, s, l), lambda b, be, nu: (jnp.minimum(b, nu[0] - 1), 0, 0)),
                pl.BlockSpec((1, d, hid), lambda b, be, nu: (be[b], 0, 0)),
                pl.BlockSpec((1, d, hid), lambda b, be, nu: (be[b], 0, 0)),
                pl.BlockSpec((1, hid, d), lambda b, be, nu: (be[b], 0, 0)),
            ],
            out_specs=pl.BlockSpec((blk, s, l), lambda b, be, nu: (b, 0, 0)),
            scratch_shapes=[pltpu.VMEM((d, hid), BF16), pltpu.VMEM((d, hid), BF16), pltpu.VMEM((hid, d), BF16)],
        ),
        out_shape=jax.ShapeDtypeStruct((n_slots, s, l), BF16),
        compiler_params=_params(("arbitrary",)),
        name="experts",
    )(block_expert, n_used, xs3, wg, wu, wd)


def _combine_kernel(c0_ref, c1_ref, n0_ref, n1_ref, h_ref, r_ref, nw_ref, ys_hbm, o_ref, buf, sem):
    i = pl.program_id(0)
    n_steps = pl.num_programs(0)
    tm = h_ref.shape[0]

    def row_copy(dst_row, src_row, slot):
        return pltpu.make_async_copy(ys_hbm.at[src_row], buf.at[slot, dst_row], sem.at[slot])

    def issue(d0, d1, slot):
        def body(a, carry):
            row_copy(a, d0[a], slot).start()
            row_copy(tm + a, d1[a], slot).start()
            return carry
        lax.fori_loop(0, tm, body, 0, unroll=8)

    @pl.when(i == 0)
    def _():
        issue(c0_ref, c1_ref, 0)

    @pl.when(i + 1 < n_steps)
    def _():
        issue(n0_ref, n1_ref, (i + 1) % 2)

    slot = i % 2
    pltpu.make_async_copy(ys_hbm.at[pl.ds(0, 2 * tm)], buf.at[slot], sem.at[slot]).wait()

    r = r_ref[...]
    y = _from_row_tiles(buf[slot, 0:tm]) * r[:, 2:3] + _from_row_tiles(buf[slot, tm:2 * tm]) * r[:, 3:4]
    h = h_ref[...] + y
    ms = jnp.mean(h * h, axis=-1, keepdims=True)
    o_ref[...] = h * lax.rsqrt(ms + EPS) * nw_ref[...]


def _combine(dest0, dest1, h1, r, nw, ys, *, tm):
    n, d = h1.shape
    n_steps = n // tm
    cur = pl.BlockSpec((tm,), lambda i: (i,), memory_space=pltpu.SMEM)
    nxt = pl.BlockSpec((tm,), lambda i: (jnp.minimum(i + 1, n_steps - 1),), memory_space=pltpu.SMEM)
    return pl.pallas_call(
        _combine_kernel,
        grid=(n_steps,),
        in_specs=[
            cur, cur, nxt, nxt,
            pl.BlockSpec((tm, d), lambda i: (i, 0)),
            pl.BlockSpec((tm, r.shape[1]), lambda i: (i, 0)),
            pl.BlockSpec((1, d), lambda i: (0, 0)),
            pl.BlockSpec(memory_space=pl.ANY),
        ],
        out_specs=pl.BlockSpec((tm, d), lambda i: (i, 0)),
        out_shape=jax.ShapeDtypeStruct((n, d), F32),
        scratch_shapes=[pltpu.VMEM((2, 2 * tm) + ys.shape[1:], BF16), pltpu.SemaphoreType.DMA((2,))],
        compiler_params=pltpu.CompilerParams(dimension_semantics=("arbitrary",), vmem_limit_bytes=VMEM_LIMIT,
                                             disable_bounds_checks=True),
        name="combine",
    )(dest0, dest1, dest0, dest1, h1, r, nw, ys)


def _layer(h, positions, norm_mix_w, w_in, gk_up, gk_bias, gla_norm_w, w_br_gla, ret_norm_w, w_br_ret, w_out,
           norm_ffn_w, rg_w, rg_b, re_w, re_b, wg, wu, wd, norm_final_w, *, tm_in, tn_in, gla_rows, tm_mix,
           sub_mix, tm_comb, disp_chunk):
    batch, seq, d = h.shape
    n = batch * seq
    x2 = h.reshape(n, d)
    gqk, gv = GLA_HEADS * GLA_DK, GLA_HEADS * GLA_DV
    rqk, rv = RET_HEADS * RET_DK, RET_HEADS * RET_DV

    sizes = (gqk, gqk, gv, gv, GLA_GATE_RANK, rqk, rqk, rv, rv, d, d)
    offs = np.concatenate([[0], np.cumsum(sizes)])
    seg = lambda i: w_in[:, offs[i]:offs[i + 1]]
    perm = np.concatenate([np.concatenate([np.arange(0, RET_DK, 2), np.arange(1, RET_DK, 2)]) + hh * RET_DK
                           for hh in range(RET_HEADS)])
    w_main = jnp.concatenate([seg(7), seg(8), seg(2), seg(3), seg(5)[:, perm], seg(6)[:, perm], seg(9), seg(10),
                              seg(0), seg(1)], axis=1).astype(BF16)
    assert rv % gv == 0 and gv == rqk == d and rv == 2 * d and gqk * 2 == d
    ret_cols = (4 + 2, 4 + 3, 0, 1)
    gla_cols = (2 * (4 + 6), 2 * (4 + 6) + 1, 4, 5)
    mix_cols = (4 + 4, 4 + 5)
    w_gd = jnp.pad(seg(4), ((0, 0), (0, LANES - GLA_GATE_RANK))).astype(BF16)

    proj, gd = _inproj(x2, norm_mix_w.reshape(1, d), w_main, w_gd, tm=tm_in, tn=tn_in)

    up = jnp.pad(gk_up, ((0, LANES - GLA_GATE_RANK), (0, 0)))
    up_hi = up.astype(BF16)
    up2 = jnp.stack([up_hi, (up - up_hi.astype(F32)).astype(BF16)])
    o_gla = _gla(proj, gd, up2, gk_bias.reshape(1, gqk), gla_norm_w.reshape(1, GLA_DV),
                 batch=batch, seq=seq, rows=gla_rows, cols=gla_cols)

    theta = (1.0 / (ROPE_BASE ** jnp.linspace(0.0, 1.0, RET_DK // 2, dtype=F32))).reshape(1, RET_DK // 2)
    o_ret = _retention(proj, positions.reshape(n, 1), theta, ret_norm_w.reshape(1, rv),
                       batch=batch, seq=seq, cols=ret_cols)

    wr = jnp.concatenate([re_w.transpose(1, 0, 2).reshape(d, N_EXPERTS), rg_w], axis=1)
    wr = jnp.pad(wr, ((0, 0), (0, LANES - N_EXPERTS - N_GROUPS)))
    wr_hi = wr.astype(BF16)
    wr2 = jnp.stack([wr_hi, (wr - wr_hi.astype(F32)).astype(BF16)])
    br = jnp.pad(jnp.concatenate([re_b.reshape(-1), rg_b]), (0, LANES - N_EXPERTS - N_GROUPS)).reshape(1, LANES)
    h1, t, r, rt, cnt = _mix(o_gla, o_ret, proj, x2, w_br_gla.astype(BF16), w_br_ret.astype(BF16),
                             w_out.astype(BF16), norm_ffn_w.reshape(1, d), wr2, br, tm=tm_mix, sub=sub_mix,
                             cols=mix_cols)

    blk = EXPERT_ROWS
    n_assign = 2 * n
    n_slots = -(-(n_assign + N_EXPERTS * (blk - 1)) // blk) * blk
    counts = cnt[0, :N_EXPERTS].astype(jnp.int32)
    padded = ((counts + blk - 1) // blk) * blk
    pad_end = jnp.cumsum(padded)
    pad_start = pad_end - padded
    dest0 = pad_start[rt[0].astype(jnp.int32)] + rt[4].astype(jnp.int32)
    dest1 = pad_start[rt[1].astype(jnp.int32)] + rt[5].astype(jnp.int32)
    block_start = jnp.arange(n_slots // blk, dtype=jnp.int32) * blk
    block_expert = jnp.minimum(jnp.sum((pad_end[None, :] <= block_start[:, None]).astype(jnp.int32), axis=1),
                               N_EXPERTS - 1)
    n_used = (pad_end[-1:] // blk).astype(jnp.int32)
    spare = pad_end[-1] + jnp.arange(N_EXPERTS, dtype=jnp.int32) * blk
    tail = jnp.concatenate([jnp.maximum(pad_end - blk, 0), jnp.minimum(spare, n_slots - blk)]).astype(jnp.int32)
    has = jnp.concatenate([counts, (spare < n_slots).astype(jnp.int32)])

    xs = _dispatch(t, dest0, dest1, tail, has, n_slots=n_slots, chunk=disp_chunk)
    ys = _experts(xs, block_expert, n_used, wg, wu, wd)
    out = _combine(dest0, dest1, h1, r, norm_final_w.reshape(1, d), ys, tm=tm_comb)
    return out.reshape(batch, seq, d)


def kernel(x, positions, norm_mix_w, w_in, gla_gk_up, gla_gk_bias, gla_norm_w, w_branch_gla, ret_norm_w,
           w_branch_ret, w_out, norm_ffn_w, router_group_w, router_group_b, router_expert_w, router_expert_b,
           expert_w_gate, expert_w_up, expert_w_down, norm_final_w):
    assert norm_mix_w.shape[0] == 1, "single-layer block"
    return _layer(x, positions, norm_mix_w[0], w_in[0], gla_gk_up[0], gla_gk_bias[0], gla_norm_w[0], w_branch_gla[0],
                  ret_norm_w[0], w_branch_ret[0], w_out[0], norm_ffn_w[0], router_group_w[0], router_group_b[0],
                  router_expert_w[0], router_expert_b[0], expert_w_gate[0], expert_w_up[0], expert_w_down[0],
                  norm_final_w, tm_in=2048, tn_in=1024, gla_rows=256, tm_mix=512, sub_mix=512, tm_comb=256,
                  disp_chunk=1024)
```

```python
import functools
import math

import jax
import jax.numpy as jnp
import numpy as np
from jax import lax
from jax.experimental import pallas as pl
from jax.experimental.pallas import tpu as pltpu

F32 = jnp.float32
BF16 = jnp.bfloat16

EPS = 1e-6
GLA_HEADS = 4
GLA_DK = 128
GLA_DV = 256
GLA_GATE_RANK = 16
GLA_GATE_TEMP = 16.0
RET_HEADS = 4
RET_DK = 256
RET_DV = 512
ROPE_BASE = 10000.0
N_GROUPS = 4
EXPERTS_PER_GROUP = 8
N_EXPERTS = N_GROUPS * EXPERTS_PER_GROUP
EXPERT_HIDDEN = 512

LANES = 128
GLA_CHUNK = 128
GLA_DIAG = 8
RET_CHUNK = 256
EXPERT_ROWS = 256
VMEM_LIMIT = 48 * 1024 * 1024


def _dot(a, b):
    return jnp.dot(a, b, preferred_element_type=F32)


def _dot_nt(a, b):
    return lax.dot_general(a, b, (((1,), (1,)), ((), ())), preferred_element_type=F32)


def _split2(a):
    hi = a.astype(BF16)
    lo = (a - hi.astype(F32)).astype(BF16)
    return hi, lo


def _params(sem, vmem=VMEM_LIMIT):
    return pltpu.CompilerParams(dimension_semantics=sem, vmem_limit_bytes=vmem)


def _inproj_kernel(x_ref, nw_ref, w_ref, wgd_ref, proj_ref, gd_ref, u_scr):
    @pl.when(pl.program_id(1) == 0)
    def _():
        x = x_ref[...]
        ms = jnp.mean(x * x, axis=-1, keepdims=True)
        u = (x * lax.rsqrt(ms + EPS) * nw_ref[...]).astype(BF16)
        u_scr[...] = u
        gd_ref[...] = _dot(u, wgd_ref[...])

    proj_ref[...] = _dot(u_scr[...], w_ref[...]).astype(BF16)


def _inproj(x2, nw, w_main, w_gd, *, tm, tn):
    n, d = x2.shape
    p = w_main.shape[1]
    return pl.pallas_call(
        _inproj_kernel,
        grid=(n // tm, p // tn),
        in_specs=[
            pl.BlockSpec((tm, d), lambda i, j: (i, 0)),
            pl.BlockSpec((1, d), lambda i, j: (0, 0)),
            pl.BlockSpec((d, tn), lambda i, j: (0, j)),
            pl.BlockSpec((d, LANES), lambda i, j: (0, 0)),
        ],
        out_specs=[
            pl.BlockSpec((tm, tn), lambda i, j: (i, j)),
            pl.BlockSpec((tm, LANES), lambda i, j: (i, 0)),
        ],
        out_shape=[
            jax.ShapeDtypeStruct((n, p), BF16),
            jax.ShapeDtypeStruct((n, LANES), F32),
        ],
        scratch_shapes=[pltpu.VMEM((tm, d), BF16)],
        compiler_params=_params(("arbitrary", "arbitrary")),
        name="inproj",
    )(x2, nw, w_main, w_gd)


def _gla_tables(c):
    levels = []
    s = c // 2
    while s >= GLA_DIAG:
        levels.append(s)
        s //= 2
    i = np.arange(c)[:, None]
    t = np.arange(c)[None, :]
    mats = [t <= i, t > i]
    masks = []
    for s in levels:
        bs = (i // s) * s
        mats.append((t > bs) & (t <= i))
        mats.append((t > i) & (t <= np.minimum(bs + s, c - 1)))
        masks.append(((i // (2 * s)) == (t // (2 * s))) & (((i // s) % 2) == 1) & (((t // s) % 2) == 0))
    for d in range(1, GLA_DIAG):
        mats.append((t > i - d) & (t <= i))
    for d in range(GLA_DIAG):
        masks.append((t == i - d) & ((i % GLA_DIAG) >= d))
    g = np.concatenate(mats, 0).astype(np.float32)
    m = np.stack(masks, 0).astype(np.float32)
    return g, m, tuple(levels)


def _gla_kernel(q_ref, k_ref, v_ref, gg_ref, gd_ref, up_ref, bias_ref, nw_ref, g_ref, m_ref,
                o_ref, st_ref, *, c, nlev):
    h_n, dk, dv = GLA_HEADS, GLA_DK, GLA_DV
    rows = q_ref.shape[0]

    @pl.when(pl.program_id(1) == 0)
    def _():
        st_ref[...] = jnp.zeros_like(st_ref)

    g_tab = g_ref[...]
    up_hi = up_ref[0]
    up_lo = up_ref[1]
    q_scale = dk ** -0.5
    band0 = 2 + 2 * nlev

    def chunk(ci, carry):
        r0 = pl.multiple_of(ci * c, c)
        rs = pl.ds(r0, c)
        gd_hi, gd_lo = _split2(gd_ref[rs, :])
        xg = _dot(gd_hi, up_hi) + _dot(gd_hi, up_lo) + _dot(gd_lo, up_hi) + bias_ref[...]
        la = (jnp.minimum(xg, 0.0) - jnp.log1p(jnp.exp(-jnp.abs(xg)))) * (1.0 / GLA_GATE_TEMP)
        la_hi, la_lo = _split2(la)
        ex = jnp.exp(_dot(g_tab, la_hi) + _dot(g_tab, la_lo))

        for h in range(h_n):
            ks = slice(h * dk, (h + 1) * dk)
            vs = slice(h * dv, (h + 1) * dv)

            def tab(m, ks=ks):
                return ex[m * c:(m + 1) * c, ks]

            qh = q_ref[rs, ks].astype(F32) * q_scale
            kh = k_ref[rs, ks].astype(F32)
            vh = v_ref[rs, vs]
            st = st_ref[h]
            eb = tab(0)
            o = _dot_nt((qh * eb).astype(BF16), st.astype(BF16))
            a = m_ref[nlev] * jnp.sum(qh * kh, axis=-1, keepdims=True)
            for li in range(nlev):
                qs = (qh * tab(2 + 2 * li)).astype(BF16)
                kk = (kh * tab(3 + 2 * li)).astype(BF16)
                a = a + m_ref[li] * _dot_nt(qs, kk)
            for d in range(1, GLA_DIAG):
                kr = pltpu.roll(kh, d, 0)
                dd = jnp.sum(qh * kr * tab(band0 + d - 1), axis=-1, keepdims=True)
                a = a + m_ref[nlev + d] * dd
            o = o + _dot(a.astype(BF16), vh)
            e_last = eb[c - 1:c, :]
            k_st = (kh * tab(1)).astype(BF16)
            v_t = vh.astype(F32).T.astype(BF16)
            st_ref[h] = st * e_last + _dot(v_t, k_st)

            ms = jnp.mean(o * o, axis=-1, keepdims=True)
            y = o * lax.rsqrt(ms + EPS) * nw_ref[...]
            gg = gg_ref[rs, vs].astype(F32)
            o_ref[rs, vs] = (y * (gg * jax.nn.sigmoid(gg))).astype(BF16)
        return carry

    lax.fori_loop(0, rows // c, chunk, 0, unroll=2)


def _gla(proj, gd, up2, bias, nw, *, batch, seq, rows, cols):
    n = proj.shape[0]
    c = GLA_CHUNK
    g_np, m_np, levels = _gla_tables(c)
    g_tab = jnp.asarray(g_np, BF16)
    m_tab = jnp.asarray(m_np, F32)
    qk_w = GLA_HEADS * GLA_DK
    v_w = GLA_HEADS * GLA_DV
    spb = seq // rows
    row = lambda b, s: b * spb + s
    kern = functools.partial(_gla_kernel, c=c, nlev=len(levels))
    return pl.pallas_call(
        kern,
        grid=(batch, spb),
        in_specs=[
            pl.BlockSpec((rows, qk_w), lambda b, s: (row(b, s), cols[0])),
            pl.BlockSpec((rows, qk_w), lambda b, s: (row(b, s), cols[1])),
            pl.BlockSpec((rows, v_w), lambda b, s: (row(b, s), cols[2])),
            pl.BlockSpec((rows, v_w), lambda b, s: (row(b, s), cols[3])),
            pl.BlockSpec((rows, LANES), lambda b, s: (row(b, s), 0)),
            pl.BlockSpec(up2.shape, lambda b, s: (0, 0, 0)),
            pl.BlockSpec(bias.shape, lambda b, s: (0, 0)),
            pl.BlockSpec(nw.shape, lambda b, s: (0, 0)),
            pl.BlockSpec(g_tab.shape, lambda b, s: (0, 0)),
            pl.BlockSpec(m_tab.shape, lambda b, s: (0, 0, 0)),
        ],
        out_specs=pl.BlockSpec((rows, v_w), lambda b, s: (row(b, s), 0)),
        out_shape=jax.ShapeDtypeStruct((n, v_w), BF16),
        scratch_shapes=[pltpu.VMEM((GLA_HEADS, GLA_DV, GLA_DK), F32)],
        compiler_params=_params(("arbitrary", "arbitrary")),
        name="gla",
    )(proj, proj, proj, proj, gd, up2, bias, nw, g_tab, m_tab)


def _ret_log_gamma(h):
    return math.log(1.0 - 2.0 ** (-5.0 - h))


def _ret_kernel(pos_ref, th_ref, q_ref, k_ref, v_ref, rg_ref, nw_ref, o_ref, s_ref, dm_ref):
    h_n, dk, dv = RET_HEADS, RET_DK, RET_DV
    c = q_ref.shape[0]
    half = dk // 2

    @pl.when(pl.program_id(1) == 0)
    def _():
        s_ref[...] = jnp.zeros_like(s_ref)

    @pl.when((pl.program_id(0) == 0) & (pl.program_id(1) == 0))
    def _():
        rel = (lax.broadcasted_iota(jnp.int32, (c, c), 0) - lax.broadcasted_iota(jnp.int32, (c, c), 1)).astype(F32)
        for h in range(h_n):
            dm_ref[h] = jnp.where(rel >= 0, jnp.exp(_ret_log_gamma(h) * jnp.maximum(rel, 0.0)), 0.0)

    ang = pos_ref[...].astype(F32) * th_ref[...]
    cs = jnp.cos(ang)
    sn = jnp.sin(ang)
    idx = lax.broadcasted_iota(jnp.int32, (c, 1), 0).astype(F32)
    k_scale = dk ** -0.5

    def rot(ref, h):
        a = ref[:, h * dk:h * dk + half].astype(F32)
        b = ref[:, h * dk + half:(h + 1) * dk].astype(F32)
        return jnp.concatenate([a * cs - b * sn, b * cs + a * sn], axis=-1)

    for h in range(h_n):
        lg = _ret_log_gamma(h)
        vs = slice(h * dv, (h + 1) * dv)
        qr = rot(q_ref, h)
        kr = rot(k_ref, h) * k_scale
        vh = v_ref[:, vs]
        scores = _dot_nt(qr.astype(BF16), kr.astype(BF16)) * dm_ref[h]
        q_dec = jnp.exp(lg * (idx + 1.0))
        k_dec = jnp.exp(lg * (c - 1.0 - idx))
        s = s_ref[h]
        o = _dot(scores.astype(BF16), vh) + _dot((qr * q_dec).astype(BF16), s.astype(BF16))
        k_t = (kr * k_dec).T.astype(BF16)
        s_ref[h] = math.exp(lg * c) * s + _dot(k_t, vh)

        mu = jnp.mean(o, axis=-1, keepdims=True)
        oc = o - mu
        var = jnp.mean(oc * oc, axis=-1, keepdims=True)
        y = oc * lax.rsqrt(var + EPS) * nw_ref[:, vs]
        rg = rg_ref[:, vs].astype(F32)
        o_ref[:, vs] = (y * (rg * jax.nn.sigmoid(rg))).astype(BF16)


def _retention(proj, pos, theta, nw, *, batch, seq, cols):
    n = proj.shape[0]
    c = RET_CHUNK
    qk_w = RET_HEADS * RET_DK
    v_w = RET_HEADS * RET_DV
    spb = seq // c
    row = lambda b, s: b * spb + s
    return pl.pallas_call(
        _ret_kernel,
        grid=(batch, spb),
        in_specs=[
            pl.BlockSpec((c, 1), lambda b, s: (row(b, s), 0)),
            pl.BlockSpec(theta.shape, lambda b, s: (0, 0)),
            pl.BlockSpec((c, qk_w), lambda b, s: (row(b, s), cols[0])),
            pl.BlockSpec((c, qk_w), lambda b, s: (row(b, s), cols[1])),
            pl.BlockSpec((c, v_w), lambda b, s: (row(b, s), cols[2])),
            pl.BlockSpec((c, v_w), lambda b, s: (row(b, s), cols[3])),
            pl.BlockSpec(nw.shape, lambda b, s: (0, 0)),
        ],
        out_specs=pl.BlockSpec((c, v_w), lambda b, s: (row(b, s), 0)),
        out_shape=jax.ShapeDtypeStruct((n, v_w), BF16),
        scratch_shapes=[
            pltpu.VMEM((RET_HEADS, RET_DK, RET_DV), F32),
            pltpu.VMEM((RET_HEADS, c, c), F32),
        ],
        compiler_params=_params(("arbitrary", "arbitrary")),
        name="retention",
    )(pos, theta, proj, proj, proj, proj, nw)


def _mix_kernel(og_ref, or_ref, ma_ref, mb_ref, x_ref, wa_ref, wb_ref, wo_ref, nw_ref, wr_ref, br_ref,
                tri_ref, h_ref, t_ref, r_ref, rt_ref, cnt_ref, base_ref, tprev_ref):
    i = pl.program_id(0)
    tm = x_ref.shape[0]

    @pl.when(i == 0)
    def _():
        base_ref[...] = jnp.zeros_like(base_ref)
        tprev_ref[...] = jnp.zeros_like(tprev_ref)

    lane = lax.broadcasted_iota(jnp.int32, (tm, LANES), 1)
    neg = jnp.float32(-1e30)
    big = jnp.int32(1 << 20)

    def first_max(v):
        m = jnp.max(v, axis=-1, keepdims=True)
        return m, jnp.min(jnp.where(v == m, lane, big), axis=-1, keepdims=True)

    t_hi, t_lo = _split2(tprev_ref[...])
    lg = _dot(t_hi, wr_ref[0]) + _dot(t_hi, wr_ref[1]) + _dot(t_lo, wr_ref[0]) + br_ref[...]
    g_valid = (lane >= N_EXPERTS) & (lane < N_EXPERTS + N_GROUPS)
    g_m, g_lane = first_max(jnp.where(g_valid, lg, neg))
    g_w = 1.0 / jnp.sum(jnp.where(g_valid, jnp.exp(lg - g_m), 0.0), axis=-1, keepdims=True)
    g_idx = g_lane - N_EXPERTS
    e_valid = (lane < N_EXPERTS) & ((lane // EXPERTS_PER_GROUP) == g_idx)
    el = jnp.where(e_valid, lg, neg)
    v1, i1 = first_max(el)
    v2, i2 = first_max(jnp.where(lane == i1, neg, el))
    e21 = jnp.exp(v2 - v1)
    w1 = g_w / (1.0 + e21)
    w2 = g_w * e21 / (1.0 + e21)

    o1 = lane == i1
    o2 = lane == i2
    live = jnp.where(i > 0, 1.0, 0.0)
    osum = jnp.where(o1 | o2, live, 0.0)
    base = base_ref[0:1, :]
    before = _dot(tri_ref[...], osum.astype(BF16)) + base
    r1 = jnp.sum(jnp.where(o1, before, 0.0), axis=-1, keepdims=True)
    r2 = jnp.sum(jnp.where(o2, before, 0.0), axis=-1, keepdims=True)
    base = base + jnp.sum(osum, axis=0, keepdims=True)
    base_ref[...] = jnp.broadcast_to(base, base_ref.shape)
    cnt_ref[...] = jnp.broadcast_to(base, cnt_ref.shape)

    cols = (i1.astype(F32), i2.astype(F32), w1, w2, r1, r2)
    wide = jnp.zeros((tm, LANES), F32)
    for ci, col in enumerate(cols):
        wide = jnp.where(lane == ci, col, wide)
    r_ref[...] = wide[:, :r_ref.shape[1]]
    rt_ref[...] = wide.T[:rt_ref.shape[0], :]

    ya = _dot(og_ref[...], wa_ref[...])
    yb = _dot(or_ref[...], wb_ref[...])
    merged = (jax.nn.sigmoid(ma_ref[...].astype(F32)) * ya + jax.nn.sigmoid(mb_ref[...].astype(F32)) * yb)
    h = x_ref[...] + _dot(merged.astype(BF16), wo_ref[...])
    h_ref[...] = h
    ms = jnp.mean(h * h, axis=-1, keepdims=True)
    t = h * lax.rsqrt(ms + EPS) * nw_ref[...]
    t_ref[...] = _to_row_tiles(t)
    tprev_ref[...] = t


def _mix(o_gla, o_ret, proj, x2, wa, wb, wo, nw, wr2, br, *, tm, cols):
    n, d = x2.shape
    nt = n // tm
    tri = jnp.asarray(np.tril(np.ones((tm, tm), np.float32), -1), BF16)
    const = lambda shape: pl.BlockSpec(shape, lambda i: (0,) * len(shape))
    mixed = lambda i: jnp.minimum(i, nt - 1)
    routed = lambda i: jnp.maximum(i - 1, 0)
    return pl.pallas_call(
        _mix_kernel,
        grid=(nt + 1,),
        in_specs=[
            pl.BlockSpec((tm, o_gla.shape[1]), lambda i: (mixed(i), 0)),
            pl.BlockSpec((tm, o_ret.shape[1]), lambda i: (mixed(i), 0)),
            pl.BlockSpec((tm, d), lambda i: (mixed(i), cols[0])),
            pl.BlockSpec((tm, d), lambda i: (mixed(i), cols[1])),
            pl.BlockSpec((tm, d), lambda i: (mixed(i), 0)),
            const(wa.shape), const(wb.shape), const(wo.shape), const(nw.shape),
            const(wr2.shape), const(br.shape), const(tri.shape),
        ],
        out_specs=[
            pl.BlockSpec((tm, d), lambda i: (mixed(i), 0)),
            pl.BlockSpec((tm, d // LANES, LANES), lambda i: (mixed(i), 0, 0)),
            pl.BlockSpec((tm, 8), lambda i: (routed(i), 0)),
            pl.BlockSpec((8, tm), lambda i: (0, routed(i))),
            pl.BlockSpec((8, LANES), lambda i: (0, 0)),
        ],
        out_shape=[
            jax.ShapeDtypeStruct((n, d), F32),
            jax.ShapeDtypeStruct((n, d // LANES, LANES), BF16),
            jax.ShapeDtypeStruct((n, 8), F32),
            jax.ShapeDtypeStruct((8, n), F32),
            jax.ShapeDtypeStruct((8, LANES), F32),
        ],
        scratch_shapes=[pltpu.VMEM((8, LANES), F32), pltpu.VMEM((tm, d), F32)],
        compiler_params=_params(("arbitrary",)),
        name="mix_router",
    )(o_gla, o_ret, proj, proj, x2, wa, wb, wo, nw, wr2, br, tri)


def _to_row_tiles(x):
    return x.reshape(x.shape[0], x.shape[1] // LANES, LANES).astype(BF16)


def _from_row_tiles(x3):
    return x3.astype(F32).reshape(x3.shape[0], x3.shape[1] * x3.shape[2])


def _dispatch_kernel(tail_ref, has_ref, d0_ref, d1_ref, t_ref, xs_hbm, zero_ref, sem, zsem):
    step = pl.program_id(0)
    tok_n = t_ref.shape[0]
    blk = zero_ref.shape[0]

    def tail_copy(e):
        return pltpu.make_async_copy(zero_ref, xs_hbm.at[pl.ds(tail_ref[e], blk)], zsem)

    @pl.when(step == 0)
    def _():
        zero_ref[...] = jnp.zeros_like(zero_ref)
        for e in range(tail_ref.shape[0]):
            @pl.when(has_ref[e] > 0)
            def _(e=e):
                tail_copy(e).start()
        for e in range(tail_ref.shape[0]):
            @pl.when(has_ref[e] > 0)
            def _(e=e):
                tail_copy(e).wait()

    def row_copy(a, dst):
        return pltpu.make_async_copy(t_ref.at[a], xs_hbm.at[dst], sem)

    def body(a, carry):
        row_copy(a, d0_ref[a]).start()
        row_copy(a, d1_ref[a]).start()
        return carry

    lax.fori_loop(0, tok_n, body, 0, unroll=8)

    for _ in range(2):
        pltpu.make_async_copy(t_ref, xs_hbm.at[pl.ds(0, tok_n)], sem).wait()


def _dispatch(t3, dest0, dest1, tail, has, *, n_slots, chunk):
    n, s, l = t3.shape
    return pl.pallas_call(
        _dispatch_kernel,
        grid_spec=pltpu.PrefetchScalarGridSpec(
            num_scalar_prefetch=2,
            grid=(n // chunk,),
            in_specs=[
                pl.BlockSpec((chunk,), lambda i, *_: (i,), memory_space=pltpu.SMEM),
                pl.BlockSpec((chunk,), lambda i, *_: (i,), memory_space=pltpu.SMEM),
                pl.BlockSpec((chunk, s, l), lambda i, *_: (i, 0, 0)),
            ],
            out_specs=pl.BlockSpec(memory_space=pl.ANY),
            scratch_shapes=[
                pltpu.VMEM((EXPERT_ROWS, s, l), BF16),
                pltpu.SemaphoreType.DMA,
                pltpu.SemaphoreType.DMA,
            ],
        ),
        out_shape=jax.ShapeDtypeStruct((n_slots, s, l), BF16),
        compiler_params=pltpu.CompilerParams(dimension_semantics=("arbitrary",), has_side_effects=True,
                                             vmem_limit_bytes=VMEM_LIMIT, disable_bounds_checks=True),
        name="dispatch",
    )(tail, has, dest0, dest1, t3)


def _expert_kernel(be_ref, nu_ref, xs_ref, wg_ref, wu_ref, wd_ref, ys_ref, wg_s, wu_s, wd_s):
    b = pl.program_id(0)

    @pl.when((b == 0) | (be_ref[b] != be_ref[jnp.maximum(b - 1, 0)]))
    def _():
        wg_s[...] = wg_ref[0].astype(BF16)
        wu_s[...] = wu_ref[0].astype(BF16)
        wd_s[...] = wd_ref[0].astype(BF16)

    @pl.when(b < nu_ref[0])
    def _():
        x = _from_row_tiles(xs_ref[...]).astype(BF16)
        g = _dot(x, wg_s[...])
        u = _dot(x, wu_s[...])
        hid = (g * jax.nn.sigmoid(g) * u).astype(BF16)
        ys_ref[...] = _to_row_tiles(_dot(hid, wd_s[...]))

    @pl.when(b >= nu_ref[0])
    def _():
        ys_ref[...] = jnp.zeros_like(ys_ref)


def _experts(xs3, block_expert, n_used, wg, wu, wd):
    n_slots, s, l = xs3.shape
    d = s * l
    blk = EXPERT_ROWS
    hid = wg.shape[2]
    return pl.pallas_call(
        _expert_kernel,
        grid_spec=pltpu.PrefetchScalarGridSpec(
            num_scalar_prefetch=2,
            grid=(n_slots // blk,),
            in_specs=[
                pl.BlockSpec((blk, s, l), lambda b, be, nu: (jnp.minimum(b, nu[0] - 1), 0, 0)),
                pl.BlockSpec((1, d, hid), lambda b, be, nu: (be[b], 0, 0)),
                pl.BlockSpec((1, d, hid), lambda b, be, nu: (be[b], 0, 0)),
                pl.BlockSpec((1, hid, d), lambda b, be, nu: (be[b], 0, 0)),
            ],
            out_specs=pl.BlockSpec((blk, s, l), lambda b, be, nu: (b, 0, 0)),
            scratch_shapes=[pltpu.VMEM((d, hid), BF16), pltpu.VMEM((d, hid), BF16), pltpu.VMEM((hid, d), BF16)],
        ),
        out_shape=jax.ShapeDtypeStruct((n_slots, s, l), BF16),
        compiler_params=_params(("arbitrary",)),
        name="experts",
    )(block_expert, n_used, xs3, wg, wu, wd)


def _combine_kernel(c0_ref, c1_ref, n0_ref, n1_ref, h_ref, r_ref, nw_ref, ys_hbm, o_ref, buf, sem):
    i = pl.program_id(0)
    n_steps = pl.num_programs(0)
    tm = h_ref.shape[0]

    def row_copy(dst_row, src_row, slot):
        return pltpu.make_async_copy(ys_hbm.at[src_row], buf.at[slot, dst_row], sem.at[slot])

    def issue(d0, d1, slot):
        def body(a, carry):
            row_copy(a, d0[a], slot).start()
            row_copy(tm + a, d1[a], slot).start()
            return carry
        lax.fori_loop(0, tm, body, 0, unroll=8)

    @pl.when(i == 0)
    def _():
        issue(c0_ref, c1_ref, 0)

    @pl.when(i + 1 < n_steps)
    def _():
        issue(n0_ref, n1_ref, (i + 1) % 2)

    slot = i % 2
    pltpu.make_async_copy(ys_hbm.at[pl.ds(0, 2 * tm)], buf.at[slot], sem.at[slot]).wait()

    r = r_ref[...]
    y = _from_row_tiles(buf[slot, 0:tm]) * r[:, 2:3] + _from_row_tiles(buf[slot, tm:2 * tm]) * r[:, 3:4]
    h = h_ref[...] + y
    ms = jnp.mean(h * h, axis=-1, keepdims=True)
    o_ref[...] = h * lax.rsqrt(ms + EPS) * nw_ref[...]


def _combine(dest0, dest1, h1, r, nw, ys, *, tm):
    n, d = h1.shape
    n_steps = n // tm
    cur = pl.BlockSpec((tm,), lambda i: (i,), memory_space=pltpu.SMEM)
    nxt = pl.BlockSpec((tm,), lambda i: (jnp.minimum(i + 1, n_steps - 1),), memory_space=pltpu.SMEM)
    return pl.pallas_call(
        _combine_kernel,
        grid=(n_steps,),
        in_specs=[
            cur, cur, nxt, nxt,
            pl.BlockSpec((tm, d), lambda i: (i, 0)),
            pl.BlockSpec((tm, r.shape[1]), lambda i: (i, 0)),
            pl.BlockSpec((1, d), lambda i: (0, 0)),
            pl.BlockSpec(memory_space=pl.ANY),
        ],
        out_specs=pl.BlockSpec((tm, d), lambda i: (i, 0)),
        out_shape=jax.ShapeDtypeStruct((n, d), F32),
        scratch_shapes=[pltpu.VMEM((2, 2 * tm) + ys.shape[1:], BF16), pltpu.SemaphoreType.DMA((2,))],
        compiler_params=pltpu.CompilerParams(dimension_semantics=("arbitrary",), vmem_limit_bytes=VMEM_LIMIT,
                                             disable_bounds_checks=True),
        name="combine",
    )(dest0, dest1, dest0, dest1, h1, r, nw, ys)


def _layer(h, positions, norm_mix_w, w_in, gk_up, gk_bias, gla_norm_w, w_br_gla, ret_norm_w, w_br_ret, w_out,
           norm_ffn_w, rg_w, rg_b, re_w, re_b, wg, wu, wd, norm_final_w, *, tm_in, tn_in, gla_rows, tm_mix,
           tm_comb, disp_chunk):
    batch, seq, d = h.shape
    n = batch * seq
    x2 = h.reshape(n, d)
    gqk, gv = GLA_HEADS * GLA_DK, GLA_HEADS * GLA_DV
    rqk, rv = RET_HEADS * RET_DK, RET_HEADS * RET_DV

    sizes = (gqk, gqk, gv, gv, GLA_GATE_RANK, rqk, rqk, rv, rv, d, d)
    offs = np.concatenate([[0], np.cumsum(sizes)])
    seg = lambda i: w_in[:, offs[i]:offs[i + 1]]
    perm = np.concatenate([np.concatenate([np.arange(0, RET_DK, 2), np.arange(1, RET_DK, 2)]) + hh * RET_DK
                           for hh in range(RET_HEADS)])
    w_main = jnp.concatenate([seg(7), seg(8), seg(2), seg(3), seg(5)[:, perm], seg(6)[:, perm], seg(9), seg(10),
                              seg(0), seg(1)], axis=1).astype(BF16)
    assert rv % gv == 0 and gv == rqk == d and rv == 2 * d and gqk * 2 == d
    ret_cols = (4 + 2, 4 + 3, 0, 1)
    gla_cols = (2 * (4 + 6), 2 * (4 + 6) + 1, 4, 5)
    mix_cols = (4 + 4, 4 + 5)
    w_gd = jnp.pad(seg(4), ((0, 0), (0, LANES - GLA_GATE_RANK))).astype(BF16)

    proj, gd = _inproj(x2, norm_mix_w.reshape(1, d), w_main, w_gd, tm=tm_in, tn=tn_in)

    up = jnp.pad(gk_up, ((0, LANES - GLA_GATE_RANK), (0, 0)))
    up_hi = up.astype(BF16)
    up2 = jnp.stack([up_hi, (up - up_hi.astype(F32)).astype(BF16)])
    o_gla = _gla(proj, gd, up2, gk_bias.reshape(1, gqk), gla_norm_w.reshape(1, GLA_DV),
                 batch=batch, seq=seq, rows=gla_rows, cols=gla_cols)

    theta = (1.0 / (ROPE_BASE ** jnp.linspace(0.0, 1.0, RET_DK // 2, dtype=F32))).reshape(1, RET_DK // 2)
    o_ret = _retention(proj, positions.reshape(n, 1), theta, ret_norm_w.reshape(1, rv),
                       batch=batch, seq=seq, cols=ret_cols)

    wr = jnp.concatenate([re_w.transpose(1, 0, 2).reshape(d, N_EXPERTS), rg_w], axis=1)
    wr = jnp.pad(wr, ((0, 0), (0, LANES - N_EXPERTS - N_GROUPS)))
    wr_hi = wr.astype(BF16)
    wr2 = jnp.stack([wr_hi, (wr - wr_hi.astype(F32)).astype(BF16)])
    br = jnp.pad(jnp.concatenate([re_b.reshape(-1), rg_b]), (0, LANES - N_EXPERTS - N_GROUPS)).reshape(1, LANES)
    h1, t, r, rt, cnt = _mix(o_gla, o_ret, proj, x2, w_br_gla.astype(BF16), w_br_ret.astype(BF16),
                             w_out.astype(BF16), norm_ffn_w.reshape(1, d), wr2, br, tm=tm_mix, cols=mix_cols)

    blk = EXPERT_ROWS
    n_assign = 2 * n
    n_slots = -(-(n_assign + N_EXPERTS * (blk - 1)) // blk) * blk
    counts = cnt[0, :N_EXPERTS].astype(jnp.int32)
    padded = ((counts + blk - 1) // blk) * blk
    pad_end = jnp.cumsum(padded)
    pad_start = pad_end - padded
    dest0 = pad_start[rt[0].astype(jnp.int32)] + rt[4].astype(jnp.int32)
    dest1 = pad_start[rt[1].astype(jnp.int32)] + rt[5].astype(jnp.int32)
    block_start = jnp.arange(n_slots // blk, dtype=jnp.int32) * blk
    block_expert = jnp.minimum(jnp.sum((pad_end[None, :] <= block_start[:, None]).astype(jnp.int32), axis=1),
                               N_EXPERTS - 1)
    n_used = (pad_end[-1:] // blk).astype(jnp.int32)
    spare = pad_end[-1] + jnp.arange(N_EXPERTS, dtype=jnp.int32) * blk
    tail = jnp.concatenate([jnp.maximum(pad_end - blk, 0), jnp.minimum(spare, n_slots - blk)]).astype(jnp.int32)
    has = jnp.concatenate([counts, (spare < n_slots).astype(jnp.int32)])

    xs = _dispatch(t, dest0, dest1, tail, has, n_slots=n_slots, chunk=disp_chunk)
    ys = _experts(xs, block_expert, n_used, wg, wu, wd)
    out = _combine(dest0, dest1, h1, r, norm_final_w.reshape(1, d), ys, tm=tm_comb)
    return out.reshape(batch, seq, d)


def kernel(x, positions, norm_mix_w, w_in, gla_gk_up, gla_gk_bias, gla_norm_w, w_branch_gla, ret_norm_w,
           w_branch_ret, w_out, norm_ffn_w, router_group_w, router_group_b, router_expert_w, router_expert_b,
           expert_w_gate, expert_w_up, expert_w_down, norm_final_w):
    assert norm_mix_w.shape[0] == 1, "single-layer block"
    return _layer(x, positions, norm_mix_w[0], w_in[0], gla_gk_up[0], gla_gk_bias[0], gla_norm_w[0], w_branch_gla[0],
                  ret_norm_w[0], w_branch_ret[0], w_out[0], norm_ffn_w[0], router_group_w[0], router_group_b[0],
                  router_expert_w[0], router_expert_b[0], expert_w_gate[0], expert_w_up[0], expert_w_down[0],
                  norm_final_w, tm_in=2048, tn_in=1024, gla_rows=256, tm_mix=512, tm_comb=256,
                  disp_chunk=1024)
```

```python
import functools
import math

import jax
import jax.numpy as jnp
import numpy as np
from jax import lax
from jax.experimental import pallas as pl
from jax.experimental.pallas import tpu as pltpu

F32 = jnp.float32
BF16 = jnp.bfloat16

EPS = 1e-6
GLA_HEADS = 4
GLA_DK = 128
GLA_DV = 256
GLA_GATE_RANK = 16
GLA_GATE_TEMP = 16.0
RET_HEADS = 4
RET_DK = 256
RET_DV = 512
ROPE_BASE = 10000.0
N_GROUPS = 4
EXPERTS_PER_GROUP = 8
N_EXPERTS = N_GROUPS * EXPERTS_PER_GROUP
EXPERT_HIDDEN = 512

LANES = 128
GLA_CHUNK = 128
GLA_DIAG = 8
RET_CHUNK = 256
EXPERT_ROWS = 256
VMEM_LIMIT = 48 * 1024 * 1024


def _dot(a, b):
    return jnp.dot(a, b, preferred_element_type=F32)


def _dot_nt(a, b):
    return lax.dot_general(a, b, (((1,), (1,)), ((), ())), preferred_element_type=F32)


def _split2(a):
    hi = a.astype(BF16)
    lo = (a - hi.astype(F32)).astype(BF16)
    return hi, lo


def _params(sem, vmem=VMEM_LIMIT):
    return pltpu.CompilerParams(dimension_semantics=sem, vmem_limit_bytes=vmem)


def _inproj_kernel(x_ref, nw_ref, w_ref, wgd_ref, proj_ref, gd_ref, u_scr):
    @pl.when(pl.program_id(1) == 0)
    def _():
        x = x_ref[...]
        ms = jnp.mean(x * x, axis=-1, keepdims=True)
        u = (x * lax.rsqrt(ms + EPS) * nw_ref[...]).astype(BF16)
        u_scr[...] = u
        gd_ref[...] = _dot(u, wgd_ref[...])

    proj_ref[...] = _dot(u_scr[...], w_ref[...]).astype(BF16)


def _inproj(x2, nw, w_main, w_gd, *, tm, tn):
    n, d = x2.shape
    p = w_main.shape[1]
    return pl.pallas_call(
        _inproj_kernel,
        grid=(n // tm, p // tn),
        in_specs=[
            pl.BlockSpec((tm, d), lambda i, j: (i, 0)),
            pl.BlockSpec((1, d), lambda i, j: (0, 0)),
            pl.BlockSpec((d, tn), lambda i, j: (0, j)),
            pl.BlockSpec((d, LANES), lambda i, j: (0, 0)),
        ],
        out_specs=[
            pl.BlockSpec((tm, tn), lambda i, j: (i, j)),
            pl.BlockSpec((tm, LANES), lambda i, j: (i, 0)),
        ],
        out_shape=[
            jax.ShapeDtypeStruct((n, p), BF16),
            jax.ShapeDtypeStruct((n, LANES), F32),
        ],
        scratch_shapes=[pltpu.VMEM((tm, d), BF16)],
        compiler_params=_params(("arbitrary", "arbitrary")),
        name="inproj",
    )(x2, nw, w_main, w_gd)


def _gla_tables(c):
    levels = []
    s = c // 2
    while s >= GLA_DIAG:
        levels.append(s)
        s //= 2
    i = np.arange(c)[:, None]
    t = np.arange(c)[None, :]
    mats = [t <= i, t > i]
    masks = []
    for s in levels:
        bs = (i // s) * s
        mats.append((t > bs) & (t <= i))
        mats.append((t > i) & (t <= np.minimum(bs + s, c - 1)))
        masks.append(((i // (2 * s)) == (t // (2 * s))) & (((i // s) % 2) == 1) & (((t // s) % 2) == 0))
    for d in range(1, GLA_DIAG):
        mats.append((t > i - d) & (t <= i))
    for d in range(GLA_DIAG):
        masks.append((t == i - d) & ((i % GLA_DIAG) >= d))
    g = np.concatenate(mats, 0).astype(np.float32)
    m = np.stack(masks, 0).astype(np.float32)
    return g, m, tuple(levels)


def _gla_kernel(q_ref, k_ref, v_ref, gg_ref, gd_ref, up_ref, bias_ref, nw_ref, g_ref, m_ref,
                o_ref, st_ref, *, c, nlev):
    h_n, dk, dv = GLA_HEADS, GLA_DK, GLA_DV
    rows = q_ref.shape[0]

    @pl.when(pl.program_id(1) == 0)
    def _():
        st_ref[...] = jnp.zeros_like(st_ref)

    g_tab = g_ref[...]
    up_hi = up_ref[0]
    up_lo = up_ref[1]
    q_scale = dk ** -0.5
    band0 = 2 + 2 * nlev

    def chunk(ci, carry):
        r0 = pl.multiple_of(ci * c, c)
        rs = pl.ds(r0, c)
        gd_hi, gd_lo = _split2(gd_ref[rs, :])
        xg = _dot(gd_hi, up_hi) + _dot(gd_hi, up_lo) + _dot(gd_lo, up_hi) + bias_ref[...]
        la = (jnp.minimum(xg, 0.0) - jnp.log1p(jnp.exp(-jnp.abs(xg)))) * (1.0 / GLA_GATE_TEMP)
        ex = jnp.exp(_dot(g_tab, la.astype(BF16)))

        for h in range(h_n):
            ks = slice(h * dk, (h + 1) * dk)
            vs = slice(h * dv, (h + 1) * dv)

            def tab(m, ks=ks):
                return ex[m * c:(m + 1) * c, ks]

            qh = q_ref[rs, ks].astype(F32) * q_scale
            kh = k_ref[rs, ks].astype(F32)
            vh = v_ref[rs, vs]
            st = st_ref[h]
            eb = tab(0)
            o = _dot_nt((qh * eb).astype(BF16), st.astype(BF16))
            a = m_ref[nlev] * jnp.sum(qh * kh, axis=-1, keepdims=True)
            for li in range(nlev):
                qs = (qh * tab(2 + 2 * li)).astype(BF16)
                kk = (kh * tab(3 + 2 * li)).astype(BF16)
                a = a + m_ref[li] * _dot_nt(qs, kk)
            for d in range(1, GLA_DIAG):
                kr = pltpu.roll(kh, d, 0)
                dd = jnp.sum(qh * kr * tab(band0 + d - 1), axis=-1, keepdims=True)
                a = a + m_ref[nlev + d] * dd
            o = o + _dot(a.astype(BF16), vh)
            e_last = eb[c - 1:c, :]
            k_st = (kh * tab(1)).astype(BF16)
            v_t = vh.astype(F32).T.astype(BF16)
            st_ref[h] = st * e_last + _dot(v_t, k_st)

            ms = jnp.mean(o * o, axis=-1, keepdims=True)
            y = o * lax.rsqrt(ms + EPS) * nw_ref[...]
            gg = gg_ref[rs, vs].astype(F32)
            o_ref[rs, vs] = (y * (gg * jax.nn.sigmoid(gg))).astype(BF16)
        return carry

    lax.fori_loop(0, rows // c, chunk, 0, unroll=2)


def _gla(proj, gd, up2, bias, nw, *, batch, seq, rows, cols):
    n = proj.shape[0]
    c = GLA_CHUNK
    g_np, m_np, levels = _gla_tables(c)
    g_tab = jnp.asarray(g_np, BF16)
    m_tab = jnp.asarray(m_np, F32)
    qk_w = GLA_HEADS * GLA_DK
    v_w = GLA_HEADS * GLA_DV
    spb = seq // rows
    row = lambda b, s: b * spb + s
    kern = functools.partial(_gla_kernel, c=c, nlev=len(levels))
    return pl.pallas_call(
        kern,
        grid=(batch, spb),
        in_specs=[
            pl.BlockSpec((rows, qk_w), lambda b, s: (row(b, s), cols[0])),
            pl.BlockSpec((rows, qk_w), lambda b, s: (row(b, s), cols[1])),
            pl.BlockSpec((rows, v_w), lambda b, s: (row(b, s), cols[2])),
            pl.BlockSpec((rows, v_w), lambda b, s: (row(b, s), cols[3])),
            pl.BlockSpec((rows, LANES), lambda b, s: (row(b, s), 0)),
            pl.BlockSpec(up2.shape, lambda b, s: (0, 0, 0)),
            pl.BlockSpec(bias.shape, lambda b, s: (0, 0)),
            pl.BlockSpec(nw.shape, lambda b, s: (0, 0)),
            pl.BlockSpec(g_tab.shape, lambda b, s: (0, 0)),
            pl.BlockSpec(m_tab.shape, lambda b, s: (0, 0, 0)),
        ],
        out_specs=pl.BlockSpec((rows, v_w), lambda b, s: (row(b, s), 0)),
        out_shape=jax.ShapeDtypeStruct((n, v_w), BF16),
        scratch_shapes=[pltpu.VMEM((GLA_HEADS, GLA_DV, GLA_DK), F32)],
        compiler_params=_params(("arbitrary", "arbitrary")),
        name="gla",
    )(proj, proj, proj, proj, gd, up2, bias, nw, g_tab, m_tab)


def _ret_log_gamma(h):
    return math.log(1.0 - 2.0 ** (-5.0 - h))


def _ret_kernel(pos_ref, th_ref, q_ref, k_ref, v_ref, rg_ref, nw_ref, o_ref, s_ref, dm_ref):
    h_n, dk, dv = RET_HEADS, RET_DK, RET_DV
    c = q_ref.shape[0]
    half = dk // 2

    @pl.when(pl.program_id(1) == 0)
    def _():
        s_ref[...] = jnp.zeros_like(s_ref)

    @pl.when((pl.program_id(0) == 0) & (pl.program_id(1) == 0))
    def _():
        rel = (lax.broadcasted_iota(jnp.int32, (c, c), 0) - lax.broadcasted_iota(jnp.int32, (c, c), 1)).astype(F32)
        for h in range(h_n):
            dm_ref[h] = jnp.where(rel >= 0, jnp.exp(_ret_log_gamma(h) * jnp.maximum(rel, 0.0)), 0.0)

    ang = pos_ref[...].astype(F32) * th_ref[...]
    cs = jnp.cos(ang)
    sn = jnp.sin(ang)
    idx = lax.broadcasted_iota(jnp.int32, (c, 1), 0).astype(F32)
    k_scale = dk ** -0.5

    def rot(ref, h):
        a = ref[:, h * dk:h * dk + half].astype(F32)
        b = ref[:, h * dk + half:(h + 1) * dk].astype(F32)
        return jnp.concatenate([a * cs - b * sn, b * cs + a * sn], axis=-1)

    for h in range(h_n):
        lg = _ret_log_gamma(h)
        vs = slice(h * dv, (h + 1) * dv)
        qr = rot(q_ref, h)
        kr = rot(k_ref, h) * k_scale
        vh = v_ref[:, vs]
        scores = _dot_nt(qr.astype(BF16), kr.astype(BF16)) * dm_ref[h]
        q_dec = jnp.exp(lg * (idx + 1.0))
        k_dec = jnp.exp(lg * (c - 1.0 - idx))
        s = s_ref[h]
        o = _dot(scores.astype(BF16), vh) + _dot((qr * q_dec).astype(BF16), s.astype(BF16))
        k_t = (kr * k_dec).T.astype(BF16)
        s_ref[h] = math.exp(lg * c) * s + _dot(k_t, vh)

        mu = jnp.mean(o, axis=-1, keepdims=True)
        oc = o - mu
        var = jnp.mean(oc * oc, axis=-1, keepdims=True)
        y = oc * lax.rsqrt(var + EPS) * nw_ref[:, vs]
        rg = rg_ref[:, vs].astype(F32)
        o_ref[:, vs] = (y * (rg * jax.nn.sigmoid(rg))).astype(BF16)


def _retention(proj, pos, theta, nw, *, batch, seq, cols):
    n = proj.shape[0]
    c = RET_CHUNK
    qk_w = RET_HEADS * RET_DK
    v_w = RET_HEADS * RET_DV
    spb = seq // c
    row = lambda b, s: b * spb + s
    return pl.pallas_call(
        _ret_kernel,
        grid=(batch, spb),
        in_specs=[
            pl.BlockSpec((c, 1), lambda b, s: (row(b, s), 0)),
            pl.BlockSpec(theta.shape, lambda b, s: (0, 0)),
            pl.BlockSpec((c, qk_w), lambda b, s: (row(b, s), cols[0])),
            pl.BlockSpec((c, qk_w), lambda b, s: (row(b, s), cols[1])),
            pl.BlockSpec((c, v_w), lambda b, s: (row(b, s), cols[2])),
            pl.BlockSpec((c, v_w), lambda b, s: (row(b, s), cols[3])),
            pl.BlockSpec(nw.shape, lambda b, s: (0, 0)),
        ],
        out_specs=pl.BlockSpec((c, v_w), lambda b, s: (row(b, s), 0)),
        out_shape=jax.ShapeDtypeStruct((n, v_w), BF16),
        scratch_shapes=[
            pltpu.VMEM((RET_HEADS, RET_DK, RET_DV), F32),
            pltpu.VMEM((RET_HEADS, c, c), F32),
        ],
        compiler_params=_params(("arbitrary", "arbitrary")),
        name="retention",
    )(pos, theta, proj, proj, proj, proj, nw)


def _mix_kernel(og_ref, or_ref, ma_ref, mb_ref, x_ref, wa_ref, wb_ref, wo_ref, nw_ref, wr_ref, br_ref,
                tri_ref, h_ref, t_ref, r_ref, rt_ref, cnt_ref, base_ref, tprev_ref):
    i = pl.program_id(0)
    tm = x_ref.shape[0]

    @pl.when(i == 0)
    def _():
        base_ref[...] = jnp.zeros_like(base_ref)
        tprev_ref[...] = jnp.zeros_like(tprev_ref)

    lane = lax.broadcasted_iota(jnp.int32, (tm, LANES), 1)
    neg = jnp.float32(-1e30)
    big = jnp.int32(1 << 20)

    def first_max(v):
        m = jnp.max(v, axis=-1, keepdims=True)
        return m, jnp.min(jnp.where(v == m, lane, big), axis=-1, keepdims=True)

    t_hi, t_lo = _split2(tprev_ref[...])
    lg = _dot(t_hi, wr_ref[0]) + _dot(t_hi, wr_ref[1]) + _dot(t_lo, wr_ref[0]) + br_ref[...]
    g_valid = (lane >= N_EXPERTS) & (lane < N_EXPERTS + N_GROUPS)
    g_m, g_lane = first_max(jnp.where(g_valid, lg, neg))
    g_w = 1.0 / jnp.sum(jnp.where(g_valid, jnp.exp(lg - g_m), 0.0), axis=-1, keepdims=True)
    g_idx = g_lane - N_EXPERTS
    e_valid = (lane < N_EXPERTS) & ((lane // EXPERTS_PER_GROUP) == g_idx)
    el = jnp.where(e_valid, lg, neg)
    v1, i1 = first_max(el)
    v2, i2 = first_max(jnp.where(lane == i1, neg, el))
    e21 = jnp.exp(v2 - v1)
    w1 = g_w / (1.0 + e21)
    w2 = g_w * e21 / (1.0 + e21)

    o1 = lane == i1
    o2 = lane == i2
    live = jnp.where(i > 0, 1.0, 0.0)
    osum = jnp.where(o1 | o2, live, 0.0)
    base = base_ref[0:1, :]
    before = _dot(tri_ref[...], osum.astype(BF16)) + base
    r1 = jnp.sum(jnp.where(o1, before, 0.0), axis=-1, keepdims=True)
    r2 = jnp.sum(jnp.where(o2, before, 0.0), axis=-1, keepdims=True)
    base = base + jnp.sum(osum, axis=0, keepdims=True)
    base_ref[...] = jnp.broadcast_to(base, base_ref.shape)
    cnt_ref[...] = jnp.broadcast_to(base, cnt_ref.shape)

    cols = (i1.astype(F32), i2.astype(F32), w1, w2, r1, r2)
    wide = jnp.zeros((tm, LANES), F32)
    for ci, col in enumerate(cols):
        wide = jnp.where(lane == ci, col, wide)
    r_ref[...] = wide[:, :r_ref.shape[1]]
    rt_ref[...] = wide.T[:rt_ref.shape[0], :]

    ya = _dot(og_ref[...], wa_ref[...])
    yb = _dot(or_ref[...], wb_ref[...])
    merged = (jax.nn.sigmoid(ma_ref[...].astype(F32)) * ya + jax.nn.sigmoid(mb_ref[...].astype(F32)) * yb)
    h = x_ref[...] + _dot(merged.astype(BF16), wo_ref[...])
    h_ref[...] = h
    ms = jnp.mean(h * h, axis=-1, keepdims=True)
    t = h * lax.rsqrt(ms + EPS) * nw_ref[...]
    t_ref[...] = _to_row_tiles(t)
    tprev_ref[...] = t


def _mix(o_gla, o_ret, proj, x2, wa, wb, wo, nw, wr2, br, *, tm, cols):
    n, d = x2.shape
    nt = n // tm
    tri = jnp.asarray(np.tril(np.ones((tm, tm), np.float32), -1), BF16)
    const = lambda shape: pl.BlockSpec(shape, lambda i: (0,) * len(shape))
    mixed = lambda i: jnp.minimum(i, nt - 1)
    routed = lambda i: jnp.maximum(i - 1, 0)
    return pl.pallas_call(
        _mix_kernel,
        grid=(nt + 1,),
        in_specs=[
            pl.BlockSpec((tm, o_gla.shape[1]), lambda i: (mixed(i), 0)),
            pl.BlockSpec((tm, o_ret.shape[1]), lambda i: (mixed(i), 0)),
            pl.BlockSpec((tm, d), lambda i: (mixed(i), cols[0])),
            pl.BlockSpec((tm, d), lambda i: (mixed(i), cols[1])),
            pl.BlockSpec((tm, d), lambda i: (mixed(i), 0)),
            const(wa.shape), const(wb.shape), const(wo.shape), const(nw.shape),
            const(wr2.shape), const(br.shape), const(tri.shape),
        ],
        out_specs=[
            pl.BlockSpec((tm, d), lambda i: (mixed(i), 0)),
            pl.BlockSpec((tm, d // LANES, LANES), lambda i: (mixed(i), 0, 0)),
            pl.BlockSpec((tm, 8), lambda i: (routed(i), 0)),
            pl.BlockSpec((8, tm), lambda i: (0, routed(i))),
            pl.BlockSpec((8, LANES), lambda i: (0, 0)),
        ],
        out_shape=[
            jax.ShapeDtypeStruct((n, d), F32),
            jax.ShapeDtypeStruct((n, d // LANES, LANES), BF16),
            jax.ShapeDtypeStruct((n, 8), F32),
            jax.ShapeDtypeStruct((8, n), F32),
            jax.ShapeDtypeStruct((8, LANES), F32),
        ],
        scratch_shapes=[pltpu.VMEM((8, LANES), F32), pltpu.VMEM((tm, d), F32)],
        compiler_params=_params(("arbitrary",)),
        name="mix_router",
    )(o_gla, o_ret, proj, proj, x2, wa, wb, wo, nw, wr2, br, tri)


def _to_row_tiles(x):
    return x.reshape(x.shape[0], x.shape[1] // LANES, LANES).astype(BF16)


def _from_row_tiles(x3):
    return x3.astype(F32).reshape(x3.shape[0], x3.shape[1] * x3.shape[2])


def _dispatch_kernel(tail_ref, has_ref, d0_ref, d1_ref, t_ref, xs_hbm, zero_ref, sem, zsem):
    step = pl.program_id(0)
    tok_n = t_ref.shape[0]
    blk = zero_ref.shape[0]

    def tail_copy(e):
        return pltpu.make_async_copy(zero_ref, xs_hbm.at[pl.ds(tail_ref[e], blk)], zsem)

    @pl.when(step == 0)
    def _():
        zero_ref[...] = jnp.zeros_like(zero_ref)
        for e in range(tail_ref.shape[0]):
            @pl.when(has_ref[e] > 0)
            def _(e=e):
                tail_copy(e).start()
        for e in range(tail_ref.shape[0]):
            @pl.when(has_ref[e] > 0)
            def _(e=e):
                tail_copy(e).wait()

    def row_copy(a, dst):
        return pltpu.make_async_copy(t_ref.at[a], xs_hbm.at[dst], sem)

    def body(a, carry):
        row_copy(a, d0_ref[a]).start(priority=0)
        row_copy(a, d1_ref[a]).start(priority=1)
        return carry

    lax.fori_loop(0, tok_n, body, 0, unroll=8)

    for _ in range(2):
        pltpu.make_async_copy(t_ref, xs_hbm.at[pl.ds(0, tok_n)], sem).wait()


def _dispatch(t3, dest0, dest1, tail, has, *, n_slots, chunk):
    n, s, l = t3.shape
    return pl.pallas_call(
        _dispatch_kernel,
        grid_spec=pltpu.PrefetchScalarGridSpec(
            num_scalar_prefetch=2,
            grid=(n // chunk,),
            in_specs=[
                pl.BlockSpec((chunk,), lambda i, *_: (i,), memory_space=pltpu.SMEM),
                pl.BlockSpec((chunk,), lambda i, *_: (i,), memory_space=pltpu.SMEM),
                pl.BlockSpec((chunk, s, l), lambda i, *_: (i, 0, 0)),
            ],
            out_specs=pl.BlockSpec(memory_space=pl.ANY),
            scratch_shapes=[
                pltpu.VMEM((EXPERT_ROWS, s, l), BF16),
                pltpu.SemaphoreType.DMA,
                pltpu.SemaphoreType.DMA,
            ],
        ),
        out_shape=jax.ShapeDtypeStruct((n_slots, s, l), BF16),
        compiler_params=pltpu.CompilerParams(dimension_semantics=("arbitrary",), has_side_effects=True,
                                             vmem_limit_bytes=VMEM_LIMIT, disable_bounds_checks=True),
        name="dispatch",
    )(tail, has, dest0, dest1, t3)


def _expert_kernel(be_ref, nu_ref, xs_ref, wg_ref, wu_ref, wd_ref, ys_ref, wg_s, wu_s, wd_s):
    b = pl.program_id(0)

    @pl.when((b == 0) | (be_ref[b] != be_ref[jnp.maximum(b - 1, 0)]))
    def _():
        wg_s[...] = wg_ref[0].astype(BF16)
        wu_s[...] = wu_ref[0].astype(BF16)
        wd_s[...] = wd_ref[0].astype(BF16)

    @pl.when(b < nu_ref[0])
    def _():
        x = _from_row_tiles(xs_ref[...]).astype(BF16)
        g = _dot(x, wg_s[...])
        u = _dot(x, wu_s[...])
        hid = (g * jax.nn.sigmoid(g) * u).astype(BF16)
        ys_ref[...] = _to_row_tiles(_dot(hid, wd_s[...]))

    @pl.when(b >= nu_ref[0])
    def _():
        ys_ref[...] = jnp.zeros_like(ys_ref)


def _experts(xs3, block_expert, n_used, wg, wu, wd):
    n_slots, s, l = xs3.shape
    d = s * l
    blk = EXPERT_ROWS
    hid = wg.shape[2]
    return pl.pallas_call(
        _expert_kernel,
        grid_spec=pltpu.PrefetchScalarGridSpec(
            num_scalar_prefetch=2,
            grid=(n_slots // blk,),
            in_specs=[
                pl.BlockSpec((blk, s, l), lambda b, be, nu: (jnp.minimum(b, nu[0] - 1), 0, 0)),
                pl.BlockSpec((1, d, hid), lambda b, be, nu: (be[b], 0, 0)),
                pl.BlockSpec((1, d, hid), lambda b, be, nu: (be[b], 0, 0)),
                pl.BlockSpec((1, hid, d), lambda b, be, nu: (be[b], 0, 0)),
            ],
            out_specs=pl.BlockSpec((blk, s, l), lambda b, be, nu: (b, 0, 0)),
            scratch_shapes=[pltpu.VMEM((d, hid), BF16), pltpu.VMEM((d, hid), BF16), pltpu.VMEM((hid, d), BF16)],
        ),
        out_shape=jax.ShapeDtypeStruct((n_slots, s, l), BF16),
        compiler_params=_params(("arbitrary",)),
        name="experts",
    )(block_expert, n_used, xs3, wg, wu, wd)


def _combine_kernel(c0_ref, c1_ref, n0_ref, n1_ref, h_ref, r_ref, nw_ref, ys_hbm, o_ref, buf, sem):
    i = pl.program_id(0)
    n_steps = pl.num_programs(0)
    tm = h_ref.shape[0]

    def row_copy(dst_row, src_row, slot):
        return pltpu.make_async_copy(ys_hbm.at[src_row], buf.at[slot, dst_row], sem.at[slot])

    def issue(d0, d1, slot):
        def body(a, carry):
            row_copy(a, d0[a], slot).start(priority=0)
            row_copy(tm + a, d1[a], slot).start(priority=1)
            return carry
        lax.fori_loop(0, tm, body, 0, unroll=8)

    @pl.when(i == 0)
    def _():
        issue(c0_ref, c1_ref, 0)

    @pl.when(i + 1 < n_steps)
    def _():
        issue(n0_ref, n1_ref, (i + 1) % 2)

    slot = i % 2
    pltpu.make_async_copy(ys_hbm.at[pl.ds(0, 2 * tm)], buf.at[slot], sem.at[slot]).wait()

    r = r_ref[...]
    y = _from_row_tiles(buf[slot, 0:tm]) * r[:, 2:3] + _from_row_tiles(buf[slot, tm:2 * tm]) * r[:, 3:4]
    h = h_ref[...] + y
    ms = jnp.mean(h * h, axis=-1, keepdims=True)
    o_ref[...] = h * lax.rsqrt(ms + EPS) * nw_ref[...]


def _combine(dest0, dest1, h1, r, nw, ys, *, tm):
    n, d = h1.shape
    n_steps = n // tm
    cur = pl.BlockSpec((tm,), lambda i: (i,), memory_space=pltpu.SMEM)
    nxt = pl.BlockSpec((tm,), lambda i: (jnp.minimum(i + 1, n_steps - 1),), memory_space=pltpu.SMEM)
    return pl.pallas_call(
        _combine_kernel,
        grid=(n_steps,),
        in_specs=[
            cur, cur, nxt, nxt,
            pl.BlockSpec((tm, d), lambda i: (i, 0)),
            pl.BlockSpec((tm, r.shape[1]), lambda i: (i, 0)),
            pl.BlockSpec((1, d), lambda i: (0, 0)),
            pl.BlockSpec(memory_space=pl.ANY),
        ],
        out_specs=pl.BlockSpec((tm, d), lambda i: (i, 0)),
        out_shape=jax.ShapeDtypeStruct((n, d), F32),
        scratch_shapes=[pltpu.VMEM((2, 2 * tm) + ys.shape[1:], BF16), pltpu.SemaphoreType.DMA((2,))],
        compiler_params=pltpu.CompilerParams(dimension_semantics=("arbitrary",), vmem_limit_bytes=VMEM_LIMIT,
                                             disable_bounds_checks=True),
        name="combine",
    )(dest0, dest1, dest0, dest1, h1, r, nw, ys)


def _layer(h, positions, norm_mix_w, w_in, gk_up, gk_bias, gla_norm_w, w_br_gla, ret_norm_w, w_br_ret, w_out,
           norm_ffn_w, rg_w, rg_b, re_w, re_b, wg, wu, wd, norm_final_w, *, tm_in, tn_in, gla_rows, tm_mix,
           tm_comb, disp_chunk):
    batch, seq, d = h.shape
    n = batch * seq
    x2 = h.reshape(n, d)
    gqk, gv = GLA_HEADS * GLA_DK, GLA_HEADS * GLA_DV
    rqk, rv = RET_HEADS * RET_DK, RET_HEADS * RET_DV

    sizes = (gqk, gqk, gv, gv, GLA_GATE_RANK, rqk, rqk, rv, rv, d, d)
    offs = np.concatenate([[0], np.cumsum(sizes)])
    seg = lambda i: w_in[:, offs[i]:offs[i + 1]]
    perm = np.concatenate([np.concatenate([np.arange(0, RET_DK, 2), np.arange(1, RET_DK, 2)]) + hh * RET_DK
                           for hh in range(RET_HEADS)])
    w_main = jnp.concatenate([seg(7), seg(8), seg(2), seg(3), seg(5)[:, perm], seg(6)[:, perm], seg(9), seg(10),
                              seg(0), seg(1)], axis=1).astype(BF16)
    assert rv % gv == 0 and gv == rqk == d and rv == 2 * d and gqk * 2 == d
    ret_cols = (4 + 2, 4 + 3, 0, 1)
    gla_cols = (2 * (4 + 6), 2 * (4 + 6) + 1, 4, 5)
    mix_cols = (4 + 4, 4 + 5)
    w_gd = jnp.pad(seg(4), ((0, 0), (0, LANES - GLA_GATE_RANK))).astype(BF16)

    proj, gd = _inproj(x2, norm_mix_w.reshape(1, d), w_main, w_gd, tm=tm_in, tn=tn_in)

    up = jnp.pad(gk_up, ((0, LANES - GLA_GATE_RANK), (0, 0)))
    up_hi = up.astype(BF16)
    up2 = jnp.stack([up_hi, (up - up_hi.astype(F32)).astype(BF16)])
    o_gla = _gla(proj, gd, up2, gk_bias.reshape(1, gqk), gla_norm_w.reshape(1, GLA_DV),
                 batch=batch, seq=seq, rows=gla_rows, cols=gla_cols)

    theta = (1.0 / (ROPE_BASE ** jnp.linspace(0.0, 1.0, RET_DK // 2, dtype=F32))).reshape(1, RET_DK // 2)
    o_ret = _retention(proj, positions.reshape(n, 1), theta, ret_norm_w.reshape(1, rv),
                       batch=batch, seq=seq, cols=ret_cols)

    wr = jnp.concatenate([re_w.transpose(1, 0, 2).reshape(d, N_EXPERTS), rg_w], axis=1)
    wr = jnp.pad(wr, ((0, 0), (0, LANES - N_EXPERTS - N_GROUPS)))
    wr_hi = wr.astype(BF16)
    wr2 = jnp.stack([wr_hi, (wr - wr_hi.astype(F32)).astype(BF16)])
    br = jnp.pad(jnp.concatenate([re_b.reshape(-1), rg_b]), (0, LANES - N_EXPERTS - N_GROUPS)).reshape(1, LANES)
    h1, t, r, rt, cnt = _mix(o_gla, o_ret, proj, x2, w_br_gla.astype(BF16), w_br_ret.astype(BF16),
                             w_out.astype(BF16), norm_ffn_w.reshape(1, d), wr2, br, tm=tm_mix, cols=mix_cols)

    blk = EXPERT_ROWS
    n_assign = 2 * n
    n_slots = -(-(n_assign + N_EXPERTS * (blk - 1)) // blk) * blk
    counts = cnt[0, :N_EXPERTS].astype(jnp.int32)
    padded = ((counts + blk - 1) // blk) * blk
    pad_end = jnp.cumsum(padded)
    pad_start = pad_end - padded
    dest0 = pad_start[rt[0].astype(jnp.int32)] + rt[4].astype(jnp.int32)
    dest1 = pad_start[rt[1].astype(jnp.int32)] + rt[5].astype(jnp.int32)
    block_start = jnp.arange(n_slots // blk, dtype=jnp.int32) * blk
    block_expert = jnp.minimum(jnp.sum((pad_end[None, :] <= block_start[:, None]).astype(jnp.int32), axis=1),
                               N_EXPERTS - 1)
    n_used = (pad_end[-1:] // blk).astype(jnp.int32)
    spare = pad_end[-1] + jnp.arange(N_EXPERTS, dtype=jnp.int32) * blk
    tail = jnp.concatenate([jnp.maximum(pad_end - blk, 0), jnp.minimum(spare, n_slots - blk)]).astype(jnp.int32)
    has = jnp.concatenate([counts, (spare < n_slots).astype(jnp.int32)])

    xs = _dispatch(t, dest0, dest1, tail, has, n_slots=n_slots, chunk=disp_chunk)
    ys = _experts(xs, block_expert, n_used, wg, wu, wd)
    out = _combine(dest0, dest1, h1, r, norm_final_w.reshape(1, d), ys, tm=tm_comb)
    return out.reshape(batch, seq, d)


def kernel(x, positions, norm_mix_w, w_in, gla_gk_up, gla_gk_bias, gla_norm_w, w_branch_gla, ret_norm_w,
           w_branch_ret, w_out, norm_ffn_w, router_group_w, router_group_b, router_expert_w, router_expert_b,
           expert_w_gate, expert_w_up, expert_w_down, norm_final_w):
    assert norm_mix_w.shape[0] == 1, "single-layer block"
    return _layer(x, positions, norm_mix_w[0], w_in[0], gla_gk_up[0], gla_gk_bias[0], gla_norm_w[0], w_branch_gla[0],
                  ret_norm_w[0], w_branch_ret[0], w_out[0], norm_ffn_w[0], router_group_w[0], router_group_b[0],
                  router_expert_w[0], router_expert_b[0], expert_w_gate[0], expert_w_up[0], expert_w_down[0],
                  norm_final_w, tm_in=2048, tn_in=1024, gla_rows=256, tm_mix=512, tm_comb=256,
                  disp_chunk=1024)
```

```python
import functools
import math

import jax
import jax.numpy as jnp
import numpy as np
from jax import lax
from jax.experimental import pallas as pl
from jax.experimental.pallas import tpu as pltpu

F32 = jnp.float32
BF16 = jnp.bfloat16

EPS = 1e-6
GLA_HEADS = 4
GLA_DK = 128
GLA_DV = 256
GLA_GATE_RANK = 16
GLA_GATE_TEMP = 16.0
RET_HEADS = 4
RET_DK = 256
RET_DV = 512
ROPE_BASE = 10000.0
N_GROUPS = 4
EXPERTS_PER_GROUP = 8
N_EXPERTS = N_GROUPS * EXPERTS_PER_GROUP
EXPERT_HIDDEN = 512

LANES = 128
GLA_CHUNK = 128
GLA_DIAG = 8
RET_CHUNK = 256
EXPERT_ROWS = 512
VMEM_LIMIT = 48 * 1024 * 1024


def _dot(a, b):
    return jnp.dot(a, b, preferred_element_type=F32)


def _dot_nt(a, b):
    return lax.dot_general(a, b, (((1,), (1,)), ((), ())), preferred_element_type=F32)


def _split2(a):
    hi = a.astype(BF16)
    lo = (a - hi.astype(F32)).astype(BF16)
    return hi, lo


def _params(sem, vmem=VMEM_LIMIT):
    return pltpu.CompilerParams(dimension_semantics=sem, vmem_limit_bytes=vmem)


def _inproj_kernel(x_ref, nw_ref, w_ref, wgd_ref, proj_ref, gd_ref, u_scr):
    @pl.when(pl.program_id(1) == 0)
    def _():
        x = x_ref[...]
        ms = jnp.mean(x * x, axis=-1, keepdims=True)
        u = (x * lax.rsqrt(ms + EPS) * nw_ref[...]).astype(BF16)
        u_scr[...] = u
        gd_ref[...] = _dot(u, wgd_ref[...])

    proj_ref[...] = _dot(u_scr[...], w_ref[...]).astype(BF16)


def _inproj(x2, nw, w_main, w_gd, *, tm, tn):
    n, d = x2.shape
    p = w_main.shape[1]
    return pl.pallas_call(
        _inproj_kernel,
        grid=(n // tm, p // tn),
        in_specs=[
            pl.BlockSpec((tm, d), lambda i, j: (i, 0)),
            pl.BlockSpec((1, d), lambda i, j: (0, 0)),
            pl.BlockSpec((d, tn), lambda i, j: (0, j)),
            pl.BlockSpec((d, LANES), lambda i, j: (0, 0)),
        ],
        out_specs=[
            pl.BlockSpec((tm, tn), lambda i, j: (i, j)),
            pl.BlockSpec((tm, LANES), lambda i, j: (i, 0)),
        ],
        out_shape=[
            jax.ShapeDtypeStruct((n, p), BF16),
            jax.ShapeDtypeStruct((n, LANES), F32),
        ],
        scratch_shapes=[pltpu.VMEM((tm, d), BF16)],
        compiler_params=_params(("arbitrary", "arbitrary")),
        name="inproj",
    )(x2, nw, w_main, w_gd)


def _gla_tables(c):
    levels = []
    s = c // 2
    while s >= GLA_DIAG:
        levels.append(s)
        s //= 2
    i = np.arange(c)[:, None]
    t = np.arange(c)[None, :]
    mats = [t <= i, t > i]
    masks = []
    for s in levels:
        bs = (i // s) * s
        mats.append((t > bs) & (t <= i))
        mats.append((t > i) & (t <= np.minimum(bs + s, c - 1)))
        masks.append(((i // (2 * s)) == (t // (2 * s))) & (((i // s) % 2) == 1) & (((t // s) % 2) == 0))
    for d in range(1, GLA_DIAG):
        mats.append((t > i - d) & (t <= i))
    for d in range(GLA_DIAG):
        masks.append((t == i - d) & ((i % GLA_DIAG) >= d))
    g = np.concatenate(mats, 0).astype(np.float32)
    m = np.stack(masks, 0).astype(np.float32)
    return g, m, tuple(levels)


def _gla_kernel(q_ref, k_ref, v_ref, gg_ref, gd_ref, up_ref, bias_ref, nw_ref, g_ref, m_ref,
                o_ref, st_ref, *, c, nlev):
    h_n, dk, dv = GLA_HEADS, GLA_DK, GLA_DV
    rows = q_ref.shape[0]

    @pl.when(pl.program_id(1) == 0)
    def _():
        st_ref[...] = jnp.zeros_like(st_ref)

    g_tab = g_ref[...]
    up_hi = up_ref[0]
    up_lo = up_ref[1]
    q_scale = dk ** -0.5
    band0 = 2 + 2 * nlev

    def chunk(ci, carry):
        r0 = pl.multiple_of(ci * c, c)
        rs = pl.ds(r0, c)
        gd_hi, gd_lo = _split2(gd_ref[rs, :])
        xg = _dot(gd_hi, up_hi) + _dot(gd_hi, up_lo) + _dot(gd_lo, up_hi) + bias_ref[...]
        la = (jnp.minimum(xg, 0.0) - jnp.log1p(jnp.exp(-jnp.abs(xg)))) * (1.0 / GLA_GATE_TEMP)
        ex = jnp.exp(_dot(g_tab, la.astype(BF16)))

        for h in range(h_n):
            ks = slice(h * dk, (h + 1) * dk)
            vs = slice(h * dv, (h + 1) * dv)

            def tab(m, ks=ks):
                return ex[m * c:(m + 1) * c, ks]

            qh = q_ref[rs, ks].astype(F32) * q_scale
            kh = k_ref[rs, ks].astype(F32)
            vh = v_ref[rs, vs]
            st = st_ref[h]
            eb = tab(0)
            o = _dot_nt((qh * eb).astype(BF16), st.astype(BF16))
            a = m_ref[nlev] * jnp.sum(qh * kh, axis=-1, keepdims=True)
            for li in range(nlev):
                qs = (qh * tab(2 + 2 * li)).astype(BF16)
                kk = (kh * tab(3 + 2 * li)).astype(BF16)
                a = a + m_ref[li] * _dot_nt(qs, kk)
            for d in range(1, GLA_DIAG):
                kr = pltpu.roll(kh, d, 0)
                dd = jnp.sum(qh * kr * tab(band0 + d - 1), axis=-1, keepdims=True)
                a = a + m_ref[nlev + d] * dd
            o = o + _dot(a.astype(BF16), vh)
            e_last = eb[c - 1:c, :]
            k_st = (kh * tab(1)).astype(BF16)
            v_t = vh.astype(F32).T.astype(BF16)
            st_ref[h] = st * e_last + _dot(v_t, k_st)

            ms = jnp.mean(o * o, axis=-1, keepdims=True)
            y = o * lax.rsqrt(ms + EPS) * nw_ref[...]
            gg = gg_ref[rs, vs].astype(F32)
            o_ref[rs, vs] = (y * (gg * jax.nn.sigmoid(gg))).astype(BF16)
        return carry

    lax.fori_loop(0, rows // c, chunk, 0, unroll=2)


def _gla(proj, gd, up2, bias, nw, *, batch, seq, rows, cols):
    n = proj.shape[0]
    c = GLA_CHUNK
    g_np, m_np, levels = _gla_tables(c)
    g_tab = jnp.asarray(g_np, BF16)
    m_tab = jnp.asarray(m_np, F32)
    qk_w = GLA_HEADS * GLA_DK
    v_w = GLA_HEADS * GLA_DV
    spb = seq // rows
    row = lambda b, s: b * spb + s
    kern = functools.partial(_gla_kernel, c=c, nlev=len(levels))
    return pl.pallas_call(
        kern,
        grid=(batch, spb),
        in_specs=[
            pl.BlockSpec((rows, qk_w), lambda b, s: (row(b, s), cols[0])),
            pl.BlockSpec((rows, qk_w), lambda b, s: (row(b, s), cols[1])),
            pl.BlockSpec((rows, v_w), lambda b, s: (row(b, s), cols[2])),
            pl.BlockSpec((rows, v_w), lambda b, s: (row(b, s), cols[3])),
            pl.BlockSpec((rows, LANES), lambda b, s: (row(b, s), 0)),
            pl.BlockSpec(up2.shape, lambda b, s: (0, 0, 0)),
            pl.BlockSpec(bias.shape, lambda b, s: (0, 0)),
            pl.BlockSpec(nw.shape, lambda b, s: (0, 0)),
            pl.BlockSpec(g_tab.shape, lambda b, s: (0, 0)),
            pl.BlockSpec(m_tab.shape, lambda b, s: (0, 0, 0)),
        ],
        out_specs=pl.BlockSpec((rows, v_w), lambda b, s: (row(b, s), 0)),
        out_shape=jax.ShapeDtypeStruct((n, v_w), BF16),
        scratch_shapes=[pltpu.VMEM((GLA_HEADS, GLA_DV, GLA_DK), F32)],
        compiler_params=_params(("arbitrary", "arbitrary")),
        name="gla",
    )(proj, proj, proj, proj, gd, up2, bias, nw, g_tab, m_tab)


def _ret_log_gamma(h):
    return math.log(1.0 - 2.0 ** (-5.0 - h))


def _ret_kernel(pos_ref, th_ref, q_ref, k_ref, v_ref, rg_ref, nw_ref, o_ref, s_ref, dm_ref):
    h_n, dk, dv = RET_HEADS, RET_DK, RET_DV
    c = q_ref.shape[0]
    half = dk // 2

    @pl.when(pl.program_id(1) == 0)
    def _():
        s_ref[...] = jnp.zeros_like(s_ref)

    @pl.when((pl.program_id(0) == 0) & (pl.program_id(1) == 0))
    def _():
        rel = (lax.broadcasted_iota(jnp.int32, (c, c), 0) - lax.broadcasted_iota(jnp.int32, (c, c), 1)).astype(F32)
        for h in range(h_n):
            dm_ref[h] = jnp.where(rel >= 0, jnp.exp(_ret_log_gamma(h) * jnp.maximum(rel, 0.0)), 0.0)

    ang = pos_ref[...].astype(F32) * th_ref[...]
    cs = jnp.cos(ang)
    sn = jnp.sin(ang)
    idx = lax.broadcasted_iota(jnp.int32, (c, 1), 0).astype(F32)
    k_scale = dk ** -0.5

    def rot(ref, h):
        a = ref[:, h * dk:h * dk + half].astype(F32)
        b = ref[:, h * dk + half:(h + 1) * dk].astype(F32)
        return jnp.concatenate([a * cs - b * sn, b * cs + a * sn], axis=-1)

    for h in range(h_n):
        lg = _ret_log_gamma(h)
        vs = slice(h * dv, (h + 1) * dv)
        qr = rot(q_ref, h)
        kr = rot(k_ref, h) * k_scale
        vh = v_ref[:, vs]
        scores = _dot_nt(qr.astype(BF16), kr.astype(BF16)) * dm_ref[h]
        q_dec = jnp.exp(lg * (idx + 1.0))
        k_dec = jnp.exp(lg * (c - 1.0 - idx))
        s = s_ref[h]
        o = _dot(scores.astype(BF16), vh) + _dot((qr * q_dec).astype(BF16), s.astype(BF16))
        k_t = (kr * k_dec).T.astype(BF16)
        s_ref[h] = math.exp(lg * c) * s + _dot(k_t, vh)

        mu = jnp.mean(o, axis=-1, keepdims=True)
        oc = o - mu
        var = jnp.mean(oc * oc, axis=-1, keepdims=True)
        y = oc * lax.rsqrt(var + EPS) * nw_ref[:, vs]
        rg = rg_ref[:, vs].astype(F32)
        o_ref[:, vs] = (y * (rg * jax.nn.sigmoid(rg))).astype(BF16)


def _retention(proj, pos, theta, nw, *, batch, seq, cols):
    n = proj.shape[0]
    c = RET_CHUNK
    qk_w = RET_HEADS * RET_DK
    v_w = RET_HEADS * RET_DV
    spb = seq // c
    row = lambda b, s: b * spb + s
    return pl.pallas_call(
        _ret_kernel,
        grid=(batch, spb),
        in_specs=[
            pl.BlockSpec((c, 1), lambda b, s: (row(b, s), 0)),
            pl.BlockSpec(theta.shape, lambda b, s: (0, 0)),
            pl.BlockSpec((c, qk_w), lambda b, s: (row(b, s), cols[0])),
            pl.BlockSpec((c, qk_w), lambda b, s: (row(b, s), cols[1])),
            pl.BlockSpec((c, v_w), lambda b, s: (row(b, s), cols[2])),
            pl.BlockSpec((c, v_w), lambda b, s: (row(b, s), cols[3])),
            pl.BlockSpec(nw.shape, lambda b, s: (0, 0)),
        ],
        out_specs=pl.BlockSpec((c, v_w), lambda b, s: (row(b, s), 0)),
        out_shape=jax.ShapeDtypeStruct((n, v_w), BF16),
        scratch_shapes=[
            pltpu.VMEM((RET_HEADS, RET_DK, RET_DV), F32),
            pltpu.VMEM((RET_HEADS, c, c), F32),
        ],
        compiler_params=_params(("arbitrary", "arbitrary")),
        name="retention",
    )(pos, theta, proj, proj, proj, proj, nw)


def _mix_kernel(og_ref, or_ref, ma_ref, mb_ref, x_ref, wa_ref, wb_ref, wo_ref, nw_ref, wr_ref, br_ref,
                tri_ref, h_ref, t_ref, r_ref, rt_ref, cnt_ref, base_ref):
    tm = x_ref.shape[0]

    @pl.when(pl.program_id(0) == 0)
    def _():
        base_ref[...] = jnp.zeros_like(base_ref)

    ya = _dot(og_ref[...], wa_ref[...])
    yb = _dot(or_ref[...], wb_ref[...])
    merged = (jax.nn.sigmoid(ma_ref[...].astype(F32)) * ya + jax.nn.sigmoid(mb_ref[...].astype(F32)) * yb)
    h = x_ref[...] + _dot(merged.astype(BF16), wo_ref[...])
    h_ref[...] = h
    ms = jnp.mean(h * h, axis=-1, keepdims=True)
    t = h * lax.rsqrt(ms + EPS) * nw_ref[...]
    t_ref[...] = _to_row_tiles(t)

    lane = lax.broadcasted_iota(jnp.int32, (tm, LANES), 1)
    neg = jnp.float32(-1e30)
    big = jnp.int32(1 << 20)

    def first_max(v):
        m = jnp.max(v, axis=-1, keepdims=True)
        return m, jnp.min(jnp.where(v == m, lane, big), axis=-1, keepdims=True)

    t_hi, t_lo = _split2(t)
    lg = _dot(t_hi, wr_ref[0]) + _dot(t_hi, wr_ref[1]) + _dot(t_lo, wr_ref[0]) + br_ref[...]
    g_valid = (lane >= N_EXPERTS) & (lane < N_EXPERTS + N_GROUPS)
    g_m, g_lane = first_max(jnp.where(g_valid, lg, neg))
    g_w = 1.0 / jnp.sum(jnp.where(g_valid, jnp.exp(lg - g_m), 0.0), axis=-1, keepdims=True)
    g_idx = g_lane - N_EXPERTS
    e_valid = (lane < N_EXPERTS) & ((lane // EXPERTS_PER_GROUP) == g_idx)
    el = jnp.where(e_valid, lg, neg)
    v1, i1 = first_max(el)
    v2, i2 = first_max(jnp.where(lane == i1, neg, el))
    e21 = jnp.exp(v2 - v1)
    w1 = g_w / (1.0 + e21)
    w2 = g_w * e21 / (1.0 + e21)

    o1 = lane == i1
    o2 = lane == i2
    osum = jnp.where(o1 | o2, 1.0, 0.0)
    base = base_ref[0:1, :]
    before = _dot(tri_ref[...], osum.astype(BF16)) + base
    r1 = jnp.sum(jnp.where(o1, before, 0.0), axis=-1, keepdims=True)
    r2 = jnp.sum(jnp.where(o2, before, 0.0), axis=-1, keepdims=True)
    base = base + jnp.sum(osum, axis=0, keepdims=True)
    base_ref[...] = jnp.broadcast_to(base, base_ref.shape)
    cnt_ref[...] = jnp.broadcast_to(base, cnt_ref.shape)

    cols = (i1.astype(F32), i2.astype(F32), w1, w2, r1, r2)
    wide = jnp.zeros((tm, LANES), F32)
    for ci, col in enumerate(cols):
        wide = jnp.where(lane == ci, col, wide)
    r_ref[...] = wide[:, :r_ref.shape[1]]
    rt_ref[...] = wide.T[:rt_ref.shape[0], :]


def _mix(o_gla, o_ret, proj, x2, wa, wb, wo, nw, wr2, br, *, tm, cols):
    n, d = x2.shape
    tri = jnp.asarray(np.tril(np.ones((tm, tm), np.float32), -1), BF16)
    const = lambda shape: pl.BlockSpec(shape, lambda i: (0,) * len(shape))
    return pl.pallas_call(
        _mix_kernel,
        grid=(n // tm,),
        in_specs=[
            pl.BlockSpec((tm, o_gla.shape[1]), lambda i: (i, 0)),
            pl.BlockSpec((tm, o_ret.shape[1]), lambda i: (i, 0)),
            pl.BlockSpec((tm, d), lambda i: (i, cols[0])),
            pl.BlockSpec((tm, d), lambda i: (i, cols[1])),
            pl.BlockSpec((tm, d), lambda i: (i, 0)),
            const(wa.shape), const(wb.shape), const(wo.shape), const(nw.shape),
            const(wr2.shape), const(br.shape), const(tri.shape),
        ],
        out_specs=[
            pl.BlockSpec((tm, d), lambda i: (i, 0)),
            pl.BlockSpec((tm, d // LANES, LANES), lambda i: (i, 0, 0)),
            pl.BlockSpec((tm, 8), lambda i: (i, 0)),
            pl.BlockSpec((8, tm), lambda i: (0, i)),
            pl.BlockSpec((8, LANES), lambda i: (0, 0)),
        ],
        out_shape=[
            jax.ShapeDtypeStruct((n, d), F32),
            jax.ShapeDtypeStruct((n, d // LANES, LANES), BF16),
            jax.ShapeDtypeStruct((n, 8), F32),
            jax.ShapeDtypeStruct((8, n), F32),
            jax.ShapeDtypeStruct((8, LANES), F32),
        ],
        scratch_shapes=[pltpu.VMEM((8, LANES), F32)],
        compiler_params=_params(("arbitrary",)),
        name="mix_router",
    )(o_gla, o_ret, proj, proj, x2, wa, wb, wo, nw, wr2, br, tri)


def _to_row_tiles(x):
    return x.reshape(x.shape[0], x.shape[1] // LANES, LANES).astype(BF16)


def _from_row_tiles(x3):
    return x3.astype(F32).reshape(x3.shape[0], x3.shape[1] * x3.shape[2])


def _dispatch_kernel(tail_ref, has_ref, d0_ref, d1_ref, t_ref, xs_hbm, zero_ref, sem, zsem):
    step = pl.program_id(0)
    tok_n = t_ref.shape[0]
    blk = zero_ref.shape[0]

    def tail_copy(e):
        return pltpu.make_async_copy(zero_ref, xs_hbm.at[pl.ds(tail_ref[e], blk)], zsem)

    @pl.when(step == 0)
    def _():
        zero_ref[...] = jnp.zeros_like(zero_ref)
        for e in range(tail_ref.shape[0]):
            @pl.when(has_ref[e] > 0)
            def _(e=e):
                tail_copy(e).start()
        for e in range(tail_ref.shape[0]):
            @pl.when(has_ref[e] > 0)
            def _(e=e):
                tail_copy(e).wait()

    def row_copy(a, dst):
        return pltpu.make_async_copy(t_ref.at[a], xs_hbm.at[dst], sem)

    def body(a, carry):
        row_copy(a, d0_ref[a]).start(priority=0)
        row_copy(a, d1_ref[a]).start(priority=1)
        return carry

    lax.fori_loop(0, tok_n, body, 0, unroll=8)

    for _ in range(2):
        pltpu.make_async_copy(t_ref, xs_hbm.at[pl.ds(0, tok_n)], sem).wait()


def _dispatch(t3, dest0, dest1, tail, has, *, n_slots, chunk):
    n, s, l = t3.shape
    return pl.pallas_call(
        _dispatch_kernel,
        grid_spec=pltpu.PrefetchScalarGridSpec(
            num_scalar_prefetch=2,
            grid=(n // chunk,),
            in_specs=[
                pl.BlockSpec((chunk,), lambda i, *_: (i,), memory_space=pltpu.SMEM),
                pl.BlockSpec((chunk,), lambda i, *_: (i,), memory_space=pltpu.SMEM),
                pl.BlockSpec((chunk, s, l), lambda i, *_: (i, 0, 0)),
            ],
            out_specs=pl.BlockSpec(memory_space=pl.ANY),
            scratch_shapes=[
                pltpu.VMEM((EXPERT_ROWS, s, l), BF16),
                pltpu.SemaphoreType.DMA,
                pltpu.SemaphoreType.DMA,
            ],
        ),
        out_shape=jax.ShapeDtypeStruct((n_slots, s, l), BF16),
        compiler_params=pltpu.CompilerParams(dimension_semantics=("arbitrary",), has_side_effects=True,
                                             vmem_limit_bytes=VMEM_LIMIT, disable_bounds_checks=True),
        name="dispatch",
    )(tail, has, dest0, dest1, t3)


def _expert_kernel(be_ref, nu_ref, xs_ref, wg_ref, wu_ref, wd_ref, ys_ref, wg_s, wu_s, wd_s):
    b = pl.program_id(0)

    @pl.when((b == 0) | (be_ref[b] != be_ref[jnp.maximum(b - 1, 0)]))
    def _():
        wg_s[...] = wg_ref[0].astype(BF16)
        wu_s[...] = wu_ref[0].astype(BF16)
        wd_s[...] = wd_ref[0].astype(BF16)

    @pl.when(b < nu_ref[0])
    def _():
        x = _from_row_tiles(xs_ref[...]).astype(BF16)
        g = _dot(x, wg_s[...])
        u = _dot(x, wu_s[...])
        hid = (g * jax.nn.sigmoid(g) * u).astype(BF16)
        ys_ref[...] = _to_row_tiles(_dot(hid, wd_s[...]))

    @pl.when(b >= nu_ref[0])
    def _():
        ys_ref[...] = jnp.zeros_like(ys_ref)


def _experts(xs3, block_expert, n_used, wg, wu, wd):
    n_slots, s, l = xs3.shape
    d = s * l
    blk = EXPERT_ROWS
    hid = wg.shape[2]
    return pl.pallas_call(
        _expert_kernel,
        grid_spec=pltpu.PrefetchScalarGridSpec(
            num_scalar_prefetch=2,
            grid=(n_slots // blk,),
            in_specs=[
                pl.BlockSpec((blk, s, l), lambda b, be, nu: (jnp.minimum(b, nu[0] - 1), 0, 0)),
                pl.BlockSpec((1, d, hid), lambda b, be, nu: (be[b], 0, 0)),
                pl.BlockSpec((1, d, hid), lambda b, be, nu: (be[b], 0, 0)),
                pl.BlockSpec((1, hid, d), lambda b, be, nu: (be[b], 0, 0)),
            ],
            out_specs=pl.BlockSpec((blk, s, l), lambda b, be, nu: (b, 0, 0)),
            scratch_shapes=[pltpu.VMEM((d, hid), BF16), pltpu.VMEM((d, hid), BF16), pltpu.VMEM((hid, d), BF16)],
        ),
        out_shape=jax.ShapeDtypeStruct((n_slots, s, l), BF16),
        compiler_params=_params(("arbitrary",)),
        name="experts",
    )(block_expert, n_used, xs3, wg, wu, wd)


def _combine_kernel(c0_ref, c1_ref, n0_ref, n1_ref, h_ref, r_ref, nw_ref, ys_hbm, o_ref, buf, sem):
    i = pl.program_id(0)
    n_steps = pl.num_programs(0)
    tm = h_ref.shape[0]

    def row_copy(dst_row, src_row, slot):
        return pltpu.make_async_copy(ys_hbm.at[src_row], buf.at[slot, dst_row], sem.at[slot])

    def issue(d0, d1, slot):
        def body(a, carry):
            row_copy(a, d0[a], slot).start(priority=0)
            row_copy(tm + a, d1[a], slot).start(priority=1)
            return carry
        lax.fori_loop(0, tm, body, 0, unroll=8)

    @pl.when(i == 0)
    def _():
        issue(c0_ref, c1_ref, 0)

    @pl.when(i + 1 < n_steps)
    def _():
        issue(n0_ref, n1_ref, (i + 1) % 2)

    slot = i % 2
    pltpu.make_async_copy(ys_hbm.at[pl.ds(0, 2 * tm)], buf.at[slot], sem.at[slot]).wait()

    r = r_ref[...]
    y = _from_row_tiles(buf[slot, 0:tm]) * r[:, 2:3] + _from_row_tiles(buf[slot, tm:2 * tm]) * r[:, 3:4]
    h = h_ref[...] + y
    ms = jnp.mean(h * h, axis=-1, keepdims=True)
    o_ref[...] = h * lax.rsqrt(ms + EPS) * nw_ref[...]


def _combine(dest0, dest1, h1, r, nw, ys, *, tm):
    n, d = h1.shape
    n_steps = n // tm
    cur = pl.BlockSpec((tm,), lambda i: (i,), memory_space=pltpu.SMEM)
    nxt = pl.BlockSpec((tm,), lambda i: (jnp.minimum(i + 1, n_steps - 1),), memory_space=pltpu.SMEM)
    return pl.pallas_call(
        _combine_kernel,
        grid=(n_steps,),
        in_specs=[
            cur, cur, nxt, nxt,
            pl.BlockSpec((tm, d), lambda i: (i, 0)),
            pl.BlockSpec((tm, r.shape[1]), lambda i: (i, 0)),
            pl.BlockSpec((1, d), lambda i: (0, 0)),
            pl.BlockSpec(memory_space=pl.ANY),
        ],
        out_specs=pl.BlockSpec((tm, d), lambda i: (i, 0)),
        out_shape=jax.ShapeDtypeStruct((n, d), F32),
        scratch_shapes=[pltpu.VMEM((2, 2 * tm) + ys.shape[1:], BF16), pltpu.SemaphoreType.DMA((2,))],
        compiler_params=pltpu.CompilerParams(dimension_semantics=("arbitrary",), vmem_limit_bytes=VMEM_LIMIT,
                                             disable_bounds_checks=True),
        name="combine",
    )(dest0, dest1, dest0, dest1, h1, r, nw, ys)


def _layer(h, positions, norm_mix_w, w_in, gk_up, gk_bias, gla_norm_w, w_br_gla, ret_norm_w, w_br_ret, w_out,
           norm_ffn_w, rg_w, rg_b, re_w, re_b, wg, wu, wd, norm_final_w, *, tm_in, tn_in, gla_rows, tm_mix,
           tm_comb, disp_chunk):
    batch, seq, d = h.shape
    n = batch * seq
    x2 = h.reshape(n, d)
    gqk, gv = GLA_HEADS * GLA_DK, GLA_HEADS * GLA_DV
    rqk, rv = RET_HEADS * RET_DK, RET_HEADS * RET_DV

    sizes = (gqk, gqk, gv, gv, GLA_GATE_RANK, rqk, rqk, rv, rv, d, d)
    offs = np.concatenate([[0], np.cumsum(sizes)])
    seg = lambda i: w_in[:, offs[i]:offs[i + 1]]
    perm = np.concatenate([np.concatenate([np.arange(0, RET_DK, 2), np.arange(1, RET_DK, 2)]) + hh * RET_DK
                           for hh in range(RET_HEADS)])
    w_main = jnp.concatenate([seg(7), seg(8), seg(2), seg(3), seg(5)[:, perm], seg(6)[:, perm], seg(9), seg(10),
                              seg(0), seg(1)], axis=1).astype(BF16)
    assert rv % gv == 0 and gv == rqk == d and rv == 2 * d and gqk * 2 == d
    ret_cols = (4 + 2, 4 + 3, 0, 1)
    gla_cols = (2 * (4 + 6), 2 * (4 + 6) + 1, 4, 5)
    mix_cols = (4 + 4, 4 + 5)
    w_gd = jnp.pad(seg(4), ((0, 0), (0, LANES - GLA_GATE_RANK))).astype(BF16)

    proj, gd = _inproj(x2, norm_mix_w.reshape(1, d), w_main, w_gd, tm=tm_in, tn=tn_in)

    up = jnp.pad(gk_up, ((0, LANES - GLA_GATE_RANK), (0, 0)))
    up_hi = up.astype(BF16)
    up2 = jnp.stack([up_hi, (up - up_hi.astype(F32)).astype(BF16)])
    o_gla = _gla(proj, gd, up2, gk_bias.reshape(1, gqk), gla_norm_w.reshape(1, GLA_DV),
                 batch=batch, seq=seq, rows=gla_rows, cols=gla_cols)

    theta = (1.0 / (ROPE_BASE ** jnp.linspace(0.0, 1.0, RET_DK // 2, dtype=F32))).reshape(1, RET_DK // 2)
    o_ret = _retention(proj, positions.reshape(n, 1), theta, ret_norm_w.reshape(1, rv),
                       batch=batch, seq=seq, cols=ret_cols)

    wr = jnp.concatenate([re_w.transpose(1, 0, 2).reshape(d, N_EXPERTS), rg_w], axis=1)
    wr = jnp.pad(wr, ((0, 0), (0, LANES - N_EXPERTS - N_GROUPS)))
    wr_hi = wr.astype(BF16)
    wr2 = jnp.stack([wr_hi, (wr - wr_hi.astype(F32)).astype(BF16)])
    br = jnp.pad(jnp.concatenate([re_b.reshape(-1), rg_b]), (0, LANES - N_EXPERTS - N_GROUPS)).reshape(1, LANES)
    h1, t, r, rt, cnt = _mix(o_gla, o_ret, proj, x2, w_br_gla.astype(BF16), w_br_ret.astype(BF16),
                             w_out.astype(BF16), norm_ffn_w.reshape(1, d), wr2, br, tm=tm_mix, cols=mix_cols)

    blk = EXPERT_ROWS
    n_assign = 2 * n
    n_slots = -(-(n_assign + N_EXPERTS * (blk - 1)) // blk) * blk
    counts = cnt[0, :N_EXPERTS].astype(jnp.int32)
    padded = ((counts + blk - 1) // blk) * blk
    pad_end = jnp.cumsum(padded)
    pad_start = pad_end - padded
    dest0 = pad_start[rt[0].astype(jnp.int32)] + rt[4].astype(jnp.int32)
    dest1 = pad_start[rt[1].astype(jnp.int32)] + rt[5].astype(jnp.int32)
    block_start = jnp.arange(n_slots // blk, dtype=jnp.int32) * blk
    block_expert = jnp.minimum(jnp.sum((pad_end[None, :] <= block_start[:, None]).astype(jnp.int32), axis=1),
                               N_EXPERTS - 1)
    n_used = (pad_end[-1:] // blk).astype(jnp.int32)
    spare = pad_end[-1] + jnp.arange(N_EXPERTS, dtype=jnp.int32) * blk
    tail = jnp.concatenate([jnp.maximum(pad_end - blk, 0), jnp.minimum(spare, n_slots - blk)]).astype(jnp.int32)
    has = jnp.concatenate([counts, (spare < n_slots).astype(jnp.int32)])

    xs = _dispatch(t, dest0, dest1, tail, has, n_slots=n_slots, chunk=disp_chunk)
    ys = _experts(xs, block_expert, n_used, wg, wu, wd)
    out = _combine(dest0, dest1, h1, r, norm_final_w.reshape(1, d), ys, tm=tm_comb)
    return out.reshape(batch, seq, d)


def kernel(x, positions, norm_mix_w, w_in, gla_gk_up, gla_gk_bias, gla_norm_w, w_branch_gla, ret_norm_w,
           w_branch_ret, w_out, norm_ffn_w, router_group_w, router_group_b, router_expert_w, router_expert_b,
           expert_w_gate, expert_w_up, expert_w_down, norm_final_w):
    assert norm_mix_w.shape[0] == 1, "single-layer block"
    return _layer(x, positions, norm_mix_w[0], w_in[0], gla_gk_up[0], gla_gk_bias[0], gla_norm_w[0], w_branch_gla[0],
                  ret_norm_w[0], w_branch_ret[0], w_out[0], norm_ffn_w[0], router_group_w[0], router_group_b[0],
                  router_expert_w[0], router_expert_b[0], expert_w_gate[0], expert_w_up[0], expert_w_down[0],
                  norm_final_w, tm_in=2048, tn_in=1024, gla_rows=256, tm_mix=512, tm_comb=256,
                  disp_chunk=1024)
```

```python
import functools
import math

import jax
import jax.numpy as jnp
import numpy as np
from jax import lax
from jax.experimental import pallas as pl
from jax.experimental.pallas import tpu as pltpu

F32 = jnp.float32
BF16 = jnp.bfloat16

EPS = 1e-6
GLA_HEADS = 4
GLA_DK = 128
GLA_DV = 256
GLA_GATE_RANK = 16
GLA_GATE_TEMP = 16.0
RET_HEADS = 4
RET_DK = 256
RET_DV = 512
ROPE_BASE = 10000.0
N_GROUPS = 4
EXPERTS_PER_GROUP = 8
N_EXPERTS = N_GROUPS * EXPERTS_PER_GROUP
EXPERT_HIDDEN = 512

LANES = 128
GLA_CHUNK = 128
GLA_DIAG = 8
RET_CHUNK = 256
EXPERT_ROWS = 512
VMEM_LIMIT = 48 * 1024 * 1024


def _dot(a, b):
    return jnp.dot(a, b, preferred_element_type=F32)


def _dot_nt(a, b):
    return lax.dot_general(a, b, (((1,), (1,)), ((), ())), preferred_element_type=F32)


def _split2(a):
    hi = a.astype(BF16)
    lo = (a - hi.astype(F32)).astype(BF16)
    return hi, lo


def _params(sem, vmem=VMEM_LIMIT):
    return pltpu.CompilerParams(dimension_semantics=sem, vmem_limit_bytes=vmem)


def _inproj_kernel(x_ref, nw_ref, w_ref, wgd_ref, proj_hbm, gd_ref, stage, sem):
    tm = x_ref.shape[0]
    tn = stage.shape[2]
    n_col = w_ref.shape[1] // tn
    row0 = pl.multiple_of(pl.program_id(0) * tm, tm)

    def out_copy(j):
        return pltpu.make_async_copy(stage.at[j % 2], proj_hbm.at[pl.ds(row0, tm), pl.ds(j * tn, tn)], sem.at[j % 2])

    x = x_ref[...]
    ms = jnp.mean(x * x, axis=-1, keepdims=True)
    u = (x * lax.rsqrt(ms + EPS) * nw_ref[...]).astype(BF16)
    gd_ref[...] = _dot(u, wgd_ref[...])
    for j in range(n_col):
        if j >= 2:
            out_copy(j - 2).wait()
        stage[j % 2] = _dot(u, w_ref[:, j * tn:(j + 1) * tn]).astype(BF16)
        out_copy(j).start()
    for j in range(max(n_col - 2, 0), n_col):
        out_copy(j).wait()


def _inproj(x2, nw, w_main, w_gd, *, tm, tn):
    n, d = x2.shape
    p = w_main.shape[1]
    return pl.pallas_call(
        _inproj_kernel,
        grid=(n // tm,),
        in_specs=[
            pl.BlockSpec((tm, d), lambda i: (i, 0)),
            pl.BlockSpec((1, d), lambda i: (0, 0)),
            pl.BlockSpec((d, p), lambda i: (0, 0), pipeline_mode=pl.Buffered(1)),
            pl.BlockSpec((d, LANES), lambda i: (0, 0)),
        ],
        out_specs=[
            pl.BlockSpec(memory_space=pl.ANY),
            pl.BlockSpec((tm, LANES), lambda i: (i, 0)),
        ],
        out_shape=[
            jax.ShapeDtypeStruct((n, p), BF16),
            jax.ShapeDtypeStruct((n, LANES), F32),
        ],
        scratch_shapes=[pltpu.VMEM((2, tm, tn), BF16), pltpu.SemaphoreType.DMA((2,))],
        compiler_params=_params(("arbitrary",)),
        name="inproj",
    )(x2, nw, w_main, w_gd)


def _gla_tables(c):
    levels = []
    s = c // 2
    while s >= GLA_DIAG:
        levels.append(s)
        s //= 2
    i = np.arange(c)[:, None]
    t = np.arange(c)[None, :]
    mats = [t <= i, t > i]
    masks = []
    for s in levels:
        bs = (i // s) * s
        mats.append((t > bs) & (t <= i))
        mats.append((t > i) & (t <= np.minimum(bs + s, c - 1)))
        masks.append(((i // (2 * s)) == (t // (2 * s))) & (((i // s) % 2) == 1) & (((t // s) % 2) == 0))
    for d in range(1, GLA_DIAG):
        mats.append((t > i - d) & (t <= i))
    for d in range(GLA_DIAG):
        masks.append((t == i - d) & ((i % GLA_DIAG) >= d))
    g = np.concatenate(mats, 0).astype(np.float32)
    m = np.stack(masks, 0).astype(np.float32)
    return g, m, tuple(levels)


def _gla_kernel(q_ref, k_ref, v_ref, gg_ref, gd_ref, up_ref, bias_ref, nw_ref, g_ref, m_ref,
                o_ref, st_ref, *, c, nlev):
    h_n, dk, dv = GLA_HEADS, GLA_DK, GLA_DV
    rows = q_ref.shape[0]

    @pl.when(pl.program_id(1) == 0)
    def _():
        st_ref[...] = jnp.zeros_like(st_ref)

    g_tab = g_ref[...]
    up_hi = up_ref[0]
    up_lo = up_ref[1]
    q_scale = dk ** -0.5
    band0 = 2 + 2 * nlev

    def chunk(ci, carry):
        r0 = pl.multiple_of(ci * c, c)
        rs = pl.ds(r0, c)
        gd_hi, gd_lo = _split2(gd_ref[rs, :])
        xg = _dot(gd_hi, up_hi) + _dot(gd_hi, up_lo) + _dot(gd_lo, up_hi) + bias_ref[...]
        la = (jnp.minimum(xg, 0.0) - jnp.log1p(jnp.exp(-jnp.abs(xg)))) * (1.0 / GLA_GATE_TEMP)
        ex = jnp.exp(_dot(g_tab, la.astype(BF16)))

        for h in range(h_n):
            ks = slice(h * dk, (h + 1) * dk)
            vs = slice(h * dv, (h + 1) * dv)

            def tab(m, ks=ks):
                return ex[m * c:(m + 1) * c, ks]

            qh = q_ref[rs, ks].astype(F32) * q_scale
            kh = k_ref[rs, ks].astype(F32)
            vh = v_ref[rs, vs]
            st = st_ref[h]
            eb = tab(0)
            o = _dot_nt((qh * eb).astype(BF16), st.astype(BF16))
            a = m_ref[nlev] * jnp.sum(qh * kh, axis=-1, keepdims=True)
            for li in range(nlev):
                qs = (qh * tab(2 + 2 * li)).astype(BF16)
                kk = (kh * tab(3 + 2 * li)).astype(BF16)
                a = a + m_ref[li] * _dot_nt(qs, kk)
            for d in range(1, GLA_DIAG):
                kr = pltpu.roll(kh, d, 0)
                dd = jnp.sum(qh * kr * tab(band0 + d - 1), axis=-1, keepdims=True)
                a = a + m_ref[nlev + d] * dd
            o = o + _dot(a.astype(BF16), vh)
            e_last = eb[c - 1:c, :]
            k_st = (kh * tab(1)).astype(BF16)
            v_t = vh.astype(F32).T.astype(BF16)
            st_ref[h] = st * e_last + _dot(v_t, k_st)

            ms = jnp.mean(o * o, axis=-1, keepdims=True)
            y = o * lax.rsqrt(ms + EPS) * nw_ref[...]
            gg = gg_ref[rs, vs].astype(F32)
            o_ref[rs, vs] = (y * (gg * jax.nn.sigmoid(gg))).astype(BF16)
        return carry

    lax.fori_loop(0, rows // c, chunk, 0, unroll=2)


def _gla(proj, gd, up2, bias, nw, *, batch, seq, rows, cols):
    n = proj.shape[0]
    c = GLA_CHUNK
    g_np, m_np, levels = _gla_tables(c)
    g_tab = jnp.asarray(g_np, BF16)
    m_tab = jnp.asarray(m_np, F32)
    qk_w = GLA_HEADS * GLA_DK
    v_w = GLA_HEADS * GLA_DV
    spb = seq // rows
    row = lambda b, s: b * spb + s
    kern = functools.partial(_gla_kernel, c=c, nlev=len(levels))
    return pl.pallas_call(
        kern,
        grid=(batch, spb),
        in_specs=[
            pl.BlockSpec((rows, qk_w), lambda b, s: (row(b, s), cols[0])),
            pl.BlockSpec((rows, qk_w), lambda b, s: (row(b, s), cols[1])),
            pl.BlockSpec((rows, v_w), lambda b, s: (row(b, s), cols[2])),
            pl.BlockSpec((rows, v_w), lambda b, s: (row(b, s), cols[3])),
            pl.BlockSpec((rows, LANES), lambda b, s: (row(b, s), 0)),
            pl.BlockSpec(up2.shape, lambda b, s: (0, 0, 0)),
            pl.BlockSpec(bias.shape, lambda b, s: (0, 0)),
            pl.BlockSpec(nw.shape, lambda b, s: (0, 0)),
            pl.BlockSpec(g_tab.shape, lambda b, s: (0, 0)),
            pl.BlockSpec(m_tab.shape, lambda b, s: (0, 0, 0)),
        ],
        out_specs=pl.BlockSpec((rows, v_w), lambda b, s: (row(b, s), 0)),
        out_shape=jax.ShapeDtypeStruct((n, v_w), BF16),
        scratch_shapes=[pltpu.VMEM((GLA_HEADS, GLA_DV, GLA_DK), F32)],
        compiler_params=_params(("arbitrary", "arbitrary")),
        name="gla",
    )(proj, proj, proj, proj, gd, up2, bias, nw, g_tab, m_tab)


def _ret_log_gamma(h):
    return math.log(1.0 - 2.0 ** (-5.0 - h))


def _ret_kernel(pos_ref, th_ref, q_ref, k_ref, v_ref, rg_ref, nw_ref, o_ref, s_ref, dm_ref):
    h_n, dk, dv = RET_HEADS, RET_DK, RET_DV
    c = q_ref.shape[0]
    half = dk // 2

    @pl.when(pl.program_id(1) == 0)
    def _():
        s_ref[...] = jnp.zeros_like(s_ref)

    @pl.when((pl.program_id(0) == 0) & (pl.program_id(1) == 0))
    def _():
        rel = (lax.broadcasted_iota(jnp.int32, (c, c), 0) - lax.broadcasted_iota(jnp.int32, (c, c), 1)).astype(F32)
        for h in range(h_n):
            dm_ref[h] = jnp.where(rel >= 0, jnp.exp(_ret_log_gamma(h) * jnp.maximum(rel, 0.0)), 0.0)

    ang = pos_ref[...].astype(F32) * th_ref[...]
    cs = jnp.cos(ang)
    sn = jnp.sin(ang)
    idx = lax.broadcasted_iota(jnp.int32, (c, 1), 0).astype(F32)
    k_scale = dk ** -0.5

    def rot(ref, h):
        a = ref[:, h * dk:h * dk + half].astype(F32)
        b = ref[:, h * dk + half:(h + 1) * dk].astype(F32)
        return jnp.concatenate([a * cs - b * sn, b * cs + a * sn], axis=-1)

    for h in range(h_n):
        lg = _ret_log_gamma(h)
        vs = slice(h * dv, (h + 1) * dv)
        qr = rot(q_ref, h)
        kr = rot(k_ref, h) * k_scale
        vh = v_ref[:, vs]
        scores = _dot_nt(qr.astype(BF16), kr.astype(BF16)) * dm_ref[h]
        q_dec = jnp.exp(lg * (idx + 1.0))
        k_dec = jnp.exp(lg * (c - 1.0 - idx))
        s = s_ref[h]
        o = _dot(scores.astype(BF16), vh) + _dot((qr * q_dec).astype(BF16), s.astype(BF16))
        k_t = (kr * k_dec).T.astype(BF16)
        s_ref[h] = math.exp(lg * c) * s + _dot(k_t, vh)

        mu = jnp.mean(o, axis=-1, keepdims=True)
        oc = o - mu
        var = jnp.mean(oc * oc, axis=-1, keepdims=True)
        y = oc * lax.rsqrt(var + EPS) * nw_ref[:, vs]
        rg = rg_ref[:, vs].astype(F32)
        o_ref[:, vs] = (y * (rg * jax.nn.sigmoid(rg))).astype(BF16)


def _retention(proj, pos, theta, nw, *, batch, seq, cols):
    n = proj.shape[0]
    c = RET_CHUNK
    qk_w = RET_HEADS * RET_DK
    v_w = RET_HEADS * RET_DV
    spb = seq // c
    row = lambda b, s: b * spb + s
    return pl.pallas_call(
        _ret_kernel,
        grid=(batch, spb),
        in_specs=[
            pl.BlockSpec((c, 1), lambda b, s: (row(b, s), 0)),
            pl.BlockSpec(theta.shape, lambda b, s: (0, 0)),
            pl.BlockSpec((c, qk_w), lambda b, s: (row(b, s), cols[0])),
            pl.BlockSpec((c, qk_w), lambda b, s: (row(b, s), cols[1])),
            pl.BlockSpec((c, v_w), lambda b, s: (row(b, s), cols[2])),
            pl.BlockSpec((c, v_w), lambda b, s: (row(b, s), cols[3])),
            pl.BlockSpec(nw.shape, lambda b, s: (0, 0)),
        ],
        out_specs=pl.BlockSpec((c, v_w), lambda b, s: (row(b, s), 0)),
        out_shape=jax.ShapeDtypeStruct((n, v_w), BF16),
        scratch_shapes=[
            pltpu.VMEM((RET_HEADS, RET_DK, RET_DV), F32),
            pltpu.VMEM((RET_HEADS, c, c), F32),
        ],
        compiler_params=_params(("arbitrary", "arbitrary")),
        name="retention",
    )(pos, theta, proj, proj, proj, proj, nw)


def _mix_kernel(og_ref, or_ref, ma_ref, mb_ref, x_ref, wa_ref, wb_ref, wo_ref, nw_ref, wr_ref, br_ref,
                tri_ref, h_ref, t_ref, r_ref, rt_ref, cnt_ref, base_ref):
    tm = x_ref.shape[0]

    @pl.when(pl.program_id(0) == 0)
    def _():
        base_ref[...] = jnp.zeros_like(base_ref)

    ya = _dot(og_ref[...], wa_ref[...])
    yb = _dot(or_ref[...], wb_ref[...])
    merged = (jax.nn.sigmoid(ma_ref[...].astype(F32)) * ya + jax.nn.sigmoid(mb_ref[...].astype(F32)) * yb)
    h = x_ref[...] + _dot(merged.astype(BF16), wo_ref[...])
    h_ref[...] = h
    ms = jnp.mean(h * h, axis=-1, keepdims=True)
    t = h * lax.rsqrt(ms + EPS) * nw_ref[...]
    t_ref[...] = _to_row_tiles(t)

    lane = lax.broadcasted_iota(jnp.int32, (tm, LANES), 1)
    neg = jnp.float32(-1e30)
    big = jnp.int32(1 << 20)

    def first_max(v):
        m = jnp.max(v, axis=-1, keepdims=True)
        return m, jnp.min(jnp.where(v == m, lane, big), axis=-1, keepdims=True)

    t_hi, t_lo = _split2(t)
    lg = _dot(t_hi, wr_ref[0]) + _dot(t_hi, wr_ref[1]) + _dot(t_lo, wr_ref[0]) + br_ref[...]
    g_valid = (lane >= N_EXPERTS) & (lane < N_EXPERTS + N_GROUPS)
    g_m, g_lane = first_max(jnp.where(g_valid, lg, neg))
    g_w = 1.0 / jnp.sum(jnp.where(g_valid, jnp.exp(lg - g_m), 0.0), axis=-1, keepdims=True)
    g_idx = g_lane - N_EXPERTS
    e_valid = (lane < N_EXPERTS) & ((lane // EXPERTS_PER_GROUP) == g_idx)
    el = jnp.where(e_valid, lg, neg)
    v1, i1 = first_max(el)
    v2, i2 = first_max(jnp.where(lane == i1, neg, el))
    e21 = jnp.exp(v2 - v1)
    w1 = g_w / (1.0 + e21)
    w2 = g_w * e21 / (1.0 + e21)

    o1 = lane == i1
    o2 = lane == i2
    osum = jnp.where(o1 | o2, 1.0, 0.0)
    base = base_ref[0:1, :]
    before = _dot(tri_ref[...], osum.astype(BF16)) + base
    r1 = jnp.sum(jnp.where(o1, before, 0.0), axis=-1, keepdims=True)
    r2 = jnp.sum(jnp.where(o2, before, 0.0), axis=-1, keepdims=True)
    base = base + jnp.sum(osum, axis=0, keepdims=True)
    base_ref[...] = jnp.broadcast_to(base, base_ref.shape)
    cnt_ref[...] = jnp.broadcast_to(base, cnt_ref.shape)

    cols = (i1.astype(F32), i2.astype(F32), w1, w2, r1, r2)
    wide = jnp.zeros((tm, LANES), F32)
    for ci, col in enumerate(cols):
        wide = jnp.where(lane == ci, col, wide)
    r_ref[...] = wide[:, :r_ref.shape[1]]
    rt_ref[...] = wide.T[:rt_ref.shape[0], :]


def _mix(o_gla, o_ret, proj, x2, wa, wb, wo, nw, wr2, br, *, tm, cols):
    n, d = x2.shape
    tri = jnp.asarray(np.tril(np.ones((tm, tm), np.float32), -1), BF16)
    const = lambda shape: pl.BlockSpec(shape, lambda i: (0,) * len(shape))
    return pl.pallas_call(
        _mix_kernel,
        grid=(n // tm,),
        in_specs=[
            pl.BlockSpec((tm, o_gla.shape[1]), lambda i: (i, 0)),
            pl.BlockSpec((tm, o_ret.shape[1]), lambda i: (i, 0)),
            pl.BlockSpec((tm, d), lambda i: (i, cols[0])),
            pl.BlockSpec((tm, d), lambda i: (i, cols[1])),
            pl.BlockSpec((tm, d), lambda i: (i, 0)),
            const(wa.shape), const(wb.shape), const(wo.shape), const(nw.shape),
            const(wr2.shape), const(br.shape), const(tri.shape),
        ],
        out_specs=[
            pl.BlockSpec((tm, d), lambda i: (i, 0)),
            pl.BlockSpec((tm, d // LANES, LANES), lambda i: (i, 0, 0)),
            pl.BlockSpec((tm, 8), lambda i: (i, 0)),
            pl.BlockSpec((8, tm), lambda i: (0, i)),
            pl.BlockSpec((8, LANES), lambda i: (0, 0)),
        ],
        out_shape=[
            jax.ShapeDtypeStruct((n, d), F32),
            jax.ShapeDtypeStruct((n, d // LANES, LANES), BF16),
            jax.ShapeDtypeStruct((n, 8), F32),
            jax.ShapeDtypeStruct((8, n), F32),
            jax.ShapeDtypeStruct((8, LANES), F32),
        ],
        scratch_shapes=[pltpu.VMEM((8, LANES), F32)],
        compiler_params=_params(("arbitrary",)),
        name="mix_router",
    )(o_gla, o_ret, proj, proj, x2, wa, wb, wo, nw, wr2, br, tri)


def _to_row_tiles(x):
    return x.reshape(x.shape[0], x.shape[1] // LANES, LANES).astype(BF16)


def _from_row_tiles(x3):
    return x3.astype(F32).reshape(x3.shape[0], x3.shape[1] * x3.shape[2])


def _dispatch_kernel(tail_ref, has_ref, d0_ref, d1_ref, t_ref, xs_hbm, zero_ref, sem, zsem):
    step = pl.program_id(0)
    tok_n = t_ref.shape[0]
    blk = zero_ref.shape[0]

    def tail_copy(e):
        return pltpu.make_async_copy(zero_ref, xs_hbm.at[pl.ds(tail_ref[e], blk)], zsem)

    @pl.when(step == 0)
    def _():
        zero_ref[...] = jnp.zeros_like(zero_ref)
        for e in range(tail_ref.shape[0]):
            @pl.when(has_ref[e] > 0)
            def _(e=e):
                tail_copy(e).start()
        for e in range(tail_ref.shape[0]):
            @pl.when(has_ref[e] > 0)
            def _(e=e):
                tail_copy(e).wait()

    def row_copy(a, dst):
        return pltpu.make_async_copy(t_ref.at[a], xs_hbm.at[dst], sem)

    def body(a, carry):
        row_copy(a, d0_ref[a]).start(priority=0)
        row_copy(a, d1_ref[a]).start(priority=1)
        return carry

    lax.fori_loop(0, tok_n, body, 0, unroll=8)

    for _ in range(2):
        pltpu.make_async_copy(t_ref, xs_hbm.at[pl.ds(0, tok_n)], sem).wait()


def _dispatch(t3, dest0, dest1, tail, has, *, n_slots, chunk):
    n, s, l = t3.shape
    return pl.pallas_call(
        _dispatch_kernel,
        grid_spec=pltpu.PrefetchScalarGridSpec(
            num_scalar_prefetch=2,
            grid=(n // chunk,),
            in_specs=[
                pl.BlockSpec((chunk,), lambda i, *_: (i,), memory_space=pltpu.SMEM),
                pl.BlockSpec((chunk,), lambda i, *_: (i,), memory_space=pltpu.SMEM),
                pl.BlockSpec((chunk, s, l), lambda i, *_: (i, 0, 0)),
            ],
            out_specs=pl.BlockSpec(memory_space=pl.ANY),
            scratch_shapes=[
                pltpu.VMEM((EXPERT_ROWS, s, l), BF16),
                pltpu.SemaphoreType.DMA,
                pltpu.SemaphoreType.DMA,
            ],
        ),
        out_shape=jax.ShapeDtypeStruct((n_slots, s, l), BF16),
        compiler_params=pltpu.CompilerParams(dimension_semantics=("arbitrary",), has_side_effects=True,
                                             vmem_limit_bytes=VMEM_LIMIT, disable_bounds_checks=True),
        name="dispatch",
    )(tail, has, dest0, dest1, t3)


def _expert_kernel(be_ref, nu_ref, xs_ref, wg_ref, wu_ref, wd_ref, ys_ref, wg_s, wu_s, wd_s):
    b = pl.program_id(0)

    @pl.when((b == 0) | (be_ref[b] != be_ref[jnp.maximum(b - 1, 0)]))
    def _():
        wg_s[...] = wg_ref[0].astype(BF16)
        wu_s[...] = wu_ref[0].astype(BF16)
        wd_s[...] = wd_ref[0].astype(BF16)

    @pl.when(b < nu_ref[0])
    def _():
        x = _from_row_tiles(xs_ref[...]).astype(BF16)
        g = _dot(x, wg_s[...])
        u = _dot(x, wu_s[...])
        hid = (g * jax.nn.sigmoid(g) * u).astype(BF16)
        ys_ref[...] = _to_row_tiles(_dot(hid, wd_s[...]))

    @pl.when(b >= nu_ref[0])
    def _():
        ys_ref[...] = jnp.zeros_like(ys_ref)


def _experts(xs3, block_expert, n_used, wg, wu, wd):
    n_slots, s, l = xs3.shape
    d = s * l
    blk = EXPERT_ROWS
    hid = wg.shape[2]
    return pl.pallas_call(
        _expert_kernel,
        grid_spec=pltpu.PrefetchScalarGridSpec(
            num_scalar_prefetch=2,
            grid=(n_slots // blk,),
            in_specs=[
                pl.BlockSpec((blk, s, l), lambda b, be, nu: (jnp.minimum(b, nu[0] - 1), 0, 0)),
                pl.BlockSpec((1, d, hid), lambda b, be, nu: (be[b], 0, 0)),
                pl.BlockSpec((1, d, hid), lambda b, be, nu: (be[b], 0, 0)),
                pl.BlockSpec((1, hid, d), lambda b, be, nu: (be[b], 0, 0)),
            ],
            out_specs=pl.BlockSpec((blk, s, l), lambda b, be, nu: (b, 0, 0)),
            scratch_shapes=[pltpu.VMEM((d, hid), BF16), pltpu.VMEM((d, hid), BF16), pltpu.VMEM((hid, d), BF16)],
        ),
        out_shape=jax.ShapeDtypeStruct((n_slots, s, l), BF16),
        compiler_params=_params(("arbitrary",)),
        name="experts",
    )(block_expert, n_used, xs3, wg, wu, wd)


def _combine_kernel(c0_ref, c1_ref, n0_ref, n1_ref, h_ref, r_ref, nw_ref, ys_hbm, o_ref, buf, sem):
    i = pl.program_id(0)
    n_steps = pl.num_programs(0)
    tm = h_ref.shape[0]

    def row_copy(dst_row, src_row, slot):
        return pltpu.make_async_copy(ys_hbm.at[src_row], buf.at[slot, dst_row], sem.at[slot])

    def issue(d0, d1, slot):
        def body(a, carry):
            row_copy(a, d0[a], slot).start(priority=0)
            row_copy(tm + a, d1[a], slot).start(priority=1)
            return carry
        lax.fori_loop(0, tm, body, 0, unroll=8)

    @pl.when(i == 0)
    def _():
        issue(c0_ref, c1_ref, 0)

    @pl.when(i + 1 < n_steps)
    def _():
        issue(n0_ref, n1_ref, (i + 1) % 2)

    slot = i % 2
    pltpu.make_async_copy(ys_hbm.at[pl.ds(0, 2 * tm)], buf.at[slot], sem.at[slot]).wait()

    r = r_ref[...]
    y = _from_row_tiles(buf[slot, 0:tm]) * r[:, 2:3] + _from_row_tiles(buf[slot, tm:2 * tm]) * r[:, 3:4]
    h = h_ref[...] + y
    ms = jnp.mean(h * h, axis=-1, keepdims=True)
    o_ref[...] = h * lax.rsqrt(ms + EPS) * nw_ref[...]


def _combine(dest0, dest1, h1, r, nw, ys, *, tm):
    n, d = h1.shape
    n_steps = n // tm
    cur = pl.BlockSpec((tm,), lambda i: (i,), memory_space=pltpu.SMEM)
    nxt = pl.BlockSpec((tm,), lambda i: (jnp.minimum(i + 1, n_steps - 1),), memory_space=pltpu.SMEM)
    return pl.pallas_call(
        _combine_kernel,
        grid=(n_steps,),
        in_specs=[
            cur, cur, nxt, nxt,
            pl.BlockSpec((tm, d), lambda i: (i, 0)),
            pl.BlockSpec((tm, r.shape[1]), lambda i: (i, 0)),
            pl.BlockSpec((1, d), lambda i: (0, 0)),
            pl.BlockSpec(memory_space=pl.ANY),
        ],
        out_specs=pl.BlockSpec((tm, d), lambda i: (i, 0)),
        out_shape=jax.ShapeDtypeStruct((n, d), F32),
        scratch_shapes=[pltpu.VMEM((2, 2 * tm) + ys.shape[1:], BF16), pltpu.SemaphoreType.DMA((2,))],
        compiler_params=pltpu.CompilerParams(dimension_semantics=("arbitrary",), vmem_limit_bytes=VMEM_LIMIT,
                                             disable_bounds_checks=True),
        name="combine",
    )(dest0, dest1, dest0, dest1, h1, r, nw, ys)


def _layer(h, positions, norm_mix_w, w_in, gk_up, gk_bias, gla_norm_w, w_br_gla, ret_norm_w, w_br_ret, w_out,
           norm_ffn_w, rg_w, rg_b, re_w, re_b, wg, wu, wd, norm_final_w, *, tm_in, tn_in, gla_rows, tm_mix,
           tm_comb, disp_chunk):
    batch, seq, d = h.shape
    n = batch * seq
    x2 = h.reshape(n, d)
    gqk, gv = GLA_HEADS * GLA_DK, GLA_HEADS * GLA_DV
    rqk, rv = RET_HEADS * RET_DK, RET_HEADS * RET_DV

    sizes = (gqk, gqk, gv, gv, GLA_GATE_RANK, rqk, rqk, rv, rv, d, d)
    offs = np.concatenate([[0], np.cumsum(sizes)])
    seg = lambda i: w_in[:, offs[i]:offs[i + 1]]
    perm = np.concatenate([np.concatenate([np.arange(0, RET_DK, 2), np.arange(1, RET_DK, 2)]) + hh * RET_DK
                           for hh in range(RET_HEADS)])
    w_main = jnp.concatenate([seg(7), seg(8), seg(2), seg(3), seg(5)[:, perm], seg(6)[:, perm], seg(9), seg(10),
                              seg(0), seg(1)], axis=1).astype(BF16)
    assert rv % gv == 0 and gv == rqk == d and rv == 2 * d and gqk * 2 == d
    ret_cols = (4 + 2, 4 + 3, 0, 1)
    gla_cols = (2 * (4 + 6), 2 * (4 + 6) + 1, 4, 5)
    mix_cols = (4 + 4, 4 + 5)
    w_gd = jnp.pad(seg(4), ((0, 0), (0, LANES - GLA_GATE_RANK))).astype(BF16)

    proj, gd = _inproj(x2, norm_mix_w.reshape(1, d), w_main, w_gd, tm=tm_in, tn=tn_in)

    up = jnp.pad(gk_up, ((0, LANES - GLA_GATE_RANK), (0, 0)))
    up_hi = up.astype(BF16)
    up2 = jnp.stack([up_hi, (up - up_hi.astype(F32)).astype(BF16)])
    o_gla = _gla(proj, gd, up2, gk_bias.reshape(1, gqk), gla_norm_w.reshape(1, GLA_DV),
                 batch=batch, seq=seq, rows=gla_rows, cols=gla_cols)

    theta = (1.0 / (ROPE_BASE ** jnp.linspace(0.0, 1.0, RET_DK // 2, dtype=F32))).reshape(1, RET_DK // 2)
    o_ret = _retention(proj, positions.reshape(n, 1), theta, ret_norm_w.reshape(1, rv),
                       batch=batch, seq=seq, cols=ret_cols)

    wr = jnp.concatenate([re_w.transpose(1, 0, 2).reshape(d, N_EXPERTS), rg_w], axis=1)
    wr = jnp.pad(wr, ((0, 0), (0, LANES - N_EXPERTS - N_GROUPS)))
    wr_hi = wr.astype(BF16)
    wr2 = jnp.stack([wr_hi, (wr - wr_hi.astype(F32)).astype(BF16)])
    br = jnp.pad(jnp.concatenate([re_b.reshape(-1), rg_b]), (0, LANES - N_EXPERTS - N_GROUPS)).reshape(1, LANES)
    h1, t, r, rt, cnt = _mix(o_gla, o_ret, proj, x2, w_br_gla.astype(BF16), w_br_ret.astype(BF16),
                             w_out.astype(BF16), norm_ffn_w.reshape(1, d), wr2, br, tm=tm_mix, cols=mix_cols)

    blk = EXPERT_ROWS
    n_assign = 2 * n
    n_slots = -(-(n_assign + N_EXPERTS * (blk - 1)) // blk) * blk
    counts = cnt[0, :N_EXPERTS].astype(jnp.int32)
    padded = ((counts + blk - 1) // blk) * blk
    pad_end = jnp.cumsum(padded)
    pad_start = pad_end - padded
    dest0 = pad_start[rt[0].astype(jnp.int32)] + rt[4].astype(jnp.int32)
    dest1 = pad_start[rt[1].astype(jnp.int32)] + rt[5].astype(jnp.int32)
    block_start = jnp.arange(n_slots // blk, dtype=jnp.int32) * blk
    block_expert = jnp.minimum(jnp.sum((pad_end[None, :] <= block_start[:, None]).astype(jnp.int32), axis=1),
                               N_EXPERTS - 1)
    n_used = (pad_end[-1:] // blk).astype(jnp.int32)
    spare = pad_end[-1] + jnp.arange(N_EXPERTS, dtype=jnp.int32) * blk
    tail = jnp.concatenate([jnp.maximum(pad_end - blk, 0), jnp.minimum(spare, n_slots - blk)]).astype(jnp.int32)
    has = jnp.concatenate([counts, (spare < n_slots).astype(jnp.int32)])

    xs = _dispatch(t, dest0, dest1, tail, has, n_slots=n_slots, chunk=disp_chunk)
    ys = _experts(xs, block_expert, n_used, wg, wu, wd)
    out = _combine(dest0, dest1, h1, r, norm_final_w.reshape(1, d), ys, tm=tm_comb)
    return out.reshape(batch, seq, d)


def kernel(x, positions, norm_mix_w, w_in, gla_gk_up, gla_gk_bias, gla_norm_w, w_branch_gla, ret_norm_w,
           w_branch_ret, w_out, norm_ffn_w, router_group_w, router_group_b, router_expert_w, router_expert_b,
           expert_w_gate, expert_w_up, expert_w_down, norm_final_w):
    assert norm_mix_w.shape[0] == 1, "single-layer block"
    return _layer(x, positions, norm_mix_w[0], w_in[0], gla_gk_up[0], gla_gk_bias[0], gla_norm_w[0], w_branch_gla[0],
                  ret_norm_w[0], w_branch_ret[0], w_out[0], norm_ffn_w[0], router_group_w[0], router_group_b[0],
                  router_expert_w[0], router_expert_b[0], expert_w_gate[0], expert_w_up[0], expert_w_down[0],
                  norm_final_w, tm_in=1024, tn_in=1024, gla_rows=512, tm_mix=512, tm_comb=256,
                  disp_chunk=1024)
```

```python
import functools
import math

import jax
import jax.numpy as jnp
import numpy as np
from jax import lax
from jax.experimental import pallas as pl
from jax.experimental.pallas import tpu as pltpu

F32 = jnp.float32
BF16 = jnp.bfloat16

EPS = 1e-6
GLA_HEADS = 4
GLA_DK = 128
GLA_DV = 256
GLA_GATE_RANK = 16
GLA_GATE_TEMP = 16.0
RET_HEADS = 4
RET_DK = 256
RET_DV = 512
ROPE_BASE = 10000.0
N_GROUPS = 4
EXPERTS_PER_GROUP = 8
N_EXPERTS = N_GROUPS * EXPERTS_PER_GROUP
EXPERT_HIDDEN = 512

LANES = 128
GLA_CHUNK = 128
GLA_DIAG = 8
RET_CHUNK = 256
EXPERT_ROWS = 512
VMEM_LIMIT = 48 * 1024 * 1024


def _dot(a, b):
    return jnp.dot(a, b, preferred_element_type=F32)


def _dot_nt(a, b):
    return lax.dot_general(a, b, (((1,), (1,)), ((), ())), preferred_element_type=F32)


def _split2(a):
    hi = a.astype(BF16)
    lo = (a - hi.astype(F32)).astype(BF16)
    return hi, lo


def _params(sem, vmem=VMEM_LIMIT):
    return pltpu.CompilerParams(dimension_semantics=sem, vmem_limit_bytes=vmem)


def _inproj_kernel(x_ref, nw_ref, w_ref, wgd_ref, proj_ref, gd_ref, u_scr):
    @pl.when(pl.program_id(1) == 0)
    def _():
        x = x_ref[...]
        ms = jnp.mean(x * x, axis=-1, keepdims=True)
        u = (x * lax.rsqrt(ms + EPS) * nw_ref[...]).astype(BF16)
        u_scr[...] = u
        gd_ref[...] = _dot(u, wgd_ref[...])

    proj_ref[...] = _dot(u_scr[...], w_ref[...]).astype(BF16)


def _inproj(x2, nw, w_main, w_gd, *, tm, tn):
    n, d = x2.shape
    p = w_main.shape[1]
    return pl.pallas_call(
        _inproj_kernel,
        grid=(n // tm, p // tn),
        in_specs=[
            pl.BlockSpec((tm, d), lambda i, j: (i, 0)),
            pl.BlockSpec((1, d), lambda i, j: (0, 0)),
            pl.BlockSpec((d, tn), lambda i, j: (0, j)),
            pl.BlockSpec((d, LANES), lambda i, j: (0, 0)),
        ],
        out_specs=[
            pl.BlockSpec((tm, tn), lambda i, j: (i, j)),
            pl.BlockSpec((tm, LANES), lambda i, j: (i, 0)),
        ],
        out_shape=[
            jax.ShapeDtypeStruct((n, p), BF16),
            jax.ShapeDtypeStruct((n, LANES), F32),
        ],
        scratch_shapes=[pltpu.VMEM((tm, d), BF16)],
        compiler_params=_params(("arbitrary", "arbitrary")),
        name="inproj",
    )(x2, nw, w_main, w_gd)


def _gla_tables(c):
    levels = []
    s = c // 2
    while s >= GLA_DIAG:
        levels.append(s)
        s //= 2
    i = np.arange(c)[:, None]
    t = np.arange(c)[None, :]
    mats = [t <= i, t > i]
    masks = []
    for s in levels:
        bs = (i // s) * s
        mats.append((t > bs) & (t <= i))
        mats.append((t > i) & (t <= np.minimum(bs + s, c - 1)))
        masks.append(((i // (2 * s)) == (t // (2 * s))) & (((i // s) % 2) == 1) & (((t // s) % 2) == 0))
    for d in range(1, GLA_DIAG):
        mats.append((t > i - d) & (t <= i))
    for d in range(GLA_DIAG):
        masks.append((t == i - d) & ((i % GLA_DIAG) >= d))
    g = np.concatenate(mats, 0).astype(np.float32)
    m = np.stack(masks, 0).astype(np.float32)
    return g, m, tuple(levels)


def _gla_kernel(q_ref, k_ref, v_ref, gg_ref, gd_ref, up_ref, bias_ref, nw_ref, g_ref, m_ref,
                o_ref, st_ref, *, c, nlev):
    h_n, dk, dv = GLA_HEADS, GLA_DK, GLA_DV
    rows = q_ref.shape[0]

    @pl.when(pl.program_id(1) == 0)
    def _():
        st_ref[...] = jnp.zeros_like(st_ref)

    g_tab = g_ref[...]
    up_hi = up_ref[0]
    up_lo = up_ref[1]
    q_scale = dk ** -0.5
    band0 = 2 + 2 * nlev

    def chunk(ci, carry):
        r0 = pl.multiple_of(ci * c, c)
        rs = pl.ds(r0, c)
        gd_hi, gd_lo = _split2(gd_ref[rs, :])
        xg = _dot(gd_hi, up_hi) + _dot(gd_hi, up_lo) + _dot(gd_lo, up_hi) + bias_ref[...]
        la2 = (jnp.minimum(xg, 0.0) - jnp.log1p(jnp.exp(-jnp.abs(xg)))) * (math.log2(math.e) / GLA_GATE_TEMP)
        ex = jnp.exp2(_dot(g_tab, la2.astype(BF16)))

        for h in range(h_n):
            ks = slice(h * dk, (h + 1) * dk)
            vs = slice(h * dv, (h + 1) * dv)

            def tab(m, ks=ks):
                return ex[m * c:(m + 1) * c, ks]

            qh = q_ref[rs, ks].astype(F32) * q_scale
            kh = k_ref[rs, ks].astype(F32)
            vh = v_ref[rs, vs]
            st = st_ref[h]
            eb = tab(0)
            o = _dot_nt((qh * eb).astype(BF16), st.astype(BF16))
            parts = [m_ref[nlev] * jnp.sum(qh * kh, axis=-1, keepdims=True)]
            for li in range(nlev):
                qs = (qh * tab(2 + 2 * li)).astype(BF16)
                kk = (kh * tab(3 + 2 * li)).astype(BF16)
                parts.append(m_ref[li] * _dot_nt(qs, kk))
            kh_tiles = kh.reshape(c // GLA_DIAG, GLA_DIAG, dk)
            for d in range(1, GLA_DIAG):
                kr = pltpu.roll(kh_tiles, d, 1).reshape(c, dk)
                dd = jnp.sum(qh * kr * tab(band0 + d - 1), axis=-1, keepdims=True)
                parts.append(m_ref[nlev + d] * dd)
            while len(parts) > 1:
                parts = [functools.reduce(jnp.add, parts[p:p + 2]) for p in range(0, len(parts), 2)]
            o = o + _dot(parts[0].astype(BF16), vh)
            e_last = eb[c - 1:c, :]
            k_st = (kh * tab(1)).astype(BF16)
            v_t = vh.astype(F32).T.astype(BF16)
            st_ref[h] = st * e_last + _dot(v_t, k_st)

            ms = jnp.mean(o * o, axis=-1, keepdims=True)
            y = o * lax.rsqrt(ms + EPS) * nw_ref[...]
            gg = gg_ref[rs, vs].astype(F32)
            o_ref[rs, vs] = (y * (gg * jax.nn.sigmoid(gg))).astype(BF16)
        return carry

    lax.fori_loop(0, rows // c, chunk, 0, unroll=2)


def _gla(proj, gd, up2, bias, nw, *, batch, seq, rows, cols):
    n = proj.shape[0]
    c = GLA_CHUNK
    g_np, m_np, levels = _gla_tables(c)
    g_tab = jnp.asarray(g_np, BF16)
    m_tab = jnp.asarray(m_np, F32)
    qk_w = GLA_HEADS * GLA_DK
    v_w = GLA_HEADS * GLA_DV
    spb = seq // rows
    row = lambda b, s: b * spb + s
    kern = functools.partial(_gla_kernel, c=c, nlev=len(levels))
    return pl.pallas_call(
        kern,
        grid=(batch, spb),
        in_specs=[
            pl.BlockSpec((rows, qk_w), lambda b, s: (row(b, s), cols[0])),
            pl.BlockSpec((rows, qk_w), lambda b, s: (row(b, s), cols[1])),
            pl.BlockSpec((rows, v_w), lambda b, s: (row(b, s), cols[2])),
            pl.BlockSpec((rows, v_w), lambda b, s: (row(b, s), cols[3])),
            pl.BlockSpec((rows, LANES), lambda b, s: (row(b, s), 0)),
            pl.BlockSpec(up2.shape, lambda b, s: (0, 0, 0)),
            pl.BlockSpec(bias.shape, lambda b, s: (0, 0)),
            pl.BlockSpec(nw.shape, lambda b, s: (0, 0)),
            pl.BlockSpec(g_tab.shape, lambda b, s: (0, 0)),
            pl.BlockSpec(m_tab.shape, lambda b, s: (0, 0, 0)),
        ],
        out_specs=pl.BlockSpec((rows, v_w), lambda b, s: (row(b, s), 0)),
        out_shape=jax.ShapeDtypeStruct((n, v_w), BF16),
        scratch_shapes=[pltpu.VMEM((GLA_HEADS, GLA_DV, GLA_DK), F32)],
        compiler_params=_params(("arbitrary", "arbitrary")),
        name="gla",
    )(proj, proj, proj, proj, gd, up2, bias, nw, g_tab, m_tab)


def _ret_log_gamma(h):
    return math.log(1.0 - 2.0 ** (-5.0 - h))


def _ret_kernel(pos_ref, th_ref, q_ref, k_ref, v_ref, rg_ref, nw_ref, o_ref, s_ref, dm_ref):
    h_n, dk, dv = RET_HEADS, RET_DK, RET_DV
    c = q_ref.shape[0]
    half = dk // 2

    @pl.when(pl.program_id(1) == 0)
    def _():
        s_ref[...] = jnp.zeros_like(s_ref)

    @pl.when((pl.program_id(0) == 0) & (pl.program_id(1) == 0))
    def _():
        rel = (lax.broadcasted_iota(jnp.int32, (c, c), 0) - lax.broadcasted_iota(jnp.int32, (c, c), 1)).astype(F32)
        for h in range(h_n):
            dm_ref[h] = jnp.where(rel >= 0, jnp.exp(_ret_log_gamma(h) * jnp.maximum(rel, 0.0)), 0.0)

    ang = pos_ref[...].astype(F32) * th_ref[...]
    cs = jnp.cos(ang)
    sn = jnp.sin(ang)
    idx = lax.broadcasted_iota(jnp.int32, (c, 1), 0).astype(F32)
    k_scale = dk ** -0.5

    def rot(ref, h):
        a = ref[:, h * dk:h * dk + half].astype(F32)
        b = ref[:, h * dk + half:(h + 1) * dk].astype(F32)
        return jnp.concatenate([a * cs - b * sn, b * cs + a * sn], axis=-1)

    for h in range(h_n):
        lg = _ret_log_gamma(h)
        vs = slice(h * dv, (h + 1) * dv)
        qr = rot(q_ref, h)
        kr = rot(k_ref, h) * k_scale
        vh = v_ref[:, vs]
        scores = _dot_nt(qr.astype(BF16), kr.astype(BF16)) * dm_ref[h]
        q_dec = jnp.exp(lg * (idx + 1.0))
        k_dec = jnp.exp(lg * (c - 1.0 - idx))
        s = s_ref[h]
        o = _dot(scores.astype(BF16), vh) + _dot((qr * q_dec).astype(BF16), s.astype(BF16))
        k_t = (kr * k_dec).T.astype(BF16)
        s_ref[h] = math.exp(lg * c) * s + _dot(k_t, vh)

        mu = jnp.mean(o, axis=-1, keepdims=True)
        oc = o - mu
        var = jnp.mean(oc * oc, axis=-1, keepdims=True)
        y = oc * lax.rsqrt(var + EPS) * nw_ref[:, vs]
        rg = rg_ref[:, vs].astype(F32)
        o_ref[:, vs] = (y * (rg * jax.nn.sigmoid(rg))).astype(BF16)


def _retention(proj, pos, theta, nw, *, batch, seq, cols):
    n = proj.shape[0]
    c = RET_CHUNK
    qk_w = RET_HEADS * RET_DK
    v_w = RET_HEADS * RET_DV
    spb = seq // c
    row = lambda b, s: b * spb + s
    return pl.pallas_call(
        _ret_kernel,
        grid=(batch, spb),
        in_specs=[
            pl.BlockSpec((c, 1), lambda b, s: (row(b, s), 0)),
            pl.BlockSpec(theta.shape, lambda b, s: (0, 0)),
            pl.BlockSpec((c, qk_w), lambda b, s: (row(b, s), cols[0])),
            pl.BlockSpec((c, qk_w), lambda b, s: (row(b, s), cols[1])),
            pl.BlockSpec((c, v_w), lambda b, s: (row(b, s), cols[2])),
            pl.BlockSpec((c, v_w), lambda b, s: (row(b, s), cols[3])),
            pl.BlockSpec(nw.shape, lambda b, s: (0, 0)),
        ],
        out_specs=pl.BlockSpec((c, v_w), lambda b, s: (row(b, s), 0)),
        out_shape=jax.ShapeDtypeStruct((n, v_w), BF16),
        scratch_shapes=[
            pltpu.VMEM((RET_HEADS, RET_DK, RET_DV), F32),
            pltpu.VMEM((RET_HEADS, c, c), F32),
        ],
        compiler_params=_params(("arbitrary", "arbitrary")),
        name="retention",
    )(pos, theta, proj, proj, proj, proj, nw)


def _mix_kernel(og_ref, or_ref, ma_ref, mb_ref, x_ref, wa_ref, wb_ref, wo_ref, nw_ref, wr_ref, br_ref,
                tri_ref, h_ref, t_ref, r_ref, rt_ref, cnt_ref, base_ref):
    tm = x_ref.shape[0]

    @pl.when(pl.program_id(0) == 0)
    def _():
        base_ref[...] = jnp.zeros_like(base_ref)

    ya = _dot(og_ref[...], wa_ref[...])
    yb = _dot(or_ref[...], wb_ref[...])
    merged = (jax.nn.sigmoid(ma_ref[...].astype(F32)) * ya + jax.nn.sigmoid(mb_ref[...].astype(F32)) * yb)
    h = x_ref[...] + _dot(merged.astype(BF16), wo_ref[...])
    h_ref[...] = h
    ms = jnp.mean(h * h, axis=-1, keepdims=True)
    t = h * lax.rsqrt(ms + EPS) * nw_ref[...]
    t_ref[...] = _to_row_tiles(t)

    lane = lax.broadcasted_iota(jnp.int32, (tm, LANES), 1)
    neg = jnp.float32(-1e30)
    big = jnp.int32(1 << 20)

    def first_max(v):
        m = jnp.max(v, axis=-1, keepdims=True)
        return m, jnp.min(jnp.where(v == m, lane, big), axis=-1, keepdims=True)

    t_hi, t_lo = _split2(t)
    lg = _dot(t_hi, wr_ref[0]) + _dot(t_hi, wr_ref[1]) + _dot(t_lo, wr_ref[0]) + br_ref[...]
    g_valid = (lane >= N_EXPERTS) & (lane < N_EXPERTS + N_GROUPS)
    g_m, g_lane = first_max(jnp.where(g_valid, lg, neg))
    g_w = 1.0 / jnp.sum(jnp.where(g_valid, jnp.exp(lg - g_m), 0.0), axis=-1, keepdims=True)
    g_idx = g_lane - N_EXPERTS
    e_valid = (lane < N_EXPERTS) & ((lane // EXPERTS_PER_GROUP) == g_idx)
    el = jnp.where(e_valid, lg, neg)
    v1, i1 = first_max(el)
    v2, i2 = first_max(jnp.where(lane == i1, neg, el))
    e21 = jnp.exp(v2 - v1)
    w1 = g_w / (1.0 + e21)
    w2 = g_w * e21 / (1.0 + e21)

    o1 = lane == i1
    o2 = lane == i2
    osum = jnp.where(o1 | o2, 1.0, 0.0)
    base = base_ref[0:1, :]
    before = _dot(tri_ref[...], osum.astype(BF16)) + base
    r1 = jnp.sum(jnp.where(o1, before, 0.0), axis=-1, keepdims=True)
    r2 = jnp.sum(jnp.where(o2, before, 0.0), axis=-1, keepdims=True)
    base = base + jnp.sum(osum, axis=0, keepdims=True)
    base_ref[...] = jnp.broadcast_to(base, base_ref.shape)
    cnt_ref[...] = jnp.broadcast_to(base, cnt_ref.shape)

    cols = (i1.astype(F32), i2.astype(F32), w1, w2, r1, r2)
    wide = jnp.zeros((tm, LANES), F32)
    for ci, col in enumerate(cols):
        wide = jnp.where(lane == ci, col, wide)
    r_ref[...] = wide[:, :r_ref.shape[1]]
    rt_ref[...] = wide.T[:rt_ref.shape[0], :]


def _mix(o_gla, o_ret, proj, x2, wa, wb, wo, nw, wr2, br, *, tm, cols):
    n, d = x2.shape
    tri = jnp.asarray(np.tril(np.ones((tm, tm), np.float32), -1), BF16)
    const = lambda shape: pl.BlockSpec(shape, lambda i: (0,) * len(shape))
    return pl.pallas_call(
        _mix_kernel,
        grid=(n // tm,),
        in_specs=[
            pl.BlockSpec((tm, o_gla.shape[1]), lambda i: (i, 0)),
            pl.BlockSpec((tm, o_ret.shape[1]), lambda i: (i, 0)),
            pl.BlockSpec((tm, d), lambda i: (i, cols[0])),
            pl.BlockSpec((tm, d), lambda i: (i, cols[1])),
            pl.BlockSpec((tm, d), lambda i: (i, 0)),
            const(wa.shape), const(wb.shape), const(wo.shape), const(nw.shape),
            const(wr2.shape), const(br.shape), const(tri.shape),
        ],
        out_specs=[
            pl.BlockSpec((tm, d), lambda i: (i, 0)),
            pl.BlockSpec((tm, d // LANES, LANES), lambda i: (i, 0, 0)),
            pl.BlockSpec((tm, 8), lambda i: (i, 0)),
            pl.BlockSpec((8, tm), lambda i: (0, i)),
            pl.BlockSpec((8, LANES), lambda i: (0, 0)),
        ],
        out_shape=[
            jax.ShapeDtypeStruct((n, d), F32),
            jax.ShapeDtypeStruct((n, d // LANES, LANES), BF16),
            jax.ShapeDtypeStruct((n, 8), F32),
            jax.ShapeDtypeStruct((8, n), F32),
            jax.ShapeDtypeStruct((8, LANES), F32),
        ],
        scratch_shapes=[pltpu.VMEM((8, LANES), F32)],
        compiler_params=_params(("arbitrary",)),
        name="mix_router",
    )(o_gla, o_ret, proj, proj, x2, wa, wb, wo, nw, wr2, br, tri)


def _to_row_tiles(x):
    return x.reshape(x.shape[0], x.shape[1] // LANES, LANES).astype(BF16)


def _from_row_tiles(x3):
    return x3.astype(F32).reshape(x3.shape[0], x3.shape[1] * x3.shape[2])


def _dispatch_kernel(tail_ref, has_ref, d0_ref, d1_ref, t_ref, xs_hbm, zero_ref, sem, zsem):
    step = pl.program_id(0)
    tok_n = t_ref.shape[0]
    blk = zero_ref.shape[0]

    def tail_copy(e):
        return pltpu.make_async_copy(zero_ref, xs_hbm.at[pl.ds(tail_ref[e], blk)], zsem)

    @pl.when(step == 0)
    def _():
        zero_ref[...] = jnp.zeros_like(zero_ref)
        for e in range(tail_ref.shape[0]):
            @pl.when(has_ref[e] > 0)
            def _(e=e):
                tail_copy(e).start()
        for e in range(tail_ref.shape[0]):
            @pl.when(has_ref[e] > 0)
            def _(e=e):
                tail_copy(e).wait()

    def row_copy(a, dst):
        return pltpu.make_async_copy(t_ref.at[a], xs_hbm.at[dst], sem)

    def body(a, carry):
        row_copy(a, d0_ref[a]).start(priority=0)
        row_copy(a, d1_ref[a]).start(priority=1)
        return carry

    lax.fori_loop(0, tok_n, body, 0, unroll=8)

    for _ in range(2):
        pltpu.make_async_copy(t_ref, xs_hbm.at[pl.ds(0, tok_n)], sem).wait()


def _dispatch(t3, dest0, dest1, tail, has, *, n_slots, chunk):
    n, s, l = t3.shape
    return pl.pallas_call(
        _dispatch_kernel,
        grid_spec=pltpu.PrefetchScalarGridSpec(
            num_scalar_prefetch=2,
            grid=(n // chunk,),
            in_specs=[
                pl.BlockSpec((chunk,), lambda i, *_: (i,), memory_space=pltpu.SMEM),
                pl.BlockSpec((chunk,), lambda i, *_: (i,), memory_space=pltpu.SMEM),
                pl.BlockSpec((chunk, s, l), lambda i, *_: (i, 0, 0)),
            ],
            out_specs=pl.BlockSpec(memory_space=pl.ANY),
            scratch_shapes=[
                pltpu.VMEM((EXPERT_ROWS, s, l), BF16),
                pltpu.SemaphoreType.DMA,
                pltpu.SemaphoreType.DMA,
            ],
        ),
        out_shape=jax.ShapeDtypeStruct((n_slots, s, l), BF16),
        compiler_params=pltpu.CompilerParams(dimension_semantics=("arbitrary",), has_side_effects=True,
                                             vmem_limit_bytes=VMEM_LIMIT, disable_bounds_checks=True),
        name="dispatch",
    )(tail, has, dest0, dest1, t3)


def _expert_kernel(be_ref, nu_ref, xs_ref, wg_ref, wu_ref, wd_ref, ys_ref, wg_s, wu_s, wd_s):
    b = pl.program_id(0)

    @pl.when((b == 0) | (be_ref[b] != be_ref[jnp.maximum(b - 1, 0)]))
    def _():
        wg_s[...] = wg_ref[0].astype(BF16)
        wu_s[...] = wu_ref[0].astype(BF16)
        wd_s[...] = wd_ref[0].astype(BF16)

    @pl.when(b < nu_ref[0])
    def _():
        x = _from_row_tiles(xs_ref[...]).astype(BF16)
        g = _dot(x, wg_s[...])
        u = _dot(x, wu_s[...])
        hid = (g * jax.nn.sigmoid(g) * u).astype(BF16)
        ys_ref[...] = _to_row_tiles(_dot(hid, wd_s[...]))

    @pl.when(b >= nu_ref[0])
    def _():
        ys_ref[...] = jnp.zeros_like(ys_ref)


def _experts(xs3, block_expert, n_used, wg, wu, wd):
    n_slots, s, l = xs3.shape
    d = s * l
    blk = EXPERT_ROWS
    hid = wg.shape[2]
    return pl.pallas_call(
        _expert_kernel,
        grid_spec=pltpu.PrefetchScalarGridSpec(
            num_scalar_prefetch=2,
            grid=(n_slots // blk,),
            in_specs=[
                pl.BlockSpec((blk, s, l), lambda b, be, nu: (jnp.minimum(b, nu[0] - 1), 0, 0)),
                pl.BlockSpec((1, d, hid), lambda b, be, nu: (be[b], 0, 0)),
                pl.BlockSpec((1, d, hid), lambda b, be, nu: (be[b], 0, 0)),
                pl.BlockSpec((1, hid, d), lambda b, be, nu: (be[b], 0, 0)),
            ],
            out_specs=pl.BlockSpec((blk, s, l), lambda b, be, nu: (b, 0, 0)),
            scratch_shapes=[pltpu.VMEM((d, hid), BF16), pltpu.VMEM((d, hid), BF16), pltpu.VMEM((hid, d), BF16)],
        ),
        out_shape=jax.ShapeDtypeStruct((n_slots, s, l), BF16),
        compiler_params=_params(("arbitrary",)),
        name="experts",
    )(block_expert, n_used, xs3, wg, wu, wd)


def _combine_kernel(c0_ref, c1_ref, n0_ref, n1_ref, h_ref, r_ref, nw_ref, ys_hbm, o_ref, buf, sem):
    i = pl.program_id(0)
    n_steps = pl.num_programs(0)
    tm = h_ref.shape[0]

    def row_copy(dst_row, src_row, slot):
        return pltpu.make_async_copy(ys_hbm.at[src_row], buf.at[slot, dst_row], sem.at[slot])

    def issue(d0, d1, slot):
        def body(a, carry):
            row_copy(a, d0[a], slot).start(priority=0)
            row_copy(tm + a, d1[a], slot).start(priority=1)
            return carry
        lax.fori_loop(0, tm, body, 0, unroll=8)

    @pl.when(i == 0)
    def _():
        issue(c0_ref, c1_ref, 0)

    @pl.when(i + 1 < n_steps)
    def _():
        issue(n0_ref, n1_ref, (i + 1) % 2)

    slot = i % 2
    pltpu.make_async_copy(ys_hbm.at[pl.ds(0, 2 * tm)], buf.at[slot], sem.at[slot]).wait()

    r = r_ref[...]
    y = _from_row_tiles(buf[slot, 0:tm]) * r[:, 2:3] + _from_row_tiles(buf[slot, tm:2 * tm]) * r[:, 3:4]
    h = h_ref[...] + y
    ms = jnp.mean(h * h, axis=-1, keepdims=True)
    o_ref[...] = h * lax.rsqrt(ms + EPS) * nw_ref[...]


def _combine(dest0, dest1, h1, r, nw, ys, *, tm):
    n, d = h1.shape
    n_steps = n // tm
    cur = pl.BlockSpec((tm,), lambda i: (i,), memory_space=pltpu.SMEM)
    nxt = pl.BlockSpec((tm,), lambda i: (jnp.minimum(i + 1, n_steps - 1),), memory_space=pltpu.SMEM)
    return pl.pallas_call(
        _combine_kernel,
        grid=(n_steps,),
        in_specs=[
            cur, cur, nxt, nxt,
            pl.BlockSpec((tm, d), lambda i: (i, 0)),
            pl.BlockSpec((tm, r.shape[1]), lambda i: (i, 0)),
            pl.BlockSpec((1, d), lambda i: (0, 0)),
            pl.BlockSpec(memory_space=pl.ANY),
        ],
        out_specs=pl.BlockSpec((tm, d), lambda i: (i, 0)),
        out_shape=jax.ShapeDtypeStruct((n, d), F32),
        scratch_shapes=[pltpu.VMEM((2, 2 * tm) + ys.shape[1:], BF16), pltpu.SemaphoreType.DMA((2,))],
        compiler_params=pltpu.CompilerParams(dimension_semantics=("arbitrary",), vmem_limit_bytes=VMEM_LIMIT,
                                             disable_bounds_checks=True),
        name="combine",
    )(dest0, dest1, dest0, dest1, h1, r, nw, ys)


def _layer(h, positions, norm_mix_w, w_in, gk_up, gk_bias, gla_norm_w, w_br_gla, ret_norm_w, w_br_ret, w_out,
           norm_ffn_w, rg_w, rg_b, re_w, re_b, wg, wu, wd, norm_final_w, *, tm_in, tn_in, gla_rows, tm_mix,
           tm_comb, disp_chunk):
    batch, seq, d = h.shape
    n = batch * seq
    x2 = h.reshape(n, d)
    gqk, gv = GLA_HEADS * GLA_DK, GLA_HEADS * GLA_DV
    rqk, rv = RET_HEADS * RET_DK, RET_HEADS * RET_DV

    sizes = (gqk, gqk, gv, gv, GLA_GATE_RANK, rqk, rqk, rv, rv, d, d)
    offs = np.concatenate([[0], np.cumsum(sizes)])
    seg = lambda i: w_in[:, offs[i]:offs[i + 1]]
    perm = np.concatenate([np.concatenate([np.arange(0, RET_DK, 2), np.arange(1, RET_DK, 2)]) + hh * RET_DK
                           for hh in range(RET_HEADS)])
    w_main = jnp.concatenate([seg(7), seg(8), seg(2), seg(3), seg(5)[:, perm], seg(6)[:, perm], seg(9), seg(10),
                              seg(0), seg(1)], axis=1).astype(BF16)
    assert rv % gv == 0 and gv == rqk == d and rv == 2 * d and gqk * 2 == d
    ret_cols = (4 + 2, 4 + 3, 0, 1)
    gla_cols = (2 * (4 + 6), 2 * (4 + 6) + 1, 4, 5)
    mix_cols = (4 + 4, 4 + 5)
    w_gd = jnp.pad(seg(4), ((0, 0), (0, LANES - GLA_GATE_RANK))).astype(BF16)

    proj, gd = _inproj(x2, norm_mix_w.reshape(1, d), w_main, w_gd, tm=tm_in, tn=tn_in)

    up = jnp.pad(gk_up, ((0, LANES - GLA_GATE_RANK), (0, 0)))
    up_hi = up.astype(BF16)
    up2 = jnp.stack([up_hi, (up - up_hi.astype(F32)).astype(BF16)])
    o_gla = _gla(proj, gd, up2, gk_bias.reshape(1, gqk), gla_norm_w.reshape(1, GLA_DV),
                 batch=batch, seq=seq, rows=gla_rows, cols=gla_cols)

    theta = (1.0 / (ROPE_BASE ** jnp.linspace(0.0, 1.0, RET_DK // 2, dtype=F32))).reshape(1, RET_DK // 2)
    o_ret = _retention(proj, positions.reshape(n, 1), theta, ret_norm_w.reshape(1, rv),
                       batch=batch, seq=seq, cols=ret_cols)

    wr = jnp.concatenate([re_w.transpose(1, 0, 2).reshape(d, N_EXPERTS), rg_w], axis=1)
    wr = jnp.pad(wr, ((0, 0), (0, LANES - N_EXPERTS - N_GROUPS)))
    wr_hi = wr.astype(BF16)
    wr2 = jnp.stack([wr_hi, (wr - wr_hi.astype(F32)).astype(BF16)])
    br = jnp.pad(jnp.concatenate([re_b.reshape(-1), rg_b]), (0, LANES - N_EXPERTS - N_GROUPS)).reshape(1, LANES)
    h1, t, r, rt, cnt = _mix(o_gla, o_ret, proj, x2, w_br_gla.astype(BF16), w_br_ret.astype(BF16),
                             w_out.astype(BF16), norm_ffn_w.reshape(1, d), wr2, br, tm=tm_mix, cols=mix_cols)

    blk = EXPERT_ROWS
    n_assign = 2 * n
    n_slots = -(-(n_assign + N_EXPERTS * (blk - 1)) // blk) * blk
    counts = cnt[0, :N_EXPERTS].astype(jnp.int32)
    padded = ((counts + blk - 1) // blk) * blk
    pad_end = jnp.cumsum(padded)
    pad_start = pad_end - padded
    dest0 = pad_start[rt[0].astype(jnp.int32)] + rt[4].astype(jnp.int32)
    dest1 = pad_start[rt[1].astype(jnp.int32)] + rt[5].astype(jnp.int32)
    block_start = jnp.arange(n_slots // blk, dtype=jnp.int32) * blk
    block_expert = jnp.minimum(jnp.sum((pad_end[None, :] <= block_start[:, None]).astype(jnp.int32), axis=1),
                               N_EXPERTS - 1)
    n_used = (pad_end[-1:] // blk).astype(jnp.int32)
    spare = pad_end[-1] + jnp.arange(N_EXPERTS, dtype=jnp.int32) * blk
    tail = jnp.concatenate([jnp.maximum(pad_end - blk, 0), jnp.minimum(spare, n_slots - blk)]).astype(jnp.int32)
    has = jnp.concatenate([counts, (spare < n_slots).astype(jnp.int32)])

    xs = _dispatch(t, dest0, dest1, tail, has, n_slots=n_slots, chunk=disp_chunk)
    ys = _experts(xs, block_expert, n_used, wg, wu, wd)
    out = _combine(dest0, dest1, h1, r, norm_final_w.reshape(1, d), ys, tm=tm_comb)
    return out.reshape(batch, seq, d)


def kernel(x, positions, norm_mix_w, w_in, gla_gk_up, gla_gk_bias, gla_norm_w, w_branch_gla, ret_norm_w,
           w_branch_ret, w_out, norm_ffn_w, router_group_w, router_group_b, router_expert_w, router_expert_b,
           expert_w_gate, expert_w_up, expert_w_down, norm_final_w):
    assert norm_mix_w.shape[0] == 1, "single-layer block"
    return _layer(x, positions, norm_mix_w[0], w_in[0], gla_gk_up[0], gla_gk_bias[0], gla_norm_w[0], w_branch_gla[0],
                  ret_norm_w[0], w_branch_ret[0], w_out[0], norm_ffn_w[0], router_group_w[0], router_group_b[0],
                  router_expert_w[0], router_expert_b[0], expert_w_gate[0], expert_w_up[0], expert_w_down[0],
                  norm_final_w, tm_in=2048, tn_in=1024, gla_rows=256, tm_mix=512, tm_comb=256,
                  disp_chunk=1024)
```

```python
import functools
import math

import jax
import jax.numpy as jnp
import numpy as np
from jax import lax
from jax.experimental import pallas as pl
from jax.experimental.pallas import tpu as pltpu

F32 = jnp.float32
BF16 = jnp.bfloat16

EPS = 1e-6
GLA_HEADS = 4
GLA_DK = 128
GLA_DV = 256
GLA_GATE_RANK = 16
GLA_GATE_TEMP = 16.0
RET_HEADS = 4
RET_DK = 256
RET_DV = 512
ROPE_BASE = 10000.0
N_GROUPS = 4
EXPERTS_PER_GROUP = 8
N_EXPERTS = N_GROUPS * EXPERTS_PER_GROUP
EXPERT_HIDDEN = 512

LANES = 128
GLA_CHUNK = 128
GLA_DIAG = 8
RET_CHUNK = 256
EXPERT_ROWS = 512
VMEM_LIMIT = 48 * 1024 * 1024


def _dot(a, b):
    return jnp.dot(a, b, preferred_element_type=F32)


def _dot_nt(a, b):
    return lax.dot_general(a, b, (((1,), (1,)), ((), ())), preferred_element_type=F32)


def _split2(a):
    hi = a.astype(BF16)
    lo = (a - hi.astype(F32)).astype(BF16)
    return hi, lo


def _params(sem, vmem=VMEM_LIMIT):
    return pltpu.CompilerParams(dimension_semantics=sem, vmem_limit_bytes=vmem)


def _inproj_kernel(x_ref, nw_ref, w_ref, wgd_ref, proj_ref, gd_ref, u_scr):
    @pl.when(pl.program_id(1) == 0)
    def _():
        x = x_ref[...]
        ms = jnp.mean(x * x, axis=-1, keepdims=True)
        u = (x * lax.rsqrt(ms + EPS) * nw_ref[...]).astype(BF16)
        u_scr[...] = u
        gd_ref[...] = _dot(u, wgd_ref[...])

    proj_ref[...] = _dot(u_scr[...], w_ref[...]).astype(BF16)


def _inproj(x2, nw, w_main, w_gd, *, tm, tn):
    n, d = x2.shape
    p = w_main.shape[1]
    return pl.pallas_call(
        _inproj_kernel,
        grid=(n // tm, p // tn),
        in_specs=[
            pl.BlockSpec((tm, d), lambda i, j: (i, 0)),
            pl.BlockSpec((1, d), lambda i, j: (0, 0)),
            pl.BlockSpec((d, tn), lambda i, j: (0, j)),
            pl.BlockSpec((d, LANES), lambda i, j: (0, 0)),
        ],
        out_specs=[
            pl.BlockSpec((tm, tn), lambda i, j: (i, j)),
            pl.BlockSpec((tm, LANES), lambda i, j: (i, 0)),
        ],
        out_shape=[
            jax.ShapeDtypeStruct((n, p), BF16),
            jax.ShapeDtypeStruct((n, LANES), F32),
        ],
        scratch_shapes=[pltpu.VMEM((tm, d), BF16)],
        compiler_params=_params(("arbitrary", "arbitrary")),
        name="inproj",
    )(x2, nw, w_main, w_gd)


def _gla_tables(c):
    levels = []
    s = c // 2
    while s >= GLA_DIAG:
        levels.append(s)
        s //= 2
    i = np.arange(c)[:, None]
    t = np.arange(c)[None, :]
    mats = [t <= i, t > i]
    masks = []
    for s in levels:
        bs = (i // s) * s
        mats.append((t > bs) & (t <= i))
        mats.append((t > i) & (t <= np.minimum(bs + s, c - 1)))
        masks.append(((i // (2 * s)) == (t // (2 * s))) & (((i // s) % 2) == 1) & (((t // s) % 2) == 0))
    for d in range(1, GLA_DIAG):
        mats.append((t > i - d) & (t <= i))
    for d in range(GLA_DIAG):
        masks.append((t == i - d) & ((i % GLA_DIAG) >= d))
    g = np.concatenate(mats, 0).astype(np.float32)
    m = np.stack(masks, 0).astype(np.float32)
    return g, m, tuple(levels)


def _gla_kernel(q_ref, k_ref, v_ref, gg_ref, gd_ref, up_ref, bias_ref, nw_ref, g_ref, m_ref,
                o_ref, st_ref, *, c, nlev):
    h_n, dk, dv = GLA_HEADS, GLA_DK, GLA_DV
    rows = q_ref.shape[0]

    @pl.when(pl.program_id(1) == 0)
    def _():
        st_ref[...] = jnp.zeros_like(st_ref)

    g_tab = g_ref[...]
    up_hi = up_ref[0]
    up_lo = up_ref[1]
    q_scale = dk ** -0.5
    band0 = 2 + 2 * nlev

    def chunk(ci, carry):
        r0 = pl.multiple_of(ci * c, c)
        rs = pl.ds(r0, c)
        gd_hi, gd_lo = _split2(gd_ref[rs, :])
        xg = _dot(gd_hi, up_hi) + _dot(gd_hi, up_lo) + _dot(gd_lo, up_hi) + bias_ref[...]
        la2 = (jnp.minimum(xg, 0.0) - jnp.log1p(jnp.exp(-jnp.abs(xg)))) * (math.log2(math.e) / GLA_GATE_TEMP)
        ex = jnp.exp2(_dot(g_tab, la2.astype(BF16)))

        for h in range(h_n):
            ks = slice(h * dk, (h + 1) * dk)
            vs = slice(h * dv, (h + 1) * dv)

            def tab(m, ks=ks):
                return ex[m * c:(m + 1) * c, ks]

            qh = q_ref[rs, ks].astype(F32) * q_scale
            kh = k_ref[rs, ks].astype(F32)
            vh = v_ref[rs, vs]
            st = st_ref[h]
            eb = tab(0)
            o = _dot_nt((qh * eb).astype(BF16), st.astype(BF16))
            parts = [m_ref[nlev] * jnp.sum(qh * kh, axis=-1, keepdims=True)]
            for li in range(nlev):
                qs = (qh * tab(2 + 2 * li)).astype(BF16)
                kk = (kh * tab(3 + 2 * li)).astype(BF16)
                parts.append(m_ref[li] * _dot_nt(qs, kk))
            kh_tiles = kh.reshape(c // GLA_DIAG, GLA_DIAG, dk)
            for d in range(1, GLA_DIAG):
                kr = pltpu.roll(kh_tiles, d, 1).reshape(c, dk)
                dd = jnp.sum(qh * kr * tab(band0 + d - 1), axis=-1, keepdims=True)
                parts.append(m_ref[nlev + d] * dd)
            while len(parts) > 1:
                parts = [functools.reduce(jnp.add, parts[p:p + 2]) for p in range(0, len(parts), 2)]
            o = o + _dot(parts[0].astype(BF16), vh)
            e_last = eb[c - 1:c, :]
            k_st = (kh * tab(1)).astype(BF16)
            v_t = vh.astype(F32).T.astype(BF16)
            st_ref[h] = st * e_last + _dot(v_t, k_st)

            ms = jnp.mean(o * o, axis=-1, keepdims=True)
            y = o * lax.rsqrt(ms + EPS) * nw_ref[...]
            gg = gg_ref[rs, vs].astype(F32)
            o_ref[rs, vs] = (y * (gg * jax.nn.sigmoid(gg))).astype(BF16)
        return carry

    lax.fori_loop(0, rows // c, chunk, 0, unroll=2)


def _gla(proj, gd, up2, bias, nw, *, batch, seq, rows, cols):
    n = proj.shape[0]
    c = GLA_CHUNK
    g_np, m_np, levels = _gla_tables(c)
    g_tab = jnp.asarray(g_np, BF16)
    m_tab = jnp.asarray(m_np, F32)
    qk_w = GLA_HEADS * GLA_DK
    v_w = GLA_HEADS * GLA_DV
    spb = seq // rows
    row = lambda b, s: b * spb + s
    kern = functools.partial(_gla_kernel, c=c, nlev=len(levels))
    return pl.pallas_call(
        kern,
        grid=(batch, spb),
        in_specs=[
            pl.BlockSpec((rows, qk_w), lambda b, s: (row(b, s), cols[0])),
            pl.BlockSpec((rows, qk_w), lambda b, s: (row(b, s), cols[1])),
            pl.BlockSpec((rows, v_w), lambda b, s: (row(b, s), cols[2])),
            pl.BlockSpec((rows, v_w), lambda b, s: (row(b, s), cols[3])),
            pl.BlockSpec((rows, LANES), lambda b, s: (row(b, s), 0)),
            pl.BlockSpec(up2.shape, lambda b, s: (0, 0, 0)),
            pl.BlockSpec(bias.shape, lambda b, s: (0, 0)),
            pl.BlockSpec(nw.shape, lambda b, s: (0, 0)),
            pl.BlockSpec(g_tab.shape, lambda b, s: (0, 0)),
            pl.BlockSpec(m_tab.shape, lambda b, s: (0, 0, 0)),
        ],
        out_specs=pl.BlockSpec((rows, v_w), lambda b, s: (row(b, s), 0)),
        out_shape=jax.ShapeDtypeStruct((n, v_w), BF16),
        scratch_shapes=[pltpu.VMEM((GLA_HEADS, GLA_DV, GLA_DK), F32)],
        compiler_params=_params(("arbitrary", "arbitrary")),
        name="gla",
    )(proj, proj, proj, proj, gd, up2, bias, nw, g_tab, m_tab)


def _ret_log_gamma(h):
    return math.log(1.0 - 2.0 ** (-5.0 - h))


def _ret_kernel(pos_ref, th_ref, q_ref, k_ref, v_ref, rg_ref, nw_ref, o_ref, s_ref, dm_ref):
    h_n, dk, dv = RET_HEADS, RET_DK, RET_DV
    c = q_ref.shape[0]
    half = dk // 2

    @pl.when(pl.program_id(1) == 0)
    def _():
        s_ref[...] = jnp.zeros_like(s_ref)

    @pl.when((pl.program_id(0) == 0) & (pl.program_id(1) == 0))
    def _():
        rel = (lax.broadcasted_iota(jnp.int32, (c, c), 0) - lax.broadcasted_iota(jnp.int32, (c, c), 1)).astype(F32)
        for h in range(h_n):
            dm_ref[h] = jnp.where(rel >= 0, jnp.exp(_ret_log_gamma(h) * jnp.maximum(rel, 0.0)), 0.0)

    ang = pos_ref[...].astype(F32) * th_ref[...]
    cs = jnp.cos(ang)
    sn = jnp.sin(ang)
    idx = lax.broadcasted_iota(jnp.int32, (c, 1), 0).astype(F32)
    k_scale = dk ** -0.5

    def rot(ref, h):
        a = ref[:, h * dk:h * dk + half].astype(F32)
        b = ref[:, h * dk + half:(h + 1) * dk].astype(F32)
        return jnp.concatenate([a * cs - b * sn, b * cs + a * sn], axis=-1)

    for h in range(h_n):
        lg = _ret_log_gamma(h)
        vs = slice(h * dv, (h + 1) * dv)
        qr = rot(q_ref, h)
        kr = rot(k_ref, h) * k_scale
        vh = v_ref[:, vs]
        scores = _dot_nt(qr.astype(BF16), kr.astype(BF16)) * dm_ref[h]
        q_dec = jnp.exp(lg * (idx + 1.0))
        k_dec = jnp.exp(lg * (c - 1.0 - idx))
        s = s_ref[h]
        o = _dot(scores.astype(BF16), vh) + _dot((qr * q_dec).astype(BF16), s.astype(BF16))
        k_t = (kr * k_dec).T.astype(BF16)
        s_ref[h] = math.exp(lg * c) * s + _dot(k_t, vh)

        mu = jnp.mean(o, axis=-1, keepdims=True)
        oc = o - mu
        var = jnp.mean(oc * oc, axis=-1, keepdims=True)
        y = oc * lax.rsqrt(var + EPS) * nw_ref[:, vs]
        rg = rg_ref[:, vs].astype(F32)
        o_ref[:, vs] = (y * (rg * jax.nn.sigmoid(rg))).astype(BF16)


def _retention(proj, pos, theta, nw, *, batch, seq, cols):
    n = proj.shape[0]
    c = RET_CHUNK
    qk_w = RET_HEADS * RET_DK
    v_w = RET_HEADS * RET_DV
    spb = seq // c
    row = lambda b, s: b * spb + s
    return pl.pallas_call(
        _ret_kernel,
        grid=(batch, spb),
        in_specs=[
            pl.BlockSpec((c, 1), lambda b, s: (row(b, s), 0)),
            pl.BlockSpec(theta.shape, lambda b, s: (0, 0)),
            pl.BlockSpec((c, qk_w), lambda b, s: (row(b, s), cols[0])),
            pl.BlockSpec((c, qk_w), lambda b, s: (row(b, s), cols[1])),
            pl.BlockSpec((c, v_w), lambda b, s: (row(b, s), cols[2])),
            pl.BlockSpec((c, v_w), lambda b, s: (row(b, s), cols[3])),
            pl.BlockSpec(nw.shape, lambda b, s: (0, 0)),
        ],
        out_specs=pl.BlockSpec((c, v_w), lambda b, s: (row(b, s), 0)),
        out_shape=jax.ShapeDtypeStruct((n, v_w), BF16),
        scratch_shapes=[
            pltpu.VMEM((RET_HEADS, RET_DK, RET_DV), F32),
            pltpu.VMEM((RET_HEADS, c, c), F32),
        ],
        compiler_params=_params(("arbitrary", "arbitrary")),
        name="retention",
    )(pos, theta, proj, proj, proj, proj, nw)


def _mix_kernel(og_ref, or_ref, ma_ref, mb_ref, x_ref, wa_ref, wb_ref, wo_ref, nw_ref, wr_ref, br_ref,
                tri_ref, h_ref, t_ref, r_ref, rt_ref, cnt_ref, base_ref):
    tm = x_ref.shape[0]

    @pl.when(pl.program_id(0) == 0)
    def _():
        base_ref[...] = jnp.zeros_like(base_ref)

    ya = _dot(og_ref[...], wa_ref[...])
    yb = _dot(or_ref[...], wb_ref[...])
    merged = (jax.nn.sigmoid(ma_ref[...].astype(F32)) * ya + jax.nn.sigmoid(mb_ref[...].astype(F32)) * yb)
    h = x_ref[...] + _dot(merged.astype(BF16), wo_ref[...])
    h_ref[...] = h
    ms = jnp.mean(h * h, axis=-1, keepdims=True)
    t = h * lax.rsqrt(ms + EPS) * nw_ref[...]
    t_ref[...] = _to_row_tiles(t)

    lane = lax.broadcasted_iota(jnp.int32, (tm, LANES), 1)
    neg = jnp.float32(-1e30)
    big = jnp.int32(1 << 20)

    def first_max(v):
        m = jnp.max(v, axis=-1, keepdims=True)
        return m, jnp.min(jnp.where(v == m, lane, big), axis=-1, keepdims=True)

    t_hi, t_lo = _split2(t)
    lg = _dot(t_hi, wr_ref[0]) + _dot(t_hi, wr_ref[1]) + _dot(t_lo, wr_ref[0]) + br_ref[...]
    g_valid = (lane >= N_EXPERTS) & (lane < N_EXPERTS + N_GROUPS)
    g_m, g_lane = first_max(jnp.where(g_valid, lg, neg))
    g_w = 1.0 / jnp.sum(jnp.where(g_valid, jnp.exp(lg - g_m), 0.0), axis=-1, keepdims=True)
    g_idx = g_lane - N_EXPERTS
    e_valid = (lane < N_EXPERTS) & ((lane // EXPERTS_PER_GROUP) == g_idx)
    el = jnp.where(e_valid, lg, neg)
    v1, i1 = first_max(el)
    v2, i2 = first_max(jnp.where(lane == i1, neg, el))
    e21 = jnp.exp(v2 - v1)
    w1 = g_w / (1.0 + e21)
    w2 = g_w * e21 / (1.0 + e21)

    o1 = lane == i1
    o2 = lane == i2
    osum = jnp.where(o1 | o2, 1.0, 0.0)
    base = base_ref[0:1, :]
    before = _dot(tri_ref[...], osum.astype(BF16)) + base
    r1 = jnp.sum(jnp.where(o1, before, 0.0), axis=-1, keepdims=True)
    r2 = jnp.sum(jnp.where(o2, before, 0.0), axis=-1, keepdims=True)
    base = base + jnp.sum(osum, axis=0, keepdims=True)
    base_ref[...] = jnp.broadcast_to(base, base_ref.shape)
    cnt_ref[...] = jnp.broadcast_to(base, cnt_ref.shape)

    cols = (i1.astype(F32), i2.astype(F32), w1, w2, r1, r2)
    wide = jnp.zeros((tm, LANES), F32)
    for ci, col in enumerate(cols):
        wide = jnp.where(lane == ci, col, wide)
    r_ref[...] = wide[:, :r_ref.shape[1]]
    rt_ref[...] = wide.T[:rt_ref.shape[0], :]


def _mix(o_gla, o_ret, proj, x2, wa, wb, wo, nw, wr2, br, *, tm, cols):
    n, d = x2.shape
    tri = jnp.asarray(np.tril(np.ones((tm, tm), np.float32), -1), BF16)
    const = lambda shape: pl.BlockSpec(shape, lambda i: (0,) * len(shape))
    return pl.pallas_call(
        _mix_kernel,
        grid=(n // tm,),
        in_specs=[
            pl.BlockSpec((tm, o_gla.shape[1]), lambda i: (i, 0)),
            pl.BlockSpec((tm, o_ret.shape[1]), lambda i: (i, 0)),
            pl.BlockSpec((tm, d), lambda i: (i, cols[0])),
            pl.BlockSpec((tm, d), lambda i: (i, cols[1])),
            pl.BlockSpec((tm, d), lambda i: (i, 0)),
            const(wa.shape), const(wb.shape), const(wo.shape), const(nw.shape),
            const(wr2.shape), const(br.shape), const(tri.shape),
        ],
        out_specs=[
            pl.BlockSpec((tm, d), lambda i: (i, 0)),
            pl.BlockSpec((tm, d // LANES, LANES), lambda i: (i, 0, 0)),
            pl.BlockSpec((tm, 8), lambda i: (i, 0)),
            pl.BlockSpec((8, tm), lambda i: (0, i)),
            pl.BlockSpec((8, LANES), lambda i: (0, 0)),
        ],
        out_shape=[
            jax.ShapeDtypeStruct((n, d), F32),
            jax.ShapeDtypeStruct((n, d // LANES, LANES), BF16),
            jax.ShapeDtypeStruct((n, 8), F32),
            jax.ShapeDtypeStruct((8, n), F32),
            jax.ShapeDtypeStruct((8, LANES), F32),
        ],
        scratch_shapes=[pltpu.VMEM((8, LANES), F32)],
        compiler_params=_params(("arbitrary",)),
        name="mix_router",
    )(o_gla, o_ret, proj, proj, x2, wa, wb, wo, nw, wr2, br, tri)


def _to_row_tiles(x):
    return x.astype(BF16).reshape(x.shape[0], x.shape[1] // LANES, LANES)


def _from_row_tiles(x3):
    return x3.reshape(x3.shape[0], x3.shape[1] * x3.shape[2])


def _dispatch_kernel(tail_ref, has_ref, d0_ref, d1_ref, t_ref, xs_hbm, zero_ref, sem, zsem):
    step = pl.program_id(0)
    tok_n = t_ref.shape[0]
    blk = zero_ref.shape[0]

    def tail_copy(e):
        return pltpu.make_async_copy(zero_ref, xs_hbm.at[pl.ds(tail_ref[e], blk)], zsem)

    @pl.when(step == 0)
    def _():
        zero_ref[...] = jnp.zeros_like(zero_ref)
        for e in range(tail_ref.shape[0]):
            @pl.when(has_ref[e] > 0)
            def _(e=e):
                tail_copy(e).start()
        for e in range(tail_ref.shape[0]):
            @pl.when(has_ref[e] > 0)
            def _(e=e):
                tail_copy(e).wait()

    def row_copy(a, dst):
        return pltpu.make_async_copy(t_ref.at[a], xs_hbm.at[dst], sem)

    def body(a, carry):
        row_copy(a, d0_ref[a]).start(priority=0)
        row_copy(a, d1_ref[a]).start(priority=1)
        return carry

    lax.fori_loop(0, tok_n, body, 0, unroll=8)

    for _ in range(2):
        pltpu.make_async_copy(t_ref, xs_hbm.at[pl.ds(0, tok_n)], sem).wait()


def _dispatch(t3, dest0, dest1, tail, has, *, n_slots, chunk):
    n, s, l = t3.shape
    return pl.pallas_call(
        _dispatch_kernel,
        grid_spec=pltpu.PrefetchScalarGridSpec(
            num_scalar_prefetch=2,
            grid=(n // chunk,),
            in_specs=[
                pl.BlockSpec((chunk,), lambda i, *_: (i,), memory_space=pltpu.SMEM),
                pl.BlockSpec((chunk,), lambda i, *_: (i,), memory_space=pltpu.SMEM),
                pl.BlockSpec((chunk, s, l), lambda i, *_: (i, 0, 0)),
            ],
            out_specs=pl.BlockSpec(memory_space=pl.ANY),
            scratch_shapes=[
                pltpu.VMEM((EXPERT_ROWS, s, l), BF16),
                pltpu.SemaphoreType.DMA,
                pltpu.SemaphoreType.DMA,
            ],
        ),
        out_shape=jax.ShapeDtypeStruct((n_slots, s, l), BF16),
        compiler_params=pltpu.CompilerParams(dimension_semantics=("arbitrary",), has_side_effects=True,
                                             vmem_limit_bytes=VMEM_LIMIT, disable_bounds_checks=True),
        name="dispatch",
    )(tail, has, dest0, dest1, t3)


def _expert_kernel(be_ref, nu_ref, xs_ref, wg_ref, wu_ref, wd_ref, ys_ref, wg_s, wu_s, wd_s):
    b = pl.program_id(0)

    @pl.when((b == 0) | (be_ref[b] != be_ref[jnp.maximum(b - 1, 0)]))
    def _():
        wg_s[...] = wg_ref[0].astype(BF16)
        wu_s[...] = wu_ref[0].astype(BF16)
        wd_s[...] = wd_ref[0].astype(BF16)

    @pl.when(b < nu_ref[0])
    def _():
        x = _from_row_tiles(xs_ref[...])
        g = _dot(x, wg_s[...])
        u = _dot(x, wu_s[...])
        hid = (g * jax.nn.sigmoid(g) * u).astype(BF16)
        ys_ref[...] = _to_row_tiles(_dot(hid, wd_s[...]))

    @pl.when(b >= nu_ref[0])
    def _():
        ys_ref[...] = jnp.zeros_like(ys_ref)


def _experts(xs3, block_expert, n_used, wg, wu, wd):
    n_slots, s, l = xs3.shape
    d = s * l
    blk = EXPERT_ROWS
    hid = wg.shape[2]
    return pl.pallas_call(
        _expert_kernel,
        grid_spec=pltpu.PrefetchScalarGridSpec(
            num_scalar_prefetch=2,
            grid=(n_slots // blk,),
            in_specs=[
                pl.BlockSpec((blk, s, l), lambda b, be, nu: (jnp.minimum(b, nu[0] - 1), 0, 0)),
                pl.BlockSpec((1, d, hid), lambda b, be, nu: (be[b], 0, 0)),
                pl.BlockSpec((1, d, hid), lambda b, be, nu: (be[b], 0, 0)),
                pl.BlockSpec((1, hid, d), lambda b, be, nu: (be[b], 0, 0)),
            ],
            out_specs=pl.BlockSpec((blk, s, l), lambda b, be, nu: (b, 0, 0)),
            scratch_shapes=[pltpu.VMEM((d, hid), BF16), pltpu.VMEM((d, hid), BF16), pltpu.VMEM((hid, d), BF16)],
        ),
        out_shape=jax.ShapeDtypeStruct((n_slots, s, l), BF16),
        compiler_params=_params(("arbitrary",)),
        name="experts",
    )(block_expert, n_used, xs3, wg, wu, wd)


def _combine_kernel(c0_ref, c1_ref, n0_ref, n1_ref, h_ref, r_ref, nw_ref, ys_hbm, o_ref, buf, sem):
    i = pl.program_id(0)
    n_steps = pl.num_programs(0)
    tm = h_ref.shape[0]

    def row_copy(dst_row, src_row, slot):
        return pltpu.make_async_copy(ys_hbm.at[src_row], buf.at[slot, dst_row], sem.at[slot])

    def issue(d0, d1, slot):
        def body(a, carry):
            row_copy(a, d0[a], slot).start(priority=0)
            row_copy(tm + a, d1[a], slot).start(priority=1)
            return carry
        lax.fori_loop(0, tm, body, 0, unroll=8)

    @pl.when(i == 0)
    def _():
        issue(c0_ref, c1_ref, 0)

    @pl.when(i + 1 < n_steps)
    def _():
        issue(n0_ref, n1_ref, (i + 1) % 2)

    slot = i % 2
    pltpu.make_async_copy(ys_hbm.at[pl.ds(0, 2 * tm)], buf.at[slot], sem.at[slot]).wait()

    r = r_ref[...]
    y = (_from_row_tiles(buf[slot, 0:tm]).astype(F32) * r[:, 2:3]
         + _from_row_tiles(buf[slot, tm:2 * tm]).astype(F32) * r[:, 3:4])
    h = h_ref[...] + y
    ms = jnp.mean(h * h, axis=-1, keepdims=True)
    o_ref[...] = h * lax.rsqrt(ms + EPS) * nw_ref[...]


def _combine(dest0, dest1, h1, r, nw, ys, *, tm):
    n, d = h1.shape
    n_steps = n // tm
    cur = pl.BlockSpec((tm,), lambda i: (i,), memory_space=pltpu.SMEM)
    nxt = pl.BlockSpec((tm,), lambda i: (jnp.minimum(i + 1, n_steps - 1),), memory_space=pltpu.SMEM)
    return pl.pallas_call(
        _combine_kernel,
        grid=(n_steps,),
        in_specs=[
            cur, cur, nxt, nxt,
            pl.BlockSpec((tm, d), lambda i: (i, 0)),
            pl.BlockSpec((tm, r.shape[1]), lambda i: (i, 0)),
            pl.BlockSpec((1, d), lambda i: (0, 0)),
            pl.BlockSpec(memory_space=pl.ANY),
        ],
        out_specs=pl.BlockSpec((tm, d), lambda i: (i, 0)),
        out_shape=jax.ShapeDtypeStruct((n, d), F32),
        scratch_shapes=[pltpu.VMEM((2, 2 * tm) + ys.shape[1:], BF16), pltpu.SemaphoreType.DMA((2,))],
        compiler_params=pltpu.CompilerParams(dimension_semantics=("arbitrary",), vmem_limit_bytes=VMEM_LIMIT,
                                             disable_bounds_checks=True),
        name="combine",
    )(dest0, dest1, dest0, dest1, h1, r, nw, ys)


def _layer(h, positions, norm_mix_w, w_in, gk_up, gk_bias, gla_norm_w, w_br_gla, ret_norm_w, w_br_ret, w_out,
           norm_ffn_w, rg_w, rg_b, re_w, re_b, wg, wu, wd, norm_final_w, *, tm_in, tn_in, gla_rows, tm_mix,
           tm_comb, disp_chunk):
    batch, seq, d = h.shape
    n = batch * seq
    x2 = h.reshape(n, d)
    gqk, gv = GLA_HEADS * GLA_DK, GLA_HEADS * GLA_DV
    rqk, rv = RET_HEADS * RET_DK, RET_HEADS * RET_DV

    sizes = (gqk, gqk, gv, gv, GLA_GATE_RANK, rqk, rqk, rv, rv, d, d)
    offs = np.concatenate([[0], np.cumsum(sizes)])
    seg = lambda i: w_in[:, offs[i]:offs[i + 1]]
    perm = np.concatenate([np.concatenate([np.arange(0, RET_DK, 2), np.arange(1, RET_DK, 2)]) + hh * RET_DK
                           for hh in range(RET_HEADS)])
    w_main = jnp.concatenate([seg(7), seg(8), seg(2), seg(3), seg(5)[:, perm], seg(6)[:, perm], seg(9), seg(10),
                              seg(0), seg(1)], axis=1).astype(BF16)
    assert rv % gv == 0 and gv == rqk == d and rv == 2 * d and gqk * 2 == d
    ret_cols = (4 + 2, 4 + 3, 0, 1)
    gla_cols = (2 * (4 + 6), 2 * (4 + 6) + 1, 4, 5)
    mix_cols = (4 + 4, 4 + 5)
    w_gd = jnp.pad(seg(4), ((0, 0), (0, LANES - GLA_GATE_RANK))).astype(BF16)

    proj, gd = _inproj(x2, norm_mix_w.reshape(1, d), w_main, w_gd, tm=tm_in, tn=tn_in)

    up = jnp.pad(gk_up, ((0, LANES - GLA_GATE_RANK), (0, 0)))
    up_hi = up.astype(BF16)
    up2 = jnp.stack([up_hi, (up - up_hi.astype(F32)).astype(BF16)])
    o_gla = _gla(proj, gd, up2, gk_bias.reshape(1, gqk), gla_norm_w.reshape(1, GLA_DV),
                 batch=batch, seq=seq, rows=gla_rows, cols=gla_cols)

    theta = (1.0 / (ROPE_BASE ** jnp.linspace(0.0, 1.0, RET_DK // 2, dtype=F32))).reshape(1, RET_DK // 2)
    o_ret = _retention(proj, positions.reshape(n, 1), theta, ret_norm_w.reshape(1, rv),
                       batch=batch, seq=seq, cols=ret_cols)

    wr = jnp.concatenate([re_w.transpose(1, 0, 2).reshape(d, N_EXPERTS), rg_w], axis=1)
    wr = jnp.pad(wr, ((0, 0), (0, LANES - N_EXPERTS - N_GROUPS)))
    wr_hi = wr.astype(BF16)
    wr2 = jnp.stack([wr_hi, (wr - wr_hi.astype(F32)).astype(BF16)])
    br = jnp.pad(jnp.concatenate([re_b.reshape(-1), rg_b]), (0, LANES - N_EXPERTS - N_GROUPS)).reshape(1, LANES)
    h1, t, r, rt, cnt = _mix(o_gla, o_ret, proj, x2, w_br_gla.astype(BF16), w_br_ret.astype(BF16),
                             w_out.astype(BF16), norm_ffn_w.reshape(1, d), wr2, br, tm=tm_mix, cols=mix_cols)

    blk = EXPERT_ROWS
    n_assign = 2 * n
    n_slots = -(-(n_assign + N_EXPERTS * (blk - 1)) // blk) * blk
    counts = cnt[0, :N_EXPERTS].astype(jnp.int32)
    padded = ((counts + blk - 1) // blk) * blk
    pad_end = jnp.cumsum(padded)
    pad_start = pad_end - padded
    dest0 = pad_start[rt[0].astype(jnp.int32)] + rt[4].astype(jnp.int32)
    dest1 = pad_start[rt[1].astype(jnp.int32)] + rt[5].astype(jnp.int32)
    block_start = jnp.arange(n_slots // blk, dtype=jnp.int32) * blk
    block_expert = jnp.minimum(jnp.sum((pad_end[None, :] <= block_start[:, None]).astype(jnp.int32), axis=1),
                               N_EXPERTS - 1)
    n_used = (pad_end[-1:] // blk).astype(jnp.int32)
    spare = pad_end[-1] + jnp.arange(N_EXPERTS, dtype=jnp.int32) * blk
    tail = jnp.concatenate([jnp.maximum(pad_end - blk, 0), jnp.minimum(spare, n_slots - blk)]).astype(jnp.int32)
    has = jnp.concatenate([counts, (spare < n_slots).astype(jnp.int32)])

    xs = _dispatch(t, dest0, dest1, tail, has, n_slots=n_slots, chunk=disp_chunk)
    ys = _experts(xs, block_expert, n_used, wg, wu, wd)
    out = _combine(dest0, dest1, h1, r, norm_final_w.reshape(1, d), ys, tm=tm_comb)
    return out.reshape(batch, seq, d)


def kernel(x, positions, norm_mix_w, w_in, gla_gk_up, gla_gk_bias, gla_norm_w, w_branch_gla, ret_norm_w,
           w_branch_ret, w_out, norm_ffn_w, router_group_w, router_group_b, router_expert_w, router_expert_b,
           expert_w_gate, expert_w_up, expert_w_down, norm_final_w):
    assert norm_mix_w.shape[0] == 1, "single-layer block"
    return _layer(x, positions, norm_mix_w[0], w_in[0], gla_gk_up[0], gla_gk_bias[0], gla_norm_w[0], w_branch_gla[0],
                  ret_norm_w[0], w_branch_ret[0], w_out[0], norm_ffn_w[0], router_group_w[0], router_group_b[0],
                  router_expert_w[0], router_expert_b[0], expert_w_gate[0], expert_w_up[0], expert_w_down[0],
                  norm_final_w, tm_in=2048, tn_in=1024, gla_rows=256, tm_mix=512, tm_comb=256,
                  disp_chunk=1024)
```

```python
import functools
import math

import jax
import jax.numpy as jnp
import numpy as np
from jax import lax
from jax.experimental import pallas as pl
from jax.experimental.pallas import tpu as pltpu

F32 = jnp.float32
BF16 = jnp.bfloat16

EPS = 1e-6
GLA_HEADS = 4
GLA_DK = 128
GLA_DV = 256
GLA_GATE_RANK = 16
GLA_GATE_TEMP = 16.0
RET_HEADS = 4
RET_DK = 256
RET_DV = 512
ROPE_BASE = 10000.0
N_GROUPS = 4
EXPERTS_PER_GROUP = 8
N_EXPERTS = N_GROUPS * EXPERTS_PER_GROUP
EXPERT_HIDDEN = 512

LANES = 128
GLA_CHUNK = 128
GLA_DIAG = 8
RET_CHUNK = 256
EXPERT_ROWS = 512
VMEM_LIMIT = 48 * 1024 * 1024


def _dot(a, b):
    return jnp.dot(a, b, preferred_element_type=F32)


def _dot_nt(a, b):
    return lax.dot_general(a, b, (((1,), (1,)), ((), ())), preferred_element_type=F32)


def _split2(a):
    hi = a.astype(BF16)
    lo = (a - hi.astype(F32)).astype(BF16)
    return hi, lo


def _params(sem, vmem=VMEM_LIMIT):
    return pltpu.CompilerParams(dimension_semantics=sem, vmem_limit_bytes=vmem)


def _inproj_kernel(x_ref, nw_ref, w_ref, wgd_ref, proj_ref, gd_ref, u_scr):
    @pl.when(pl.program_id(1) == 0)
    def _():
        x = x_ref[...]
        ms = jnp.mean(x * x, axis=-1, keepdims=True)
        u = (x * lax.rsqrt(ms + EPS) * nw_ref[...]).astype(BF16)
        u_scr[...] = u
        gd_ref[...] = _dot(u, wgd_ref[...])

    proj_ref[...] = _dot(u_scr[...], w_ref[...]).astype(BF16)


def _inproj(x2, nw, w_main, w_gd, *, tm, tn):
    n, d = x2.shape
    p = w_main.shape[1]
    return pl.pallas_call(
        _inproj_kernel,
        grid=(n // tm, p // tn),
        in_specs=[
            pl.BlockSpec((tm, d), lambda i, j: (i, 0)),
            pl.BlockSpec((1, d), lambda i, j: (0, 0)),
            pl.BlockSpec((d, tn), lambda i, j: (0, j)),
            pl.BlockSpec((d, LANES), lambda i, j: (0, 0)),
        ],
        out_specs=[
            pl.BlockSpec((tm, tn), lambda i, j: (i, j)),
            pl.BlockSpec((tm, LANES), lambda i, j: (i, 0)),
        ],
        out_shape=[
            jax.ShapeDtypeStruct((n, p), BF16),
            jax.ShapeDtypeStruct((n, LANES), F32),
        ],
        scratch_shapes=[pltpu.VMEM((tm, d), BF16)],
        compiler_params=_params(("arbitrary", "arbitrary")),
        name="inproj",
    )(x2, nw, w_main, w_gd)


def _gla_tables(c):
    levels = []
    s = c // 2
    while s >= GLA_DIAG:
        levels.append(s)
        s //= 2
    i = np.arange(c)[:, None]
    t = np.arange(c)[None, :]
    mats = [t <= i, t > i]
    masks = []
    for s in levels:
        bs = (i // s) * s
        mats.append((t > bs) & (t <= i))
        mats.append((t > i) & (t <= np.minimum(bs + s, c - 1)))
        masks.append(((i // (2 * s)) == (t // (2 * s))) & (((i // s) % 2) == 1) & (((t // s) % 2) == 0))
    for d in range(1, GLA_DIAG):
        mats.append((t > i - d) & (t <= i))
    for d in range(GLA_DIAG):
        masks.append((t == i - d) & ((i % GLA_DIAG) >= d))
    g = np.concatenate(mats, 0).astype(np.float32)
    m = np.stack(masks, 0).astype(np.float32)
    return g, m, tuple(levels)


def _gla_kernel(q_ref, k_ref, v_ref, gg_ref, gd_ref, up_ref, bias_ref, nw_ref, g_ref, m_ref,
                o_ref, st_ref, *, c, nlev):
    h_n, dk, dv = GLA_HEADS, GLA_DK, GLA_DV
    rows = q_ref.shape[0]

    @pl.when(pl.program_id(1) == 0)
    def _():
        st_ref[...] = jnp.zeros_like(st_ref)

    g_tab = g_ref[...]
    up_hi = up_ref[0]
    up_lo = up_ref[1]
    q_scale = dk ** -0.5
    band0 = 2 + 2 * nlev

    def chunk(ci, carry):
        r0 = pl.multiple_of(ci * c, c)
        rs = pl.ds(r0, c)
        gd_hi, gd_lo = _split2(gd_ref[rs, :])
        xg = _dot(gd_hi, up_hi) + _dot(gd_hi, up_lo) + _dot(gd_lo, up_hi) + bias_ref[...]
        la2 = (jnp.minimum(xg, 0.0) - jnp.log1p(jnp.exp(-jnp.abs(xg)))) * (math.log2(math.e) / GLA_GATE_TEMP)
        ex = jnp.exp2(_dot(g_tab, la2.astype(BF16)))

        for h in range(h_n):
            ks = slice(h * dk, (h + 1) * dk)
            vs = slice(h * dv, (h + 1) * dv)

            def tab(m, ks=ks):
                return ex[m * c:(m + 1) * c, ks]

            qh = q_ref[rs, ks].astype(F32) * q_scale
            kh = k_ref[rs, ks].astype(F32)
            vh = v_ref[rs, vs]
            st = st_ref[h]
            eb = tab(0)
            o = _dot_nt((qh * eb).astype(BF16), st.astype(BF16))
            parts = [m_ref[nlev] * jnp.sum(qh * kh, axis=-1, keepdims=True)]
            for li in range(nlev):
                qs = (qh * tab(2 + 2 * li)).astype(BF16)
                kk = (kh * tab(3 + 2 * li)).astype(BF16)
                parts.append(m_ref[li] * _dot_nt(qs, kk))
            kh_tiles = kh.reshape(c // GLA_DIAG, GLA_DIAG, dk)
            for d in range(1, GLA_DIAG):
                kr = pltpu.roll(kh_tiles, d, 1).reshape(c, dk)
                dd = jnp.sum(qh * kr * tab(band0 + d - 1), axis=-1, keepdims=True)
                parts.append(m_ref[nlev + d] * dd)
            while len(parts) > 1:
                parts = [functools.reduce(jnp.add, parts[p:p + 2]) for p in range(0, len(parts), 2)]
            o = o + _dot(parts[0].astype(BF16), vh)
            e_last = eb[c - 1:c, :]
            k_st = (kh * tab(1)).astype(BF16)
            v_t = vh.astype(F32).T.astype(BF16)
            st_ref[h] = st * e_last + _dot(v_t, k_st)

            ms = jnp.mean(o * o, axis=-1, keepdims=True)
            y = o * lax.rsqrt(ms + EPS) * nw_ref[...]
            gg = gg_ref[rs, vs].astype(F32)
            o_ref[rs, vs] = (y * (gg * jax.nn.sigmoid(gg))).astype(BF16)
        return carry

    lax.fori_loop(0, rows // c, chunk, 0, unroll=2)


def _gla(proj, gd, up2, bias, nw, *, batch, seq, rows, cols):
    n = proj.shape[0]
    c = GLA_CHUNK
    g_np, m_np, levels = _gla_tables(c)
    g_tab = jnp.asarray(g_np, BF16)
    m_tab = jnp.asarray(m_np, F32)
    qk_w = GLA_HEADS * GLA_DK
    v_w = GLA_HEADS * GLA_DV
    spb = seq // rows
    row = lambda b, s: b * spb + s
    kern = functools.partial(_gla_kernel, c=c, nlev=len(levels))
    return pl.pallas_call(
        kern,
        grid=(batch, spb),
        in_specs=[
            pl.BlockSpec((rows, qk_w), lambda b, s: (row(b, s), cols[0])),
            pl.BlockSpec((rows, qk_w), lambda b, s: (row(b, s), cols[1])),
            pl.BlockSpec((rows, v_w), lambda b, s: (row(b, s), cols[2])),
            pl.BlockSpec((rows, v_w), lambda b, s: (row(b, s), cols[3])),
            pl.BlockSpec((rows, LANES), lambda b, s: (row(b, s), 0)),
            pl.BlockSpec(up2.shape, lambda b, s: (0, 0, 0)),
            pl.BlockSpec(bias.shape, lambda b, s: (0, 0)),
            pl.BlockSpec(nw.shape, lambda b, s: (0, 0)),
            pl.BlockSpec(g_tab.shape, lambda b, s: (0, 0)),
            pl.BlockSpec(m_tab.shape, lambda b, s: (0, 0, 0)),
        ],
        out_specs=pl.BlockSpec((rows, v_w), lambda b, s: (row(b, s), 0)),
        out_shape=jax.ShapeDtypeStruct((n, v_w), BF16),
        scratch_shapes=[pltpu.VMEM((GLA_HEADS, GLA_DV, GLA_DK), F32)],
        compiler_params=_params(("arbitrary", "arbitrary")),
        name="gla",
    )(proj, proj, proj, proj, gd, up2, bias, nw, g_tab, m_tab)


def _ret_log_gamma(h):
    return math.log(1.0 - 2.0 ** (-5.0 - h))


def _ret_kernel(pos_ref, th_ref, q_ref, k_ref, v_ref, rg_ref, nw_ref, o_ref, s_ref, dm_ref):
    h_n, dk, dv = RET_HEADS, RET_DK, RET_DV
    c = q_ref.shape[0]
    half = dk // 2

    @pl.when(pl.program_id(1) == 0)
    def _():
        s_ref[...] = jnp.zeros_like(s_ref)

    @pl.when((pl.program_id(0) == 0) & (pl.program_id(1) == 0))
    def _():
        rel = (lax.broadcasted_iota(jnp.int32, (c, c), 0) - lax.broadcasted_iota(jnp.int32, (c, c), 1)).astype(F32)
        for h in range(h_n):
            dm_ref[h] = jnp.where(rel >= 0, jnp.exp(_ret_log_gamma(h) * jnp.maximum(rel, 0.0)), 0.0)

    ang = pos_ref[...].astype(F32) * th_ref[...]
    cs = jnp.cos(ang)
    sn = jnp.sin(ang)
    idx = lax.broadcasted_iota(jnp.int32, (c, 1), 0).astype(F32)
    k_scale = dk ** -0.5

    def rot(ref, h):
        a = ref[:, h * dk:h * dk + half].astype(F32)
        b = ref[:, h * dk + half:(h + 1) * dk].astype(F32)
        return jnp.concatenate([a * cs - b * sn, b * cs + a * sn], axis=-1)

    for h in range(h_n):
        lg = _ret_log_gamma(h)
        vs = slice(h * dv, (h + 1) * dv)
        qr = rot(q_ref, h)
        kr = rot(k_ref, h) * k_scale
        vh = v_ref[:, vs]
        scores = _dot_nt(qr.astype(BF16), kr.astype(BF16)) * dm_ref[h]
        q_dec = jnp.exp(lg * (idx + 1.0))
        k_dec = jnp.exp(lg * (c - 1.0 - idx))
        s = s_ref[h]
        o = _dot(scores.astype(BF16), vh) + _dot((qr * q_dec).astype(BF16), s.astype(BF16))
        k_t = (kr * k_dec).T.astype(BF16)
        s_ref[h] = math.exp(lg * c) * s + _dot(k_t, vh)

        mu = jnp.mean(o, axis=-1, keepdims=True)
        oc = o - mu
        var = jnp.mean(oc * oc, axis=-1, keepdims=True)
        y = oc * lax.rsqrt(var + EPS) * nw_ref[:, vs]
        rg = rg_ref[:, vs].astype(F32)
        o_ref[:, vs] = (y * (rg * jax.nn.sigmoid(rg))).astype(BF16)


def _retention(proj, pos, theta, nw, *, batch, seq, cols):
    n = proj.shape[0]
    c = RET_CHUNK
    qk_w = RET_HEADS * RET_DK
    v_w = RET_HEADS * RET_DV
    spb = seq // c
    row = lambda b, s: b * spb + s
    return pl.pallas_call(
        _ret_kernel,
        grid=(batch, spb),
        in_specs=[
            pl.BlockSpec((c, 1), lambda b, s: (row(b, s), 0)),
            pl.BlockSpec(theta.shape, lambda b, s: (0, 0)),
            pl.BlockSpec((c, qk_w), lambda b, s: (row(b, s), cols[0])),
            pl.BlockSpec((c, qk_w), lambda b, s: (row(b, s), cols[1])),
            pl.BlockSpec((c, v_w), lambda b, s: (row(b, s), cols[2])),
            pl.BlockSpec((c, v_w), lambda b, s: (row(b, s), cols[3])),
            pl.BlockSpec(nw.shape, lambda b, s: (0, 0)),
        ],
        out_specs=pl.BlockSpec((c, v_w), lambda b, s: (row(b, s), 0)),
        out_shape=jax.ShapeDtypeStruct((n, v_w), BF16),
        scratch_shapes=[
            pltpu.VMEM((RET_HEADS, RET_DK, RET_DV), F32),
            pltpu.VMEM((RET_HEADS, c, c), F32),
        ],
        compiler_params=_params(("arbitrary", "arbitrary")),
        name="retention",
    )(pos, theta, proj, proj, proj, proj, nw)


def _mix_kernel(og_ref, or_ref, ma_ref, mb_ref, x_ref, wa_ref, wb_ref, wo_ref, nw_ref, wr_ref, br_ref,
                tri_ref, h_ref, t_ref, r_ref, rt_ref, cnt_ref, base_ref):
    tm = x_ref.shape[0]

    @pl.when(pl.program_id(0) == 0)
    def _():
        base_ref[...] = jnp.zeros_like(base_ref)

    ya = _dot(og_ref[...], wa_ref[...])
    yb = _dot(or_ref[...], wb_ref[...])
    merged = (jax.nn.sigmoid(ma_ref[...].astype(F32)) * ya + jax.nn.sigmoid(mb_ref[...].astype(F32)) * yb)
    h = x_ref[...] + _dot(merged.astype(BF16), wo_ref[...])
    h_ref[...] = h
    ms = jnp.mean(h * h, axis=-1, keepdims=True)
    t = h * lax.rsqrt(ms + EPS) * nw_ref[...]
    t_ref[...] = _to_row_tiles(t)

    lane = lax.broadcasted_iota(jnp.int32, (tm, LANES), 1)
    neg = jnp.float32(-1e30)
    big = jnp.int32(1 << 20)

    def first_max(v):
        m = jnp.max(v, axis=-1, keepdims=True)
        return m, jnp.min(jnp.where(v == m, lane, big), axis=-1, keepdims=True)

    t_hi, t_lo = _split2(t)
    lg = _dot(t_hi, wr_ref[0]) + _dot(t_hi, wr_ref[1]) + _dot(t_lo, wr_ref[0]) + br_ref[...]
    g_valid = (lane >= N_EXPERTS) & (lane < N_EXPERTS + N_GROUPS)
    g_m, g_lane = first_max(jnp.where(g_valid, lg, neg))
    g_w = 1.0 / jnp.sum(jnp.where(g_valid, jnp.exp(lg - g_m), 0.0), axis=-1, keepdims=True)
    g_idx = g_lane - N_EXPERTS
    e_valid = (lane < N_EXPERTS) & ((lane // EXPERTS_PER_GROUP) == g_idx)
    el = jnp.where(e_valid, lg, neg)
    v1, i1 = first_max(el)
    v2, i2 = first_max(jnp.where(lane == i1, neg, el))
    e21 = jnp.exp(v2 - v1)
    w1 = g_w / (1.0 + e21)
    w2 = g_w * e21 / (1.0 + e21)

    o1 = lane == i1
    o2 = lane == i2
    osum = jnp.where(o1 | o2, 1.0, 0.0)
    base = base_ref[0:1, :]
    before = _dot(tri_ref[...], osum.astype(BF16)) + base
    r1 = jnp.sum(jnp.where(o1, before, 0.0), axis=-1, keepdims=True)
    r2 = jnp.sum(jnp.where(o2, before, 0.0), axis=-1, keepdims=True)
    base = base + jnp.sum(osum, axis=0, keepdims=True)
    base_ref[...] = jnp.broadcast_to(base, base_ref.shape)
    cnt_ref[...] = jnp.broadcast_to(base, cnt_ref.shape)

    cols = (i1.astype(F32), i2.astype(F32), w1, w2, r1, r2)
    wide = jnp.zeros((tm, LANES), F32)
    for ci, col in enumerate(cols):
        wide = jnp.where(lane == ci, col, wide)
    r_ref[...] = wide[:, :r_ref.shape[1]]
    rt_ref[...] = wide.T[:rt_ref.shape[0], :]


def _mix(o_gla, o_ret, proj, x2, wa, wb, wo, nw, wr2, br, *, tm, cols):
    n, d = x2.shape
    tri = jnp.asarray(np.tril(np.ones((tm, tm), np.float32), -1), BF16)
    const = lambda shape: pl.BlockSpec(shape, lambda i: (0,) * len(shape))
    return pl.pallas_call(
        _mix_kernel,
        grid=(n // tm,),
        in_specs=[
            pl.BlockSpec((tm, o_gla.shape[1]), lambda i: (i, 0)),
            pl.BlockSpec((tm, o_ret.shape[1]), lambda i: (i, 0)),
            pl.BlockSpec((tm, d), lambda i: (i, cols[0])),
            pl.BlockSpec((tm, d), lambda i: (i, cols[1])),
            pl.BlockSpec((tm, d), lambda i: (i, 0)),
            const(wa.shape), const(wb.shape), const(wo.shape), const(nw.shape),
            const(wr2.shape), const(br.shape), const(tri.shape),
        ],
        out_specs=[
            pl.BlockSpec((tm, d), lambda i: (i, 0)),
            pl.BlockSpec((tm, d // LANES, LANES), lambda i: (i, 0, 0)),
            pl.BlockSpec((tm, 8), lambda i: (i, 0)),
            pl.BlockSpec((8, tm), lambda i: (0, i)),
            pl.BlockSpec((8, LANES), lambda i: (0, 0)),
        ],
        out_shape=[
            jax.ShapeDtypeStruct((n, d), F32),
            jax.ShapeDtypeStruct((n, d // LANES, LANES), BF16),
            jax.ShapeDtypeStruct((n, 8), F32),
            jax.ShapeDtypeStruct((8, n), F32),
            jax.ShapeDtypeStruct((8, LANES), F32),
        ],
        scratch_shapes=[pltpu.VMEM((8, LANES), F32)],
        compiler_params=_params(("arbitrary",)),
        name="mix_router",
    )(o_gla, o_ret, proj, proj, x2, wa, wb, wo, nw, wr2, br, tri)


def _to_row_tiles(x):
    return x.astype(BF16).reshape(x.shape[0], x.shape[1] // LANES, LANES)


def _from_row_tiles(x3):
    return x3.reshape(x3.shape[0], x3.shape[1] * x3.shape[2])


def _dispatch_kernel(tail_ref, has_ref, d0_ref, d1_ref, t_ref, xs_hbm, zero_ref, sem, zsem):
    step = pl.program_id(0)
    tok_n = t_ref.shape[0]
    blk = zero_ref.shape[0]

    def tail_copy(e):
        return pltpu.make_async_copy(zero_ref, xs_hbm.at[pl.ds(tail_ref[e], blk)], zsem)

    @pl.when(step == 0)
    def _():
        zero_ref[...] = jnp.zeros_like(zero_ref)
        for e in range(tail_ref.shape[0]):
            @pl.when(has_ref[e] > 0)
            def _(e=e):
                tail_copy(e).start()
        for e in range(tail_ref.shape[0]):
            @pl.when(has_ref[e] > 0)
            def _(e=e):
                tail_copy(e).wait()

    def row_copy(a, dst):
        return pltpu.make_async_copy(t_ref.at[a], xs_hbm.at[dst], sem)

    def body(a, carry):
        row_copy(a, d0_ref[a]).start(priority=0)
        row_copy(a, d1_ref[a]).start(priority=1)
        return carry

    lax.fori_loop(0, tok_n, body, 0, unroll=8)

    for _ in range(2):
        pltpu.make_async_copy(t_ref, xs_hbm.at[pl.ds(0, tok_n)], sem).wait()


def _dispatch(t3, dest0, dest1, tail, has, *, n_slots, chunk):
    n, s, l = t3.shape
    return pl.pallas_call(
        _dispatch_kernel,
        grid_spec=pltpu.PrefetchScalarGridSpec(
            num_scalar_prefetch=2,
            grid=(n // chunk,),
            in_specs=[
                pl.BlockSpec((chunk,), lambda i, *_: (i,), memory_space=pltpu.SMEM),
                pl.BlockSpec((chunk,), lambda i, *_: (i,), memory_space=pltpu.SMEM),
                pl.BlockSpec((chunk, s, l), lambda i, *_: (i, 0, 0)),
            ],
            out_specs=pl.BlockSpec(memory_space=pl.ANY),
            scratch_shapes=[
                pltpu.VMEM((EXPERT_ROWS, s, l), BF16),
                pltpu.SemaphoreType.DMA,
                pltpu.SemaphoreType.DMA,
            ],
        ),
        out_shape=jax.ShapeDtypeStruct((n_slots, s, l), BF16),
        compiler_params=pltpu.CompilerParams(dimension_semantics=("arbitrary",), has_side_effects=True,
                                             vmem_limit_bytes=VMEM_LIMIT, disable_bounds_checks=True),
        name="dispatch",
    )(tail, has, dest0, dest1, t3)


def _expert_kernel(be_ref, nu_ref, xs_ref, wg_ref, wu_ref, wd_ref, ys_ref, wg_s, wu_s, wd_s, hid_s):
    b = pl.program_id(0)
    last = pl.num_programs(0) - 2
    up_blk = jnp.minimum(b, last)
    up_prev = jnp.minimum(jnp.maximum(b - 1, 0), last)
    dn_blk = jnp.maximum(b - 1, 0)
    dn_prev = jnp.maximum(b - 2, 0)

    @pl.when(b == 0)
    def _():
        hid_s[...] = jnp.zeros_like(hid_s)

    @pl.when((b == 0) | (be_ref[up_blk] != be_ref[up_prev]))
    def _():
        wg_s[...] = wg_ref[0].astype(BF16)
        wu_s[...] = wu_ref[0].astype(BF16)

    @pl.when((b <= 1) | (be_ref[dn_blk] != be_ref[dn_prev]))
    def _():
        wd_s[...] = wd_ref[0].astype(BF16)

    slot = b % 2
    x = _from_row_tiles(xs_ref[...])
    out = _dot(hid_s[1 - slot], wd_s[...])
    g = _dot(x, wg_s[...])
    u = _dot(x, wu_s[...])
    ys_ref[...] = _to_row_tiles(out)
    hid_s[slot] = (g * jax.nn.sigmoid(g) * u).astype(BF16)


def _experts(xs3, block_expert, n_used, wg, wu, wd):
    n_slots, s, l = xs3.shape
    d = s * l
    blk = EXPERT_ROWS
    hid = wg.shape[2]
    nb = n_slots // blk
    up = lambda b: jnp.minimum(b, nb - 1)
    dn = lambda b: jnp.maximum(b - 1, 0)
    return pl.pallas_call(
        _expert_kernel,
        grid_spec=pltpu.PrefetchScalarGridSpec(
            num_scalar_prefetch=2,
            grid=(nb + 1,),
            in_specs=[
                pl.BlockSpec((blk, s, l), lambda b, be, nu: (jnp.minimum(b, nu[0] - 1), 0, 0)),
                pl.BlockSpec((1, d, hid), lambda b, be, nu: (be[up(b)], 0, 0)),
                pl.BlockSpec((1, d, hid), lambda b, be, nu: (be[up(b)], 0, 0)),
                pl.BlockSpec((1, hid, d), lambda b, be, nu: (be[dn(b)], 0, 0)),
            ],
            out_specs=pl.BlockSpec((blk, s, l), lambda b, be, nu: (dn(b), 0, 0)),
            scratch_shapes=[pltpu.VMEM((d, hid), BF16), pltpu.VMEM((d, hid), BF16), pltpu.VMEM((hid, d), BF16),
                            pltpu.VMEM((2, blk, hid), BF16)],
        ),
        out_shape=jax.ShapeDtypeStruct((n_slots, s, l), BF16),
        compiler_params=_params(("arbitrary",)),
        name="experts",
    )(block_expert, n_used, xs3, wg, wu, wd)


def _combine_kernel(c0_ref, c1_ref, n0_ref, n1_ref, h_ref, r_ref, nw_ref, ys_hbm, o_ref, buf, sem):
    i = pl.program_id(0)
    n_steps = pl.num_programs(0)
    tm = h_ref.shape[0]

    def row_copy(dst_row, src_row, slot):
        return pltpu.make_async_copy(ys_hbm.at[src_row], buf.at[slot, dst_row], sem.at[slot])

    def issue(d0, d1, slot):
        def body(a, carry):
            row_copy(a, d0[a], slot).start(priority=0)
            row_copy(tm + a, d1[a], slot).start(priority=1)
            return carry
        lax.fori_loop(0, tm, body, 0, unroll=8)

    @pl.when(i == 0)
    def _():
        issue(c0_ref, c1_ref, 0)

    @pl.when(i + 1 < n_steps)
    def _():
        issue(n0_ref, n1_ref, (i + 1) % 2)

    slot = i % 2
    pltpu.make_async_copy(ys_hbm.at[pl.ds(0, 2 * tm)], buf.at[slot], sem.at[slot]).wait()

    r = r_ref[...]
    y = (_from_row_tiles(buf[slot, 0:tm]).astype(F32) * r[:, 2:3]
         + _from_row_tiles(buf[slot, tm:2 * tm]).astype(F32) * r[:, 3:4])
    h = h_ref[...] + y
    ms = jnp.mean(h * h, axis=-1, keepdims=True)
    o_ref[...] = h * lax.rsqrt(ms + EPS) * nw_ref[...]


def _combine(dest0, dest1, h1, r, nw, ys, *, tm):
    n, d = h1.shape
    n_steps = n // tm
    cur = pl.BlockSpec((tm,), lambda i: (i,), memory_space=pltpu.SMEM)
    nxt = pl.BlockSpec((tm,), lambda i: (jnp.minimum(i + 1, n_steps - 1),), memory_space=pltpu.SMEM)
    return pl.pallas_call(
        _combine_kernel,
        grid=(n_steps,),
        in_specs=[
            cur, cur, nxt, nxt,
            pl.BlockSpec((tm, d), lambda i: (i, 0)),
            pl.BlockSpec((tm, r.shape[1]), lambda i: (i, 0)),
            pl.BlockSpec((1, d), lambda i: (0, 0)),
            pl.BlockSpec(memory_space=pl.ANY),
        ],
        out_specs=pl.BlockSpec((tm, d), lambda i: (i, 0)),
        out_shape=jax.ShapeDtypeStruct((n, d), F32),
        scratch_shapes=[pltpu.VMEM((2, 2 * tm) + ys.shape[1:], BF16), pltpu.SemaphoreType.DMA((2,))],
        compiler_params=pltpu.CompilerParams(dimension_semantics=("arbitrary",), vmem_limit_bytes=VMEM_LIMIT,
                                             disable_bounds_checks=True),
        name="combine",
    )(dest0, dest1, dest0, dest1, h1, r, nw, ys)


def _layer(h, positions, norm_mix_w, w_in, gk_up, gk_bias, gla_norm_w, w_br_gla, ret_norm_w, w_br_ret, w_out,
           norm_ffn_w, rg_w, rg_b, re_w, re_b, wg, wu, wd, norm_final_w, *, tm_in, tn_in, gla_rows, tm_mix,
           tm_comb, disp_chunk):
    batch, seq, d = h.shape
    n = batch * seq
    x2 = h.reshape(n, d)
    gqk, gv = GLA_HEADS * GLA_DK, GLA_HEADS * GLA_DV
    rqk, rv = RET_HEADS * RET_DK, RET_HEADS * RET_DV

    sizes = (gqk, gqk, gv, gv, GLA_GATE_RANK, rqk, rqk, rv, rv, d, d)
    offs = np.concatenate([[0], np.cumsum(sizes)])
    seg = lambda i: w_in[:, offs[i]:offs[i + 1]]
    perm = np.concatenate([np.concatenate([np.arange(0, RET_DK, 2), np.arange(1, RET_DK, 2)]) + hh * RET_DK
                           for hh in range(RET_HEADS)])
    w_main = jnp.concatenate([seg(7), seg(8), seg(2), seg(3), seg(5)[:, perm], seg(6)[:, perm], seg(9), seg(10),
                              seg(0), seg(1)], axis=1).astype(BF16)
    assert rv % gv == 0 and gv == rqk == d and rv == 2 * d and gqk * 2 == d
    ret_cols = (4 + 2, 4 + 3, 0, 1)
    gla_cols = (2 * (4 + 6), 2 * (4 + 6) + 1, 4, 5)
    mix_cols = (4 + 4, 4 + 5)
    w_gd = jnp.pad(seg(4), ((0, 0), (0, LANES - GLA_GATE_RANK))).astype(BF16)

    proj, gd = _inproj(x2, norm_mix_w.reshape(1, d), w_main, w_gd, tm=tm_in, tn=tn_in)

    up = jnp.pad(gk_up, ((0, LANES - GLA_GATE_RANK), (0, 0)))
    up_hi = up.astype(BF16)
    up2 = jnp.stack([up_hi, (up - up_hi.astype(F32)).astype(BF16)])
    o_gla = _gla(proj, gd, up2, gk_bias.reshape(1, gqk), gla_norm_w.reshape(1, GLA_DV),
                 batch=batch, seq=seq, rows=gla_rows, cols=gla_cols)

    theta = (1.0 / (ROPE_BASE ** jnp.linspace(0.0, 1.0, RET_DK // 2, dtype=F32))).reshape(1, RET_DK // 2)
    o_ret = _retention(proj, positions.reshape(n, 1), theta, ret_norm_w.reshape(1, rv),
                       batch=batch, seq=seq, cols=ret_cols)

    wr = jnp.concatenate([re_w.transpose(1, 0, 2).reshape(d, N_EXPERTS), rg_w], axis=1)
    wr = jnp.pad(wr, ((0, 0), (0, LANES - N_EXPERTS - N_GROUPS)))
    wr_hi = wr.astype(BF16)
    wr2 = jnp.stack([wr_hi, (wr - wr_hi.astype(F32)).astype(BF16)])
    br = jnp.pad(jnp.concatenate([re_b.reshape(-1), rg_b]), (0, LANES - N_EXPERTS - N_GROUPS)).reshape(1, LANES)
    h1, t, r, rt, cnt = _mix(o_gla, o_ret, proj, x2, w_br_gla.astype(BF16), w_br_ret.astype(BF16),
                             w_out.astype(BF16), norm_ffn_w.reshape(1, d), wr2, br, tm=tm_mix, cols=mix_cols)

    blk = EXPERT_ROWS
    n_assign = 2 * n
    n_slots = -(-(n_assign + N_EXPERTS * (blk - 1)) // blk) * blk
    counts = cnt[0, :N_EXPERTS].astype(jnp.int32)
    padded = ((counts + blk - 1) // blk) * blk
    pad_end = jnp.cumsum(padded)
    pad_start = pad_end - padded
    dest0 = pad_start[rt[0].astype(jnp.int32)] + rt[4].astype(jnp.int32)
    dest1 = pad_start[rt[1].astype(jnp.int32)] + rt[5].astype(jnp.int32)
    block_start = jnp.arange(n_slots // blk, dtype=jnp.int32) * blk
    block_expert = jnp.minimum(jnp.sum((pad_end[None, :] <= block_start[:, None]).astype(jnp.int32), axis=1),
                               N_EXPERTS - 1)
    n_used = (pad_end[-1:] // blk).astype(jnp.int32)
    spare = pad_end[-1] + jnp.arange(N_EXPERTS, dtype=jnp.int32) * blk
    tail = jnp.concatenate([jnp.maximum(pad_end - blk, 0), jnp.minimum(spare, n_slots - blk)]).astype(jnp.int32)
    has = jnp.concatenate([counts, (spare < n_slots).astype(jnp.int32)])

    xs = _dispatch(t, dest0, dest1, tail, has, n_slots=n_slots, chunk=disp_chunk)
    ys = _experts(xs, block_expert, n_used, wg, wu, wd)
    out = _combine(dest0, dest1, h1, r, norm_final_w.reshape(1, d), ys, tm=tm_comb)
    return out.reshape(batch, seq, d)


def kernel(x, positions, norm_mix_w, w_in, gla_gk_up, gla_gk_bias, gla_norm_w, w_branch_gla, ret_norm_w,
           w_branch_ret, w_out, norm_ffn_w, router_group_w, router_group_b, router_expert_w, router_expert_b,
           expert_w_gate, expert_w_up, expert_w_down, norm_final_w):
    assert norm_mix_w.shape[0] == 1, "single-layer block"
    return _layer(x, positions, norm_mix_w[0], w_in[0], gla_gk_up[0], gla_gk_bias[0], gla_norm_w[0], w_branch_gla[0],
                  ret_norm_w[0], w_branch_ret[0], w_out[0], norm_ffn_w[0], router_group_w[0], router_group_b[0],
                  router_expert_w[0], router_expert_b[0], expert_w_gate[0], expert_w_up[0], expert_w_down[0],
                  norm_final_w, tm_in=2048, tn_in=1024, gla_rows=256, tm_mix=512, tm_comb=256,
                  disp_chunk=1024)
```

```python
import functools
import math

import jax
import jax.numpy as jnp
import numpy as np
from jax import lax
from jax.experimental import pallas as pl
from jax.experimental.pallas import tpu as pltpu

F32 = jnp.float32
BF16 = jnp.bfloat16

EPS = 1e-6
GLA_HEADS = 4
GLA_DK = 128
GLA_DV = 256
GLA_GATE_RANK = 16
GLA_GATE_TEMP = 16.0
RET_HEADS = 4
RET_DK = 256
RET_DV = 512
ROPE_BASE = 10000.0
N_GROUPS = 4
EXPERTS_PER_GROUP = 8
N_EXPERTS = N_GROUPS * EXPERTS_PER_GROUP
EXPERT_HIDDEN = 512

LANES = 128
GLA_CHUNK = 128
GLA_DIAG = 8
RET_CHUNK = 256
EXPERT_ROWS = 512
VMEM_LIMIT = 48 * 1024 * 1024


def _dot(a, b):
    return jnp.dot(a, b, preferred_element_type=F32)


def _dot_nt(a, b):
    return lax.dot_general(a, b, (((1,), (1,)), ((), ())), preferred_element_type=F32)


def _split2(a):
    hi = a.astype(BF16)
    lo = (a - hi.astype(F32)).astype(BF16)
    return hi, lo


def _params(sem, vmem=VMEM_LIMIT):
    return pltpu.CompilerParams(dimension_semantics=sem, vmem_limit_bytes=vmem)


def _inproj_kernel(x_ref, nw_ref, w_ref, wgd_ref, proj_ref, gd_ref, u_scr):
    @pl.when(pl.program_id(1) == 0)
    def _():
        x = x_ref[...]
        ms = jnp.mean(x * x, axis=-1, keepdims=True)
        u = (x * lax.rsqrt(ms + EPS) * nw_ref[...]).astype(BF16)
        u_scr[...] = u
        gd_ref[...] = _dot(u, wgd_ref[...])

    proj_ref[...] = _dot(u_scr[...], w_ref[...]).astype(BF16)


def _inproj(x2, nw, w_main, w_gd, *, tm, tn):
    n, d = x2.shape
    p = w_main.shape[1]
    return pl.pallas_call(
        _inproj_kernel,
        grid=(n // tm, p // tn),
        in_specs=[
            pl.BlockSpec((tm, d), lambda i, j: (i, 0)),
            pl.BlockSpec((1, d), lambda i, j: (0, 0)),
            pl.BlockSpec((d, tn), lambda i, j: (0, j)),
            pl.BlockSpec((d, LANES), lambda i, j: (0, 0)),
        ],
        out_specs=[
            pl.BlockSpec((tm, tn), lambda i, j: (i, j)),
            pl.BlockSpec((tm, LANES), lambda i, j: (i, 0)),
        ],
        out_shape=[
            jax.ShapeDtypeStruct((n, p), BF16),
            jax.ShapeDtypeStruct((n, LANES), F32),
        ],
        scratch_shapes=[pltpu.VMEM((tm, d), BF16)],
        compiler_params=_params(("arbitrary", "arbitrary")),
        name="inproj",
    )(x2, nw, w_main, w_gd)


def _gla_tables(c):
    levels = []
    s = c // 2
    while s >= GLA_DIAG:
        levels.append(s)
        s //= 2
    i = np.arange(c)[:, None]
    t = np.arange(c)[None, :]
    mats = [t <= i, t > i]
    masks = []
    for s in levels:
        bs = (i // s) * s
        mats.append((t > bs) & (t <= i))
        mats.append((t > i) & (t <= np.minimum(bs + s, c - 1)))
        masks.append(((i // (2 * s)) == (t // (2 * s))) & (((i // s) % 2) == 1) & (((t // s) % 2) == 0))
    for d in range(1, GLA_DIAG):
        mats.append((t > i - d) & (t <= i))
    for d in range(GLA_DIAG):
        masks.append((t == i - d) & ((i % GLA_DIAG) >= d))
    g = np.concatenate(mats, 0).astype(np.float32)
    m = np.stack(masks, 0).astype(np.float32)
    return g, m, tuple(levels)


def _gla_kernel(q_ref, k_ref, v_ref, gg_ref, gd_ref, up_ref, bias_ref, nw_ref, g_ref, m_ref,
                o_ref, st_ref, *, c, nlev):
    h_n, dk, dv = GLA_HEADS, GLA_DK, GLA_DV
    rows = q_ref.shape[0]

    @pl.when(pl.program_id(1) == 0)
    def _():
        st_ref[...] = jnp.zeros_like(st_ref)

    g_tab = g_ref[...]
    up_hi = up_ref[0]
    up_lo = up_ref[1]
    q_scale = dk ** -0.5
    band0 = 2 + 2 * nlev

    def chunk(ci, carry):
        r0 = pl.multiple_of(ci * c, c)
        rs = pl.ds(r0, c)
        gd_hi, gd_lo = _split2(gd_ref[rs, :])
        xg = _dot(gd_hi, up_hi) + _dot(gd_hi, up_lo) + _dot(gd_lo, up_hi) + bias_ref[...]
        la2 = (jnp.minimum(xg, 0.0) - jnp.log1p(jnp.exp(-jnp.abs(xg)))) * (math.log2(math.e) / GLA_GATE_TEMP)
        ex = jnp.exp2(_dot(g_tab, la2.astype(BF16)))

        for h in range(h_n):
            ks = slice(h * dk, (h + 1) * dk)
            vs = slice(h * dv, (h + 1) * dv)

            def tab(m, ks=ks):
                return ex[m * c:(m + 1) * c, ks]

            qh = q_ref[rs, ks].astype(F32) * q_scale
            kh = k_ref[rs, ks].astype(F32)
            vh = v_ref[rs, vs]
            st = st_ref[h]
            eb = tab(0)
            o = _dot_nt((qh * eb).astype(BF16), st.astype(BF16))
            parts = [m_ref[nlev] * jnp.sum(qh * kh, axis=-1, keepdims=True)]
            for li in range(nlev):
                qs = (qh * tab(2 + 2 * li)).astype(BF16)
                kk = (kh * tab(3 + 2 * li)).astype(BF16)
                parts.append(m_ref[li] * _dot_nt(qs, kk))
            kh_tiles = kh.reshape(c // GLA_DIAG, GLA_DIAG, dk)
            for d in range(1, GLA_DIAG):
                kr = pltpu.roll(kh_tiles, d, 1).reshape(c, dk)
                dd = jnp.sum(qh * kr * tab(band0 + d - 1), axis=-1, keepdims=True)
                parts.append(m_ref[nlev + d] * dd)
            while len(parts) > 1:
                parts = [functools.reduce(jnp.add, parts[p:p + 2]) for p in range(0, len(parts), 2)]
            o = o + _dot(parts[0].astype(BF16), vh)
            e_last = eb[c - 1:c, :]
            k_st = (kh * tab(1)).astype(BF16)
            v_t = vh.astype(F32).T.astype(BF16)
            st_ref[h] = st * e_last + _dot(v_t, k_st)

            ms = jnp.mean(o * o, axis=-1, keepdims=True)
            y = o * lax.rsqrt(ms + EPS) * nw_ref[...]
            gg = gg_ref[rs, vs].astype(F32)
            o_ref[rs, vs] = (y * (gg * jax.nn.sigmoid(gg))).astype(BF16)
        return carry

    lax.fori_loop(0, rows // c, chunk, 0, unroll=2)


def _gla(proj, gd, up2, bias, nw, *, batch, seq, rows, cols):
    n = proj.shape[0]
    c = GLA_CHUNK
    g_np, m_np, levels = _gla_tables(c)
    g_tab = jnp.asarray(g_np, BF16)
    m_tab = jnp.asarray(m_np, F32)
    qk_w = GLA_HEADS * GLA_DK
    v_w = GLA_HEADS * GLA_DV
    spb = seq // rows
    row = lambda b, s: b * spb + s
    kern = functools.partial(_gla_kernel, c=c, nlev=len(levels))
    return pl.pallas_call(
        kern,
        grid=(batch, spb),
        in_specs=[
            pl.BlockSpec((rows, qk_w), lambda b, s: (row(b, s), cols[0])),
            pl.BlockSpec((rows, qk_w), lambda b, s: (row(b, s), cols[1])),
            pl.BlockSpec((rows, v_w), lambda b, s: (row(b, s), cols[2])),
            pl.BlockSpec((rows, v_w), lambda b, s: (row(b, s), cols[3])),
            pl.BlockSpec((rows, LANES), lambda b, s: (row(b, s), 0)),
            pl.BlockSpec(up2.shape, lambda b, s: (0, 0, 0)),
            pl.BlockSpec(bias.shape, lambda b, s: (0, 0)),
            pl.BlockSpec(nw.shape, lambda b, s: (0, 0)),
            pl.BlockSpec(g_tab.shape, lambda b, s: (0, 0)),
            pl.BlockSpec(m_tab.shape, lambda b, s: (0, 0, 0)),
        ],
        out_specs=pl.BlockSpec((rows, v_w), lambda b, s: (row(b, s), 0)),
        out_shape=jax.ShapeDtypeStruct((n, v_w), BF16),
        scratch_shapes=[pltpu.VMEM((GLA_HEADS, GLA_DV, GLA_DK), F32)],
        compiler_params=_params(("arbitrary", "arbitrary")),
        name="gla",
    )(proj, proj, proj, proj, gd, up2, bias, nw, g_tab, m_tab)


def _ret_log_gamma(h):
    return math.log(1.0 - 2.0 ** (-5.0 - h))


def _ret_kernel(pos_ref, th_ref, q_ref, k_ref, v_ref, rg_ref, nw_ref, o_ref, s_ref, dm_ref):
    h_n, dk, dv = RET_HEADS, RET_DK, RET_DV
    c = q_ref.shape[0]
    half = dk // 2

    @pl.when(pl.program_id(1) == 0)
    def _():
        s_ref[...] = jnp.zeros_like(s_ref)

    @pl.when((pl.program_id(0) == 0) & (pl.program_id(1) == 0))
    def _():
        rel = (lax.broadcasted_iota(jnp.int32, (c, c), 0) - lax.broadcasted_iota(jnp.int32, (c, c), 1)).astype(F32)
        for h in range(h_n):
            dm_ref[h] = jnp.where(rel >= 0, jnp.exp(_ret_log_gamma(h) * jnp.maximum(rel, 0.0)), 0.0)

    ang = pos_ref[...].astype(F32) * th_ref[...]
    cs = jnp.cos(ang)
    sn = jnp.sin(ang)
    idx = lax.broadcasted_iota(jnp.int32, (c, 1), 0).astype(F32)
    k_scale = dk ** -0.5

    def rot(ref, h):
        a = ref[:, h * dk:h * dk + half].astype(F32)
        b = ref[:, h * dk + half:(h + 1) * dk].astype(F32)
        return jnp.concatenate([a * cs - b * sn, b * cs + a * sn], axis=-1)

    for h in range(h_n):
        lg = _ret_log_gamma(h)
        vs = slice(h * dv, (h + 1) * dv)
        qr = rot(q_ref, h)
        kr = rot(k_ref, h) * k_scale
        vh = v_ref[:, vs]
        scores = _dot_nt(qr.astype(BF16), kr.astype(BF16)) * dm_ref[h]
        q_dec = jnp.exp(lg * (idx + 1.0))
        k_dec = jnp.exp(lg * (c - 1.0 - idx))
        s = s_ref[h]
        o = _dot(scores.astype(BF16), vh) + _dot((qr * q_dec).astype(BF16), s.astype(BF16))
        k_t = (kr * k_dec).T.astype(BF16)
        s_ref[h] = math.exp(lg * c) * s + _dot(k_t, vh)

        mu = jnp.mean(o, axis=-1, keepdims=True)
        oc = o - mu
        var = jnp.mean(oc * oc, axis=-1, keepdims=True)
        y = oc * lax.rsqrt(var + EPS) * nw_ref[:, vs]
        rg = rg_ref[:, vs].astype(F32)
        o_ref[:, vs] = (y * (rg * jax.nn.sigmoid(rg))).astype(BF16)


def _retention(proj, pos, theta, nw, *, batch, seq, cols):
    n = proj.shape[0]
    c = RET_CHUNK
    qk_w = RET_HEADS * RET_DK
    v_w = RET_HEADS * RET_DV
    spb = seq // c
    row = lambda b, s: b * spb + s
    return pl.pallas_call(
        _ret_kernel,
        grid=(batch, spb),
        in_specs=[
            pl.BlockSpec((c, 1), lambda b, s: (row(b, s), 0)),
            pl.BlockSpec(theta.shape, lambda b, s: (0, 0)),
            pl.BlockSpec((c, qk_w), lambda b, s: (row(b, s), cols[0])),
            pl.BlockSpec((c, qk_w), lambda b, s: (row(b, s), cols[1])),
            pl.BlockSpec((c, v_w), lambda b, s: (row(b, s), cols[2])),
            pl.BlockSpec((c, v_w), lambda b, s: (row(b, s), cols[3])),
            pl.BlockSpec(nw.shape, lambda b, s: (0, 0)),
        ],
        out_specs=pl.BlockSpec((c, v_w), lambda b, s: (row(b, s), 0)),
        out_shape=jax.ShapeDtypeStruct((n, v_w), BF16),
        scratch_shapes=[
            pltpu.VMEM((RET_HEADS, RET_DK, RET_DV), F32),
            pltpu.VMEM((RET_HEADS, c, c), F32),
        ],
        compiler_params=_params(("arbitrary", "arbitrary")),
        name="retention",
    )(pos, theta, proj, proj, proj, proj, nw)


def _mix_kernel(og_ref, or_ref, ma_ref, mb_ref, x_ref, wa_ref, wb_ref, wo_ref, nw_ref, wr_ref, br_ref,
                tri_ref, h_ref, t_ref, r_ref, rt_ref, cnt_ref, base_ref):
    tm = x_ref.shape[0]

    @pl.when(pl.program_id(0) == 0)
    def _():
        base_ref[...] = jnp.zeros_like(base_ref)

    ya = _dot(og_ref[...], wa_ref[...])
    yb = _dot(or_ref[...], wb_ref[...])
    merged = (jax.nn.sigmoid(ma_ref[...].astype(F32)) * ya + jax.nn.sigmoid(mb_ref[...].astype(F32)) * yb)
    h = x_ref[...] + _dot(merged.astype(BF16), wo_ref[...])
    h_ref[...] = h
    ms = jnp.mean(h * h, axis=-1, keepdims=True)
    t = h * lax.rsqrt(ms + EPS) * nw_ref[...]
    t_ref[...] = _to_row_tiles(t)

    lane = lax.broadcasted_iota(jnp.int32, (tm, LANES), 1)
    neg = jnp.float32(-1e30)
    big = jnp.int32(1 << 20)

    def first_max(v):
        m = jnp.max(v, axis=-1, keepdims=True)
        return m, jnp.min(jnp.where(v == m, lane, big), axis=-1, keepdims=True)

    t_hi, t_lo = _split2(t)
    lg = _dot(t_hi, wr_ref[0]) + _dot(t_hi, wr_ref[1]) + _dot(t_lo, wr_ref[0]) + br_ref[...]
    g_valid = (lane >= N_EXPERTS) & (lane < N_EXPERTS + N_GROUPS)
    g_m, g_lane = first_max(jnp.where(g_valid, lg, neg))
    g_w = 1.0 / jnp.sum(jnp.where(g_valid, jnp.exp(lg - g_m), 0.0), axis=-1, keepdims=True)
    g_idx = g_lane - N_EXPERTS
    e_valid = (lane < N_EXPERTS) & ((lane // EXPERTS_PER_GROUP) == g_idx)
    el = jnp.where(e_valid, lg, neg)
    v1, i1 = first_max(el)
    v2, i2 = first_max(jnp.where(lane == i1, neg, el))
    e21 = jnp.exp(v2 - v1)
    w1 = g_w / (1.0 + e21)
    w2 = g_w * e21 / (1.0 + e21)

    o1 = lane == i1
    o2 = lane == i2
    osum = jnp.where(o1 | o2, 1.0, 0.0)
    base = base_ref[0:1, :]
    before = _dot(tri_ref[...], osum.astype(BF16)) + base
    r1 = jnp.sum(jnp.where(o1, before, 0.0), axis=-1, keepdims=True)
    r2 = jnp.sum(jnp.where(o2, before, 0.0), axis=-1, keepdims=True)
    base = base + jnp.sum(osum, axis=0, keepdims=True)
    base_ref[...] = jnp.broadcast_to(base, base_ref.shape)
    cnt_ref[...] = jnp.broadcast_to(base, cnt_ref.shape)

    cols = (i1.astype(F32), i2.astype(F32), w1, w2, r1, r2)
    wide = jnp.zeros((tm, LANES), F32)
    for ci, col in enumerate(cols):
        wide = jnp.where(lane == ci, col, wide)
    r_ref[...] = wide[:, :r_ref.shape[1]]
    rt_ref[...] = wide.T[:rt_ref.shape[0], :]


def _mix(o_gla, o_ret, proj, x2, wa, wb, wo, nw, wr2, br, *, tm, cols):
    n, d = x2.shape
    tri = jnp.asarray(np.tril(np.ones((tm, tm), np.float32), -1), BF16)
    const = lambda shape: pl.BlockSpec(shape, lambda i: (0,) * len(shape))
    return pl.pallas_call(
        _mix_kernel,
        grid=(n // tm,),
        in_specs=[
            pl.BlockSpec((tm, o_gla.shape[1]), lambda i: (i, 0)),
            pl.BlockSpec((tm, o_ret.shape[1]), lambda i: (i, 0)),
            pl.BlockSpec((tm, d), lambda i: (i, cols[0])),
            pl.BlockSpec((tm, d), lambda i: (i, cols[1])),
            pl.BlockSpec((tm, d), lambda i: (i, 0)),
            const(wa.shape), const(wb.shape), const(wo.shape), const(nw.shape),
            const(wr2.shape), const(br.shape), const(tri.shape),
        ],
        out_specs=[
            pl.BlockSpec((tm, d), lambda i: (i, 0)),
            pl.BlockSpec((tm, d // LANES, LANES), lambda i: (i, 0, 0)),
            pl.BlockSpec((tm, 8), lambda i: (i, 0)),
            pl.BlockSpec((8, tm), lambda i: (0, i)),
            pl.BlockSpec((8, LANES), lambda i: (0, 0)),
        ],
        out_shape=[
            jax.ShapeDtypeStruct((n, d), F32),
            jax.ShapeDtypeStruct((n, d // LANES, LANES), BF16),
            jax.ShapeDtypeStruct((n, 8), F32),
            jax.ShapeDtypeStruct((8, n), F32),
            jax.ShapeDtypeStruct((8, LANES), F32),
        ],
        scratch_shapes=[pltpu.VMEM((8, LANES), F32)],
        compiler_params=_params(("arbitrary",)),
        name="mix_router",
    )(o_gla, o_ret, proj, proj, x2, wa, wb, wo, nw, wr2, br, tri)


def _to_row_tiles(x):
    return x.astype(BF16).reshape(x.shape[0], x.shape[1] // LANES, LANES)


def _from_row_tiles(x3):
    return x3.reshape(x3.shape[0], x3.shape[1] * x3.shape[2])


def _dispatch_kernel(tail_ref, has_ref, d0_ref, d1_ref, t_ref, xs_hbm, zero_ref, sem, zsem):
    step = pl.program_id(0)
    tok_n = t_ref.shape[0]
    blk = zero_ref.shape[0]

    def tail_copy(e):
        return pltpu.make_async_copy(zero_ref, xs_hbm.at[pl.ds(tail_ref[e], blk)], zsem)

    @pl.when(step == 0)
    def _():
        zero_ref[...] = jnp.zeros_like(zero_ref)
        for e in range(tail_ref.shape[0]):
            @pl.when(has_ref[e] > 0)
            def _(e=e):
                tail_copy(e).start()
        for e in range(tail_ref.shape[0]):
            @pl.when(has_ref[e] > 0)
            def _(e=e):
                tail_copy(e).wait()

    def row_copy(a, dst):
        return pltpu.make_async_copy(t_ref.at[a], xs_hbm.at[dst], sem)

    def body(a, carry):
        row_copy(a, d0_ref[a]).start(priority=0)
        row_copy(a, d1_ref[a]).start(priority=1)
        return carry

    lax.fori_loop(0, tok_n, body, 0, unroll=8)

    for _ in range(2):
        pltpu.make_async_copy(t_ref, xs_hbm.at[pl.ds(0, tok_n)], sem).wait()


def _dispatch(t3, dest0, dest1, tail, has, *, n_slots, chunk):
    n, s, l = t3.shape
    return pl.pallas_call(
        _dispatch_kernel,
        grid_spec=pltpu.PrefetchScalarGridSpec(
            num_scalar_prefetch=2,
            grid=(n // chunk,),
            in_specs=[
                pl.BlockSpec((chunk,), lambda i, *_: (i,), memory_space=pltpu.SMEM),
                pl.BlockSpec((chunk,), lambda i, *_: (i,), memory_space=pltpu.SMEM),
                pl.BlockSpec((chunk, s, l), lambda i, *_: (i, 0, 0)),
            ],
            out_specs=pl.BlockSpec(memory_space=pl.ANY),
            scratch_shapes=[
                pltpu.VMEM((EXPERT_ROWS, s, l), BF16),
                pltpu.SemaphoreType.DMA,
                pltpu.SemaphoreType.DMA,
            ],
        ),
        out_shape=jax.ShapeDtypeStruct((n_slots, s, l), BF16),
        compiler_params=pltpu.CompilerParams(dimension_semantics=("arbitrary",), has_side_effects=True,
                                             vmem_limit_bytes=VMEM_LIMIT, disable_bounds_checks=True),
        name="dispatch",
    )(tail, has, dest0, dest1, t3)


def _expert_kernel(be_ref, nx_ref, nu_ref, xs_ref, wg_hbm, wu_hbm, wd_hbm, ys_ref,
                   wg_f, wu_f, wd_f, wg_s, wu_s, wd_s, sem):
    b = pl.program_id(0)
    e = be_ref[b]

    def fetch(expert):
        return (pltpu.make_async_copy(wg_hbm.at[expert], wg_f, sem.at[0]),
                pltpu.make_async_copy(wu_hbm.at[expert], wu_f, sem.at[1]),
                pltpu.make_async_copy(wd_hbm.at[expert], wd_f, sem.at[2]))

    @pl.when(b == 0)
    def _():
        for cp in fetch(e):
            cp.start()

    @pl.when((b == 0) | (e != be_ref[jnp.maximum(b - 1, 0)]))
    def _():
        for cp in fetch(e):
            cp.wait()
        wg_s[...] = wg_f[...].astype(BF16)
        wu_s[...] = wu_f[...].astype(BF16)
        wd_s[...] = wd_f[...].astype(BF16)

        @pl.when(nx_ref[b] >= 0)
        def _():
            for cp in fetch(nx_ref[b]):
                cp.start()

    @pl.when(b < nu_ref[0])
    def _():
        x = _from_row_tiles(xs_ref[...])
        g = _dot(x, wg_s[...])
        u = _dot(x, wu_s[...])
        hid = (g * jax.nn.sigmoid(g) * u).astype(BF16)
        ys_ref[...] = _to_row_tiles(_dot(hid, wd_s[...]))

    @pl.when(b >= nu_ref[0])
    def _():
        ys_ref[...] = jnp.zeros_like(ys_ref)


def _experts(xs3, block_expert, next_expert, n_used, wg, wu, wd):
    n_slots, s, l = xs3.shape
    d = s * l
    blk = EXPERT_ROWS
    hid = wg.shape[2]
    return pl.pallas_call(
        _expert_kernel,
        grid_spec=pltpu.PrefetchScalarGridSpec(
            num_scalar_prefetch=3,
            grid=(n_slots // blk,),
            in_specs=[
                pl.BlockSpec((blk, s, l), lambda b, be, nx, nu: (jnp.minimum(b, nu[0] - 1), 0, 0)),
                pl.BlockSpec(memory_space=pl.ANY),
                pl.BlockSpec(memory_space=pl.ANY),
                pl.BlockSpec(memory_space=pl.ANY),
            ],
            out_specs=pl.BlockSpec((blk, s, l), lambda b, be, nx, nu: (b, 0, 0)),
            scratch_shapes=[pltpu.VMEM((d, hid), F32), pltpu.VMEM((d, hid), F32), pltpu.VMEM((hid, d), F32),
                            pltpu.VMEM((d, hid), BF16), pltpu.VMEM((d, hid), BF16), pltpu.VMEM((hid, d), BF16),
                            pltpu.SemaphoreType.DMA((3,))],
        ),
        out_shape=jax.ShapeDtypeStruct((n_slots, s, l), BF16),
        compiler_params=_params(("arbitrary",)),
        name="experts",
    )(block_expert, next_expert, n_used, xs3, wg, wu, wd)


def _combine_kernel(c0_ref, c1_ref, n0_ref, n1_ref, h_ref, r_ref, nw_ref, ys_hbm, o_ref, buf, sem):
    i = pl.program_id(0)
    n_steps = pl.num_programs(0)
    tm = h_ref.shape[0]

    def row_copy(dst_row, src_row, slot):
        return pltpu.make_async_copy(ys_hbm.at[src_row], buf.at[slot, dst_row], sem.at[slot])

    def issue(d0, d1, slot):
        def body(a, carry):
            row_copy(a, d0[a], slot).start(priority=0)
            row_copy(tm + a, d1[a], slot).start(priority=1)
            return carry
        lax.fori_loop(0, tm, body, 0, unroll=8)

    @pl.when(i == 0)
    def _():
        issue(c0_ref, c1_ref, 0)

    @pl.when(i + 1 < n_steps)
    def _():
        issue(n0_ref, n1_ref, (i + 1) % 2)

    slot = i % 2
    pltpu.make_async_copy(ys_hbm.at[pl.ds(0, 2 * tm)], buf.at[slot], sem.at[slot]).wait()

    r = r_ref[...]
    y = (_from_row_tiles(buf[slot, 0:tm]).astype(F32) * r[:, 2:3]
         + _from_row_tiles(buf[slot, tm:2 * tm]).astype(F32) * r[:, 3:4])
    h = h_ref[...] + y
    ms = jnp.mean(h * h, axis=-1, keepdims=True)
    o_ref[...] = h * lax.rsqrt(ms + EPS) * nw_ref[...]


def _combine(dest0, dest1, h1, r, nw, ys, *, tm):
    n, d = h1.shape
    n_steps = n // tm
    cur = pl.BlockSpec((tm,), lambda i: (i,), memory_space=pltpu.SMEM)
    nxt = pl.BlockSpec((tm,), lambda i: (jnp.minimum(i + 1, n_steps - 1),), memory_space=pltpu.SMEM)
    return pl.pallas_call(
        _combine_kernel,
        grid=(n_steps,),
        in_specs=[
            cur, cur, nxt, nxt,
            pl.BlockSpec((tm, d), lambda i: (i, 0)),
            pl.BlockSpec((tm, r.shape[1]), lambda i: (i, 0)),
            pl.BlockSpec((1, d), lambda i: (0, 0)),
            pl.BlockSpec(memory_space=pl.ANY),
        ],
        out_specs=pl.BlockSpec((tm, d), lambda i: (i, 0)),
        out_shape=jax.ShapeDtypeStruct((n, d), F32),
        scratch_shapes=[pltpu.VMEM((2, 2 * tm) + ys.shape[1:], BF16), pltpu.SemaphoreType.DMA((2,))],
        compiler_params=pltpu.CompilerParams(dimension_semantics=("arbitrary",), vmem_limit_bytes=VMEM_LIMIT,
                                             disable_bounds_checks=True),
        name="combine",
    )(dest0, dest1, dest0, dest1, h1, r, nw, ys)


def _layer(h, positions, norm_mix_w, w_in, gk_up, gk_bias, gla_norm_w, w_br_gla, ret_norm_w, w_br_ret, w_out,
           norm_ffn_w, rg_w, rg_b, re_w, re_b, wg, wu, wd, norm_final_w, *, tm_in, tn_in, gla_rows, tm_mix,
           tm_comb, disp_chunk):
    batch, seq, d = h.shape
    n = batch * seq
    x2 = h.reshape(n, d)
    gqk, gv = GLA_HEADS * GLA_DK, GLA_HEADS * GLA_DV
    rqk, rv = RET_HEADS * RET_DK, RET_HEADS * RET_DV

    sizes = (gqk, gqk, gv, gv, GLA_GATE_RANK, rqk, rqk, rv, rv, d, d)
    offs = np.concatenate([[0], np.cumsum(sizes)])
    seg = lambda i: w_in[:, offs[i]:offs[i + 1]]
    perm = np.concatenate([np.concatenate([np.arange(0, RET_DK, 2), np.arange(1, RET_DK, 2)]) + hh * RET_DK
                           for hh in range(RET_HEADS)])
    w_main = jnp.concatenate([seg(7), seg(8), seg(2), seg(3), seg(5)[:, perm], seg(6)[:, perm], seg(9), seg(10),
                              seg(0), seg(1)], axis=1).astype(BF16)
    assert rv % gv == 0 and gv == rqk == d and rv == 2 * d and gqk * 2 == d
    ret_cols = (4 + 2, 4 + 3, 0, 1)
    gla_cols = (2 * (4 + 6), 2 * (4 + 6) + 1, 4, 5)
    mix_cols = (4 + 4, 4 + 5)
    w_gd = jnp.pad(seg(4), ((0, 0), (0, LANES - GLA_GATE_RANK))).astype(BF16)

    proj, gd = _inproj(x2, norm_mix_w.reshape(1, d), w_main, w_gd, tm=tm_in, tn=tn_in)

    up = jnp.pad(gk_up, ((0, LANES - GLA_GATE_RANK), (0, 0)))
    up_hi = up.astype(BF16)
    up2 = jnp.stack([up_hi, (up - up_hi.astype(F32)).astype(BF16)])
    o_gla = _gla(proj, gd, up2, gk_bias.reshape(1, gqk), gla_norm_w.reshape(1, GLA_DV),
                 batch=batch, seq=seq, rows=gla_rows, cols=gla_cols)

    theta = (1.0 / (ROPE_BASE ** jnp.linspace(0.0, 1.0, RET_DK // 2, dtype=F32))).reshape(1, RET_DK // 2)
    o_ret = _retention(proj, positions.reshape(n, 1), theta, ret_norm_w.reshape(1, rv),
                       batch=batch, seq=seq, cols=ret_cols)

    wr = jnp.concatenate([re_w.transpose(1, 0, 2).reshape(d, N_EXPERTS), rg_w], axis=1)
    wr = jnp.pad(wr, ((0, 0), (0, LANES - N_EXPERTS - N_GROUPS)))
    wr_hi = wr.astype(BF16)
    wr2 = jnp.stack([wr_hi, (wr - wr_hi.astype(F32)).astype(BF16)])
    br = jnp.pad(jnp.concatenate([re_b.reshape(-1), rg_b]), (0, LANES - N_EXPERTS - N_GROUPS)).reshape(1, LANES)
    h1, t, r, rt, cnt = _mix(o_gla, o_ret, proj, x2, w_br_gla.astype(BF16), w_br_ret.astype(BF16),
                             w_out.astype(BF16), norm_ffn_w.reshape(1, d), wr2, br, tm=tm_mix, cols=mix_cols)

    blk = EXPERT_ROWS
    n_assign = 2 * n
    n_slots = -(-(n_assign + N_EXPERTS * (blk - 1)) // blk) * blk
    counts = cnt[0, :N_EXPERTS].astype(jnp.int32)
    padded = ((counts + blk - 1) // blk) * blk
    pad_end = jnp.cumsum(padded)
    pad_start = pad_end - padded
    dest0 = pad_start[rt[0].astype(jnp.int32)] + rt[4].astype(jnp.int32)
    dest1 = pad_start[rt[1].astype(jnp.int32)] + rt[5].astype(jnp.int32)
    block_start = jnp.arange(n_slots // blk, dtype=jnp.int32) * blk
    block_expert = jnp.minimum(jnp.sum((pad_end[None, :] <= block_start[:, None]).astype(jnp.int32), axis=1),
                               N_EXPERTS - 1)
    n_blocks = n_slots // blk
    blk_id = jnp.arange(n_blocks, dtype=jnp.int32)
    later_other = (block_expert[None, :] != block_expert[:, None]) & (blk_id[None, :] > blk_id[:, None])
    first_other = jnp.min(jnp.where(later_other, blk_id[None, :], n_blocks), axis=1)
    next_expert = jnp.where(first_other < n_blocks, block_expert[jnp.minimum(first_other, n_blocks - 1)], -1)
    n_used = (pad_end[-1:] // blk).astype(jnp.int32)
    spare = pad_end[-1] + jnp.arange(N_EXPERTS, dtype=jnp.int32) * blk
    tail = jnp.concatenate([jnp.maximum(pad_end - blk, 0), jnp.minimum(spare, n_slots - blk)]).astype(jnp.int32)
    has = jnp.concatenate([counts, (spare < n_slots).astype(jnp.int32)])

    xs = _dispatch(t, dest0, dest1, tail, has, n_slots=n_slots, chunk=disp_chunk)
    ys = _experts(xs, block_expert, next_expert.astype(jnp.int32), n_used, wg, wu, wd)
    out = _combine(dest0, dest1, h1, r, norm_final_w.reshape(1, d), ys, tm=tm_comb)
    return out.reshape(batch, seq, d)


def kernel(x, positions, norm_mix_w, w_in, gla_gk_up, gla_gk_bias, gla_norm_w, w_branch_gla, ret_norm_w,
           w_branch_ret, w_out, norm_ffn_w, router_group_w, router_group_b, router_expert_w, router_expert_b,
           expert_w_gate, expert_w_up, expert_w_down, norm_final_w):
    assert norm_mix_w.shape[0] == 1, "single-layer block"
    return _layer(x, positions, norm_mix_w[0], w_in[0], gla_gk_up[0], gla_gk_bias[0], gla_norm_w[0], w_branch_gla[0],
                  ret_norm_w[0], w_branch_ret[0], w_out[0], norm_ffn_w[0], router_group_w[0], router_group_b[0],
                  router_expert_w[0], router_expert_b[0], expert_w_gate[0], expert_w_up[0], expert_w_down[0],
                  norm_final_w, tm_in=2048, tn_in=1024, gla_rows=256, tm_mix=512, tm_comb=256,
                  disp_chunk=1024)
```

```python
import functools
import math

import jax
import jax.numpy as jnp
import numpy as np
from jax import lax
from jax.experimental import pallas as pl
from jax.experimental.pallas import tpu as pltpu

F32 = jnp.float32
BF16 = jnp.bfloat16

EPS = 1e-6
GLA_HEADS = 4
GLA_DK = 128
GLA_DV = 256
GLA_GATE_RANK = 16
GLA_GATE_TEMP = 16.0
RET_HEADS = 4
RET_DK = 256
RET_DV = 512
ROPE_BASE = 10000.0
N_GROUPS = 4
EXPERTS_PER_GROUP = 8
N_EXPERTS = N_GROUPS * EXPERTS_PER_GROUP
EXPERT_HIDDEN = 512

LANES = 128
GLA_CHUNK = 128
GLA_DIAG = 8
RET_CHUNK = 256
EXPERT_ROWS = 512
VMEM_LIMIT = 48 * 1024 * 1024


def _dot(a, b):
    return jnp.dot(a, b, preferred_element_type=F32)


def _dot_nt(a, b):
    return lax.dot_general(a, b, (((1,), (1,)), ((), ())), preferred_element_type=F32)


def _split2(a):
    hi = a.astype(BF16)
    lo = (a - hi.astype(F32)).astype(BF16)
    return hi, lo


def _params(sem, vmem=VMEM_LIMIT):
    return pltpu.CompilerParams(dimension_semantics=sem, vmem_limit_bytes=vmem)


def _inproj_kernel(x_ref, nw_ref, w_ref, wgd_ref, proj_ref, gd_ref, u_scr):
    @pl.when(pl.program_id(1) == 0)
    def _():
        x = x_ref[...]
        ms = jnp.mean(x * x, axis=-1, keepdims=True)
        u = (x * lax.rsqrt(ms + EPS) * nw_ref[...]).astype(BF16)
        u_scr[...] = u
        gd_ref[...] = _dot(u, wgd_ref[...])

    proj_ref[...] = _dot(u_scr[...], w_ref[...]).astype(BF16)


def _inproj(x2, nw, w_main, w_gd, *, tm, tn):
    n, d = x2.shape
    p = w_main.shape[1]
    return pl.pallas_call(
        _inproj_kernel,
        grid=(n // tm, p // tn),
        in_specs=[
            pl.BlockSpec((tm, d), lambda i, j: (i, 0)),
            pl.BlockSpec((1, d), lambda i, j: (0, 0)),
            pl.BlockSpec((d, tn), lambda i, j: (0, j)),
            pl.BlockSpec((d, LANES), lambda i, j: (0, 0)),
        ],
        out_specs=[
            pl.BlockSpec((tm, tn), lambda i, j: (i, j)),
            pl.BlockSpec((tm, LANES), lambda i, j: (i, 0)),
        ],
        out_shape=[
            jax.ShapeDtypeStruct((n, p), BF16),
            jax.ShapeDtypeStruct((n, LANES), F32),
        ],
        scratch_shapes=[pltpu.VMEM((tm, d), BF16)],
        compiler_params=_params(("arbitrary", "arbitrary")),
        name="inproj",
    )(x2, nw, w_main, w_gd)


def _gla_tables(c):
    levels = []
    s = c // 2
    while s >= GLA_DIAG:
        levels.append(s)
        s //= 2
    i = np.arange(c)[:, None]
    t = np.arange(c)[None, :]
    mats = [t <= i, t > i]
    masks = []
    for s in levels:
        bs = (i // s) * s
        mats.append((t > bs) & (t <= i))
        mats.append((t > i) & (t <= np.minimum(bs + s, c - 1)))
        masks.append(((i // (2 * s)) == (t // (2 * s))) & (((i // s) % 2) == 1) & (((t // s) % 2) == 0))
    for d in range(1, GLA_DIAG):
        mats.append((t > i - d) & (t <= i))
    for d in range(GLA_DIAG):
        masks.append((t == i - d) & ((i % GLA_DIAG) >= d))
    g = np.concatenate(mats, 0).astype(np.float32)
    m = np.stack(masks, 0).astype(np.float32)
    return g, m, tuple(levels)


def _gla_kernel(q_ref, k_ref, v_ref, gg_ref, gd_ref, up_ref, bias_ref, nw_ref, g_ref, m_ref,
                o_ref, st_ref, *, c, nlev):
    h_n, dk, dv = GLA_HEADS, GLA_DK, GLA_DV
    rows = q_ref.shape[0]

    @pl.when(pl.program_id(1) == 0)
    def _():
        st_ref[...] = jnp.zeros_like(st_ref)

    g_tab = g_ref[...]
    up_hi = up_ref[0]
    up_lo = up_ref[1]
    q_scale = dk ** -0.5
    band0 = 2 + 2 * nlev

    def chunk(ci, carry):
        r0 = pl.multiple_of(ci * c, c)
        rs = pl.ds(r0, c)
        gd_hi, gd_lo = _split2(gd_ref[rs, :])
        xg = _dot(gd_hi, up_hi) + _dot(gd_hi, up_lo) + _dot(gd_lo, up_hi) + bias_ref[...]
        la2 = (jnp.minimum(xg, 0.0) - jnp.log1p(jnp.exp(-jnp.abs(xg)))) * (math.log2(math.e) / GLA_GATE_TEMP)
        ex = jnp.exp2(_dot(g_tab, la2.astype(BF16)))

        for h in range(h_n):
            ks = slice(h * dk, (h + 1) * dk)
            vs = slice(h * dv, (h + 1) * dv)

            def tab(m, ks=ks):
                return ex[m * c:(m + 1) * c, ks]

            qh = q_ref[rs, ks].astype(F32) * q_scale
            kh = k_ref[rs, ks].astype(F32)
            vh = v_ref[rs, vs]
            st = st_ref[h]
            eb = tab(0)
            o = _dot_nt((qh * eb).astype(BF16), st.astype(BF16))
            parts = [m_ref[nlev] * jnp.sum(qh * kh, axis=-1, keepdims=True)]
            for li in range(nlev):
                qs = (qh * tab(2 + 2 * li)).astype(BF16)
                kk = (kh * tab(3 + 2 * li)).astype(BF16)
                parts.append(m_ref[li] * _dot_nt(qs, kk))
            kh_tiles = kh.reshape(c // GLA_DIAG, GLA_DIAG, dk)
            for d in range(1, GLA_DIAG):
                kr = pltpu.roll(kh_tiles, d, 1).reshape(c, dk)
                dd = jnp.sum(qh * kr * tab(band0 + d - 1), axis=-1, keepdims=True)
                parts.append(m_ref[nlev + d] * dd)
            while len(parts) > 1:
                parts = [functools.reduce(jnp.add, parts[p:p + 2]) for p in range(0, len(parts), 2)]
            o = o + _dot(parts[0].astype(BF16), vh)
            e_last = eb[c - 1:c, :]
            k_st = (kh * tab(1)).astype(BF16)
            v_t = vh.astype(F32).T.astype(BF16)
            st_ref[h] = st * e_last + _dot(v_t, k_st)

            ms = jnp.mean(o * o, axis=-1, keepdims=True)
            y = o * lax.rsqrt(ms + EPS) * nw_ref[...]
            gg = gg_ref[rs, vs].astype(F32)
            o_ref[rs, vs] = (y * (gg * jax.nn.sigmoid(gg))).astype(BF16)
        return carry

    lax.fori_loop(0, rows // c, chunk, 0, unroll=2)


def _gla(proj, gd, up2, bias, nw, *, batch, seq, rows, cols):
    n = proj.shape[0]
    c = GLA_CHUNK
    g_np, m_np, levels = _gla_tables(c)
    g_tab = jnp.asarray(g_np, BF16)
    m_tab = jnp.asarray(m_np, F32)
    qk_w = GLA_HEADS * GLA_DK
    v_w = GLA_HEADS * GLA_DV
    spb = seq // rows
    row = lambda b, s: b * spb + s
    kern = functools.partial(_gla_kernel, c=c, nlev=len(levels))
    return pl.pallas_call(
        kern,
        grid=(batch, spb),
        in_specs=[
            pl.BlockSpec((rows, qk_w), lambda b, s: (row(b, s), cols[0])),
            pl.BlockSpec((rows, qk_w), lambda b, s: (row(b, s), cols[1])),
            pl.BlockSpec((rows, v_w), lambda b, s: (row(b, s), cols[2])),
            pl.BlockSpec((rows, v_w), lambda b, s: (row(b, s), cols[3])),
            pl.BlockSpec((rows, LANES), lambda b, s: (row(b, s), 0)),
            pl.BlockSpec(up2.shape, lambda b, s: (0, 0, 0)),
            pl.BlockSpec(bias.shape, lambda b, s: (0, 0)),
            pl.BlockSpec(nw.shape, lambda b, s: (0, 0)),
            pl.BlockSpec(g_tab.shape, lambda b, s: (0, 0)),
            pl.BlockSpec(m_tab.shape, lambda b, s: (0, 0, 0)),
        ],
        out_specs=pl.BlockSpec((rows, v_w), lambda b, s: (row(b, s), 0)),
        out_shape=jax.ShapeDtypeStruct((n, v_w), BF16),
        scratch_shapes=[pltpu.VMEM((GLA_HEADS, GLA_DV, GLA_DK), F32)],
        compiler_params=_params(("arbitrary", "arbitrary")),
        name="gla",
    )(proj, proj, proj, proj, gd, up2, bias, nw, g_tab, m_tab)


def _ret_log_gamma(h):
    return math.log(1.0 - 2.0 ** (-5.0 - h))


def _ret_kernel(pos_ref, th_ref, q_ref, k_ref, v_ref, rg_ref, nw_ref, o_ref, s_ref, dm_ref):
    h_n, dk, dv = RET_HEADS, RET_DK, RET_DV
    c = q_ref.shape[0]
    half = dk // 2

    @pl.when(pl.program_id(1) == 0)
    def _():
        s_ref[...] = jnp.zeros_like(s_ref)

    @pl.when((pl.program_id(0) == 0) & (pl.program_id(1) == 0))
    def _():
        rel = (lax.broadcasted_iota(jnp.int32, (c, c), 0) - lax.broadcasted_iota(jnp.int32, (c, c), 1)).astype(F32)
        for h in range(h_n):
            dm_ref[h] = jnp.where(rel >= 0, jnp.exp(_ret_log_gamma(h) * jnp.maximum(rel, 0.0)), 0.0)

    ang = pos_ref[...].astype(F32) * th_ref[...]
    cs = jnp.cos(ang)
    sn = jnp.sin(ang)
    idx = lax.broadcasted_iota(jnp.int32, (c, 1), 0).astype(F32)
    k_scale = dk ** -0.5

    def rot(ref, h):
        a = ref[:, h * dk:h * dk + half].astype(F32)
        b = ref[:, h * dk + half:(h + 1) * dk].astype(F32)
        return jnp.concatenate([a * cs - b * sn, b * cs + a * sn], axis=-1)

    for h in range(h_n):
        lg = _ret_log_gamma(h)
        vs = slice(h * dv, (h + 1) * dv)
        qr = rot(q_ref, h)
        kr = rot(k_ref, h) * k_scale
        vh = v_ref[:, vs]
        scores = _dot_nt(qr.astype(BF16), kr.astype(BF16)) * dm_ref[h]
        q_dec = jnp.exp(lg * (idx + 1.0))
        k_dec = jnp.exp(lg * (c - 1.0 - idx))
        s = s_ref[h]
        o = _dot(scores.astype(BF16), vh) + _dot((qr * q_dec).astype(BF16), s.astype(BF16))
        k_t = (kr * k_dec).T.astype(BF16)
        s_ref[h] = math.exp(lg * c) * s + _dot(k_t, vh)

        mu = jnp.mean(o, axis=-1, keepdims=True)
        oc = o - mu
        var = jnp.mean(oc * oc, axis=-1, keepdims=True)
        y = oc * lax.rsqrt(var + EPS) * nw_ref[:, vs]
        rg = rg_ref[:, vs].astype(F32)
        o_ref[:, vs] = (y * (rg * jax.nn.sigmoid(rg))).astype(BF16)


def _retention(proj, pos, theta, nw, *, batch, seq, cols):
    n = proj.shape[0]
    c = RET_CHUNK
    qk_w = RET_HEADS * RET_DK
    v_w = RET_HEADS * RET_DV
    spb = seq // c
    row = lambda b, s: b * spb + s
    return pl.pallas_call(
        _ret_kernel,
        grid=(batch, spb),
        in_specs=[
            pl.BlockSpec((c, 1), lambda b, s: (row(b, s), 0)),
            pl.BlockSpec(theta.shape, lambda b, s: (0, 0)),
            pl.BlockSpec((c, qk_w), lambda b, s: (row(b, s), cols[0])),
            pl.BlockSpec((c, qk_w), lambda b, s: (row(b, s), cols[1])),
            pl.BlockSpec((c, v_w), lambda b, s: (row(b, s), cols[2])),
            pl.BlockSpec((c, v_w), lambda b, s: (row(b, s), cols[3])),
            pl.BlockSpec(nw.shape, lambda b, s: (0, 0)),
        ],
        out_specs=pl.BlockSpec((c, v_w), lambda b, s: (row(b, s), 0)),
        out_shape=jax.ShapeDtypeStruct((n, v_w), BF16),
        scratch_shapes=[
            pltpu.VMEM((RET_HEADS, RET_DK, RET_DV), F32),
            pltpu.VMEM((RET_HEADS, c, c), F32),
        ],
        compiler_params=_params(("arbitrary", "arbitrary")),
        name="retention",
    )(pos, theta, proj, proj, proj, proj, nw)


def _mix_kernel(og_ref, or_ref, ma_ref, mb_ref, x_ref, wa_ref, wb_ref, wo_ref, nw_ref, wr_ref, br_ref,
                tri_ref, h_ref, t_ref, r_ref, rt_ref, cnt_ref, base_ref):
    tm = x_ref.shape[0]

    @pl.when(pl.program_id(0) == 0)
    def _():
        base_ref[...] = jnp.zeros_like(base_ref)

    ya = _dot(og_ref[...], wa_ref[...])
    yb = _dot(or_ref[...], wb_ref[...])
    merged = (jax.nn.sigmoid(ma_ref[...].astype(F32)) * ya + jax.nn.sigmoid(mb_ref[...].astype(F32)) * yb)
    h = x_ref[...] + _dot(merged.astype(BF16), wo_ref[...])
    h_ref[...] = h
    ms = jnp.mean(h * h, axis=-1, keepdims=True)
    t = h * lax.rsqrt(ms + EPS) * nw_ref[...]
    t_ref[...] = _to_row_tiles(t)

    lane = lax.broadcasted_iota(jnp.int32, (tm, LANES), 1)
    neg = jnp.float32(-1e30)
    big = jnp.int32(1 << 20)

    def first_max(v):
        m = jnp.max(v, axis=-1, keepdims=True)
        return m, jnp.min(jnp.where(v == m, lane, big), axis=-1, keepdims=True)

    t_hi, t_lo = _split2(t)
    lg = _dot(t_hi, wr_ref[0]) + _dot(t_hi, wr_ref[1]) + _dot(t_lo, wr_ref[0]) + br_ref[...]
    g_valid = (lane >= N_EXPERTS) & (lane < N_EXPERTS + N_GROUPS)
    g_m, g_lane = first_max(jnp.where(g_valid, lg, neg))
    g_w = 1.0 / jnp.sum(jnp.where(g_valid, jnp.exp(lg - g_m), 0.0), axis=-1, keepdims=True)
    g_idx = g_lane - N_EXPERTS
    e_valid = (lane < N_EXPERTS) & ((lane // EXPERTS_PER_GROUP) == g_idx)
    el = jnp.where(e_valid, lg, neg)
    v1, i1 = first_max(el)
    v2, i2 = first_max(jnp.where(lane == i1, neg, el))
    e21 = jnp.exp(v2 - v1)
    w1 = g_w / (1.0 + e21)
    w2 = g_w * e21 / (1.0 + e21)

    o1 = lane == i1
    o2 = lane == i2
    osum = jnp.where(o1 | o2, 1.0, 0.0)
    base = base_ref[0:1, :]
    before = _dot(tri_ref[...], osum.astype(BF16)) + base
    r1 = jnp.sum(jnp.where(o1, before, 0.0), axis=-1, keepdims=True)
    r2 = jnp.sum(jnp.where(o2, before, 0.0), axis=-1, keepdims=True)
    base = base + jnp.sum(osum, axis=0, keepdims=True)
    base_ref[...] = jnp.broadcast_to(base, base_ref.shape)
    cnt_ref[...] = jnp.broadcast_to(base, cnt_ref.shape)

    cols = (i1.astype(F32), i2.astype(F32), w1, w2, r1, r2)
    wide = jnp.zeros((tm, LANES), F32)
    for ci, col in enumerate(cols):
        wide = jnp.where(lane == ci, col, wide)
    r_ref[...] = wide[:, :r_ref.shape[1]]
    rt_ref[...] = wide.T[:rt_ref.shape[0], :]


def _mix(o_gla, o_ret, proj, x2, wa, wb, wo, nw, wr2, br, *, tm, cols):
    n, d = x2.shape
    tri = jnp.asarray(np.tril(np.ones((tm, tm), np.float32), -1), BF16)
    const = lambda shape: pl.BlockSpec(shape, lambda i: (0,) * len(shape))
    return pl.pallas_call(
        _mix_kernel,
        grid=(n // tm,),
        in_specs=[
            pl.BlockSpec((tm, o_gla.shape[1]), lambda i: (i, 0)),
            pl.BlockSpec((tm, o_ret.shape[1]), lambda i: (i, 0)),
            pl.BlockSpec((tm, d), lambda i: (i, cols[0])),
            pl.BlockSpec((tm, d), lambda i: (i, cols[1])),
            pl.BlockSpec((tm, d), lambda i: (i, 0)),
            const(wa.shape), const(wb.shape), const(wo.shape), const(nw.shape),
            const(wr2.shape), const(br.shape), const(tri.shape),
        ],
        out_specs=[
            pl.BlockSpec((tm, d), lambda i: (i, 0)),
            pl.BlockSpec((tm, d // LANES, LANES), lambda i: (i, 0, 0)),
            pl.BlockSpec((tm, 8), lambda i: (i, 0)),
            pl.BlockSpec((8, tm), lambda i: (0, i)),
            pl.BlockSpec((8, LANES), lambda i: (0, 0)),
        ],
        out_shape=[
            jax.ShapeDtypeStruct((n, d), F32),
            jax.ShapeDtypeStruct((n, d // LANES, LANES), BF16),
            jax.ShapeDtypeStruct((n, 8), F32),
            jax.ShapeDtypeStruct((8, n), F32),
            jax.ShapeDtypeStruct((8, LANES), F32),
        ],
        scratch_shapes=[pltpu.VMEM((8, LANES), F32)],
        compiler_params=_params(("arbitrary",)),
        name="mix_router",
    )(o_gla, o_ret, proj, proj, x2, wa, wb, wo, nw, wr2, br, tri)


def _to_row_tiles(x):
    return x.astype(BF16).reshape(x.shape[0], x.shape[1] // LANES, LANES)


def _from_row_tiles(x3):
    return x3.reshape(x3.shape[0], x3.shape[1] * x3.shape[2])


def _dispatch_kernel(tail_ref, has_ref, d0_ref, d1_ref, t_ref, xs_hbm, zero_ref, sem, zsem):
    step = pl.program_id(0)
    tok_n = t_ref.shape[0]
    blk = zero_ref.shape[0]

    def tail_copy(e):
        return pltpu.make_async_copy(zero_ref, xs_hbm.at[pl.ds(tail_ref[e], blk)], zsem)

    @pl.when(step == 0)
    def _():
        zero_ref[...] = jnp.zeros_like(zero_ref)
        for e in range(tail_ref.shape[0]):
            @pl.when(has_ref[e] > 0)
            def _(e=e):
                tail_copy(e).start()
        for e in range(tail_ref.shape[0]):
            @pl.when(has_ref[e] > 0)
            def _(e=e):
                tail_copy(e).wait()

    def row_copy(a, dst):
        return pltpu.make_async_copy(t_ref.at[a], xs_hbm.at[dst], sem)

    def body(a, carry):
        row_copy(a, d0_ref[a]).start(priority=0)
        row_copy(a, d1_ref[a]).start(priority=1)
        return carry

    lax.fori_loop(0, tok_n, body, 0, unroll=8)

    for _ in range(2):
        pltpu.make_async_copy(t_ref, xs_hbm.at[pl.ds(0, tok_n)], sem).wait()


def _dispatch(t3, dest0, dest1, tail, has, *, n_slots, chunk):
    n, s, l = t3.shape
    return pl.pallas_call(
        _dispatch_kernel,
        grid_spec=pltpu.PrefetchScalarGridSpec(
            num_scalar_prefetch=2,
            grid=(n // chunk,),
            in_specs=[
                pl.BlockSpec((chunk,), lambda i, *_: (i,), memory_space=pltpu.SMEM),
                pl.BlockSpec((chunk,), lambda i, *_: (i,), memory_space=pltpu.SMEM),
                pl.BlockSpec((chunk, s, l), lambda i, *_: (i, 0, 0)),
            ],
            out_specs=pl.BlockSpec(memory_space=pl.ANY),
            scratch_shapes=[
                pltpu.VMEM((EXPERT_ROWS, s, l), BF16),
                pltpu.SemaphoreType.DMA,
                pltpu.SemaphoreType.DMA,
            ],
        ),
        out_shape=jax.ShapeDtypeStruct((n_slots, s, l), BF16),
        compiler_params=pltpu.CompilerParams(dimension_semantics=("arbitrary",), has_side_effects=True,
                                             vmem_limit_bytes=VMEM_LIMIT, disable_bounds_checks=True),
        name="dispatch",
    )(tail, has, dest0, dest1, t3)


def _expert_kernel(be_ref, nx_ref, nu_ref, xs_ref, wg_hbm, wu_hbm, wd_hbm, ys_ref,
                   wg_f, wu_f, wd_f, wg_s, wu_s, wd_s, sem):
    b = pl.program_id(0)
    e = be_ref[b]

    def fetch(expert):
        return (pltpu.make_async_copy(wg_hbm.at[expert], wg_f, sem.at[0]),
                pltpu.make_async_copy(wu_hbm.at[expert], wu_f, sem.at[1]),
                pltpu.make_async_copy(wd_hbm.at[expert], wd_f, sem.at[2]))

    @pl.when(b == 0)
    def _():
        for cp in fetch(e):
            cp.start()

    @pl.when((b == 0) | (e != be_ref[jnp.maximum(b - 1, 0)]))
    def _():
        for cp in fetch(e):
            cp.wait()
        wg_s[...] = wg_f[...].astype(BF16)
        wu_s[...] = wu_f[...].astype(BF16)
        wd_s[...] = wd_f[...].astype(BF16)

        @pl.when(nx_ref[b] >= 0)
        def _():
            for cp in fetch(nx_ref[b]):
                cp.start()

    @pl.when(b < nu_ref[0])
    def _():
        x = _from_row_tiles(xs_ref[...])
        g = _dot(x, wg_s[...])
        u = _dot(x, wu_s[...])
        hid = (g * jax.nn.sigmoid(g) * u).astype(BF16)
        ys_ref[...] = _to_row_tiles(_dot(hid, wd_s[...]))

    @pl.when(b >= nu_ref[0])
    def _():
        ys_ref[...] = jnp.zeros_like(ys_ref)


def _experts(xs3, block_expert, next_expert, n_used, wg, wu, wd):
    n_slots, s, l = xs3.shape
    d = s * l
    blk = EXPERT_ROWS
    hid = wg.shape[2]
    return pl.pallas_call(
        _expert_kernel,
        grid_spec=pltpu.PrefetchScalarGridSpec(
            num_scalar_prefetch=3,
            grid=(n_slots // blk,),
            in_specs=[
                pl.BlockSpec((blk, s, l), lambda b, be, nx, nu: (jnp.minimum(b, nu[0] - 1), 0, 0)),
                pl.BlockSpec(memory_space=pl.ANY),
                pl.BlockSpec(memory_space=pl.ANY),
                pl.BlockSpec(memory_space=pl.ANY),
            ],
            out_specs=pl.BlockSpec((blk, s, l), lambda b, be, nx, nu: (b, 0, 0)),
            scratch_shapes=[pltpu.VMEM((d, hid), F32), pltpu.VMEM((d, hid), F32), pltpu.VMEM((hid, d), F32),
                            pltpu.VMEM((d, hid), BF16), pltpu.VMEM((d, hid), BF16), pltpu.VMEM((hid, d), BF16),
                            pltpu.SemaphoreType.DMA((3,))],
        ),
        out_shape=jax.ShapeDtypeStruct((n_slots, s, l), BF16),
        compiler_params=_params(("arbitrary",)),
        name="experts",
    )(block_expert, next_expert, n_used, xs3, wg, wu, wd)


def _combine_kernel(c0_ref, c1_ref, n0_ref, n1_ref, h_ref, r_ref, nw_ref, ys_hbm, o_ref, buf, sem):
    i = pl.program_id(0)
    n_steps = pl.num_programs(0)
    tm = h_ref.shape[0]

    def row_copy(dst_row, src_row, slot):
        return pltpu.make_async_copy(ys_hbm.at[src_row], buf.at[slot, dst_row], sem.at[slot])

    def issue(d0, d1, slot):
        def body(a, carry):
            row_copy(a, d0[a], slot).start(priority=0)
            row_copy(tm + a, d1[a], slot).start(priority=1)
            return carry
        lax.fori_loop(0, tm, body, 0, unroll=8)

    @pl.when(i == 0)
    def _():
        issue(c0_ref, c1_ref, 0)

    @pl.when(i + 1 < n_steps)
    def _():
        issue(n0_ref, n1_ref, (i + 1) % 2)

    slot = i % 2
    pltpu.make_async_copy(ys_hbm.at[pl.ds(0, 2 * tm)], buf.at[slot], sem.at[slot]).wait()

    r = r_ref[...]
    y = (_from_row_tiles(buf[slot, 0:tm]).astype(F32) * r[:, 2:3]
         + _from_row_tiles(buf[slot, tm:2 * tm]).astype(F32) * r[:, 3:4])
    h = h_ref[...] + y
    ms = jnp.mean(h * h, axis=-1, keepdims=True)
    o_ref[...] = h * lax.rsqrt(ms + EPS) * nw_ref[...]


def _combine(dest0, dest1, h1, r, nw, ys, *, tm):
    n, d = h1.shape
    n_steps = n // tm
    cur = pl.BlockSpec((tm,), lambda i: (i,), memory_space=pltpu.SMEM)
    nxt = pl.BlockSpec((tm,), lambda i: (jnp.minimum(i + 1, n_steps - 1),), memory_space=pltpu.SMEM)
    return pl.pallas_call(
        _combine_kernel,
        grid=(n_steps,),
        in_specs=[
            cur, cur, nxt, nxt,
            pl.BlockSpec((tm, d), lambda i: (i, 0)),
            pl.BlockSpec((tm, r.shape[1]), lambda i: (i, 0)),
            pl.BlockSpec((1, d), lambda i: (0, 0)),
            pl.BlockSpec(memory_space=pl.ANY),
        ],
        out_specs=pl.BlockSpec((tm, d), lambda i: (i, 0)),
        out_shape=jax.ShapeDtypeStruct((n, d), F32),
        scratch_shapes=[pltpu.VMEM((2, 2 * tm) + ys.shape[1:], BF16), pltpu.SemaphoreType.DMA((2,))],
        compiler_params=pltpu.CompilerParams(dimension_semantics=("arbitrary",), vmem_limit_bytes=VMEM_LIMIT,
                                             disable_bounds_checks=True),
        name="combine",
    )(dest0, dest1, dest0, dest1, h1, r, nw, ys)


def _layer(h, positions, norm_mix_w, w_in, gk_up, gk_bias, gla_norm_w, w_br_gla, ret_norm_w, w_br_ret, w_out,
           norm_ffn_w, rg_w, rg_b, re_w, re_b, wg, wu, wd, norm_final_w, *, tm_in, tn_in, gla_rows, tm_mix,
           tm_comb, disp_chunk):
    batch, seq, d = h.shape
    n = batch * seq
    x2 = h.reshape(n, d)
    gqk, gv = GLA_HEADS * GLA_DK, GLA_HEADS * GLA_DV
    rqk, rv = RET_HEADS * RET_DK, RET_HEADS * RET_DV

    sizes = (gqk, gqk, gv, gv, GLA_GATE_RANK, rqk, rqk, rv, rv, d, d)
    offs = np.concatenate([[0], np.cumsum(sizes)])
    w_bf = w_in.astype(BF16)
    seg = lambda i: w_bf[:, offs[i]:offs[i + 1]]

    def even_odd(w):
        return w.reshape(d, RET_HEADS, RET_DK // 2, 2).transpose(0, 1, 3, 2).reshape(d, rqk)

    w_main = jnp.concatenate([seg(7), seg(8), seg(2), seg(3), even_odd(seg(5)), even_odd(seg(6)), seg(9), seg(10),
                              seg(0), seg(1)], axis=1)
    assert rv % gv == 0 and gv == rqk == d and rv == 2 * d and gqk * 2 == d
    ret_cols = (4 + 2, 4 + 3, 0, 1)
    gla_cols = (2 * (4 + 6), 2 * (4 + 6) + 1, 4, 5)
    mix_cols = (4 + 4, 4 + 5)
    w_gd = jnp.pad(seg(4), ((0, 0), (0, LANES - GLA_GATE_RANK)))

    proj, gd = _inproj(x2, norm_mix_w.reshape(1, d), w_main, w_gd, tm=tm_in, tn=tn_in)

    up = jnp.pad(gk_up, ((0, LANES - GLA_GATE_RANK), (0, 0)))
    up_hi = up.astype(BF16)
    up2 = jnp.stack([up_hi, (up - up_hi.astype(F32)).astype(BF16)])
    o_gla = _gla(proj, gd, up2, gk_bias.reshape(1, gqk), gla_norm_w.reshape(1, GLA_DV),
                 batch=batch, seq=seq, rows=gla_rows, cols=gla_cols)

    theta = (1.0 / (ROPE_BASE ** jnp.linspace(0.0, 1.0, RET_DK // 2, dtype=F32))).reshape(1, RET_DK // 2)
    o_ret = _retention(proj, positions.reshape(n, 1), theta, ret_norm_w.reshape(1, rv),
                       batch=batch, seq=seq, cols=ret_cols)

    wr = jnp.concatenate([re_w.transpose(1, 0, 2).reshape(d, N_EXPERTS), rg_w], axis=1)
    wr = jnp.pad(wr, ((0, 0), (0, LANES - N_EXPERTS - N_GROUPS)))
    wr_hi = wr.astype(BF16)
    wr2 = jnp.stack([wr_hi, (wr - wr_hi.astype(F32)).astype(BF16)])
    br = jnp.pad(jnp.concatenate([re_b.reshape(-1), rg_b]), (0, LANES - N_EXPERTS - N_GROUPS)).reshape(1, LANES)
    h1, t, r, rt, cnt = _mix(o_gla, o_ret, proj, x2, w_br_gla.astype(BF16), w_br_ret.astype(BF16),
                             w_out.astype(BF16), norm_ffn_w.reshape(1, d), wr2, br, tm=tm_mix, cols=mix_cols)

    blk = EXPERT_ROWS
    n_assign = 2 * n
    n_slots = -(-(n_assign + N_EXPERTS * (blk - 1)) // blk) * blk
    counts = cnt[0, :N_EXPERTS].astype(jnp.int32)
    padded = ((counts + blk - 1) // blk) * blk
    pad_end = jnp.cumsum(padded)
    pad_start = pad_end - padded
    dest0 = pad_start[rt[0].astype(jnp.int32)] + rt[4].astype(jnp.int32)
    dest1 = pad_start[rt[1].astype(jnp.int32)] + rt[5].astype(jnp.int32)
    block_start = jnp.arange(n_slots // blk, dtype=jnp.int32) * blk
    block_expert = jnp.minimum(jnp.sum((pad_end[None, :] <= block_start[:, None]).astype(jnp.int32), axis=1),
                               N_EXPERTS - 1)
    n_blocks = n_slots // blk
    blk_id = jnp.arange(n_blocks, dtype=jnp.int32)
    later_other = (block_expert[None, :] != block_expert[:, None]) & (blk_id[None, :] > blk_id[:, None])
    first_other = jnp.min(jnp.where(later_other, blk_id[None, :], n_blocks), axis=1)
    next_expert = jnp.where(first_other < n_blocks, block_expert[jnp.minimum(first_other, n_blocks - 1)], -1)
    n_used = (pad_end[-1:] // blk).astype(jnp.int32)
    spare = pad_end[-1] + jnp.arange(N_EXPERTS, dtype=jnp.int32) * blk
    tail = jnp.concatenate([jnp.maximum(pad_end - blk, 0), jnp.minimum(spare, n_slots - blk)]).astype(jnp.int32)
    has = jnp.concatenate([counts, (spare < n_slots).astype(jnp.int32)])

    xs = _dispatch(t, dest0, dest1, tail, has, n_slots=n_slots, chunk=disp_chunk)
    ys = _experts(xs, block_expert, next_expert.astype(jnp.int32), n_used, wg, wu, wd)
    out = _combine(dest0, dest1, h1, r, norm_final_w.reshape(1, d), ys, tm=tm_comb)
    return out.reshape(batch, seq, d)


def kernel(x, positions, norm_mix_w, w_in, gla_gk_up, gla_gk_bias, gla_norm_w, w_branch_gla, ret_norm_w,
           w_branch_ret, w_out, norm_ffn_w, router_group_w, router_group_b, router_expert_w, router_expert_b,
           expert_w_gate, expert_w_up, expert_w_down, norm_final_w):
    assert norm_mix_w.shape[0] == 1, "single-layer block"
    return _layer(x, positions, norm_mix_w[0], w_in[0], gla_gk_up[0], gla_gk_bias[0], gla_norm_w[0], w_branch_gla[0],
                  ret_norm_w[0], w_branch_ret[0], w_out[0], norm_ffn_w[0], router_group_w[0], router_group_b[0],
                  router_expert_w[0], router_expert_b[0], expert_w_gate[0], expert_w_up[0], expert_w_down[0],
                  norm_final_w, tm_in=2048, tn_in=1024, gla_rows=256, tm_mix=512, tm_comb=512,
                  disp_chunk=2048)
```

```python
import functools
import math

import jax
import jax.numpy as jnp
import numpy as np
from jax import lax
from jax.experimental import pallas as pl
from jax.experimental.pallas import tpu as pltpu

F32 = jnp.float32
BF16 = jnp.bfloat16

EPS = 1e-6
GLA_HEADS = 4
GLA_DK = 128
GLA_DV = 256
GLA_GATE_RANK = 16
GLA_GATE_TEMP = 16.0
RET_HEADS = 4
RET_DK = 256
RET_DV = 512
ROPE_BASE = 10000.0
N_GROUPS = 4
EXPERTS_PER_GROUP = 8
N_EXPERTS = N_GROUPS * EXPERTS_PER_GROUP
EXPERT_HIDDEN = 512

LANES = 128
GLA_CHUNK = 128
GLA_DIAG = 8
RET_CHUNK = 256
EXPERT_ROWS = 512
VMEM_LIMIT = 48 * 1024 * 1024


def _dot(a, b):
    return jnp.dot(a, b, preferred_element_type=F32)


def _dot_nt(a, b):
    return lax.dot_general(a, b, (((1,), (1,)), ((), ())), preferred_element_type=F32)


def _split2(a):
    hi = a.astype(BF16)
    lo = (a - hi.astype(F32)).astype(BF16)
    return hi, lo


def _params(sem, vmem=VMEM_LIMIT):
    return pltpu.CompilerParams(dimension_semantics=sem, vmem_limit_bytes=vmem)


def _inproj_kernel(x_ref, nw_ref, w_ref, wgd_ref, proj_ref, gd_ref, u_scr):
    @pl.when(pl.program_id(1) == 0)
    def _():
        x = x_ref[...]
        ms = jnp.mean(x * x, axis=-1, keepdims=True)
        u = (x * lax.rsqrt(ms + EPS) * nw_ref[...]).astype(BF16)
        u_scr[...] = u
        gd_ref[...] = _dot(u, wgd_ref[...])

    proj_ref[...] = _dot(u_scr[...], w_ref[...]).astype(BF16)


def _inproj(x2, nw, w_main, w_gd, *, tm, tn):
    n, d = x2.shape
    p = w_main.shape[1]
    return pl.pallas_call(
        _inproj_kernel,
        grid=(n // tm, p // tn),
        in_specs=[
            pl.BlockSpec((tm, d), lambda i, j: (i, 0)),
            pl.BlockSpec((1, d), lambda i, j: (0, 0)),
            pl.BlockSpec((d, tn), lambda i, j: (0, j)),
            pl.BlockSpec((d, LANES), lambda i, j: (0, 0)),
        ],
        out_specs=[
            pl.BlockSpec((tm, tn), lambda i, j: (i, j)),
            pl.BlockSpec((tm, LANES), lambda i, j: (i, 0)),
        ],
        out_shape=[
            jax.ShapeDtypeStruct((n, p), BF16),
            jax.ShapeDtypeStruct((n, LANES), F32),
        ],
        scratch_shapes=[pltpu.VMEM((tm, d), BF16)],
        compiler_params=_params(("arbitrary", "arbitrary")),
        name="inproj",
    )(x2, nw, w_main, w_gd)


def _gla_tables(c):
    levels = []
    s = c // 2
    while s >= GLA_DIAG:
        levels.append(s)
        s //= 2
    i = np.arange(c)[:, None]
    t = np.arange(c)[None, :]
    mats = [t <= i, t > i]
    masks = []
    for s in levels:
        bs = (i // s) * s
        mats.append((t > bs) & (t <= i))
        mats.append((t > i) & (t <= np.minimum(bs + s, c - 1)))
        masks.append(((i // (2 * s)) == (t // (2 * s))) & (((i // s) % 2) == 1) & (((t // s) % 2) == 0))
    for d in range(1, GLA_DIAG):
        mats.append((t > i - d) & (t <= i))
    for d in range(GLA_DIAG):
        masks.append((t == i - d) & ((i % GLA_DIAG) >= d))
    g = np.concatenate(mats, 0).astype(np.float32)
    m = np.stack(masks, 0).astype(np.float32)
    return g, m, tuple(levels)


def _gla_kernel(q_ref, k_ref, v_ref, gg_ref, gd_ref, up_ref, bias_ref, nw_ref, g_ref, m_ref,
                o_ref, st_ref, *, c, nlev):
    h_n, dk, dv = GLA_HEADS, GLA_DK, GLA_DV
    rows = q_ref.shape[0]

    @pl.when(pl.program_id(1) == 0)
    def _():
        st_ref[...] = jnp.zeros_like(st_ref)

    g_tab = g_ref[...]
    up_hi = up_ref[0]
    up_lo = up_ref[1]
    q_scale = dk ** -0.5
    band0 = 2 + 2 * nlev

    def chunk(ci, carry):
        r0 = pl.multiple_of(ci * c, c)
        rs = pl.ds(r0, c)
        gd_hi, gd_lo = _split2(gd_ref[rs, :])
        xg = _dot(gd_hi, up_hi) + _dot(gd_hi, up_lo) + _dot(gd_lo, up_hi) + bias_ref[...]
        la2 = (jnp.minimum(xg, 0.0) - jnp.log1p(jnp.exp(-jnp.abs(xg)))) * (math.log2(math.e) / GLA_GATE_TEMP)
        ex = jnp.exp2(_dot(g_tab, la2.astype(BF16)))

        for h in range(h_n):
            ks = slice(h * dk, (h + 1) * dk)
            vs = slice(h * dv, (h + 1) * dv)

            def tab(m, ks=ks):
                return ex[m * c:(m + 1) * c, ks]

            qh = q_ref[rs, ks].astype(F32) * q_scale
            kh = k_ref[rs, ks].astype(F32)
            vh = v_ref[rs, vs]
            st = st_ref[h]
            eb = tab(0)
            o = _dot_nt((qh * eb).astype(BF16), st.astype(BF16))
            parts = [m_ref[nlev] * jnp.sum(qh * kh, axis=-1, keepdims=True)]
            for li in range(nlev):
                qs = (qh * tab(2 + 2 * li)).astype(BF16)
                kk = (kh * tab(3 + 2 * li)).astype(BF16)
                parts.append(m_ref[li] * _dot_nt(qs, kk))
            kh_tiles = kh.reshape(c // GLA_DIAG, GLA_DIAG, dk)
            for d in range(1, GLA_DIAG):
                kr = pltpu.roll(kh_tiles, d, 1).reshape(c, dk)
                dd = jnp.sum(qh * kr * tab(band0 + d - 1), axis=-1, keepdims=True)
                parts.append(m_ref[nlev + d] * dd)
            while len(parts) > 1:
                parts = [functools.reduce(jnp.add, parts[p:p + 2]) for p in range(0, len(parts), 2)]
            o = o + _dot(parts[0].astype(BF16), vh)
            e_last = eb[c - 1:c, :]
            k_st = (kh * tab(1)).astype(BF16)
            v_t = vh.astype(F32).T.astype(BF16)
            st_ref[h] = st * e_last + _dot(v_t, k_st)

            ms = jnp.mean(o * o, axis=-1, keepdims=True)
            y = o * lax.rsqrt(ms + EPS) * nw_ref[...]
            gg = gg_ref[rs, vs].astype(F32)
            o_ref[rs, vs] = (y * (gg * jax.nn.sigmoid(gg))).astype(BF16)
        return carry

    lax.fori_loop(0, rows // c, chunk, 0, unroll=2)


def _gla(proj, gd, up2, bias, nw, *, batch, seq, rows, cols):
    n = proj.shape[0]
    c = GLA_CHUNK
    g_np, m_np, levels = _gla_tables(c)
    g_tab = jnp.asarray(g_np, BF16)
    m_tab = jnp.asarray(m_np, F32)
    qk_w = GLA_HEADS * GLA_DK
    v_w = GLA_HEADS * GLA_DV
    spb = seq // rows
    row = lambda b, s: b * spb + s
    kern = functools.partial(_gla_kernel, c=c, nlev=len(levels))
    return pl.pallas_call(
        kern,
        grid=(batch, spb),
        in_specs=[
            pl.BlockSpec((rows, qk_w), lambda b, s: (row(b, s), cols[0])),
            pl.BlockSpec((rows, qk_w), lambda b, s: (row(b, s), cols[1])),
            pl.BlockSpec((rows, v_w), lambda b, s: (row(b, s), cols[2])),
            pl.BlockSpec((rows, v_w), lambda b, s: (row(b, s), cols[3])),
            pl.BlockSpec((rows, LANES), lambda b, s: (row(b, s), 0)),
            pl.BlockSpec(up2.shape, lambda b, s: (0, 0, 0)),
            pl.BlockSpec(bias.shape, lambda b, s: (0, 0)),
            pl.BlockSpec(nw.shape, lambda b, s: (0, 0)),
            pl.BlockSpec(g_tab.shape, lambda b, s: (0, 0)),
            pl.BlockSpec(m_tab.shape, lambda b, s: (0, 0, 0)),
        ],
        out_specs=pl.BlockSpec((rows, v_w), lambda b, s: (row(b, s), 0)),
        out_shape=jax.ShapeDtypeStruct((n, v_w), BF16),
        scratch_shapes=[pltpu.VMEM((GLA_HEADS, GLA_DV, GLA_DK), F32)],
        compiler_params=_params(("arbitrary", "arbitrary")),
        name="gla",
    )(proj, proj, proj, proj, gd, up2, bias, nw, g_tab, m_tab)


def _ret_log_gamma(h):
    return math.log(1.0 - 2.0 ** (-5.0 - h))


def _ret_kernel(pos_ref, th_ref, q_ref, k_ref, v_ref, rg_ref, nw_ref, o_ref, s_ref, dm_ref):
    h_n, dk, dv = RET_HEADS, RET_DK, RET_DV
    c = q_ref.shape[0]
    half = dk // 2

    @pl.when(pl.program_id(1) == 0)
    def _():
        s_ref[...] = jnp.zeros_like(s_ref)

    @pl.when((pl.program_id(0) == 0) & (pl.program_id(1) == 0))
    def _():
        rel = (lax.broadcasted_iota(jnp.int32, (c, c), 0) - lax.broadcasted_iota(jnp.int32, (c, c), 1)).astype(F32)
        for h in range(h_n):
            dm_ref[h] = jnp.where(rel >= 0, jnp.exp(_ret_log_gamma(h) * jnp.maximum(rel, 0.0)), 0.0)

    ang = pos_ref[...].astype(F32) * th_ref[...]
    cs = jnp.cos(ang)
    sn = jnp.sin(ang)
    idx = lax.broadcasted_iota(jnp.int32, (c, 1), 0).astype(F32)
    k_scale = dk ** -0.5

    def rot(ref, h):
        a = ref[:, h * dk:h * dk + half].astype(F32)
        b = ref[:, h * dk + half:(h + 1) * dk].astype(F32)
        return jnp.concatenate([a * cs - b * sn, b * cs + a * sn], axis=-1)

    for h in range(h_n):
        lg = _ret_log_gamma(h)
        vs = slice(h * dv, (h + 1) * dv)
        qr = rot(q_ref, h)
        kr = rot(k_ref, h) * k_scale
        vh = v_ref[:, vs]
        scores = _dot_nt(qr.astype(BF16), kr.astype(BF16)) * dm_ref[h]
        q_dec = jnp.exp(lg * (idx + 1.0))
        k_dec = jnp.exp(lg * (c - 1.0 - idx))
        s = s_ref[h]
        o = _dot(scores.astype(BF16), vh) + _dot((qr * q_dec).astype(BF16), s.astype(BF16))
        k_t = (kr * k_dec).T.astype(BF16)
        s_ref[h] = math.exp(lg * c) * s + _dot(k_t, vh)

        mu = jnp.mean(o, axis=-1, keepdims=True)
        oc = o - mu
        var = jnp.mean(oc * oc, axis=-1, keepdims=True)
        y = oc * lax.rsqrt(var + EPS) * nw_ref[:, vs]
        rg = rg_ref[:, vs].astype(F32)
        o_ref[:, vs] = (y * (rg * jax.nn.sigmoid(rg))).astype(BF16)


def _retention(proj, pos, theta, nw, *, batch, seq, cols):
    n = proj.shape[0]
    c = RET_CHUNK
    qk_w = RET_HEADS * RET_DK
    v_w = RET_HEADS * RET_DV
    spb = seq // c
    row = lambda b, s: b * spb + s
    return pl.pallas_call(
        _ret_kernel,
        grid=(batch, spb),
        in_specs=[
            pl.BlockSpec((c, 1), lambda b, s: (row(b, s), 0)),
            pl.BlockSpec(theta.shape, lambda b, s: (0, 0)),
            pl.BlockSpec((c, qk_w), lambda b, s: (row(b, s), cols[0])),
            pl.BlockSpec((c, qk_w), lambda b, s: (row(b, s), cols[1])),
            pl.BlockSpec((c, v_w), lambda b, s: (row(b, s), cols[2])),
            pl.BlockSpec((c, v_w), lambda b, s: (row(b, s), cols[3])),
            pl.BlockSpec(nw.shape, lambda b, s: (0, 0)),
        ],
        out_specs=pl.BlockSpec((c, v_w), lambda b, s: (row(b, s), 0)),
        out_shape=jax.ShapeDtypeStruct((n, v_w), BF16),
        scratch_shapes=[
            pltpu.VMEM((RET_HEADS, RET_DK, RET_DV), F32),
            pltpu.VMEM((RET_HEADS, c, c), F32),
        ],
        compiler_params=_params(("arbitrary", "arbitrary")),
        name="retention",
    )(pos, theta, proj, proj, proj, proj, nw)


def _mix_kernel(og_ref, or_ref, ma_ref, mb_ref, x_ref, wa_ref, wb_ref, wo_ref, nw_ref, wr_ref, br_ref,
                tri_ref, h_ref, t_ref, r_ref, rt_ref, cnt_ref, base_ref):
    tm = x_ref.shape[0]

    @pl.when(pl.program_id(0) == 0)
    def _():
        base_ref[...] = jnp.zeros_like(base_ref)

    ya = _dot(og_ref[...], wa_ref[...])
    yb = _dot(or_ref[...], wb_ref[...])
    merged = (jax.nn.sigmoid(ma_ref[...].astype(F32)) * ya + jax.nn.sigmoid(mb_ref[...].astype(F32)) * yb)
    h = x_ref[...] + _dot(merged.astype(BF16), wo_ref[...])
    h_ref[...] = h
    ms = jnp.mean(h * h, axis=-1, keepdims=True)
    t = h * lax.rsqrt(ms + EPS) * nw_ref[...]
    t_ref[...] = _to_row_tiles(t)

    lane = lax.broadcasted_iota(jnp.int32, (tm, LANES), 1)
    neg = jnp.float32(-1e30)
    big = jnp.int32(1 << 20)

    def first_max(v):
        m = jnp.max(v, axis=-1, keepdims=True)
        return m, jnp.min(jnp.where(v == m, lane, big), axis=-1, keepdims=True)

    t_hi, t_lo = _split2(t)
    lg = _dot(t_hi, wr_ref[0]) + _dot(t_hi, wr_ref[1]) + _dot(t_lo, wr_ref[0]) + br_ref[...]
    g_valid = (lane >= N_EXPERTS) & (lane < N_EXPERTS + N_GROUPS)
    g_m, g_lane = first_max(jnp.where(g_valid, lg, neg))
    g_w = 1.0 / jnp.sum(jnp.where(g_valid, jnp.exp(lg - g_m), 0.0), axis=-1, keepdims=True)
    g_idx = g_lane - N_EXPERTS
    e_valid = (lane < N_EXPERTS) & ((lane // EXPERTS_PER_GROUP) == g_idx)
    el = jnp.where(e_valid, lg, neg)
    v1, i1 = first_max(el)
    v2, i2 = first_max(jnp.where(lane == i1, neg, el))
    e21 = jnp.exp(v2 - v1)
    w1 = g_w / (1.0 + e21)
    w2 = g_w * e21 / (1.0 + e21)

    o1 = lane == i1
    o2 = lane == i2
    osum = jnp.where(o1 | o2, 1.0, 0.0)
    base = base_ref[0:1, :]
    before = _dot(tri_ref[...], osum.astype(BF16)) + base
    r1 = jnp.sum(jnp.where(o1, before, 0.0), axis=-1, keepdims=True)
    r2 = jnp.sum(jnp.where(o2, before, 0.0), axis=-1, keepdims=True)
    base = base + jnp.sum(osum, axis=0, keepdims=True)
    base_ref[...] = jnp.broadcast_to(base, base_ref.shape)
    cnt_ref[...] = jnp.broadcast_to(base, cnt_ref.shape)

    cols = (i1.astype(F32), i2.astype(F32), w1, w2, r1, r2)
    wide = jnp.zeros((tm, LANES), F32)
    for ci, col in enumerate(cols):
        wide = jnp.where(lane == ci, col, wide)
    r_ref[...] = wide[:, :r_ref.shape[1]]
    rt_ref[...] = wide.T[:rt_ref.shape[0], :]


def _mix(o_gla, o_ret, proj, x2, wa, wb, wo, nw, wr2, br, *, tm, cols):
    n, d = x2.shape
    tri = jnp.asarray(np.tril(np.ones((tm, tm), np.float32), -1), BF16)
    const = lambda shape: pl.BlockSpec(shape, lambda i: (0,) * len(shape))
    return pl.pallas_call(
        _mix_kernel,
        grid=(n // tm,),
        in_specs=[
            pl.BlockSpec((tm, o_gla.shape[1]), lambda i: (i, 0)),
            pl.BlockSpec((tm, o_ret.shape[1]), lambda i: (i, 0)),
            pl.BlockSpec((tm, d), lambda i: (i, cols[0])),
            pl.BlockSpec((tm, d), lambda i: (i, cols[1])),
            pl.BlockSpec((tm, d), lambda i: (i, 0)),
            const(wa.shape), const(wb.shape), const(wo.shape), const(nw.shape),
            const(wr2.shape), const(br.shape), const(tri.shape),
        ],
        out_specs=[
            pl.BlockSpec((tm, d), lambda i: (i, 0)),
            pl.BlockSpec((tm, d // LANES, LANES), lambda i: (i, 0, 0)),
            pl.BlockSpec((tm, 8), lambda i: (i, 0)),
            pl.BlockSpec((8, tm), lambda i: (0, i)),
            pl.BlockSpec((8, LANES), lambda i: (0, 0)),
        ],
        out_shape=[
            jax.ShapeDtypeStruct((n, d), F32),
            jax.ShapeDtypeStruct((n, d // LANES, LANES), BF16),
            jax.ShapeDtypeStruct((n, 8), F32),
            jax.ShapeDtypeStruct((8, n), F32),
            jax.ShapeDtypeStruct((8, LANES), F32),
        ],
        scratch_shapes=[pltpu.VMEM((8, LANES), F32)],
        compiler_params=_params(("arbitrary",)),
        name="mix_router",
    )(o_gla, o_ret, proj, proj, x2, wa, wb, wo, nw, wr2, br, tri)


def _to_row_tiles(x):
    return x.astype(BF16).reshape(x.shape[0], x.shape[1] // LANES, LANES)


def _from_row_tiles(x3):
    return x3.reshape(x3.shape[0], x3.shape[1] * x3.shape[2])


def _dispatch_kernel(tail_ref, has_ref, d0_ref, d1_ref, t_ref, xs_hbm, zero_ref, sem, zsem):
    step = pl.program_id(0)
    tok_n = t_ref.shape[0]
    blk = zero_ref.shape[0]

    def tail_copy(e):
        return pltpu.make_async_copy(zero_ref, xs_hbm.at[pl.ds(tail_ref[e], blk)], zsem)

    @pl.when(step == 0)
    def _():
        zero_ref[...] = jnp.zeros_like(zero_ref)
        for e in range(tail_ref.shape[0]):
            @pl.when(has_ref[e] > 0)
            def _(e=e):
                tail_copy(e).start()
        for e in range(tail_ref.shape[0]):
            @pl.when(has_ref[e] > 0)
            def _(e=e):
                tail_copy(e).wait()

    def row_copy(a, dst):
        return pltpu.make_async_copy(t_ref.at[a], xs_hbm.at[dst], sem)

    def body(a, carry):
        row_copy(a, d0_ref[a]).start(priority=0)
        row_copy(a, d1_ref[a]).start(priority=1)
        return carry

    lax.fori_loop(0, tok_n, body, 0, unroll=8)

    for _ in range(2):
        pltpu.make_async_copy(t_ref, xs_hbm.at[pl.ds(0, tok_n)], sem).wait()


def _dispatch(t3, dest0, dest1, tail, has, *, n_slots, chunk):
    n, s, l = t3.shape
    return pl.pallas_call(
        _dispatch_kernel,
        grid_spec=pltpu.PrefetchScalarGridSpec(
            num_scalar_prefetch=2,
            grid=(n // chunk,),
            in_specs=[
                pl.BlockSpec((chunk,), lambda i, *_: (i,), memory_space=pltpu.SMEM),
                pl.BlockSpec((chunk,), lambda i, *_: (i,), memory_space=pltpu.SMEM),
                pl.BlockSpec((chunk, s, l), lambda i, *_: (i, 0, 0)),
            ],
            out_specs=pl.BlockSpec(memory_space=pl.ANY),
            scratch_shapes=[
                pltpu.VMEM((EXPERT_ROWS, s, l), BF16),
                pltpu.SemaphoreType.DMA,
                pltpu.SemaphoreType.DMA,
            ],
        ),
        out_shape=jax.ShapeDtypeStruct((n_slots, s, l), BF16),
        compiler_params=pltpu.CompilerParams(dimension_semantics=("arbitrary",), has_side_effects=True,
                                             vmem_limit_bytes=VMEM_LIMIT, disable_bounds_checks=True),
        name="dispatch",
    )(tail, has, dest0, dest1, t3)


def _expert_kernel(be_ref, nx_ref, nu_ref, xs_ref, wg_hbm, wu_hbm, wd_hbm, ys_ref,
                   wg_f, wu_f, wd_f, wg_s, wu_s, wd_s, sem):
    b = pl.program_id(0)
    e = be_ref[b]

    def fetch(expert):
        return (pltpu.make_async_copy(wg_hbm.at[expert], wg_f, sem.at[0]),
                pltpu.make_async_copy(wu_hbm.at[expert], wu_f, sem.at[1]),
                pltpu.make_async_copy(wd_hbm.at[expert], wd_f, sem.at[2]))

    @pl.when(b == 0)
    def _():
        for cp in fetch(e):
            cp.start()

    @pl.when((b == 0) | (e != be_ref[jnp.maximum(b - 1, 0)]))
    def _():
        for cp in fetch(e):
            cp.wait()
        wg_s[...] = wg_f[...].astype(BF16)
        wu_s[...] = wu_f[...].astype(BF16)
        wd_s[...] = wd_f[...].astype(BF16)

        @pl.when(nx_ref[b] >= 0)
        def _():
            for cp in fetch(nx_ref[b]):
                cp.start()

    @pl.when(b < nu_ref[0])
    def _():
        x = _from_row_tiles(xs_ref[...])
        g = _dot(x, wg_s[...])
        u = _dot(x, wu_s[...])
        hid = (g * jax.nn.sigmoid(g) * u).astype(BF16)
        ys_ref[...] = _to_row_tiles(_dot(hid, wd_s[...]))

    @pl.when(b >= nu_ref[0])
    def _():
        ys_ref[...] = jnp.zeros_like(ys_ref)


def _experts(xs3, block_expert, next_expert, n_used, wg, wu, wd):
    n_slots, s, l = xs3.shape
    d = s * l
    blk = EXPERT_ROWS
    hid = wg.shape[2]
    return pl.pallas_call(
        _expert_kernel,
        grid_spec=pltpu.PrefetchScalarGridSpec(
            num_scalar_prefetch=3,
            grid=(n_slots // blk,),
            in_specs=[
                pl.BlockSpec((blk, s, l), lambda b, be, nx, nu: (jnp.minimum(b, nu[0] - 1), 0, 0)),
                pl.BlockSpec(memory_space=pl.ANY),
                pl.BlockSpec(memory_space=pl.ANY),
                pl.BlockSpec(memory_space=pl.ANY),
            ],
            out_specs=pl.BlockSpec((blk, s, l), lambda b, be, nx, nu: (b, 0, 0)),
            scratch_shapes=[pltpu.VMEM((d, hid), F32), pltpu.VMEM((d, hid), F32), pltpu.VMEM((hid, d), F32),
                            pltpu.VMEM((d, hid), BF16), pltpu.VMEM((d, hid), BF16), pltpu.VMEM((hid, d), BF16),
                            pltpu.SemaphoreType.DMA((3,))],
        ),
        out_shape=jax.ShapeDtypeStruct((n_slots, s, l), BF16),
        compiler_params=_params(("arbitrary",)),
        name="experts",
    )(block_expert, next_expert, n_used, xs3, wg, wu, wd)


def _combine_kernel(c0_ref, c1_ref, n0_ref, n1_ref, h_ref, r_ref, nw_ref, ys_hbm, o_ref, buf, sem):
    i = pl.program_id(0)
    n_steps = pl.num_programs(0)
    tm = h_ref.shape[0]

    def row_copy(dst_row, src_row, slot):
        return pltpu.make_async_copy(ys_hbm.at[src_row], buf.at[slot, dst_row], sem.at[slot])

    def issue(d0, d1, slot):
        def body(a, carry):
            row_copy(a, d0[a], slot).start(priority=0)
            row_copy(tm + a, d1[a], slot).start(priority=1)
            return carry
        lax.fori_loop(0, tm, body, 0, unroll=8)

    @pl.when(i == 0)
    def _():
        issue(c0_ref, c1_ref, 0)

    @pl.when(i + 1 < n_steps)
    def _():
        issue(n0_ref, n1_ref, (i + 1) % 2)

    slot = i % 2
    pltpu.make_async_copy(ys_hbm.at[pl.ds(0, 2 * tm)], buf.at[slot], sem.at[slot]).wait()

    r = r_ref[...]
    y = (_from_row_tiles(buf[slot, 0:tm]).astype(F32) * r[:, 2:3]
         + _from_row_tiles(buf[slot, tm:2 * tm]).astype(F32) * r[:, 3:4])
    h = h_ref[...] + y
    ms = jnp.mean(h * h, axis=-1, keepdims=True)
    o_ref[...] = h * lax.rsqrt(ms + EPS) * nw_ref[...]


def _combine(dest0, dest1, h1, r, nw, ys, *, tm):
    n, d = h1.shape
    n_steps = n // tm
    cur = pl.BlockSpec((tm,), lambda i: (i,), memory_space=pltpu.SMEM)
    nxt = pl.BlockSpec((tm,), lambda i: (jnp.minimum(i + 1, n_steps - 1),), memory_space=pltpu.SMEM)
    return pl.pallas_call(
        _combine_kernel,
        grid=(n_steps,),
        in_specs=[
            cur, cur, nxt, nxt,
            pl.BlockSpec((tm, d), lambda i: (i, 0)),
            pl.BlockSpec((tm, r.shape[1]), lambda i: (i, 0)),
            pl.BlockSpec((1, d), lambda i: (0, 0)),
            pl.BlockSpec(memory_space=pl.ANY),
        ],
        out_specs=pl.BlockSpec((tm, d), lambda i: (i, 0)),
        out_shape=jax.ShapeDtypeStruct((n, d), F32),
        scratch_shapes=[pltpu.VMEM((2, 2 * tm) + ys.shape[1:], BF16), pltpu.SemaphoreType.DMA((2,))],
        compiler_params=pltpu.CompilerParams(dimension_semantics=("arbitrary",), vmem_limit_bytes=VMEM_LIMIT,
                                             disable_bounds_checks=True),
        name="combine",
    )(dest0, dest1, dest0, dest1, h1, r, nw, ys)


def _layer(h, positions, norm_mix_w, w_in, gk_up, gk_bias, gla_norm_w, w_br_gla, ret_norm_w, w_br_ret, w_out,
           norm_ffn_w, rg_w, rg_b, re_w, re_b, wg, wu, wd, norm_final_w, *, tm_in, tn_in, gla_rows, tm_mix,
           tm_comb, disp_chunk):
    batch, seq, d = h.shape
    n = batch * seq
    x2 = h.reshape(n, d)
    gqk, gv = GLA_HEADS * GLA_DK, GLA_HEADS * GLA_DV
    rqk, rv = RET_HEADS * RET_DK, RET_HEADS * RET_DV

    sizes = (gqk, gqk, gv, gv, GLA_GATE_RANK, rqk, rqk, rv, rv, d, d)
    offs = np.concatenate([[0], np.cumsum(sizes)])
    tail0 = offs[5]
    w_head = w_in[:, :offs[4]].astype(BF16)
    w_tail = lax.optimization_barrier(w_in[:, tail0:].astype(BF16))
    seg = lambda i: (w_head[:, offs[i]:offs[i + 1]] if i < 4 else w_tail[:, offs[i] - tail0:offs[i + 1] - tail0])
    perm = np.concatenate([np.concatenate([np.arange(0, RET_DK, 2), np.arange(1, RET_DK, 2)]) + hh * RET_DK
                           for hh in range(RET_HEADS)])
    w_main = jnp.concatenate([seg(7), seg(8), seg(2), seg(3), seg(5)[:, perm], seg(6)[:, perm], seg(9), seg(10),
                              seg(0), seg(1)], axis=1)
    assert rv % gv == 0 and gv == rqk == d and rv == 2 * d and gqk * 2 == d
    ret_cols = (4 + 2, 4 + 3, 0, 1)
    gla_cols = (2 * (4 + 6), 2 * (4 + 6) + 1, 4, 5)
    mix_cols = (4 + 4, 4 + 5)
    w_gd = jnp.pad(w_in[:, offs[4]:offs[5]], ((0, 0), (0, LANES - GLA_GATE_RANK))).astype(BF16)

    proj, gd = _inproj(x2, norm_mix_w.reshape(1, d), w_main, w_gd, tm=tm_in, tn=tn_in)

    up = jnp.pad(gk_up, ((0, LANES - GLA_GATE_RANK), (0, 0)))
    up_hi = up.astype(BF16)
    up2 = jnp.stack([up_hi, (up - up_hi.astype(F32)).astype(BF16)])
    o_gla = _gla(proj, gd, up2, gk_bias.reshape(1, gqk), gla_norm_w.reshape(1, GLA_DV),
                 batch=batch, seq=seq, rows=gla_rows, cols=gla_cols)

    theta = (1.0 / (ROPE_BASE ** jnp.linspace(0.0, 1.0, RET_DK // 2, dtype=F32))).reshape(1, RET_DK // 2)
    o_ret = _retention(proj, positions.reshape(n, 1), theta, ret_norm_w.reshape(1, rv),
                       batch=batch, seq=seq, cols=ret_cols)

    wr = jnp.concatenate([re_w.transpose(1, 0, 2).reshape(d, N_EXPERTS), rg_w], axis=1)
    wr = jnp.pad(wr, ((0, 0), (0, LANES - N_EXPERTS - N_GROUPS)))
    wr_hi = wr.astype(BF16)
    wr2 = jnp.stack([wr_hi, (wr - wr_hi.astype(F32)).astype(BF16)])
    br = jnp.pad(jnp.concatenate([re_b.reshape(-1), rg_b]), (0, LANES - N_EXPERTS - N_GROUPS)).reshape(1, LANES)
    h1, t, r, rt, cnt = _mix(o_gla, o_ret, proj, x2, w_br_gla.astype(BF16), w_br_ret.astype(BF16),
                             w_out.astype(BF16), norm_ffn_w.reshape(1, d), wr2, br, tm=tm_mix, cols=mix_cols)

    blk = EXPERT_ROWS
    n_assign = 2 * n
    n_slots = -(-(n_assign + N_EXPERTS * (blk - 1)) // blk) * blk
    counts = cnt[0, :N_EXPERTS].astype(jnp.int32)
    padded = ((counts + blk - 1) // blk) * blk
    pad_end = jnp.cumsum(padded)
    pad_start = pad_end - padded
    dest0 = pad_start[rt[0].astype(jnp.int32)] + rt[4].astype(jnp.int32)
    dest1 = pad_start[rt[1].astype(jnp.int32)] + rt[5].astype(jnp.int32)
    block_start = jnp.arange(n_slots // blk, dtype=jnp.int32) * blk
    block_expert = jnp.minimum(jnp.sum((pad_end[None, :] <= block_start[:, None]).astype(jnp.int32), axis=1),
                               N_EXPERTS - 1)
    n_blocks = n_slots // blk
    blk_id = jnp.arange(n_blocks, dtype=jnp.int32)
    later_other = (block_expert[None, :] != block_expert[:, None]) & (blk_id[None, :] > blk_id[:, None])
    first_other = jnp.min(jnp.where(later_other, blk_id[None, :], n_blocks), axis=1)
    next_expert = jnp.where(first_other < n_blocks, block_expert[jnp.minimum(first_other, n_blocks - 1)], -1)
    n_used = (pad_end[-1:] // blk).astype(jnp.int32)
    spare = pad_end[-1] + jnp.arange(N_EXPERTS, dtype=jnp.int32) * blk
    tail = jnp.concatenate([jnp.maximum(pad_end - blk, 0), jnp.minimum(spare, n_slots - blk)]).astype(jnp.int32)
    has = jnp.concatenate([counts, (spare < n_slots).astype(jnp.int32)])

    xs = _dispatch(t, dest0, dest1, tail, has, n_slots=n_slots, chunk=disp_chunk)
    ys = _experts(xs, block_expert, next_expert.astype(jnp.int32), n_used, wg, wu, wd)
    out = _combine(dest0, dest1, h1, r, norm_final_w.reshape(1, d), ys, tm=tm_comb)
    return out.reshape(batch, seq, d)


def kernel(x, positions, norm_mix_w, w_in, gla_gk_up, gla_gk_bias, gla_norm_w, w_branch_gla, ret_norm_w,
           w_branch_ret, w_out, norm_ffn_w, router_group_w, router_group_b, router_expert_w, router_expert_b,
           expert_w_gate, expert_w_up, expert_w_down, norm_final_w):
    assert norm_mix_w.shape[0] == 1, "single-layer block"
    return _layer(x, positions, norm_mix_w[0], w_in[0], gla_gk_up[0], gla_gk_bias[0], gla_norm_w[0], w_branch_gla[0],
                  ret_norm_w[0], w_branch_ret[0], w_out[0], norm_ffn_w[0], router_group_w[0], router_group_b[0],
                  router_expert_w[0], router_expert_b[0], expert_w_gate[0], expert_w_up[0], expert_w_down[0],
                  norm_final_w, tm_in=2048, tn_in=1024, gla_rows=256, tm_mix=512, tm_comb=256,
                  disp_chunk=1024)
```

```python
import functools
import math

import jax
import jax.numpy as jnp
import numpy as np
from jax import lax
from jax.experimental import pallas as pl
from jax.experimental.pallas import tpu as pltpu

F32 = jnp.float32
BF16 = jnp.bfloat16

EPS = 1e-6
GLA_HEADS = 4
GLA_DK = 128
GLA_DV = 256
GLA_GATE_RANK = 16
GLA_GATE_TEMP = 16.0
RET_HEADS = 4
RET_DK = 256
RET_DV = 512
ROPE_BASE = 10000.0
N_GROUPS = 4
EXPERTS_PER_GROUP = 8
N_EXPERTS = N_GROUPS * EXPERTS_PER_GROUP
EXPERT_HIDDEN = 512

LANES = 128
GLA_CHUNK = 128
GLA_DIAG = 8
RET_CHUNK = 256
EXPERT_ROWS = 512
VMEM_LIMIT = 48 * 1024 * 1024


def _dot(a, b):
    return jnp.dot(a, b, preferred_element_type=F32)


def _dot_nt(a, b):
    return lax.dot_general(a, b, (((1,), (1,)), ((), ())), preferred_element_type=F32)


def _split2(a):
    hi = a.astype(BF16)
    lo = (a - hi.astype(F32)).astype(BF16)
    return hi, lo


def _params(sem, vmem=VMEM_LIMIT):
    return pltpu.CompilerParams(dimension_semantics=sem, vmem_limit_bytes=vmem)


def _inproj_kernel(x_ref, nw_ref, w_ref, wgd_ref, proj_ref, gd_ref, u_scr):
    @pl.when(pl.program_id(1) == 0)
    def _():
        x = x_ref[...]
        ms = jnp.mean(x * x, axis=-1, keepdims=True)
        u = (x * lax.rsqrt(ms + EPS) * nw_ref[...]).astype(BF16)
        u_scr[...] = u
        gd_ref[...] = _dot(u, wgd_ref[...])

    proj_ref[...] = _dot(u_scr[...], w_ref[...]).astype(BF16)


def _inproj(x2, nw, w_main, w_gd, *, tm, tn):
    n, d = x2.shape
    p = w_main.shape[1]
    return pl.pallas_call(
        _inproj_kernel,
        grid=(n // tm, p // tn),
        in_specs=[
            pl.BlockSpec((tm, d), lambda i, j: (i, 0)),
            pl.BlockSpec((1, d), lambda i, j: (0, 0)),
            pl.BlockSpec((d, tn), lambda i, j: (0, j)),
            pl.BlockSpec((d, LANES), lambda i, j: (0, 0)),
        ],
        out_specs=[
            pl.BlockSpec((tm, tn), lambda i, j: (i, j)),
            pl.BlockSpec((tm, LANES), lambda i, j: (i, 0)),
        ],
        out_shape=[
            jax.ShapeDtypeStruct((n, p), BF16),
            jax.ShapeDtypeStruct((n, LANES), F32),
        ],
        scratch_shapes=[pltpu.VMEM((tm, d), BF16)],
        compiler_params=_params(("arbitrary", "arbitrary")),
        name="inproj",
    )(x2, nw, w_main, w_gd)


def _gla_tables(c):
    levels = []
    s = c // 2
    while s >= GLA_DIAG:
        levels.append(s)
        s //= 2
    i = np.arange(c)[:, None]
    t = np.arange(c)[None, :]
    mats = [t <= i, t > i]
    masks = []
    for s in levels:
        bs = (i // s) * s
        mats.append((t > bs) & (t <= i))
        mats.append((t > i) & (t <= np.minimum(bs + s, c - 1)))
        masks.append(((i // (2 * s)) == (t // (2 * s))) & (((i // s) % 2) == 1) & (((t // s) % 2) == 0))
    for d in range(1, GLA_DIAG):
        mats.append((t > i - d) & (t <= i))
    for d in range(GLA_DIAG):
        masks.append((t == i - d) & ((i % GLA_DIAG) >= d))
    g = np.concatenate(mats, 0).astype(np.float32)
    m = np.stack(masks, 0).astype(np.float32)
    return g, m, tuple(levels)


def _gla_kernel(q_ref, k_ref, v_ref, gg_ref, gd_ref, up_ref, bias_ref, nw_ref, g_ref, m_ref,
                o_ref, st_ref, *, c, nlev):
    h_n, dk, dv = GLA_HEADS, GLA_DK, GLA_DV
    rows = q_ref.shape[0]

    @pl.when(pl.program_id(1) == 0)
    def _():
        st_ref[...] = jnp.zeros_like(st_ref)

    g_tab = g_ref[...]
    up_hi = up_ref[0]
    up_lo = up_ref[1]
    q_scale = dk ** -0.5
    band0 = 2 + 2 * nlev

    def chunk(ci, carry):
        r0 = pl.multiple_of(ci * c, c)
        rs = pl.ds(r0, c)
        gd_hi, gd_lo = _split2(gd_ref[rs, :])
        xg = _dot(gd_hi, up_hi) + _dot(gd_hi, up_lo) + _dot(gd_lo, up_hi) + bias_ref[...]
        la2 = (jnp.minimum(xg, 0.0) - jnp.log1p(jnp.exp(-jnp.abs(xg)))) * (math.log2(math.e) / GLA_GATE_TEMP)
        ex = jnp.exp2(_dot(g_tab, la2.astype(BF16)))

        for h in range(h_n):
            ks = slice(h * dk, (h + 1) * dk)
            vs = slice(h * dv, (h + 1) * dv)

            def tab(m, ks=ks):
                return ex[m * c:(m + 1) * c, ks]

            qh = q_ref[rs, ks].astype(F32) * q_scale
            kh = k_ref[rs, ks].astype(F32)
            vh = v_ref[rs, vs]
            st = st_ref[h]
            eb = tab(0)
            o = _dot_nt((qh * eb).astype(BF16), st.astype(BF16))
            parts = [m_ref[nlev] * jnp.sum(qh * kh, axis=-1, keepdims=True)]
            for li in range(nlev):
                qs = (qh * tab(2 + 2 * li)).astype(BF16)
                kk = (kh * tab(3 + 2 * li)).astype(BF16)
                parts.append(m_ref[li] * _dot_nt(qs, kk))
            kh_tiles = kh.reshape(c // GLA_DIAG, GLA_DIAG, dk)
            for d in range(1, GLA_DIAG):
                kr = pltpu.roll(kh_tiles, d, 1).reshape(c, dk)
                dd = jnp.sum(qh * kr * tab(band0 + d - 1), axis=-1, keepdims=True)
                parts.append(m_ref[nlev + d] * dd)
            while len(parts) > 1:
                parts = [functools.reduce(jnp.add, parts[p:p + 2]) for p in range(0, len(parts), 2)]
            o = o + _dot(parts[0].astype(BF16), vh)
            e_last = eb[c - 1:c, :]
            k_st = (kh * tab(1)).astype(BF16)
            v_t = vh.astype(F32).T.astype(BF16)
            st_ref[h] = st * e_last + _dot(v_t, k_st)

            ms = jnp.mean(o * o, axis=-1, keepdims=True)
            y = o * lax.rsqrt(ms + EPS) * nw_ref[...]
            gg = gg_ref[rs, vs].astype(F32)
            o_ref[rs, vs] = (y * (gg * jax.nn.sigmoid(gg))).astype(BF16)
        return carry

    lax.fori_loop(0, rows // c, chunk, 0, unroll=2)


def _gla(proj, gd, up2, bias, nw, *, batch, seq, rows, cols):
    n = proj.shape[0]
    c = GLA_CHUNK
    g_np, m_np, levels = _gla_tables(c)
    g_tab = jnp.asarray(g_np, BF16)
    m_tab = jnp.asarray(m_np, F32)
    qk_w = GLA_HEADS * GLA_DK
    v_w = GLA_HEADS * GLA_DV
    spb = seq // rows
    row = lambda b, s: b * spb + s
    kern = functools.partial(_gla_kernel, c=c, nlev=len(levels))
    return pl.pallas_call(
        kern,
        grid=(batch, spb),
        in_specs=[
            pl.BlockSpec((rows, qk_w), lambda b, s: (row(b, s), cols[0])),
            pl.BlockSpec((rows, qk_w), lambda b, s: (row(b, s), cols[1])),
            pl.BlockSpec((rows, v_w), lambda b, s: (row(b, s), cols[2])),
            pl.BlockSpec((rows, v_w), lambda b, s: (row(b, s), cols[3])),
            pl.BlockSpec((rows, LANES), lambda b, s: (row(b, s), 0)),
            pl.BlockSpec(up2.shape, lambda b, s: (0, 0, 0)),
            pl.BlockSpec(bias.shape, lambda b, s: (0, 0)),
            pl.BlockSpec(nw.shape, lambda b, s: (0, 0)),
            pl.BlockSpec(g_tab.shape, lambda b, s: (0, 0)),
            pl.BlockSpec(m_tab.shape, lambda b, s: (0, 0, 0)),
        ],
        out_specs=pl.BlockSpec((rows, v_w), lambda b, s: (row(b, s), 0)),
        out_shape=jax.ShapeDtypeStruct((n, v_w), BF16),
        scratch_shapes=[pltpu.VMEM((GLA_HEADS, GLA_DV, GLA_DK), F32)],
        compiler_params=_params(("arbitrary", "arbitrary")),
        name="gla",
    )(proj, proj, proj, proj, gd, up2, bias, nw, g_tab, m_tab)


def _ret_log_gamma(h):
    return math.log(1.0 - 2.0 ** (-5.0 - h))


def _ret_kernel(pos_ref, th_ref, q_ref, k_ref, v_ref, rg_ref, nw_ref, o_ref, s_ref, dm_ref):
    h_n, dk, dv = RET_HEADS, RET_DK, RET_DV
    c = q_ref.shape[0]
    half = dk // 2

    @pl.when(pl.program_id(1) == 0)
    def _():
        s_ref[...] = jnp.zeros_like(s_ref)

    @pl.when((pl.program_id(0) == 0) & (pl.program_id(1) == 0))
    def _():
        rel = (lax.broadcasted_iota(jnp.int32, (c, c), 0) - lax.broadcasted_iota(jnp.int32, (c, c), 1)).astype(F32)
        for h in range(h_n):
            dm_ref[h] = jnp.where(rel >= 0, jnp.exp(_ret_log_gamma(h) * jnp.maximum(rel, 0.0)), 0.0)

    ang = pos_ref[...].astype(F32) * th_ref[...]
    cs = jnp.cos(ang)
    sn = jnp.sin(ang)
    idx = lax.broadcasted_iota(jnp.int32, (c, 1), 0).astype(F32)
    k_scale = dk ** -0.5

    def rot(ref, h):
        a = ref[:, h * dk:h * dk + half].astype(F32)
        b = ref[:, h * dk + half:(h + 1) * dk].astype(F32)
        return jnp.concatenate([a * cs - b * sn, b * cs + a * sn], axis=-1)

    for h in range(h_n):
        lg = _ret_log_gamma(h)
        vs = slice(h * dv, (h + 1) * dv)
        qr = rot(q_ref, h)
        kr = rot(k_ref, h) * k_scale
        vh = v_ref[:, vs]
        scores = _dot_nt(qr.astype(BF16), kr.astype(BF16)) * dm_ref[h]
        q_dec = jnp.exp(lg * (idx + 1.0))
        k_dec = jnp.exp(lg * (c - 1.0 - idx))
        s = s_ref[h]
        o = _dot(scores.astype(BF16), vh) + _dot((qr * q_dec).astype(BF16), s.astype(BF16))
        k_t = (kr * k_dec).T.astype(BF16)
        s_ref[h] = math.exp(lg * c) * s + _dot(k_t, vh)

        mu = jnp.mean(o, axis=-1, keepdims=True)
        oc = o - mu
        var = jnp.mean(oc * oc, axis=-1, keepdims=True)
        y = oc * lax.rsqrt(var + EPS) * nw_ref[:, vs]
        rg = rg_ref[:, vs].astype(F32)
        o_ref[:, vs] = (y * (rg * jax.nn.sigmoid(rg))).astype(BF16)


def _retention(proj, pos, theta, nw, *, batch, seq, cols):
    n = proj.shape[0]
    c = RET_CHUNK
    qk_w = RET_HEADS * RET_DK
    v_w = RET_HEADS * RET_DV
    spb = seq // c
    row = lambda b, s: b * spb + s
    return pl.pallas_call(
        _ret_kernel,
        grid=(batch, spb),
        in_specs=[
            pl.BlockSpec((c, 1), lambda b, s: (row(b, s), 0)),
            pl.BlockSpec(theta.shape, lambda b, s: (0, 0)),
            pl.BlockSpec((c, qk_w), lambda b, s: (row(b, s), cols[0])),
            pl.BlockSpec((c, qk_w), lambda b, s: (row(b, s), cols[1])),
            pl.BlockSpec((c, v_w), lambda b, s: (row(b, s), cols[2])),
            pl.BlockSpec((c, v_w), lambda b, s: (row(b, s), cols[3])),
            pl.BlockSpec(nw.shape, lambda b, s: (0, 0)),
        ],
        out_specs=pl.BlockSpec((c, v_w), lambda b, s: (row(b, s), 0)),
        out_shape=jax.ShapeDtypeStruct((n, v_w), BF16),
        scratch_shapes=[
            pltpu.VMEM((RET_HEADS, RET_DK, RET_DV), F32),
            pltpu.VMEM((RET_HEADS, c, c), F32),
        ],
        compiler_params=_params(("arbitrary", "arbitrary")),
        name="retention",
    )(pos, theta, proj, proj, proj, proj, nw)


def _mix_kernel(og_ref, or_ref, ma_ref, mb_ref, x_ref, wa_ref, wb_ref, wo_ref, nw_ref, wr_ref, br_ref,
                tri_ref, h_ref, t_ref, r_ref, rt_ref, cnt_ref, base_ref):
    tm = x_ref.shape[0]

    @pl.when(pl.program_id(0) == 0)
    def _():
        base_ref[...] = jnp.zeros_like(base_ref)

    ya = _dot(og_ref[...], wa_ref[...])
    yb = _dot(or_ref[...], wb_ref[...])
    merged = (jax.nn.sigmoid(ma_ref[...].astype(F32)) * ya + jax.nn.sigmoid(mb_ref[...].astype(F32)) * yb)
    h = x_ref[...] + _dot(merged.astype(BF16), wo_ref[...])
    h_ref[...] = h
    ms = jnp.mean(h * h, axis=-1, keepdims=True)
    t = h * lax.rsqrt(ms + EPS) * nw_ref[...]
    t_ref[...] = _to_row_tiles(t)

    lane = lax.broadcasted_iota(jnp.int32, (tm, LANES), 1)
    neg = jnp.float32(-1e30)
    big = jnp.int32(1 << 20)

    def first_max(v):
        m = jnp.max(v, axis=-1, keepdims=True)
        return m, jnp.min(jnp.where(v == m, lane, big), axis=-1, keepdims=True)

    t_hi, t_lo = _split2(t)
    lg = _dot(t_hi, wr_ref[0]) + _dot(t_hi, wr_ref[1]) + _dot(t_lo, wr_ref[0]) + br_ref[...]
    g_valid = (lane >= N_EXPERTS) & (lane < N_EXPERTS + N_GROUPS)
    g_m, g_lane = first_max(jnp.where(g_valid, lg, neg))
    g_w = 1.0 / jnp.sum(jnp.where(g_valid, jnp.exp(lg - g_m), 0.0), axis=-1, keepdims=True)
    g_idx = g_lane - N_EXPERTS
    e_valid = (lane < N_EXPERTS) & ((lane // EXPERTS_PER_GROUP) == g_idx)
    el = jnp.where(e_valid, lg, neg)
    v1, i1 = first_max(el)
    v2, i2 = first_max(jnp.where(lane == i1, neg, el))
    e21 = jnp.exp(v2 - v1)
    w1 = g_w / (1.0 + e21)
    w2 = g_w * e21 / (1.0 + e21)

    o1 = lane == i1
    o2 = lane == i2
    osum = jnp.where(o1 | o2, 1.0, 0.0)
    base = base_ref[0:1, :]
    before = _dot(tri_ref[...], osum.astype(BF16)) + base
    r1 = jnp.sum(jnp.where(o1, before, 0.0), axis=-1, keepdims=True)
    r2 = jnp.sum(jnp.where(o2, before, 0.0), axis=-1, keepdims=True)
    base = base + jnp.sum(osum, axis=0, keepdims=True)
    base_ref[...] = jnp.broadcast_to(base, base_ref.shape)
    cnt_ref[...] = jnp.broadcast_to(base, cnt_ref.shape)

    cols = (i1.astype(F32), i2.astype(F32), w1, w2, r1, r2)
    wide = jnp.zeros((tm, LANES), F32)
    for ci, col in enumerate(cols):
        wide = jnp.where(lane == ci, col, wide)
    r_ref[...] = wide[:, :r_ref.shape[1]]
    rt_ref[...] = wide.T[:rt_ref.shape[0], :]


def _mix(o_gla, o_ret, proj, x2, wa, wb, wo, nw, wr2, br, *, tm, cols):
    n, d = x2.shape
    tri = jnp.asarray(np.tril(np.ones((tm, tm), np.float32), -1), BF16)
    const = lambda shape: pl.BlockSpec(shape, lambda i: (0,) * len(shape))
    return pl.pallas_call(
        _mix_kernel,
        grid=(n // tm,),
        in_specs=[
            pl.BlockSpec((tm, o_gla.shape[1]), lambda i: (i, 0)),
            pl.BlockSpec((tm, o_ret.shape[1]), lambda i: (i, 0)),
            pl.BlockSpec((tm, d), lambda i: (i, cols[0])),
            pl.BlockSpec((tm, d), lambda i: (i, cols[1])),
            pl.BlockSpec((tm, d), lambda i: (i, 0)),
            const(wa.shape), const(wb.shape), const(wo.shape), const(nw.shape),
            const(wr2.shape), const(br.shape), const(tri.shape),
        ],
        out_specs=[
            pl.BlockSpec((tm, d), lambda i: (i, 0)),
            pl.BlockSpec((tm, d // LANES, LANES), lambda i: (i, 0, 0)),
            pl.BlockSpec((tm, 8), lambda i: (i, 0)),
            pl.BlockSpec((8, tm), lambda i: (0, i)),
            pl.BlockSpec((8, LANES), lambda i: (0, 0)),
        ],
        out_shape=[
            jax.ShapeDtypeStruct((n, d), F32),
            jax.ShapeDtypeStruct((n, d // LANES, LANES), BF16),
            jax.ShapeDtypeStruct((n, 8), F32),
            jax.ShapeDtypeStruct((8, n), F32),
            jax.ShapeDtypeStruct((8, LANES), F32),
        ],
        scratch_shapes=[pltpu.VMEM((8, LANES), F32)],
        compiler_params=_params(("arbitrary",)),
        name="mix_router",
    )(o_gla, o_ret, proj, proj, x2, wa, wb, wo, nw, wr2, br, tri)


def _to_row_tiles(x):
    return x.astype(BF16).reshape(x.shape[0], x.shape[1] // LANES, LANES)


def _from_row_tiles(x3):
    return x3.reshape(x3.shape[0], x3.shape[1] * x3.shape[2])


def _dispatch_kernel(tail_ref, has_ref, d0_ref, d1_ref, t_ref, xs_hbm, zero_ref, sem, zsem):
    step = pl.program_id(0)
    tok_n = t_ref.shape[0]
    blk = zero_ref.shape[0]

    def tail_copy(e):
        return pltpu.make_async_copy(zero_ref, xs_hbm.at[pl.ds(tail_ref[e], blk)], zsem)

    @pl.when(step == 0)
    def _():
        zero_ref[...] = jnp.zeros_like(zero_ref)
        for e in range(tail_ref.shape[0]):
            @pl.when(has_ref[e] > 0)
            def _(e=e):
                tail_copy(e).start()
        for e in range(tail_ref.shape[0]):
            @pl.when(has_ref[e] > 0)
            def _(e=e):
                tail_copy(e).wait()

    def row_copy(a, dst):
        return pltpu.make_async_copy(t_ref.at[a], xs_hbm.at[dst], sem)

    def body(a, carry):
        row_copy(a, d0_ref[a]).start(priority=0)
        row_copy(a, d1_ref[a]).start(priority=1)
        return carry

    lax.fori_loop(0, tok_n, body, 0, unroll=8)

    for _ in range(2):
        pltpu.make_async_copy(t_ref, xs_hbm.at[pl.ds(0, tok_n)], sem).wait()


def _dispatch(t3, dest0, dest1, tail, has, *, n_slots, chunk):
    n, s, l = t3.shape
    return pl.pallas_call(
        _dispatch_kernel,
        grid_spec=pltpu.PrefetchScalarGridSpec(
            num_scalar_prefetch=2,
            grid=(n // chunk,),
            in_specs=[
                pl.BlockSpec((chunk,), lambda i, *_: (i,), memory_space=pltpu.SMEM),
                pl.BlockSpec((chunk,), lambda i, *_: (i,), memory_space=pltpu.SMEM),
                pl.BlockSpec((chunk, s, l), lambda i, *_: (i, 0, 0)),
            ],
            out_specs=pl.BlockSpec(memory_space=pl.ANY),
            scratch_shapes=[
                pltpu.VMEM((EXPERT_ROWS, s, l), BF16),
                pltpu.SemaphoreType.DMA,
                pltpu.SemaphoreType.DMA,
            ],
        ),
        out_shape=jax.ShapeDtypeStruct((n_slots, s, l), BF16),
        compiler_params=pltpu.CompilerParams(dimension_semantics=("arbitrary",), has_side_effects=True,
                                             vmem_limit_bytes=VMEM_LIMIT, disable_bounds_checks=True),
        name="dispatch",
    )(tail, has, dest0, dest1, t3)


def _expert_kernel(be_ref, nx_ref, nu_ref, xs_ref, wg_hbm, wu_hbm, wd_hbm, ys_ref,
                   wg_f, wu_f, wd_f, wg_s, wu_s, wd_s, sem):
    b = pl.program_id(0)
    e = be_ref[b]

    def fetch(expert):
        return (pltpu.make_async_copy(wg_hbm.at[expert], wg_f, sem.at[0]),
                pltpu.make_async_copy(wu_hbm.at[expert], wu_f, sem.at[1]),
                pltpu.make_async_copy(wd_hbm.at[expert], wd_f, sem.at[2]))

    @pl.when(b == 0)
    def _():
        for cp in fetch(e):
            cp.start()

    @pl.when((b == 0) | (e != be_ref[jnp.maximum(b - 1, 0)]))
    def _():
        for cp in fetch(e):
            cp.wait()
        wg_s[...] = wg_f[...].astype(BF16)
        wu_s[...] = wu_f[...].astype(BF16)
        wd_s[...] = wd_f[...].astype(BF16)

        @pl.when(nx_ref[b] >= 0)
        def _():
            for cp in fetch(nx_ref[b]):
                cp.start()

    @pl.when(b < nu_ref[0])
    def _():
        x = _from_row_tiles(xs_ref[...])
        g = _dot(x, wg_s[...])
        u = _dot(x, wu_s[...])
        hid = (g * jax.nn.sigmoid(g) * u).astype(BF16)
        ys_ref[...] = _to_row_tiles(_dot(hid, wd_s[...]))

    @pl.when(b >= nu_ref[0])
    def _():
        ys_ref[...] = jnp.zeros_like(ys_ref)


def _experts(xs3, block_expert, next_expert, n_used, wg, wu, wd):
    n_slots, s, l = xs3.shape
    d = s * l
    blk = EXPERT_ROWS
    hid = wg.shape[2]
    return pl.pallas_call(
        _expert_kernel,
        grid_spec=pltpu.PrefetchScalarGridSpec(
            num_scalar_prefetch=3,
            grid=(n_slots // blk,),
            in_specs=[
                pl.BlockSpec((blk, s, l), lambda b, be, nx, nu: (jnp.minimum(b, nu[0] - 1), 0, 0)),
                pl.BlockSpec(memory_space=pl.ANY),
                pl.BlockSpec(memory_space=pl.ANY),
                pl.BlockSpec(memory_space=pl.ANY),
            ],
            out_specs=pl.BlockSpec((blk, s, l), lambda b, be, nx, nu: (b, 0, 0)),
            scratch_shapes=[pltpu.VMEM((d, hid), F32), pltpu.VMEM((d, hid), F32), pltpu.VMEM((hid, d), F32),
                            pltpu.VMEM((d, hid), BF16), pltpu.VMEM((d, hid), BF16), pltpu.VMEM((hid, d), BF16),
                            pltpu.SemaphoreType.DMA((3,))],
        ),
        out_shape=jax.ShapeDtypeStruct((n_slots, s, l), BF16),
        compiler_params=_params(("arbitrary",)),
        name="experts",
    )(block_expert, next_expert, n_used, xs3, wg, wu, wd)


def _combine_kernel(c0_ref, c1_ref, n0_ref, n1_ref, h_ref, r_ref, nw_ref, ys_hbm, o_ref, buf, sem):
    i = pl.program_id(0)
    n_steps = pl.num_programs(0)
    tm = h_ref.shape[0]

    def row_copy(dst_row, src_row, slot):
        return pltpu.make_async_copy(ys_hbm.at[src_row], buf.at[slot, dst_row], sem.at[slot])

    def issue(d0, d1, slot):
        def body(a, carry):
            row_copy(a, d0[a], slot).start(priority=0)
            row_copy(tm + a, d1[a], slot).start(priority=1)
            return carry
        lax.fori_loop(0, tm, body, 0, unroll=8)

    @pl.when(i == 0)
    def _():
        issue(c0_ref, c1_ref, 0)

    @pl.when(i + 1 < n_steps)
    def _():
        issue(n0_ref, n1_ref, (i + 1) % 2)

    slot = i % 2
    pltpu.make_async_copy(ys_hbm.at[pl.ds(0, 2 * tm)], buf.at[slot], sem.at[slot]).wait()

    r = r_ref[...]
    y = (_from_row_tiles(buf[slot, 0:tm]).astype(F32) * r[:, 2:3]
         + _from_row_tiles(buf[slot, tm:2 * tm]).astype(F32) * r[:, 3:4])
    h = h_ref[...] + y
    ms = jnp.mean(h * h, axis=-1, keepdims=True)
    o_ref[...] = h * lax.rsqrt(ms + EPS) * nw_ref[...]


def _combine(dest0, dest1, h1, r, nw, ys, *, tm):
    n, d = h1.shape
    n_steps = n // tm
    cur = pl.BlockSpec((tm,), lambda i: (i,), memory_space=pltpu.SMEM)
    nxt = pl.BlockSpec((tm,), lambda i: (jnp.minimum(i + 1, n_steps - 1),), memory_space=pltpu.SMEM)
    return pl.pallas_call(
        _combine_kernel,
        grid=(n_steps,),
        in_specs=[
            cur, cur, nxt, nxt,
            pl.BlockSpec((tm, d), lambda i: (i, 0)),
            pl.BlockSpec((tm, r.shape[1]), lambda i: (i, 0)),
            pl.BlockSpec((1, d), lambda i: (0, 0)),
            pl.BlockSpec(memory_space=pl.ANY),
        ],
        out_specs=pl.BlockSpec((tm, d), lambda i: (i, 0)),
        out_shape=jax.ShapeDtypeStruct((n, d), F32),
        scratch_shapes=[pltpu.VMEM((2, 2 * tm) + ys.shape[1:], BF16), pltpu.SemaphoreType.DMA((2,))],
        compiler_params=pltpu.CompilerParams(dimension_semantics=("arbitrary",), vmem_limit_bytes=VMEM_LIMIT,
                                             disable_bounds_checks=True),
        name="combine",
    )(dest0, dest1, dest0, dest1, h1, r, nw, ys)


def _layer(h, positions, norm_mix_w, w_in, gk_up, gk_bias, gla_norm_w, w_br_gla, ret_norm_w, w_br_ret, w_out,
           norm_ffn_w, rg_w, rg_b, re_w, re_b, wg, wu, wd, norm_final_w, *, tm_in, tn_in, gla_rows, tm_mix,
           tm_comb, disp_chunk):
    batch, seq, d = h.shape
    n = batch * seq
    x2 = h.reshape(n, d)
    gqk, gv = GLA_HEADS * GLA_DK, GLA_HEADS * GLA_DV
    rqk, rv = RET_HEADS * RET_DK, RET_HEADS * RET_DV

    sizes = (gqk, gqk, gv, gv, GLA_GATE_RANK, rqk, rqk, rv, rv, d, d)
    offs = np.concatenate([[0], np.cumsum(sizes)])
    seg = lambda i: w_in[:, offs[i]:offs[i + 1]]
    perm = np.concatenate([np.concatenate([np.arange(0, RET_DK, 2), np.arange(1, RET_DK, 2)]) + hh * RET_DK
                           for hh in range(RET_HEADS)])
    w_main = jnp.concatenate([seg(7), seg(8), seg(2), seg(3), seg(5)[:, perm], seg(6)[:, perm], seg(9), seg(10),
                              seg(0), seg(1)], axis=1).astype(BF16)
    assert rv % gv == 0 and gv == rqk == d and rv == 2 * d and gqk * 2 == d
    ret_cols = (4 + 2, 4 + 3, 0, 1)
    gla_cols = (2 * (4 + 6), 2 * (4 + 6) + 1, 4, 5)
    mix_cols = (4 + 4, 4 + 5)
    w_gd = jnp.pad(seg(4), ((0, 0), (0, LANES - GLA_GATE_RANK))).astype(BF16)

    proj, gd = _inproj(x2, norm_mix_w.reshape(1, d), w_main, w_gd, tm=tm_in, tn=tn_in)

    up = jnp.pad(gk_up, ((0, LANES - GLA_GATE_RANK), (0, 0)))
    up_hi = up.astype(BF16)
    up2 = jnp.stack([up_hi, (up - up_hi.astype(F32)).astype(BF16)])
    o_gla = _gla(proj, gd, up2, gk_bias.reshape(1, gqk), gla_norm_w.reshape(1, GLA_DV),
                 batch=batch, seq=seq, rows=gla_rows, cols=gla_cols)

    theta = (1.0 / (ROPE_BASE ** jnp.linspace(0.0, 1.0, RET_DK // 2, dtype=F32))).reshape(1, RET_DK // 2)
    o_ret = _retention(proj, positions.reshape(n, 1), theta, ret_norm_w.reshape(1, rv),
                       batch=batch, seq=seq, cols=ret_cols)

    wr = jnp.concatenate([re_w.transpose(1, 0, 2).reshape(d, N_EXPERTS), rg_w], axis=1)
    wr = jnp.pad(wr, ((0, 0), (0, LANES - N_EXPERTS - N_GROUPS)))
    wr_hi = wr.astype(BF16)
    wr2 = jnp.stack([wr_hi, (wr - wr_hi.astype(F32)).astype(BF16)])
    br = jnp.pad(jnp.concatenate([re_b.reshape(-1), rg_b]), (0, LANES - N_EXPERTS - N_GROUPS)).reshape(1, LANES)
    h1, t, r, rt, cnt = _mix(o_gla, o_ret, proj, x2, w_br_gla.astype(BF16), w_br_ret.astype(BF16),
                             w_out.astype(BF16), norm_ffn_w.reshape(1, d), wr2, br, tm=tm_mix, cols=mix_cols)

    blk = EXPERT_ROWS
    n_assign = 2 * n
    n_slots = -(-(n_assign + N_EXPERTS * (blk - 1)) // blk) * blk
    counts = cnt[0, :N_EXPERTS].astype(jnp.int32)
    padded = ((counts + blk - 1) // blk) * blk
    pad_end = jnp.cumsum(padded)
    pad_start = pad_end - padded
    def first_slot(e):
        out = jnp.zeros_like(e)
        for k in range(N_EXPERTS):
            out = jnp.where(e == k, pad_start[k], out)
        return out

    dest0 = first_slot(rt[0].astype(jnp.int32)) + rt[4].astype(jnp.int32)
    dest1 = first_slot(rt[1].astype(jnp.int32)) + rt[5].astype(jnp.int32)
    block_start = jnp.arange(n_slots // blk, dtype=jnp.int32) * blk
    block_expert = jnp.minimum(jnp.sum((pad_end[None, :] <= block_start[:, None]).astype(jnp.int32), axis=1),
                               N_EXPERTS - 1)
    n_blocks = n_slots // blk
    blk_id = jnp.arange(n_blocks, dtype=jnp.int32)
    later_other = (block_expert[None, :] != block_expert[:, None]) & (blk_id[None, :] > blk_id[:, None])
    first_other = jnp.min(jnp.where(later_other, blk_id[None, :], n_blocks), axis=1)
    next_expert = jnp.where(first_other < n_blocks, block_expert[jnp.minimum(first_other, n_blocks - 1)], -1)
    n_used = (pad_end[-1:] // blk).astype(jnp.int32)
    spare = pad_end[-1] + jnp.arange(N_EXPERTS, dtype=jnp.int32) * blk
    tail = jnp.concatenate([jnp.maximum(pad_end - blk, 0), jnp.minimum(spare, n_slots - blk)]).astype(jnp.int32)
    has = jnp.concatenate([counts, (spare < n_slots).astype(jnp.int32)])

    xs = _dispatch(t, dest0, dest1, tail, has, n_slots=n_slots, chunk=disp_chunk)
    ys = _experts(xs, block_expert, next_expert.astype(jnp.int32), n_used, wg, wu, wd)
    out = _combine(dest0, dest1, h1, r, norm_final_w.reshape(1, d), ys, tm=tm_comb)
    return out.reshape(batch, seq, d)


def kernel(x, positions, norm_mix_w, w_in, gla_gk_up, gla_gk_bias, gla_norm_w, w_branch_gla, ret_norm_w,
           w_branch_ret, w_out, norm_ffn_w, router_group_w, router_group_b, router_expert_w, router_expert_b,
           expert_w_gate, expert_w_up, expert_w_down, norm_final_w):
    assert norm_mix_w.shape[0] == 1, "single-layer block"
    return _layer(x, positions, norm_mix_w[0], w_in[0], gla_gk_up[0], gla_gk_bias[0], gla_norm_w[0], w_branch_gla[0],
                  ret_norm_w[0], w_branch_ret[0], w_out[0], norm_ffn_w[0], router_group_w[0], router_group_b[0],
                  router_expert_w[0], router_expert_b[0], expert_w_gate[0], expert_w_up[0], expert_w_down[0],
                  norm_final_w, tm_in=2048, tn_in=1024, gla_rows=256, tm_mix=512, tm_comb=256,
                  disp_chunk=1024)
```

```python
import functools
import math

import jax
import jax.numpy as jnp
import numpy as np
from jax import lax
from jax.experimental import pallas as pl
from jax.experimental.pallas import tpu as pltpu

F32 = jnp.float32
BF16 = jnp.bfloat16

EPS = 1e-6
GLA_HEADS = 4
GLA_DK = 128
GLA_DV = 256
GLA_GATE_RANK = 16
GLA_GATE_TEMP = 16.0
RET_HEADS = 4
RET_DK = 256
RET_DV = 512
ROPE_BASE = 10000.0
N_GROUPS = 4
EXPERTS_PER_GROUP = 8
N_EXPERTS = N_GROUPS * EXPERTS_PER_GROUP
EXPERT_HIDDEN = 512

LANES = 128
GLA_CHUNK = 128
GLA_DIAG = 8
RET_CHUNK = 256
EXPERT_ROWS = 512
VMEM_LIMIT = 48 * 1024 * 1024


def _dot(a, b):
    return jnp.dot(a, b, preferred_element_type=F32)


def _dot_nt(a, b):
    return lax.dot_general(a, b, (((1,), (1,)), ((), ())), preferred_element_type=F32)


def _split2(a):
    hi = a.astype(BF16)
    lo = (a - hi.astype(F32)).astype(BF16)
    return hi, lo


def _params(sem, vmem=VMEM_LIMIT):
    return pltpu.CompilerParams(dimension_semantics=sem, vmem_limit_bytes=vmem)


def _inproj_kernel(x_ref, nw_ref, w_ref, wgd_ref, proj_ref, gd_ref, u_scr):
    @pl.when(pl.program_id(1) == 0)
    def _():
        x = x_ref[...]
        ms = jnp.mean(x * x, axis=-1, keepdims=True)
        u = (x * lax.rsqrt(ms + EPS) * nw_ref[...]).astype(BF16)
        u_scr[...] = u
        gd_ref[...] = _dot(u, wgd_ref[...])

    proj_ref[...] = _dot(u_scr[...], w_ref[...]).astype(BF16)


def _inproj(x2, nw, w_main, w_gd, *, tm, tn):
    n, d = x2.shape
    p = w_main.shape[1]
    return pl.pallas_call(
        _inproj_kernel,
        grid=(n // tm, p // tn),
        in_specs=[
            pl.BlockSpec((tm, d), lambda i, j: (i, 0)),
            pl.BlockSpec((1, d), lambda i, j: (0, 0)),
            pl.BlockSpec((d, tn), lambda i, j: (0, j)),
            pl.BlockSpec((d, LANES), lambda i, j: (0, 0)),
        ],
        out_specs=[
            pl.BlockSpec((tm, tn), lambda i, j: (i, j)),
            pl.BlockSpec((tm, LANES), lambda i, j: (i, 0)),
        ],
        out_shape=[
            jax.ShapeDtypeStruct((n, p), BF16),
            jax.ShapeDtypeStruct((n, LANES), F32),
        ],
        scratch_shapes=[pltpu.VMEM((tm, d), BF16)],
        compiler_params=_params(("arbitrary", "arbitrary")),
        name="inproj",
    )(x2, nw, w_main, w_gd)


def _gla_tables(c):
    levels = []
    s = c // 2
    while s >= GLA_DIAG:
        levels.append(s)
        s //= 2
    i = np.arange(c)[:, None]
    t = np.arange(c)[None, :]
    mats = [t <= i, t > i]
    masks = []
    for s in levels:
        bs = (i // s) * s
        mats.append((t > bs) & (t <= i))
        mats.append((t > i) & (t <= np.minimum(bs + s, c - 1)))
        masks.append(((i // (2 * s)) == (t // (2 * s))) & (((i // s) % 2) == 1) & (((t // s) % 2) == 0))
    for d in range(1, GLA_DIAG):
        mats.append((t > i - d) & (t <= i))
    for d in range(GLA_DIAG):
        masks.append((t == i - d) & ((i % GLA_DIAG) >= d))
    g = np.concatenate(mats, 0).astype(np.float32)
    m = np.stack(masks, 0).astype(np.float32)
    return g, m, tuple(levels)


def _gla_kernel(q_ref, k_ref, v_ref, gg_ref, gd_ref, up_ref, bias_ref, nw_ref, g_ref, m_ref,
                o_ref, st_ref, *, c, nlev):
    h_n, dk, dv = GLA_HEADS, GLA_DK, GLA_DV
    rows = q_ref.shape[0]

    @pl.when(pl.program_id(1) == 0)
    def _():
        st_ref[...] = jnp.zeros_like(st_ref)

    g_tab = g_ref[...]
    up_hi = up_ref[0]
    up_lo = up_ref[1]
    q_scale = dk ** -0.5
    band0 = 2 + 2 * nlev

    def chunk(ci, carry):
        r0 = pl.multiple_of(ci * c, c)
        rs = pl.ds(r0, c)
        gd_hi, gd_lo = _split2(gd_ref[rs, :])
        xg = _dot(gd_hi, up_hi) + _dot(gd_hi, up_lo) + _dot(gd_lo, up_hi) + bias_ref[...]
        la2 = (jnp.minimum(xg, 0.0) - jnp.log1p(jnp.exp(-jnp.abs(xg)))) * (math.log2(math.e) / GLA_GATE_TEMP)
        ex = jnp.exp2(_dot(g_tab, la2.astype(BF16)))

        for h in range(h_n):
            ks = slice(h * dk, (h + 1) * dk)
            vs = slice(h * dv, (h + 1) * dv)

            def tab(m, ks=ks):
                return ex[m * c:(m + 1) * c, ks]

            qh = q_ref[rs, ks].astype(F32) * q_scale
            kh = k_ref[rs, ks].astype(F32)
            vh = v_ref[rs, vs]
            st = st_ref[h]
            eb = tab(0)
            o = _dot_nt((qh * eb).astype(BF16), st.astype(BF16))
            parts = [m_ref[nlev] * jnp.sum(qh * kh, axis=-1, keepdims=True)]
            for li in range(nlev):
                qs = (qh * tab(2 + 2 * li)).astype(BF16)
                kk = (kh * tab(3 + 2 * li)).astype(BF16)
                parts.append(m_ref[li] * _dot_nt(qs, kk))
            kh_tiles = kh.reshape(c // GLA_DIAG, GLA_DIAG, dk)
            for d in range(1, GLA_DIAG):
                kr = pltpu.roll(kh_tiles, d, 1).reshape(c, dk)
                dd = jnp.sum(qh * kr * tab(band0 + d - 1), axis=-1, keepdims=True)
                parts.append(m_ref[nlev + d] * dd)
            while len(parts) > 1:
                parts = [functools.reduce(jnp.add, parts[p:p + 2]) for p in range(0, len(parts), 2)]
            o = o + _dot(parts[0].astype(BF16), vh)
            e_last = eb[c - 1:c, :]
            k_st = (kh * tab(1)).astype(BF16)
            v_t = vh.astype(F32).T.astype(BF16)
            st_ref[h] = st * e_last + _dot(v_t, k_st)

            ms = jnp.mean(o * o, axis=-1, keepdims=True)
            y = o * lax.rsqrt(ms + EPS) * nw_ref[...]
            gg = gg_ref[rs, vs].astype(F32)
            o_ref[rs, vs] = (y * (gg * jax.nn.sigmoid(gg))).astype(BF16)
        return carry

    lax.fori_loop(0, rows // c, chunk, 0, unroll=2)


def _gla(proj, gd, up2, bias, nw, *, batch, seq, rows, cols):
    n = proj.shape[0]
    c = GLA_CHUNK
    g_np, m_np, levels = _gla_tables(c)
    g_tab = jnp.asarray(g_np, BF16)
    m_tab = jnp.asarray(m_np, F32)
    qk_w = GLA_HEADS * GLA_DK
    v_w = GLA_HEADS * GLA_DV
    spb = seq // rows
    row = lambda b, s: b * spb + s
    kern = functools.partial(_gla_kernel, c=c, nlev=len(levels))
    return pl.pallas_call(
        kern,
        grid=(batch, spb),
        in_specs=[
            pl.BlockSpec((rows, qk_w), lambda b, s: (row(b, s), cols[0])),
            pl.BlockSpec((rows, qk_w), lambda b, s: (row(b, s), cols[1])),
            pl.BlockSpec((rows, v_w), lambda b, s: (row(b, s), cols[2])),
            pl.BlockSpec((rows, v_w), lambda b, s: (row(b, s), cols[3])),
            pl.BlockSpec((rows, LANES), lambda b, s: (row(b, s), 0)),
            pl.BlockSpec(up2.shape, lambda b, s: (0, 0, 0)),
            pl.BlockSpec(bias.shape, lambda b, s: (0, 0)),
            pl.BlockSpec(nw.shape, lambda b, s: (0, 0)),
            pl.BlockSpec(g_tab.shape, lambda b, s: (0, 0)),
            pl.BlockSpec(m_tab.shape, lambda b, s: (0, 0, 0)),
        ],
        out_specs=pl.BlockSpec((rows, v_w), lambda b, s: (row(b, s), 0)),
        out_shape=jax.ShapeDtypeStruct((n, v_w), BF16),
        scratch_shapes=[pltpu.VMEM((GLA_HEADS, GLA_DV, GLA_DK), F32)],
        compiler_params=_params(("arbitrary", "arbitrary")),
        name="gla",
    )(proj, proj, proj, proj, gd, up2, bias, nw, g_tab, m_tab)


def _ret_log_gamma(h):
    return math.log(1.0 - 2.0 ** (-5.0 - h))


def _ret_kernel(pos_ref, th_ref, q_ref, k_ref, v_ref, rg_ref, nw_ref, o_ref, s_ref, dm_ref):
    h_n, dk, dv = RET_HEADS, RET_DK, RET_DV
    c = q_ref.shape[0]
    half = dk // 2

    @pl.when(pl.program_id(1) == 0)
    def _():
        s_ref[...] = jnp.zeros_like(s_ref)

    @pl.when((pl.program_id(0) == 0) & (pl.program_id(1) == 0))
    def _():
        rel = (lax.broadcasted_iota(jnp.int32, (c, c), 0) - lax.broadcasted_iota(jnp.int32, (c, c), 1)).astype(F32)
        for h in range(h_n):
            dm_ref[h] = jnp.where(rel >= 0, jnp.exp(_ret_log_gamma(h) * jnp.maximum(rel, 0.0)), 0.0)

    ang = pos_ref[...].astype(F32) * th_ref[...]
    cs = jnp.cos(ang)
    sn = jnp.sin(ang)
    idx = lax.broadcasted_iota(jnp.int32, (c, 1), 0).astype(F32)
    k_scale = dk ** -0.5

    def rot(ref, h):
        a = ref[:, h * dk:h * dk + half].astype(F32)
        b = ref[:, h * dk + half:(h + 1) * dk].astype(F32)
        return jnp.concatenate([a * cs - b * sn, b * cs + a * sn], axis=-1)

    for h in range(h_n):
        lg = _ret_log_gamma(h)
        vs = slice(h * dv, (h + 1) * dv)
        qr = rot(q_ref, h)
        kr = rot(k_ref, h) * k_scale
        vh = v_ref[:, vs]
        scores = _dot_nt(qr.astype(BF16), kr.astype(BF16)) * dm_ref[h]
        q_dec = jnp.exp(lg * (idx + 1.0))
        k_dec = jnp.exp(lg * (c - 1.0 - idx))
        s = s_ref[h]
        o = _dot(scores.astype(BF16), vh) + _dot((qr * q_dec).astype(BF16), s.astype(BF16))
        k_t = (kr * k_dec).T.astype(BF16)
        s_ref[h] = math.exp(lg * c) * s + _dot(k_t, vh)

        mu = jnp.mean(o, axis=-1, keepdims=True)
        oc = o - mu
        var = jnp.mean(oc * oc, axis=-1, keepdims=True)
        y = oc * lax.rsqrt(var + EPS) * nw_ref[:, vs]
        rg = rg_ref[:, vs].astype(F32)
        o_ref[:, vs] = (y * (rg * jax.nn.sigmoid(rg))).astype(BF16)


def _retention(proj, pos, theta, nw, *, batch, seq, cols):
    n = proj.shape[0]
    c = RET_CHUNK
    qk_w = RET_HEADS * RET_DK
    v_w = RET_HEADS * RET_DV
    spb = seq // c
    row = lambda b, s: b * spb + s
    return pl.pallas_call(
        _ret_kernel,
        grid=(batch, spb),
        in_specs=[
            pl.BlockSpec((c, 1), lambda b, s: (row(b, s), 0)),
            pl.BlockSpec(theta.shape, lambda b, s: (0, 0)),
            pl.BlockSpec((c, qk_w), lambda b, s: (row(b, s), cols[0])),
            pl.BlockSpec((c, qk_w), lambda b, s: (row(b, s), cols[1])),
            pl.BlockSpec((c, v_w), lambda b, s: (row(b, s), cols[2])),
            pl.BlockSpec((c, v_w), lambda b, s: (row(b, s), cols[3])),
            pl.BlockSpec(nw.shape, lambda b, s: (0, 0)),
        ],
        out_specs=pl.BlockSpec((c, v_w), lambda b, s: (row(b, s), 0)),
        out_shape=jax.ShapeDtypeStruct((n, v_w), BF16),
        scratch_shapes=[
            pltpu.VMEM((RET_HEADS, RET_DK, RET_DV), F32),
            pltpu.VMEM((RET_HEADS, c, c), F32),
        ],
        compiler_params=_params(("arbitrary", "arbitrary")),
        name="retention",
    )(pos, theta, proj, proj, proj, proj, nw)


def _mix_kernel(og_ref, or_ref, ma_ref, mb_ref, x_ref, wa_ref, wb_ref, wo_ref, nw_ref, wr_ref, br_ref,
                tri_ref, h_ref, t_ref, r_ref, rt_ref, cnt_ref, base_ref):
    tm = x_ref.shape[0]

    @pl.when(pl.program_id(0) == 0)
    def _():
        base_ref[...] = jnp.zeros_like(base_ref)

    ya = _dot(og_ref[...], wa_ref[...])
    yb = _dot(or_ref[...], wb_ref[...])
    merged = (jax.nn.sigmoid(ma_ref[...].astype(F32)) * ya + jax.nn.sigmoid(mb_ref[...].astype(F32)) * yb)
    h = x_ref[...] + _dot(merged.astype(BF16), wo_ref[...])
    h_ref[...] = h
    ms = jnp.mean(h * h, axis=-1, keepdims=True)
    t = h * lax.rsqrt(ms + EPS) * nw_ref[...]
    t_ref[...] = _to_row_tiles(t)

    lane = lax.broadcasted_iota(jnp.int32, (tm, LANES), 1)
    neg = jnp.float32(-1e30)
    big = jnp.int32(1 << 20)

    def first_max(v):
        m = jnp.max(v, axis=-1, keepdims=True)
        return m, jnp.min(jnp.where(v == m, lane, big), axis=-1, keepdims=True)

    t_hi, t_lo = _split2(t)
    lg = _dot(t_hi, wr_ref[0]) + _dot(t_hi, wr_ref[1]) + _dot(t_lo, wr_ref[0]) + br_ref[...]
    g_valid = (lane >= N_EXPERTS) & (lane < N_EXPERTS + N_GROUPS)
    g_m, g_lane = first_max(jnp.where(g_valid, lg, neg))
    g_w = 1.0 / jnp.sum(jnp.where(g_valid, jnp.exp(lg - g_m), 0.0), axis=-1, keepdims=True)
    g_idx = g_lane - N_EXPERTS
    e_valid = (lane < N_EXPERTS) & ((lane // EXPERTS_PER_GROUP) == g_idx)
    el = jnp.where(e_valid, lg, neg)
    v1, i1 = first_max(el)
    v2, i2 = first_max(jnp.where(lane == i1, neg, el))
    e21 = jnp.exp(v2 - v1)
    w1 = g_w / (1.0 + e21)
    w2 = g_w * e21 / (1.0 + e21)

    o1 = lane == i1
    o2 = lane == i2
    osum = jnp.where(o1 | o2, 1.0, 0.0)
    base = base_ref[0:1, :]
    before = _dot(tri_ref[...], osum.astype(BF16)) + base
    r1 = jnp.sum(jnp.where(o1, before, 0.0), axis=-1, keepdims=True)
    r2 = jnp.sum(jnp.where(o2, before, 0.0), axis=-1, keepdims=True)
    base = base + jnp.sum(osum, axis=0, keepdims=True)
    base_ref[...] = jnp.broadcast_to(base, base_ref.shape)
    cnt_ref[...] = jnp.broadcast_to(base, cnt_ref.shape)

    cols = (i1.astype(F32), i2.astype(F32), w1, w2, r1, r2)
    wide = jnp.zeros((tm, LANES), F32)
    for ci, col in enumerate(cols):
        wide = jnp.where(lane == ci, col, wide)
    r_ref[...] = wide[:, :r_ref.shape[1]]
    rt_ref[...] = wide.T[:rt_ref.shape[0], :]


def _mix(o_gla, o_ret, proj, x2, wa, wb, wo, nw, wr2, br, *, tm, cols):
    n, d = x2.shape
    tri = jnp.asarray(np.tril(np.ones((tm, tm), np.float32), -1), BF16)
    const = lambda shape: pl.BlockSpec(shape, lambda i: (0,) * len(shape))
    return pl.pallas_call(
        _mix_kernel,
        grid=(n // tm,),
        in_specs=[
            pl.BlockSpec((tm, o_gla.shape[1]), lambda i: (i, 0)),
            pl.BlockSpec((tm, o_ret.shape[1]), lambda i: (i, 0)),
            pl.BlockSpec((tm, d), lambda i: (i, cols[0])),
            pl.BlockSpec((tm, d), lambda i: (i, cols[1])),
            pl.BlockSpec((tm, d), lambda i: (i, 0)),
            const(wa.shape), const(wb.shape), const(wo.shape), const(nw.shape),
            const(wr2.shape), const(br.shape), const(tri.shape),
        ],
        out_specs=[
            pl.BlockSpec((tm, d), lambda i: (i, 0)),
            pl.BlockSpec((tm, d // LANES, LANES), lambda i: (i, 0, 0)),
            pl.BlockSpec((tm, 8), lambda i: (i, 0)),
            pl.BlockSpec((8, tm), lambda i: (0, i)),
            pl.BlockSpec((8, LANES), lambda i: (0, 0)),
        ],
        out_shape=[
            jax.ShapeDtypeStruct((n, d), F32),
            jax.ShapeDtypeStruct((n, d // LANES, LANES), BF16),
            jax.ShapeDtypeStruct((n, 8), F32),
            jax.ShapeDtypeStruct((8, n), F32),
            jax.ShapeDtypeStruct((8, LANES), F32),
        ],
        scratch_shapes=[pltpu.VMEM((8, LANES), F32)],
        compiler_params=_params(("arbitrary",)),
        name="mix_router",
    )(o_gla, o_ret, proj, proj, x2, wa, wb, wo, nw, wr2, br, tri)


def _slots_kernel(ps_ref, rt_ref, o_ref):
    e = rt_ref[0:2, :].astype(jnp.int32)
    first = jnp.zeros_like(e)
    for k in range(N_EXPERTS):
        first = jnp.where(e == k, ps_ref[k], first)
    o_ref[...] = first + rt_ref[4:6, :].astype(jnp.int32)


def _slots(pad_start, rt):
    n = rt.shape[1]
    return pl.pallas_call(
        _slots_kernel,
        grid_spec=pltpu.PrefetchScalarGridSpec(
            num_scalar_prefetch=1,
            grid=(1,),
            in_specs=[pl.BlockSpec(rt.shape, lambda i, ps: (0, 0))],
            out_specs=pl.BlockSpec((2, n), lambda i, ps: (0, 0)),
        ),
        out_shape=jax.ShapeDtypeStruct((2, n), jnp.int32),
        compiler_params=_params(("arbitrary",)),
        name="slots",
    )(pad_start, rt)


def _to_row_tiles(x):
    return x.astype(BF16).reshape(x.shape[0], x.shape[1] // LANES, LANES)


def _from_row_tiles(x3):
    return x3.reshape(x3.shape[0], x3.shape[1] * x3.shape[2])


def _dispatch_kernel(tail_ref, has_ref, d0_ref, d1_ref, t_ref, xs_hbm, zero_ref, sem, zsem):
    step = pl.program_id(0)
    tok_n = t_ref.shape[0]
    blk = zero_ref.shape[0]

    def tail_copy(e):
        return pltpu.make_async_copy(zero_ref, xs_hbm.at[pl.ds(tail_ref[e], blk)], zsem)

    @pl.when(step == 0)
    def _():
        zero_ref[...] = jnp.zeros_like(zero_ref)
        for e in range(tail_ref.shape[0]):
            @pl.when(has_ref[e] > 0)
            def _(e=e):
                tail_copy(e).start()
        for e in range(tail_ref.shape[0]):
            @pl.when(has_ref[e] > 0)
            def _(e=e):
                tail_copy(e).wait()

    def row_copy(a, dst):
        return pltpu.make_async_copy(t_ref.at[a], xs_hbm.at[dst], sem)

    def body(a, carry):
        row_copy(a, d0_ref[a]).start(priority=0)
        row_copy(a, d1_ref[a]).start(priority=1)
        return carry

    lax.fori_loop(0, tok_n, body, 0, unroll=8)

    for _ in range(2):
        pltpu.make_async_copy(t_ref, xs_hbm.at[pl.ds(0, tok_n)], sem).wait()


def _dispatch(t3, dest0, dest1, tail, has, *, n_slots, chunk):
    n, s, l = t3.shape
    return pl.pallas_call(
        _dispatch_kernel,
        grid_spec=pltpu.PrefetchScalarGridSpec(
            num_scalar_prefetch=2,
            grid=(n // chunk,),
            in_specs=[
                pl.BlockSpec((chunk,), lambda i, *_: (i,), memory_space=pltpu.SMEM),
                pl.BlockSpec((chunk,), lambda i, *_: (i,), memory_space=pltpu.SMEM),
                pl.BlockSpec((chunk, s, l), lambda i, *_: (i, 0, 0)),
            ],
            out_specs=pl.BlockSpec(memory_space=pl.ANY),
            scratch_shapes=[
                pltpu.VMEM((EXPERT_ROWS, s, l), BF16),
                pltpu.SemaphoreType.DMA,
                pltpu.SemaphoreType.DMA,
            ],
        ),
        out_shape=jax.ShapeDtypeStruct((n_slots, s, l), BF16),
        compiler_params=pltpu.CompilerParams(dimension_semantics=("arbitrary",), has_side_effects=True,
                                             vmem_limit_bytes=VMEM_LIMIT, disable_bounds_checks=True),
        name="dispatch",
    )(tail, has, dest0, dest1, t3)


def _expert_kernel(be_ref, nx_ref, nu_ref, xs_ref, wg_hbm, wu_hbm, wd_hbm, ys_ref,
                   wg_f, wu_f, wd_f, wg_s, wu_s, wd_s, sem):
    b = pl.program_id(0)
    e = be_ref[b]

    def fetch(expert):
        return (pltpu.make_async_copy(wg_hbm.at[expert], wg_f, sem.at[0]),
                pltpu.make_async_copy(wu_hbm.at[expert], wu_f, sem.at[1]),
                pltpu.make_async_copy(wd_hbm.at[expert], wd_f, sem.at[2]))

    @pl.when(b == 0)
    def _():
        for cp in fetch(e):
            cp.start()

    @pl.when((b == 0) | (e != be_ref[jnp.maximum(b - 1, 0)]))
    def _():
        for cp in fetch(e):
            cp.wait()
        wg_s[...] = wg_f[...].astype(BF16)
        wu_s[...] = wu_f[...].astype(BF16)
        wd_s[...] = wd_f[...].astype(BF16)

        @pl.when(nx_ref[b] >= 0)
        def _():
            for cp in fetch(nx_ref[b]):
                cp.start()

    @pl.when(b < nu_ref[0])
    def _():
        x = _from_row_tiles(xs_ref[...])
        g = _dot(x, wg_s[...])
        u = _dot(x, wu_s[...])
        hid = (g * jax.nn.sigmoid(g) * u).astype(BF16)
        ys_ref[...] = _to_row_tiles(_dot(hid, wd_s[...]))

    @pl.when(b >= nu_ref[0])
    def _():
        ys_ref[...] = jnp.zeros_like(ys_ref)


def _experts(xs3, block_expert, next_expert, n_used, wg, wu, wd):
    n_slots, s, l = xs3.shape
    d = s * l
    blk = EXPERT_ROWS
    hid = wg.shape[2]
    return pl.pallas_call(
        _expert_kernel,
        grid_spec=pltpu.PrefetchScalarGridSpec(
            num_scalar_prefetch=3,
            grid=(n_slots // blk,),
            in_specs=[
                pl.BlockSpec((blk, s, l), lambda b, be, nx, nu: (jnp.minimum(b, nu[0] - 1), 0, 0)),
                pl.BlockSpec(memory_space=pl.ANY),
                pl.BlockSpec(memory_space=pl.ANY),
                pl.BlockSpec(memory_space=pl.ANY),
            ],
            out_specs=pl.BlockSpec((blk, s, l), lambda b, be, nx, nu: (b, 0, 0)),
            scratch_shapes=[pltpu.VMEM((d, hid), F32), pltpu.VMEM((d, hid), F32), pltpu.VMEM((hid, d), F32),
                            pltpu.VMEM((d, hid), BF16), pltpu.VMEM((d, hid), BF16), pltpu.VMEM((hid, d), BF16),
                            pltpu.SemaphoreType.DMA((3,))],
        ),
        out_shape=jax.ShapeDtypeStruct((n_slots, s, l), BF16),
        compiler_params=_params(("arbitrary",)),
        name="experts",
    )(block_expert, next_expert, n_used, xs3, wg, wu, wd)


def _combine_kernel(c0_ref, c1_ref, n0_ref, n1_ref, h_ref, r_ref, nw_ref, ys_hbm, o_ref, buf, sem):
    i = pl.program_id(0)
    n_steps = pl.num_programs(0)
    tm = h_ref.shape[0]

    def row_copy(dst_row, src_row, slot):
        return pltpu.make_async_copy(ys_hbm.at[src_row], buf.at[slot, dst_row], sem.at[slot])

    def issue(d0, d1, slot):
        def body(a, carry):
            row_copy(a, d0[a], slot).start(priority=0)
            row_copy(tm + a, d1[a], slot).start(priority=1)
            return carry
        lax.fori_loop(0, tm, body, 0, unroll=8)

    @pl.when(i == 0)
    def _():
        issue(c0_ref, c1_ref, 0)

    @pl.when(i + 1 < n_steps)
    def _():
        issue(n0_ref, n1_ref, (i + 1) % 2)

    slot = i % 2
    pltpu.make_async_copy(ys_hbm.at[pl.ds(0, 2 * tm)], buf.at[slot], sem.at[slot]).wait()

    r = r_ref[...]
    y = (_from_row_tiles(buf[slot, 0:tm]).astype(F32) * r[:, 2:3]
         + _from_row_tiles(buf[slot, tm:2 * tm]).astype(F32) * r[:, 3:4])
    h = h_ref[...] + y
    ms = jnp.mean(h * h, axis=-1, keepdims=True)
    o_ref[...] = h * lax.rsqrt(ms + EPS) * nw_ref[...]


def _combine(dest0, dest1, h1, r, nw, ys, *, tm):
    n, d = h1.shape
    n_steps = n // tm
    cur = pl.BlockSpec((tm,), lambda i: (i,), memory_space=pltpu.SMEM)
    nxt = pl.BlockSpec((tm,), lambda i: (jnp.minimum(i + 1, n_steps - 1),), memory_space=pltpu.SMEM)
    return pl.pallas_call(
        _combine_kernel,
        grid=(n_steps,),
        in_specs=[
            cur, cur, nxt, nxt,
            pl.BlockSpec((tm, d), lambda i: (i, 0)),
            pl.BlockSpec((tm, r.shape[1]), lambda i: (i, 0)),
            pl.BlockSpec((1, d), lambda i: (0, 0)),
            pl.BlockSpec(memory_space=pl.ANY),
        ],
        out_specs=pl.BlockSpec((tm, d), lambda i: (i, 0)),
        out_shape=jax.ShapeDtypeStruct((n, d), F32),
        scratch_shapes=[pltpu.VMEM((2, 2 * tm) + ys.shape[1:], BF16), pltpu.SemaphoreType.DMA((2,))],
        compiler_params=pltpu.CompilerParams(dimension_semantics=("arbitrary",), vmem_limit_bytes=VMEM_LIMIT,
                                             disable_bounds_checks=True),
        name="combine",
    )(dest0, dest1, dest0, dest1, h1, r, nw, ys)


def _layer(h, positions, norm_mix_w, w_in, gk_up, gk_bias, gla_norm_w, w_br_gla, ret_norm_w, w_br_ret, w_out,
           norm_ffn_w, rg_w, rg_b, re_w, re_b, wg, wu, wd, norm_final_w, *, tm_in, tn_in, gla_rows, tm_mix,
           tm_comb, disp_chunk):
    batch, seq, d = h.shape
    n = batch * seq
    x2 = h.reshape(n, d)
    gqk, gv = GLA_HEADS * GLA_DK, GLA_HEADS * GLA_DV
    rqk, rv = RET_HEADS * RET_DK, RET_HEADS * RET_DV

    sizes = (gqk, gqk, gv, gv, GLA_GATE_RANK, rqk, rqk, rv, rv, d, d)
    offs = np.concatenate([[0], np.cumsum(sizes)])
    seg = lambda i: w_in[:, offs[i]:offs[i + 1]]
    perm = np.concatenate([np.concatenate([np.arange(0, RET_DK, 2), np.arange(1, RET_DK, 2)]) + hh * RET_DK
                           for hh in range(RET_HEADS)])
    w_main = jnp.concatenate([seg(7), seg(8), seg(2), seg(3), seg(5)[:, perm], seg(6)[:, perm], seg(9), seg(10),
                              seg(0), seg(1)], axis=1).astype(BF16)
    assert rv % gv == 0 and gv == rqk == d and rv == 2 * d and gqk * 2 == d
    ret_cols = (4 + 2, 4 + 3, 0, 1)
    gla_cols = (2 * (4 + 6), 2 * (4 + 6) + 1, 4, 5)
    mix_cols = (4 + 4, 4 + 5)
    w_gd = jnp.pad(seg(4), ((0, 0), (0, LANES - GLA_GATE_RANK))).astype(BF16)

    proj, gd = _inproj(x2, norm_mix_w.reshape(1, d), w_main, w_gd, tm=tm_in, tn=tn_in)

    up = jnp.pad(gk_up, ((0, LANES - GLA_GATE_RANK), (0, 0)))
    up_hi = up.astype(BF16)
    up2 = jnp.stack([up_hi, (up - up_hi.astype(F32)).astype(BF16)])
    o_gla = _gla(proj, gd, up2, gk_bias.reshape(1, gqk), gla_norm_w.reshape(1, GLA_DV),
                 batch=batch, seq=seq, rows=gla_rows, cols=gla_cols)

    theta = (1.0 / (ROPE_BASE ** jnp.linspace(0.0, 1.0, RET_DK // 2, dtype=F32))).reshape(1, RET_DK // 2)
    o_ret = _retention(proj, positions.reshape(n, 1), theta, ret_norm_w.reshape(1, rv),
                       batch=batch, seq=seq, cols=ret_cols)

    wr = jnp.concatenate([re_w.transpose(1, 0, 2).reshape(d, N_EXPERTS), rg_w], axis=1)
    wr = jnp.pad(wr, ((0, 0), (0, LANES - N_EXPERTS - N_GROUPS)))
    wr_hi = wr.astype(BF16)
    wr2 = jnp.stack([wr_hi, (wr - wr_hi.astype(F32)).astype(BF16)])
    br = jnp.pad(jnp.concatenate([re_b.reshape(-1), rg_b]), (0, LANES - N_EXPERTS - N_GROUPS)).reshape(1, LANES)
    h1, t, r, rt, cnt = _mix(o_gla, o_ret, proj, x2, w_br_gla.astype(BF16), w_br_ret.astype(BF16),
                             w_out.astype(BF16), norm_ffn_w.reshape(1, d), wr2, br, tm=tm_mix, cols=mix_cols)

    blk = EXPERT_ROWS
    n_assign = 2 * n
    n_slots = -(-(n_assign + N_EXPERTS * (blk - 1)) // blk) * blk
    counts = cnt[0, :N_EXPERTS].astype(jnp.int32)
    padded = ((counts + blk - 1) // blk) * blk
    pad_end = jnp.cumsum(padded)
    pad_start = pad_end - padded
    dest = _slots(pad_start.astype(jnp.int32), rt)
    dest0, dest1 = dest[0], dest[1]
    block_start = jnp.arange(n_slots // blk, dtype=jnp.int32) * blk
    block_expert = jnp.minimum(jnp.sum((pad_end[None, :] <= block_start[:, None]).astype(jnp.int32), axis=1),
                               N_EXPERTS - 1)
    n_blocks = n_slots // blk
    blk_id = jnp.arange(n_blocks, dtype=jnp.int32)
    later_other = (block_expert[None, :] != block_expert[:, None]) & (blk_id[None, :] > blk_id[:, None])
    first_other = jnp.min(jnp.where(later_other, blk_id[None, :], n_blocks), axis=1)
    next_expert = jnp.where(first_other < n_blocks, block_expert[jnp.minimum(first_other, n_blocks - 1)], -1)
    n_used = (pad_end[-1:] // blk).astype(jnp.int32)
    spare = pad_end[-1] + jnp.arange(N_EXPERTS, dtype=jnp.int32) * blk
    tail = jnp.concatenate([jnp.maximum(pad_end - blk, 0), jnp.minimum(spare, n_slots - blk)]).astype(jnp.int32)
    has = jnp.concatenate([counts, (spare < n_slots).astype(jnp.int32)])

    xs = _dispatch(t, dest0, dest1, tail, has, n_slots=n_slots, chunk=disp_chunk)
    ys = _experts(xs, block_expert, next_expert.astype(jnp.int32), n_used, wg, wu, wd)
    out = _combine(dest0, dest1, h1, r, norm_final_w.reshape(1, d), ys, tm=tm_comb)
    return out.reshape(batch, seq, d)


def kernel(x, positions, norm_mix_w, w_in, gla_gk_up, gla_gk_bias, gla_norm_w, w_branch_gla, ret_norm_w,
           w_branch_ret, w_out, norm_ffn_w, router_group_w, router_group_b, router_expert_w, router_expert_b,
           expert_w_gate, expert_w_up, expert_w_down, norm_final_w):
    assert norm_mix_w.shape[0] == 1, "single-layer block"
    return _layer(x, positions, norm_mix_w[0], w_in[0], gla_gk_up[0], gla_gk_bias[0], gla_norm_w[0], w_branch_gla[0],
                  ret_norm_w[0], w_branch_ret[0], w_out[0], norm_ffn_w[0], router_group_w[0], router_group_b[0],
                  router_expert_w[0], router_expert_b[0], expert_w_gate[0], expert_w_up[0], expert_w_down[0],
                  norm_final_w, tm_in=2048, tn_in=1024, gla_rows=256, tm_mix=512, tm_comb=256,
                  disp_chunk=1024)
```

```python
import functools
import math

import jax
import jax.numpy as jnp
import numpy as np
from jax import lax
from jax.experimental import pallas as pl
from jax.experimental.pallas import tpu as pltpu

F32 = jnp.float32
BF16 = jnp.bfloat16

EPS = 1e-6
GLA_HEADS = 4
GLA_DK = 128
GLA_DV = 256
GLA_GATE_RANK = 16
GLA_GATE_TEMP = 16.0
RET_HEADS = 4
RET_DK = 256
RET_DV = 512
ROPE_BASE = 10000.0
N_GROUPS = 4
EXPERTS_PER_GROUP = 8
N_EXPERTS = N_GROUPS * EXPERTS_PER_GROUP
EXPERT_HIDDEN = 512

LANES = 128
GLA_CHUNK = 128
GLA_DIAG = 8
RET_CHUNK = 256
EXPERT_ROWS = 512
VMEM_LIMIT = 48 * 1024 * 1024


def _dot(a, b):
    return jnp.dot(a, b, preferred_element_type=F32)


def _dot_nt(a, b):
    return lax.dot_general(a, b, (((1,), (1,)), ((), ())), preferred_element_type=F32)


def _split2(a):
    hi = a.astype(BF16)
    lo = (a - hi.astype(F32)).astype(BF16)
    return hi, lo


def _params(sem, vmem=VMEM_LIMIT):
    return pltpu.CompilerParams(dimension_semantics=sem, vmem_limit_bytes=vmem)


def _inproj_kernel(x_ref, nw_ref, w_ref, wgd_ref, proj_ref, gd_ref, u_scr):
    @pl.when(pl.program_id(1) == 0)
    def _():
        x = x_ref[...]
        ms = jnp.mean(x * x, axis=-1, keepdims=True)
        u = (x * lax.rsqrt(ms + EPS) * nw_ref[...]).astype(BF16)
        u_scr[...] = u
        gd_ref[...] = _dot(u, wgd_ref[...])

    proj_ref[...] = _dot(u_scr[...], w_ref[...]).astype(BF16)


def _inproj(x2, nw, w_main, w_gd, *, tm, tn):
    n, d = x2.shape
    p = w_main.shape[1]
    return pl.pallas_call(
        _inproj_kernel,
        grid=(n // tm, p // tn),
        in_specs=[
            pl.BlockSpec((tm, d), lambda i, j: (i, 0)),
            pl.BlockSpec((1, d), lambda i, j: (0, 0)),
            pl.BlockSpec((d, tn), lambda i, j: (0, j)),
            pl.BlockSpec((d, LANES), lambda i, j: (0, 0)),
        ],
        out_specs=[
            pl.BlockSpec((tm, tn), lambda i, j: (i, j)),
            pl.BlockSpec((tm, LANES), lambda i, j: (i, 0)),
        ],
        out_shape=[
            jax.ShapeDtypeStruct((n, p), BF16),
            jax.ShapeDtypeStruct((n, LANES), F32),
        ],
        scratch_shapes=[pltpu.VMEM((tm, d), BF16)],
        compiler_params=_params(("arbitrary", "arbitrary")),
        name="inproj",
    )(x2, nw, w_main, w_gd)


def _gla_tables(c):
    levels = []
    s = c // 2
    while s >= GLA_DIAG:
        levels.append(s)
        s //= 2
    i = np.arange(c)[:, None]
    t = np.arange(c)[None, :]
    mats = [t <= i, t > i]
    masks = []
    for s in levels:
        bs = (i // s) * s
        mats.append((t > bs) & (t <= i))
        mats.append((t > i) & (t <= np.minimum(bs + s, c - 1)))
        masks.append(((i // (2 * s)) == (t // (2 * s))) & (((i // s) % 2) == 1) & (((t // s) % 2) == 0))
    for d in range(1, GLA_DIAG):
        mats.append((t > i - d) & (t <= i))
    for d in range(GLA_DIAG):
        masks.append((t == i - d) & ((i % GLA_DIAG) >= d))
    g = np.concatenate(mats, 0).astype(np.float32)
    m = np.stack(masks, 0).astype(np.float32)
    return g, m, tuple(levels)


def _gla_kernel(q_ref, k_ref, v_ref, gg_ref, gd_ref, up_ref, bias_ref, nw_ref, g_ref, m_ref,
                o_ref, st_ref, *, c, nlev):
    h_n, dk, dv = GLA_HEADS, GLA_DK, GLA_DV
    rows = q_ref.shape[0]

    @pl.when(pl.program_id(1) == 0)
    def _():
        st_ref[...] = jnp.zeros_like(st_ref)

    g_tab = g_ref[...]
    up_hi = up_ref[0]
    up_lo = up_ref[1]
    q_scale = dk ** -0.5
    band0 = 2 + 2 * nlev

    def chunk(ci, carry):
        r0 = pl.multiple_of(ci * c, c)
        rs = pl.ds(r0, c)
        gd_hi, gd_lo = _split2(gd_ref[rs, :])
        xg = _dot(gd_hi, up_hi) + _dot(gd_hi, up_lo) + _dot(gd_lo, up_hi) + bias_ref[...]
        la2 = (jnp.minimum(xg, 0.0) - jnp.log1p(jnp.exp(-jnp.abs(xg)))) * (math.log2(math.e) / GLA_GATE_TEMP)
        ex = jnp.exp2(_dot(g_tab, la2.astype(BF16)))

        for h in range(h_n):
            ks = slice(h * dk, (h + 1) * dk)
            vs = slice(h * dv, (h + 1) * dv)

            def tab(m, ks=ks):
                return ex[m * c:(m + 1) * c, ks]

            qh = q_ref[rs, ks].astype(F32) * q_scale
            kh = k_ref[rs, ks].astype(F32)
            vh = v_ref[rs, vs]
            st = st_ref[h]
            eb = tab(0)
            o = _dot_nt((qh * eb).astype(BF16), st.astype(BF16))
            parts = [m_ref[nlev] * jnp.sum(qh * kh, axis=-1, keepdims=True)]
            for li in range(nlev):
                qs = (qh * tab(2 + 2 * li)).astype(BF16)
                kk = (kh * tab(3 + 2 * li)).astype(BF16)
                parts.append(m_ref[li] * _dot_nt(qs, kk))
            kh_tiles = kh.reshape(c // GLA_DIAG, GLA_DIAG, dk)
            for d in range(1, GLA_DIAG):
                kr = pltpu.roll(kh_tiles, d, 1).reshape(c, dk)
                dd = jnp.sum(qh * kr * tab(band0 + d - 1), axis=-1, keepdims=True)
                parts.append(m_ref[nlev + d] * dd)
            while len(parts) > 1:
                parts = [functools.reduce(jnp.add, parts[p:p + 2]) for p in range(0, len(parts), 2)]
            o = o + _dot(parts[0].astype(BF16), vh)
            e_last = eb[c - 1:c, :]
            k_st = (kh * tab(1)).astype(BF16)
            v_t = vh.astype(F32).T.astype(BF16)
            st_ref[h] = st * e_last + _dot(v_t, k_st)

            ms = jnp.mean(o * o, axis=-1, keepdims=True)
            y = o * lax.rsqrt(ms + EPS) * nw_ref[...]
            gg = gg_ref[rs, vs].astype(F32)
            o_ref[rs, vs] = (y * (gg * jax.nn.sigmoid(gg))).astype(BF16)
        return carry

    lax.fori_loop(0, rows // c, chunk, 0, unroll=2)


def _gla(proj, gd, up2, bias, nw, *, batch, seq, rows, cols):
    n = proj.shape[0]
    c = GLA_CHUNK
    g_np, m_np, levels = _gla_tables(c)
    g_tab = jnp.asarray(g_np, BF16)
    m_tab = jnp.asarray(m_np, F32)
    qk_w = GLA_HEADS * GLA_DK
    v_w = GLA_HEADS * GLA_DV
    spb = seq // rows
    row = lambda b, s: b * spb + s
    kern = functools.partial(_gla_kernel, c=c, nlev=len(levels))
    return pl.pallas_call(
        kern,
        grid=(batch, spb),
        in_specs=[
            pl.BlockSpec((rows, qk_w), lambda b, s: (row(b, s), cols[0])),
            pl.BlockSpec((rows, qk_w), lambda b, s: (row(b, s), cols[1])),
            pl.BlockSpec((rows, v_w), lambda b, s: (row(b, s), cols[2])),
            pl.BlockSpec((rows, v_w), lambda b, s: (row(b, s), cols[3])),
            pl.BlockSpec((rows, LANES), lambda b, s: (row(b, s), 0)),
            pl.BlockSpec(up2.shape, lambda b, s: (0, 0, 0)),
            pl.BlockSpec(bias.shape, lambda b, s: (0, 0)),
            pl.BlockSpec(nw.shape, lambda b, s: (0, 0)),
            pl.BlockSpec(g_tab.shape, lambda b, s: (0, 0)),
            pl.BlockSpec(m_tab.shape, lambda b, s: (0, 0, 0)),
        ],
        out_specs=pl.BlockSpec((rows, v_w), lambda b, s: (row(b, s), 0)),
        out_shape=jax.ShapeDtypeStruct((n, v_w), BF16),
        scratch_shapes=[pltpu.VMEM((GLA_HEADS, GLA_DV, GLA_DK), F32)],
        compiler_params=_params(("arbitrary", "arbitrary")),
        name="gla",
    )(proj, proj, proj, proj, gd, up2, bias, nw, g_tab, m_tab)


def _ret_log_gamma(h):
    return math.log(1.0 - 2.0 ** (-5.0 - h))


def _ret_kernel(pos_ref, th_ref, q_ref, k_ref, v_ref, rg_ref, nw_ref, o_ref, s_ref, dm_ref):
    h_n, dk, dv = RET_HEADS, RET_DK, RET_DV
    c = q_ref.shape[0]
    half = dk // 2

    @pl.when(pl.program_id(1) == 0)
    def _():
        s_ref[...] = jnp.zeros_like(s_ref)

    @pl.when((pl.program_id(0) == 0) & (pl.program_id(1) == 0))
    def _():
        rel = (lax.broadcasted_iota(jnp.int32, (c, c), 0) - lax.broadcasted_iota(jnp.int32, (c, c), 1)).astype(F32)
        for h in range(h_n):
            dm_ref[h] = jnp.where(rel >= 0, jnp.exp(_ret_log_gamma(h) * jnp.maximum(rel, 0.0)), 0.0)

    ang = pos_ref[...].astype(F32) * th_ref[...]
    cs = jnp.cos(ang)
    sn = jnp.sin(ang)
    idx = lax.broadcasted_iota(jnp.int32, (c, 1), 0).astype(F32)
    k_scale = dk ** -0.5

    def rot(ref, h):
        a = ref[:, h * dk:h * dk + half].astype(F32)
        b = ref[:, h * dk + half:(h + 1) * dk].astype(F32)
        return jnp.concatenate([a * cs - b * sn, b * cs + a * sn], axis=-1)

    for h in range(h_n):
        lg = _ret_log_gamma(h)
        vs = slice(h * dv, (h + 1) * dv)
        qr = rot(q_ref, h)
        kr = rot(k_ref, h) * k_scale
        vh = v_ref[:, vs]
        scores = _dot_nt(qr.astype(BF16), kr.astype(BF16)) * dm_ref[h]
        q_dec = jnp.exp(lg * (idx + 1.0))
        k_dec = jnp.exp(lg * (c - 1.0 - idx))
        s = s_ref[h]
        o = _dot(scores.astype(BF16), vh) + _dot((qr * q_dec).astype(BF16), s.astype(BF16))
        k_t = (kr * k_dec).T.astype(BF16)
        s_ref[h] = math.exp(lg * c) * s + _dot(k_t, vh)

        mu = jnp.mean(o, axis=-1, keepdims=True)
        oc = o - mu
        var = jnp.mean(oc * oc, axis=-1, keepdims=True)
        y = oc * lax.rsqrt(var + EPS) * nw_ref[:, vs]
        rg = rg_ref[:, vs].astype(F32)
        o_ref[:, vs] = (y * (rg * jax.nn.sigmoid(rg))).astype(BF16)


def _retention(proj, pos, theta, nw, *, batch, seq, cols):
    n = proj.shape[0]
    c = RET_CHUNK
    qk_w = RET_HEADS * RET_DK
    v_w = RET_HEADS * RET_DV
    spb = seq // c
    row = lambda b, s: b * spb + s
    return pl.pallas_call(
        _ret_kernel,
        grid=(batch, spb),
        in_specs=[
            pl.BlockSpec((c, 1), lambda b, s: (row(b, s), 0)),
            pl.BlockSpec(theta.shape, lambda b, s: (0, 0)),
            pl.BlockSpec((c, qk_w), lambda b, s: (row(b, s), cols[0])),
            pl.BlockSpec((c, qk_w), lambda b, s: (row(b, s), cols[1])),
            pl.BlockSpec((c, v_w), lambda b, s: (row(b, s), cols[2])),
            pl.BlockSpec((c, v_w), lambda b, s: (row(b, s), cols[3])),
            pl.BlockSpec(nw.shape, lambda b, s: (0, 0)),
        ],
        out_specs=pl.BlockSpec((c, v_w), lambda b, s: (row(b, s), 0)),
        out_shape=jax.ShapeDtypeStruct((n, v_w), BF16),
        scratch_shapes=[
            pltpu.VMEM((RET_HEADS, RET_DK, RET_DV), F32),
            pltpu.VMEM((RET_HEADS, c, c), F32),
        ],
        compiler_params=_params(("arbitrary", "arbitrary")),
        name="retention",
    )(pos, theta, proj, proj, proj, proj, nw)


def _mix_kernel(og_ref, or_ref, ma_ref, mb_ref, x_ref, wa_ref, wb_ref, wo_ref, nw_ref, wr_ref, br_ref,
                tri_ref, h_ref, t_ref, r_ref, rt_ref, cnt_ref, base_ref):
    tm = x_ref.shape[0]

    @pl.when(pl.program_id(0) == 0)
    def _():
        base_ref[...] = jnp.zeros_like(base_ref)

    ya = _dot(og_ref[...], wa_ref[...])
    yb = _dot(or_ref[...], wb_ref[...])
    merged = (jax.nn.sigmoid(ma_ref[...].astype(F32)) * ya + jax.nn.sigmoid(mb_ref[...].astype(F32)) * yb)
    h = x_ref[...] + _dot(merged.astype(BF16), wo_ref[...])
    h_ref[...] = h
    ms = jnp.mean(h * h, axis=-1, keepdims=True)
    t = h * lax.rsqrt(ms + EPS) * nw_ref[...]
    t_ref[...] = _to_row_tiles(t)

    lane = lax.broadcasted_iota(jnp.int32, (tm, LANES), 1)
    neg = jnp.float32(-1e30)
    big = jnp.int32(1 << 20)

    def first_max(v):
        m = jnp.max(v, axis=-1, keepdims=True)
        return m, jnp.min(jnp.where(v == m, lane, big), axis=-1, keepdims=True)

    t_hi, t_lo = _split2(t)
    pr = _dot(jnp.concatenate([t_hi, t_lo], axis=0), wr_ref[...])
    lg = pr[:tm, :LANES] + pr[:tm, LANES:] + pr[tm:, :LANES] + br_ref[...]
    g_valid = (lane >= N_EXPERTS) & (lane < N_EXPERTS + N_GROUPS)
    g_m, g_lane = first_max(jnp.where(g_valid, lg, neg))
    g_w = 1.0 / jnp.sum(jnp.where(g_valid, jnp.exp(lg - g_m), 0.0), axis=-1, keepdims=True)
    g_idx = g_lane - N_EXPERTS
    e_valid = (lane < N_EXPERTS) & ((lane // EXPERTS_PER_GROUP) == g_idx)
    el = jnp.where(e_valid, lg, neg)
    v1, i1 = first_max(el)
    v2, i2 = first_max(jnp.where(lane == i1, neg, el))
    e21 = jnp.exp(v2 - v1)
    w1 = g_w / (1.0 + e21)
    w2 = g_w * e21 / (1.0 + e21)

    o1 = lane == i1
    o2 = lane == i2
    osum = jnp.where(o1 | o2, 1.0, 0.0)
    base = base_ref[0:1, :]
    before = _dot(tri_ref[...], osum.astype(BF16)) + base
    r1 = jnp.sum(jnp.where(o1, before, 0.0), axis=-1, keepdims=True)
    r2 = jnp.sum(jnp.where(o2, before, 0.0), axis=-1, keepdims=True)
    base = base + jnp.sum(osum, axis=0, keepdims=True)
    base_ref[...] = jnp.broadcast_to(base, base_ref.shape)
    cnt_ref[...] = jnp.broadcast_to(base, cnt_ref.shape)

    cols = (i1.astype(F32), i2.astype(F32), w1, w2, r1, r2)
    wide = jnp.zeros((tm, LANES), F32)
    for ci, col in enumerate(cols):
        wide = jnp.where(lane == ci, col, wide)
    r_ref[...] = wide[:, :r_ref.shape[1]]
    rt_ref[...] = wide.T[:rt_ref.shape[0], :]


def _mix(o_gla, o_ret, proj, x2, wa, wb, wo, nw, wr2, br, *, tm, cols):
    n, d = x2.shape
    tri = jnp.asarray(np.tril(np.ones((tm, tm), np.float32), -1), BF16)
    const = lambda shape: pl.BlockSpec(shape, lambda i: (0,) * len(shape))
    return pl.pallas_call(
        _mix_kernel,
        grid=(n // tm,),
        in_specs=[
            pl.BlockSpec((tm, o_gla.shape[1]), lambda i: (i, 0)),
            pl.BlockSpec((tm, o_ret.shape[1]), lambda i: (i, 0)),
            pl.BlockSpec((tm, d), lambda i: (i, cols[0])),
            pl.BlockSpec((tm, d), lambda i: (i, cols[1])),
            pl.BlockSpec((tm, d), lambda i: (i, 0)),
            const(wa.shape), const(wb.shape), const(wo.shape), const(nw.shape),
            const(wr2.shape), const(br.shape), const(tri.shape),
        ],
        out_specs=[
            pl.BlockSpec((tm, d), lambda i: (i, 0)),
            pl.BlockSpec((tm, d // LANES, LANES), lambda i: (i, 0, 0)),
            pl.BlockSpec((tm, 8), lambda i: (i, 0)),
            pl.BlockSpec((8, tm), lambda i: (0, i)),
            pl.BlockSpec((8, LANES), lambda i: (0, 0)),
        ],
        out_shape=[
            jax.ShapeDtypeStruct((n, d), F32),
            jax.ShapeDtypeStruct((n, d // LANES, LANES), BF16),
            jax.ShapeDtypeStruct((n, 8), F32),
            jax.ShapeDtypeStruct((8, n), F32),
            jax.ShapeDtypeStruct((8, LANES), F32),
        ],
        scratch_shapes=[pltpu.VMEM((8, LANES), F32)],
        compiler_params=_params(("arbitrary",)),
        name="mix_router",
    )(o_gla, o_ret, proj, proj, x2, wa, wb, wo, nw, wr2, br, tri)


def _slots_kernel(ps_ref, rt_ref, o_ref):
    e = rt_ref[0:2, :].astype(jnp.int32)
    first = jnp.zeros_like(e)
    for k in range(N_EXPERTS):
        first = jnp.where(e == k, ps_ref[k], first)
    o_ref[...] = first + rt_ref[4:6, :].astype(jnp.int32)


def _slots(pad_start, rt):
    n = rt.shape[1]
    return pl.pallas_call(
        _slots_kernel,
        grid_spec=pltpu.PrefetchScalarGridSpec(
            num_scalar_prefetch=1,
            grid=(1,),
            in_specs=[pl.BlockSpec(rt.shape, lambda i, ps: (0, 0))],
            out_specs=pl.BlockSpec((2, n), lambda i, ps: (0, 0)),
        ),
        out_shape=jax.ShapeDtypeStruct((2, n), jnp.int32),
        compiler_params=_params(("arbitrary",)),
        name="slots",
    )(pad_start, rt)


def _to_row_tiles(x):
    return x.astype(BF16).reshape(x.shape[0], x.shape[1] // LANES, LANES)


def _from_row_tiles(x3):
    return x3.reshape(x3.shape[0], x3.shape[1] * x3.shape[2])


def _dispatch_kernel(tail_ref, has_ref, d0_ref, d1_ref, t_ref, xs_hbm, zero_ref, sem, zsem):
    step = pl.program_id(0)
    tok_n = t_ref.shape[0]
    blk = zero_ref.shape[0]

    def tail_copy(e):
        return pltpu.make_async_copy(zero_ref, xs_hbm.at[pl.ds(tail_ref[e], blk)], zsem)

    @pl.when(step == 0)
    def _():
        zero_ref[...] = jnp.zeros_like(zero_ref)
        for e in range(tail_ref.shape[0]):
            @pl.when(has_ref[e] > 0)
            def _(e=e):
                tail_copy(e).start()
        for e in range(tail_ref.shape[0]):
            @pl.when(has_ref[e] > 0)
            def _(e=e):
                tail_copy(e).wait()

    def row_copy(a, dst):
        return pltpu.make_async_copy(t_ref.at[a], xs_hbm.at[dst], sem)

    def body(a, carry):
        row_copy(a, d0_ref[a]).start(priority=0)
        row_copy(a, d1_ref[a]).start(priority=1)
        return carry

    lax.fori_loop(0, tok_n, body, 0, unroll=8)

    for _ in range(2):
        pltpu.make_async_copy(t_ref, xs_hbm.at[pl.ds(0, tok_n)], sem).wait()


def _dispatch(t3, dest0, dest1, tail, has, *, n_slots, chunk):
    n, s, l = t3.shape
    return pl.pallas_call(
        _dispatch_kernel,
        grid_spec=pltpu.PrefetchScalarGridSpec(
            num_scalar_prefetch=2,
            grid=(n // chunk,),
            in_specs=[
                pl.BlockSpec((chunk,), lambda i, *_: (i,), memory_space=pltpu.SMEM),
                pl.BlockSpec((chunk,), lambda i, *_: (i,), memory_space=pltpu.SMEM),
                pl.BlockSpec((chunk, s, l), lambda i, *_: (i, 0, 0)),
            ],
            out_specs=pl.BlockSpec(memory_space=pl.ANY),
            scratch_shapes=[
                pltpu.VMEM((EXPERT_ROWS, s, l), BF16),
                pltpu.SemaphoreType.DMA,
                pltpu.SemaphoreType.DMA,
            ],
        ),
        out_shape=jax.ShapeDtypeStruct((n_slots, s, l), BF16),
        compiler_params=pltpu.CompilerParams(dimension_semantics=("arbitrary",), has_side_effects=True,
                                             vmem_limit_bytes=VMEM_LIMIT, disable_bounds_checks=True),
        name="dispatch",
    )(tail, has, dest0, dest1, t3)


def _expert_kernel(be_ref, nx_ref, rp_ref, nu_ref, xs_ref, wg_hbm, wu_hbm, wd_hbm, ys_ref,
                   wg_f, wu_f, wd_f, wg_s, wu_s, wd_s, hid_s, sem):
    b = pl.program_id(0)
    last = pl.num_programs(0) - 2
    ub = jnp.minimum(b, last)
    e = be_ref[ub]

    def fetch(expert):
        return (pltpu.make_async_copy(wg_hbm.at[expert], wg_f, sem.at[0]),
                pltpu.make_async_copy(wu_hbm.at[expert], wu_f, sem.at[1]),
                pltpu.make_async_copy(wd_hbm.at[expert], wd_f, sem.at[2]))

    @pl.when(b == 0)
    def _():
        hid_s[...] = jnp.zeros_like(hid_s)
        for cp in fetch(e):
            cp.start()

    @pl.when((b == 0) | (e != be_ref[jnp.maximum(b - 1, 0)]))
    def _():
        for cp in fetch(e):
            cp.wait()
        wg_s[...] = wg_f[...].astype(BF16)
        wu_s[...] = wu_f[...].astype(BF16)
        wd_s[rp_ref[ub]] = wd_f[...].astype(BF16)

        @pl.when(nx_ref[ub] >= 0)
        def _():
            for cp in fetch(nx_ref[ub]):
                cp.start()

    slot = b % 2
    x = _from_row_tiles(xs_ref[...])
    out = _dot(hid_s[1 - slot], wd_s[rp_ref[jnp.maximum(b - 1, 0)]])
    g = _dot(x, wg_s[...])
    u = _dot(x, wu_s[...])
    ys_ref[...] = _to_row_tiles(out)
    hid_s[slot] = (g * jax.nn.sigmoid(g) * u).astype(BF16)


def _experts(xs3, block_expert, next_expert, run_parity, n_used, wg, wu, wd):
    n_slots, s, l = xs3.shape
    d = s * l
    blk = EXPERT_ROWS
    hid = wg.shape[2]
    return pl.pallas_call(
        _expert_kernel,
        grid_spec=pltpu.PrefetchScalarGridSpec(
            num_scalar_prefetch=4,
            grid=(n_slots // blk + 1,),
            in_specs=[
                pl.BlockSpec((blk, s, l), lambda b, be, nx, rp, nu: (jnp.minimum(b, nu[0] - 1), 0, 0)),
                pl.BlockSpec(memory_space=pl.ANY),
                pl.BlockSpec(memory_space=pl.ANY),
                pl.BlockSpec(memory_space=pl.ANY),
            ],
            out_specs=pl.BlockSpec((blk, s, l), lambda b, be, nx, rp, nu: (jnp.maximum(b - 1, 0), 0, 0)),
            scratch_shapes=[pltpu.VMEM((d, hid), F32), pltpu.VMEM((d, hid), F32), pltpu.VMEM((hid, d), F32),
                            pltpu.VMEM((d, hid), BF16), pltpu.VMEM((d, hid), BF16), pltpu.VMEM((2, hid, d), BF16),
                            pltpu.VMEM((2, blk, hid), BF16), pltpu.SemaphoreType.DMA((3,))],
        ),
        out_shape=jax.ShapeDtypeStruct((n_slots, s, l), BF16),
        compiler_params=_params(("arbitrary",)),
        name="experts",
    )(block_expert, next_expert, run_parity, n_used, xs3, wg, wu, wd)


def _combine_kernel(c0_ref, c1_ref, n0_ref, n1_ref, h_ref, r_ref, nw_ref, ys_hbm, o_ref, buf, sem):
    i = pl.program_id(0)
    n_steps = pl.num_programs(0)
    tm = h_ref.shape[0]

    def row_copy(dst_row, src_row, slot):
        return pltpu.make_async_copy(ys_hbm.at[src_row], buf.at[slot, dst_row], sem.at[slot])

    def issue(d0, d1, slot):
        def body(a, carry):
            row_copy(a, d0[a], slot).start(priority=0)
            row_copy(tm + a, d1[a], slot).start(priority=1)
            return carry
        lax.fori_loop(0, tm, body, 0, unroll=8)

    @pl.when(i == 0)
    def _():
        issue(c0_ref, c1_ref, 0)

    @pl.when(i + 1 < n_steps)
    def _():
        issue(n0_ref, n1_ref, (i + 1) % 2)

    slot = i % 2
    pltpu.make_async_copy(ys_hbm.at[pl.ds(0, 2 * tm)], buf.at[slot], sem.at[slot]).wait()

    r = r_ref[...]
    y = (_from_row_tiles(buf[slot, 0:tm]).astype(F32) * r[:, 2:3]
         + _from_row_tiles(buf[slot, tm:2 * tm]).astype(F32) * r[:, 3:4])
    h = h_ref[...] + y
    ms = jnp.mean(h * h, axis=-1, keepdims=True)
    o_ref[...] = h * lax.rsqrt(ms + EPS) * nw_ref[...]


def _combine(dest0, dest1, h1, r, nw, ys, *, tm):
    n, d = h1.shape
    n_steps = n // tm
    cur = pl.BlockSpec((tm,), lambda i: (i,), memory_space=pltpu.SMEM)
    nxt = pl.BlockSpec((tm,), lambda i: (jnp.minimum(i + 1, n_steps - 1),), memory_space=pltpu.SMEM)
    return pl.pallas_call(
        _combine_kernel,
        grid=(n_steps,),
        in_specs=[
            cur, cur, nxt, nxt,
            pl.BlockSpec((tm, d), lambda i: (i, 0)),
            pl.BlockSpec((tm, r.shape[1]), lambda i: (i, 0)),
            pl.BlockSpec((1, d), lambda i: (0, 0)),
            pl.BlockSpec(memory_space=pl.ANY),
        ],
        out_specs=pl.BlockSpec((tm, d), lambda i: (i, 0)),
        out_shape=jax.ShapeDtypeStruct((n, d), F32),
        scratch_shapes=[pltpu.VMEM((2, 2 * tm) + ys.shape[1:], BF16), pltpu.SemaphoreType.DMA((2,))],
        compiler_params=pltpu.CompilerParams(dimension_semantics=("arbitrary",), vmem_limit_bytes=VMEM_LIMIT,
                                             disable_bounds_checks=True),
        name="combine",
    )(dest0, dest1, dest0, dest1, h1, r, nw, ys)


def _layer(h, positions, norm_mix_w, w_in, gk_up, gk_bias, gla_norm_w, w_br_gla, ret_norm_w, w_br_ret, w_out,
           norm_ffn_w, rg_w, rg_b, re_w, re_b, wg, wu, wd, norm_final_w, *, tm_in, tn_in, gla_rows, tm_mix,
           tm_comb, disp_chunk):
    batch, seq, d = h.shape
    n = batch * seq
    x2 = h.reshape(n, d)
    gqk, gv = GLA_HEADS * GLA_DK, GLA_HEADS * GLA_DV
    rqk, rv = RET_HEADS * RET_DK, RET_HEADS * RET_DV

    sizes = (gqk, gqk, gv, gv, GLA_GATE_RANK, rqk, rqk, rv, rv, d, d)
    offs = np.concatenate([[0], np.cumsum(sizes)])
    seg = lambda i: w_in[:, offs[i]:offs[i + 1]]
    perm = np.concatenate([np.concatenate([np.arange(0, RET_DK, 2), np.arange(1, RET_DK, 2)]) + hh * RET_DK
                           for hh in range(RET_HEADS)])
    w_main = jnp.concatenate([seg(7), seg(8), seg(2), seg(3), seg(5)[:, perm], seg(6)[:, perm], seg(9), seg(10),
                              seg(0), seg(1)], axis=1).astype(BF16)
    assert rv % gv == 0 and gv == rqk == d and rv == 2 * d and gqk * 2 == d
    ret_cols = (4 + 2, 4 + 3, 0, 1)
    gla_cols = (2 * (4 + 6), 2 * (4 + 6) + 1, 4, 5)
    mix_cols = (4 + 4, 4 + 5)
    w_gd = jnp.pad(seg(4), ((0, 0), (0, LANES - GLA_GATE_RANK))).astype(BF16)

    proj, gd = _inproj(x2, norm_mix_w.reshape(1, d), w_main, w_gd, tm=tm_in, tn=tn_in)

    up = jnp.pad(gk_up, ((0, LANES - GLA_GATE_RANK), (0, 0)))
    up_hi = up.astype(BF16)
    up2 = jnp.stack([up_hi, (up - up_hi.astype(F32)).astype(BF16)])
    o_gla = _gla(proj, gd, up2, gk_bias.reshape(1, gqk), gla_norm_w.reshape(1, GLA_DV),
                 batch=batch, seq=seq, rows=gla_rows, cols=gla_cols)

    theta = (1.0 / (ROPE_BASE ** jnp.linspace(0.0, 1.0, RET_DK // 2, dtype=F32))).reshape(1, RET_DK // 2)
    o_ret = _retention(proj, positions.reshape(n, 1), theta, ret_norm_w.reshape(1, rv),
                       batch=batch, seq=seq, cols=ret_cols)

    wr = jnp.concatenate([re_w.transpose(1, 0, 2).reshape(d, N_EXPERTS), rg_w], axis=1)
    wr = jnp.pad(wr, ((0, 0), (0, LANES - N_EXPERTS - N_GROUPS)))
    wr_hi = wr.astype(BF16)
    wr2 = jnp.concatenate([wr_hi, (wr - wr_hi.astype(F32)).astype(BF16)], axis=1)
    br = jnp.pad(jnp.concatenate([re_b.reshape(-1), rg_b]), (0, LANES - N_EXPERTS - N_GROUPS)).reshape(1, LANES)
    h1, t, r, rt, cnt = _mix(o_gla, o_ret, proj, x2, w_br_gla.astype(BF16), w_br_ret.astype(BF16),
                             w_out.astype(BF16), norm_ffn_w.reshape(1, d), wr2, br, tm=tm_mix, cols=mix_cols)

    blk = EXPERT_ROWS
    n_assign = 2 * n
    n_slots = -(-(n_assign + N_EXPERTS * (blk - 1)) // blk) * blk
    counts = cnt[0, :N_EXPERTS].astype(jnp.int32)
    padded = ((counts + blk - 1) // blk) * blk
    pad_end = jnp.cumsum(padded)
    pad_start = pad_end - padded
    dest = _slots(pad_start.astype(jnp.int32), rt)
    dest0, dest1 = dest[0], dest[1]
    block_start = jnp.arange(n_slots // blk, dtype=jnp.int32) * blk
    block_expert = jnp.minimum(jnp.sum((pad_end[None, :] <= block_start[:, None]).astype(jnp.int32), axis=1),
                               N_EXPERTS - 1)
    larger = jnp.where(block_expert[None, :] > block_expert[:, None], block_expert[None, :], N_EXPERTS)
    next_expert = jnp.min(larger, axis=1)
    next_expert = jnp.where(next_expert < N_EXPERTS, next_expert, -1)
    changed = jnp.concatenate([jnp.zeros((1,), jnp.int32),
                               (block_expert[1:] != block_expert[:-1]).astype(jnp.int32)])
    run_parity = jnp.cumsum(changed) % 2
    n_used = (pad_end[-1:] // blk).astype(jnp.int32)
    spare = pad_end[-1] + jnp.arange(N_EXPERTS, dtype=jnp.int32) * blk
    tail = jnp.concatenate([jnp.maximum(pad_end - blk, 0), jnp.minimum(spare, n_slots - blk)]).astype(jnp.int32)
    has = jnp.concatenate([counts, (spare < n_slots).astype(jnp.int32)])

    xs = _dispatch(t, dest0, dest1, tail, has, n_slots=n_slots, chunk=disp_chunk)
    ys = _experts(xs, block_expert, next_expert.astype(jnp.int32), run_parity.astype(jnp.int32), n_used,
                  wg, wu, wd)
    out = _combine(dest0, dest1, h1, r, norm_final_w.reshape(1, d), ys, tm=tm_comb)
    return out.reshape(batch, seq, d)


def kernel(x, positions, norm_mix_w, w_in, gla_gk_up, gla_gk_bias, gla_norm_w, w_branch_gla, ret_norm_w,
           w_branch_ret, w_out, norm_ffn_w, router_group_w, router_group_b, router_expert_w, router_expert_b,
           expert_w_gate, expert_w_up, expert_w_down, norm_final_w):
    assert norm_mix_w.shape[0] == 1, "single-layer block"
    return _layer(x, positions, norm_mix_w[0], w_in[0], gla_gk_up[0], gla_gk_bias[0], gla_norm_w[0], w_branch_gla[0],
                  ret_norm_w[0], w_branch_ret[0], w_out[0], norm_ffn_w[0], router_group_w[0], router_group_b[0],
                  router_expert_w[0], router_expert_b[0], expert_w_gate[0], expert_w_up[0], expert_w_down[0],
                  norm_final_w, tm_in=2048, tn_in=1024, gla_rows=256, tm_mix=512, tm_comb=256,
                  disp_chunk=1024)
```
